```python
import jax, jax.numpy as jnp
from jax import lax
import numpy as np


D_MODEL = 1024
BATCH = 8
SEQ = 4096
DEPTH = 4

ATT_HEAD_DIM = 64
D_ATT = D_MODEL // 2
N_ATT_HEADS = D_ATT // ATT_HEAD_DIM
D_CONV = D_MODEL // 2
N_CONV_GROUPS = 8
CONV_WIDTH = 31
Q_BLOCK = 128
LN_EPS = 1e-5
GATE_INIT_STD = 0.02
DEEPNORM_ALPHA = (2 * DEPTH) ** 0.25
DEEPNORM_BETA = (8 * DEPTH) ** -0.25
IN_WIDTHS = (D_ATT, D_ATT, D_ATT, D_ATT, D_CONV, D_CONV, D_CONV, D_MODEL, D_MODEL)
D_IN = sum(IN_WIDTHS)

kernel_name = 'stickbreak_conformer_gated_hybrid'


def _split_points():
    pts, acc = [], 0
    for w in IN_WIDTHS[:-1]:
        acc += w
        pts.append(acc)
    return pts


def layer_norm(x, g, b):
    xf = x.astype(jnp.float32)
    mu = jnp.mean(xf, axis=-1, keepdims=True)
    var = jnp.mean(jnp.square(xf - mu), axis=-1, keepdims=True)
    return ((xf - mu) * lax.rsqrt(var + LN_EPS) * g + b).astype(x.dtype)


def stick_breaking_attention(q, k, v):
    S = q.shape[1]
    scale = ATT_HEAD_DIM ** -0.5
    outs = []
    for blk in range(S // Q_BLOCK):
        q0 = blk * Q_BLOCK
        q1 = q0 + Q_BLOCK
        qb = q[:, q0:q1]
        kb = k[:, :q1]
        vb = v[:, :q1]
        z = jnp.einsum('bthd,bshd->bhts', qb, kb).astype(jnp.float32) * scale
        t_pos = q0 + jnp.arange(Q_BLOCK)[:, None]
        s_pos = jnp.arange(q1)[None, :]
        causal = s_pos < t_pos
        log_fail = jnp.where(causal, jax.nn.log_sigmoid(-z), 0.0)
        later = lax.cumsum(log_fail, axis=3, reverse=True) - log_fail
        w = jnp.where(causal, jnp.exp(jax.nn.log_sigmoid(z) + later), 0.0)
        outs.append(jnp.einsum('bhts,bshd->bthd', w.astype(vb.dtype), vb))
    return jnp.concatenate(outs, axis=1)


def causal_depthwise_conv(u, w, b):
    C = u.shape[-1]
    up = jnp.pad(u, ((0, 0), (CONV_WIDTH - 1, 0), (0, 0)))
    out = lax.conv_general_dilated(up, w[:, None, :].astype(u.dtype), window_strides=(1,), padding='VALID',
                                   dimension_numbers=('NWC', 'WIO', 'NWC'), feature_group_count=C)
    return out + b


def hybrid_layer(x, w_in, b_in, conv_w, conv_b, conv_ln_g, conv_ln_b,
                 w_att_proj, w_conv_proj, b_conv_proj, w_out, ln_g, ln_b):
    B, S, _ = x.shape
    u = jnp.einsum('bsd,de->bse', x, w_in) + b_in
    q, k, v, z_att, glu_a, glu_b, z_conv, g_att, g_conv = jnp.split(u, _split_points(), axis=-1)

    heads = lambda t: t.reshape(B, S, N_ATT_HEADS, ATT_HEAD_DIM)
    att = stick_breaking_attention(heads(q), heads(k), heads(v)).reshape(B, S, D_ATT)
    att_branch = jnp.einsum('bsc,cd->bsd', att * jax.nn.silu(z_att), w_att_proj)

    c = glu_a * jax.nn.sigmoid(glu_b)
    c = causal_depthwise_conv(c, conv_w, conv_b)
    c = jax.nn.silu(layer_norm(c, conv_ln_g, conv_ln_b))
    conv_branch = jnp.einsum('bsc,cd->bsd', c * jax.nn.silu(z_conv), w_conv_proj) + b_conv_proj

    merged = jax.nn.sigmoid(g_att) * att_branch + jax.nn.sigmoid(g_conv) * conv_branch
    y = jnp.einsum('bsd,de->bse', merged, w_out)
    return layer_norm(DEEPNORM_ALPHA * x + y, ln_g, ln_b)


def _fwd_setup_inputs(seed: int = 0) -> dict:
    key = jax.random.key(seed)
    ks = jax.random.split(key, 13)
    L, D = DEPTH, D_MODEL
    nrm = lambda k, shape, s: jax.random.normal(k, shape, jnp.float32) * s
    x = nrm(ks[0], (BATCH, SEQ, D), 1.0)
    col_scale = jnp.ones((D_IN,), jnp.float32).at[2 * D_ATT:3 * D_ATT].set(DEEPNORM_BETA)
    w_in = nrm(ks[1], (L, D, D_IN), D ** -0.5) * col_scale
    b_in = nrm(ks[2], (L, D_IN), GATE_INIT_STD)
    conv_w = nrm(ks[3], (L, CONV_WIDTH, D_CONV), CONV_WIDTH ** -0.5)
    conv_b = nrm(ks[4], (L, D_CONV), GATE_INIT_STD)
    conv_ln_g = 1.0 + nrm(ks[5], (L, D_CONV), GATE_INIT_STD)
    conv_ln_b = nrm(ks[6], (L, D_CONV), GATE_INIT_STD)
    w_att_proj = nrm(ks[7], (L, D_ATT, D), D_ATT ** -0.5 * DEEPNORM_BETA)
    w_conv_proj = nrm(ks[8], (L, D_CONV, D), D_CONV ** -0.5 * DEEPNORM_BETA)
    b_conv_proj = nrm(ks[9], (L, D), GATE_INIT_STD)
    w_out = nrm(ks[10], (L, D, D), D ** -0.5 * DEEPNORM_BETA)
    ln_g = 1.0 + nrm(ks[11], (L, D), GATE_INIT_STD)
    ln_b = nrm(ks[12], (L, D), GATE_INIT_STD)
    return {'x': x, 'w_in': w_in, 'b_in': b_in, 'conv_w': conv_w, 'conv_b': conv_b,
            'conv_ln_g': conv_ln_g, 'conv_ln_b': conv_ln_b, 'w_att_proj': w_att_proj,
            'w_conv_proj': w_conv_proj, 'b_conv_proj': b_conv_proj, 'w_out': w_out,
            'ln_g': ln_g, 'ln_b': ln_b}


def _fwd_reference(x, w_in, b_in, conv_w, conv_b, conv_ln_g, conv_ln_b,
              w_att_proj, w_conv_proj, b_conv_proj, w_out, ln_g, ln_b):
    for l in range(DEPTH):
        x = hybrid_layer(x, w_in[l], b_in[l], conv_w[l], conv_b[l], conv_ln_g[l], conv_ln_b[l],
                         w_att_proj[l], w_conv_proj[l], b_conv_proj[l], w_out[l], ln_g[l], ln_b[l])
    return x


import jax as _jax
import jax.numpy as _jnp

TWIN_FORMAT = 'train_step'
FWD_PARAMS = ['x', 'w_in', 'b_in', 'conv_w', 'conv_b', 'conv_ln_g', 'conv_ln_b', 'w_att_proj', 'w_conv_proj', 'b_conv_proj', 'w_out', 'ln_g', 'ln_b']
TWIN_WEIGHTS = ['w_in', 'b_in', 'conv_w', 'conv_b', 'conv_ln_g', 'conv_ln_b', 'w_att_proj', 'w_conv_proj', 'b_conv_proj', 'w_out', 'ln_g', 'ln_b']
TWIN_DIFF_INPUT = 'x'
TWIN_INPUTS = ['x', 'w_in', 'b_in', 'conv_w', 'conv_b', 'conv_ln_g', 'conv_ln_b', 'w_att_proj', 'w_conv_proj', 'b_conv_proj', 'w_out', 'ln_g', 'ln_b', 'loss_target', 'm_w_in', 'm_b_in', 'm_conv_w', 'm_conv_b', 'm_conv_ln_g', 'm_conv_ln_b', 'm_w_att_proj', 'm_w_conv_proj', 'm_b_conv_proj', 'm_w_out', 'm_ln_g', 'm_ln_b', 'v_w_in', 'v_b_in', 'v_conv_w', 'v_conv_b', 'v_conv_ln_g', 'v_conv_ln_b', 'v_w_att_proj', 'v_w_conv_proj', 'v_b_conv_proj', 'v_w_out', 'v_ln_g', 'v_ln_b']
TWIN_OUTPUTS = ['loss', 'grad_x', 'grad_w_in', 'grad_b_in', 'grad_conv_w', 'grad_conv_b', 'grad_conv_ln_g', 'grad_conv_ln_b', 'grad_w_att_proj', 'grad_w_conv_proj', 'grad_b_conv_proj', 'grad_w_out', 'grad_ln_g', 'grad_ln_b', 'delta_w_in', 'delta_b_in', 'delta_conv_w', 'delta_conv_b', 'delta_conv_ln_g', 'delta_conv_ln_b', 'delta_w_att_proj', 'delta_w_conv_proj', 'delta_b_conv_proj', 'delta_w_out', 'delta_ln_g', 'delta_ln_b', 'new_m_w_in', 'new_m_b_in', 'new_m_conv_w', 'new_m_conv_b', 'new_m_conv_ln_g', 'new_m_conv_ln_b', 'new_m_w_att_proj', 'new_m_w_conv_proj', 'new_m_b_conv_proj', 'new_m_w_out', 'new_m_ln_g', 'new_m_ln_b', 'new_v_w_in', 'new_v_b_in', 'new_v_conv_w', 'new_v_conv_b', 'new_v_conv_ln_g', 'new_v_conv_ln_b', 'new_v_w_att_proj', 'new_v_w_conv_proj', 'new_v_b_conv_proj', 'new_v_w_out', 'new_v_ln_g', 'new_v_ln_b']
TWIN_LEAF_KINDS = {'loss': 'loss', 'grad_x': 'grad_x', 'grad_w_in': 'grad_w', 'grad_b_in': 'grad_w', 'grad_conv_w': 'grad_w', 'grad_conv_b': 'grad_w', 'grad_conv_ln_g': 'grad_w', 'grad_conv_ln_b': 'grad_w', 'grad_w_att_proj': 'grad_w', 'grad_w_conv_proj': 'grad_w', 'grad_b_conv_proj': 'grad_w', 'grad_w_out': 'grad_w', 'grad_ln_g': 'grad_w', 'grad_ln_b': 'grad_w', 'delta_w_in': 'delta_w', 'delta_b_in': 'delta_w', 'delta_conv_w': 'delta_w', 'delta_conv_b': 'delta_w', 'delta_conv_ln_g': 'delta_w', 'delta_conv_ln_b': 'delta_w', 'delta_w_att_proj': 'delta_w', 'delta_w_conv_proj': 'delta_w', 'delta_b_conv_proj': 'delta_w', 'delta_w_out': 'delta_w', 'delta_ln_g': 'delta_w', 'delta_ln_b': 'delta_w', 'new_m_w_in': 'new_m', 'new_m_b_in': 'new_m', 'new_m_conv_w': 'new_m', 'new_m_conv_b': 'new_m', 'new_m_conv_ln_g': 'new_m', 'new_m_conv_ln_b': 'new_m', 'new_m_w_att_proj': 'new_m', 'new_m_w_conv_proj': 'new_m', 'new_m_b_conv_proj': 'new_m', 'new_m_w_out': 'new_m', 'new_m_ln_g': 'new_m', 'new_m_ln_b': 'new_m', 'new_v_w_in': 'new_v', 'new_v_b_in': 'new_v', 'new_v_conv_w': 'new_v', 'new_v_conv_b': 'new_v', 'new_v_conv_ln_g': 'new_v', 'new_v_conv_ln_b': 'new_v', 'new_v_w_att_proj': 'new_v', 'new_v_w_conv_proj': 'new_v', 'new_v_b_conv_proj': 'new_v', 'new_v_w_out': 'new_v', 'new_v_ln_g': 'new_v', 'new_v_ln_b': 'new_v'}


def _forward(args):
    return _fwd_reference(*[args[k] for k in FWD_PARAMS])


def _output_shape():
    out = _jax.eval_shape(lambda: _forward(_fwd_setup_inputs(0)))
    return out.shape, out.dtype

N_MICROBATCH = 1
ADAM_LR = 0.001
ADAM_B1 = 0.9
ADAM_B2 = 0.999
ADAM_EPS = 1e-08
ADAM_WD = 0.01
ADAM_STEP = 10
PER_EXAMPLE_BATCH_AXIS = {'x': 0, 'loss_target': 0}
SHARED_INPUTS = []
_WEIGHT_DTYPES = {'w_in': _jnp.float32, 'b_in': _jnp.float32, 'conv_w': _jnp.float32, 'conv_b': _jnp.float32, 'conv_ln_g': _jnp.float32, 'conv_ln_b': _jnp.float32, 'w_att_proj': _jnp.float32, 'w_conv_proj': _jnp.float32, 'b_conv_proj': _jnp.float32, 'w_out': _jnp.float32, 'ln_g': _jnp.float32, 'ln_b': _jnp.float32}
MOMENT_SCALE = {'w_in': 3.051507e-03, 'b_in': 4.192600e-03, 'conv_w': 5.274233e-03, 'conv_b': 1.091145e-02, 'conv_ln_g': 6.433984e-03, 'conv_ln_b': 6.080252e-03, 'w_att_proj': 3.888745e-03, 'w_conv_proj': 8.711887e-03, 'b_conv_proj': 5.758470e-02, 'w_out': 9.805258e-03, 'ln_g': 1.603671e+01, 'ln_b': 6.289261e-01}


def _to_microbatches(a, axis):
    t = _jnp.moveaxis(a, axis, 0)
    t = t.reshape((N_MICROBATCH, t.shape[0] // N_MICROBATCH) + t.shape[1:])
    return _jnp.moveaxis(t, 1, axis + 1)


def setup_inputs(seed: int = 0) -> dict:
    inp = _fwd_setup_inputs(seed)
    key = _jax.random.fold_in(_jax.random.key(seed), 7919)
    shape, _ = _output_shape()
    out = dict(inp)
    out["loss_target"] = _jax.random.normal(_jax.random.fold_in(key, 0), shape, _jnp.float32)
    for i, name in enumerate(TWIN_WEIGHTS):
        w = inp[name].astype(_jnp.float32)
        if MOMENT_SCALE is None:
            s = _jnp.sqrt(_jnp.mean(_jnp.square(w)) + 1e-30)
        else:
            s = MOMENT_SCALE[name]
        km, kv = _jax.random.split(_jax.random.fold_in(key, i + 1))
        out[name] = w
        out["m_" + name] = s * _jax.random.normal(km, w.shape, _jnp.float32)
        out["v_" + name] = (s * s) * _jax.random.uniform(kv, w.shape, _jnp.float32, 0.5, 1.5)
    if N_MICROBATCH > 1:
        for name, axis in PER_EXAMPLE_BATCH_AXIS.items():
            out[name] = _to_microbatches(out[name], axis)
    return {'x': out['x'], 'w_in': out['w_in'], 'b_in': out['b_in'], 'conv_w': out['conv_w'], 'conv_b': out['conv_b'], 'conv_ln_g': out['conv_ln_g'], 'conv_ln_b': out['conv_ln_b'], 'w_att_proj': out['w_att_proj'], 'w_conv_proj': out['w_conv_proj'], 'b_conv_proj': out['b_conv_proj'], 'w_out': out['w_out'], 'ln_g': out['ln_g'], 'ln_b': out['ln_b'], 'loss_target': out['loss_target'], 'm_w_in': out['m_w_in'], 'm_b_in': out['m_b_in'], 'm_conv_w': out['m_conv_w'], 'm_conv_b': out['m_conv_b'], 'm_conv_ln_g': out['m_conv_ln_g'], 'm_conv_ln_b': out['m_conv_ln_b'], 'm_w_att_proj': out['m_w_att_proj'], 'm_w_conv_proj': out['m_w_conv_proj'], 'm_b_conv_proj': out['m_b_conv_proj'], 'm_w_out': out['m_w_out'], 'm_ln_g': out['m_ln_g'], 'm_ln_b': out['m_ln_b'], 'v_w_in': out['v_w_in'], 'v_b_in': out['v_b_in'], 'v_conv_w': out['v_conv_w'], 'v_conv_b': out['v_conv_b'], 'v_conv_ln_g': out['v_conv_ln_g'], 'v_conv_ln_b': out['v_conv_ln_b'], 'v_w_att_proj': out['v_w_att_proj'], 'v_w_conv_proj': out['v_w_conv_proj'], 'v_b_conv_proj': out['v_b_conv_proj'], 'v_w_out': out['v_w_out'], 'v_ln_g': out['v_ln_g'], 'v_ln_b': out['v_ln_b']}


def _loss(weights, diff, rest, loss_target):
    with _jax.named_scope("forward"):
        args = {**rest, TWIN_DIFF_INPUT: diff, **{k: w.astype(_WEIGHT_DTYPES[k]) for k, w in weights.items()}}
        y = _forward(args)
    with _jax.named_scope("loss_head"):
        err = _jnp.square(y.astype(_jnp.float32) - loss_target)
        return 0.5 * _jnp.sum(_jnp.mean(err, axis=-1)) if err.ndim else 0.5 * err


def _adamw(w, g, m, v):
    m = ADAM_B1 * m + (1.0 - ADAM_B1) * g
    v = ADAM_B2 * v + (1.0 - ADAM_B2) * _jnp.square(g)
    m_hat = m / (1.0 - ADAM_B1 ** ADAM_STEP)
    v_hat = v / (1.0 - ADAM_B2 ** ADAM_STEP)
    delta = -ADAM_LR * (m_hat / (_jnp.sqrt(v_hat) + ADAM_EPS) + ADAM_WD * w)
    return delta, m, v


def reference(x, w_in, b_in, conv_w, conv_b, conv_ln_g, conv_ln_b, w_att_proj, w_conv_proj, b_conv_proj, w_out, ln_g, ln_b, loss_target, m_w_in, m_b_in, m_conv_w, m_conv_b, m_conv_ln_g, m_conv_ln_b, m_w_att_proj, m_w_conv_proj, m_b_conv_proj, m_w_out, m_ln_g, m_ln_b, v_w_in, v_b_in, v_conv_w, v_conv_b, v_conv_ln_g, v_conv_ln_b, v_w_att_proj, v_w_conv_proj, v_b_conv_proj, v_w_out, v_ln_g, v_ln_b):
    given = dict(x=x, w_in=w_in, b_in=b_in, conv_w=conv_w, conv_b=conv_b, conv_ln_g=conv_ln_g, conv_ln_b=conv_ln_b, w_att_proj=w_att_proj, w_conv_proj=w_conv_proj, b_conv_proj=b_conv_proj, w_out=w_out, ln_g=ln_g, ln_b=ln_b, loss_target=loss_target, m_w_in=m_w_in, m_b_in=m_b_in, m_conv_w=m_conv_w, m_conv_b=m_conv_b, m_conv_ln_g=m_conv_ln_g, m_conv_ln_b=m_conv_ln_b, m_w_att_proj=m_w_att_proj, m_w_conv_proj=m_w_conv_proj, m_b_conv_proj=m_b_conv_proj, m_w_out=m_w_out, m_ln_g=m_ln_g, m_ln_b=m_ln_b, v_w_in=v_w_in, v_b_in=v_b_in, v_conv_w=v_conv_w, v_conv_b=v_conv_b, v_conv_ln_g=v_conv_ln_g, v_conv_ln_b=v_conv_ln_b, v_w_att_proj=v_w_att_proj, v_w_conv_proj=v_w_conv_proj, v_b_conv_proj=v_b_conv_proj, v_w_out=v_w_out, v_ln_g=v_ln_g, v_ln_b=v_ln_b)
    weights = {n: given[n] for n in TWIN_WEIGHTS}
    shared = {n: given[n] for n in SHARED_INPUTS}
    per_example = {n: given[n] for n in ['x']}
    grad_fn = _jax.value_and_grad(_loss, argnums=(0, 1))

    def one_microbatch(ex, loss_target):
        ex = dict(ex)
        diff = ex.pop(TWIN_DIFF_INPUT)
        return grad_fn(weights, diff, {**shared, **ex}, loss_target)

    if N_MICROBATCH == 1:
        loss, (grad_w, grad_x) = one_microbatch(per_example, given["loss_target"])
    else:
        def body(carry, xs):
            loss_sum, grad_sum = carry
            l_k, (gw_k, gx_k) = one_microbatch(xs[0], xs[1])
            with _jax.named_scope("update"):
                return (loss_sum + l_k, _jax.tree.map(_jnp.add, grad_sum, gw_k)), gx_k

        init = (_jnp.zeros((), _jnp.float32), _jax.tree.map(_jnp.zeros_like, weights))
        (loss, grad_w), grad_x = _jax.lax.scan(body, init, (per_example, given["loss_target"]))
    with _jax.named_scope("update"):
        delta_w, new_m, new_v = {}, {}, {}
        for n in TWIN_WEIGHTS:
            delta_w[n], new_m[n], new_v[n] = _adamw(weights[n], grad_w[n], given["m_" + n], given["v_" + n])
    return (loss, grad_x, *[grad_w[n] for n in TWIN_WEIGHTS], *[delta_w[n] for n in TWIN_WEIGHTS],
            *[new_m[n] for n in TWIN_WEIGHTS], *[new_v[n] for n in TWIN_WEIGHTS])
```

```python
import functools

import jax
import jax.numpy as jnp
from jax import lax
from jax.experimental import pallas as pl
from jax.experimental.pallas import tpu as pltpu

F32, BF16 = jnp.float32, jnp.bfloat16

D_MODEL = 1024
D_ATT = 512
D_CONV = 512
HEAD_DIM = 64
CONV_WIDTH = 31
CONV_PAD = 32
DEPTH = 4
N_DEV = 8
SEG = 512
D_IN = 4 * D_ATT + 3 * D_CONV + 2 * D_MODEL
N_SEG = D_IN // SEG
N_QKV_SEG = 3
W_IN_SHARD = D_IN // N_DEV
LN_EPS = 1e-5
DEEPNORM_ALPHA = (2 * DEPTH) ** 0.25
ATT_SCALE = HEAD_DIM ** -0.5
ATT_BLOCK = 256
ADAM_LR, ADAM_B1, ADAM_B2, ADAM_EPS, ADAM_WD, ADAM_STEP = 0.001, 0.9, 0.999, 1e-08, 0.01, 10
VMEM_LIMIT = 56 * 1024 * 1024
MESH = pl.DeviceIdType.MESH
ANY = pl.BlockSpec(memory_space=pl.ANY)


def _params(*semantics):
    return pltpu.CompilerParams(dimension_semantics=semantics, vmem_limit_bytes=VMEM_LIMIT)


def _sig(x):
    return 1.0 / (1.0 + jnp.exp(-x))


def _dot(a, b):
    return jnp.dot(a, b, preferred_element_type=F32)


def _dot_nt(a, b):
    return lax.dot_general(a, b, (((1,), (1,)), ((), ())), preferred_element_type=F32)


def _dot_tn(a, b):
    return lax.dot_general(a, b, (((0,), (0,)), ((), ())), preferred_element_type=F32)


def _my_index():
    return 4 * lax.axis_index("x") + 2 * lax.axis_index("y") + lax.axis_index("c")


def _all_gather(shards, name):
    n = len(shards)

    def body(*refs):
        ins, outs = refs[:n], refs[n:2 * n]
        send_sems, recv_sems, local_sems = refs[2 * n:]
        x, y, c = lax.axis_index("x"), lax.axis_index("y"), lax.axis_index("c")
        me, sibling = (x, y, c), (x, y, 1 - c)
        chips = [(1 - x, y), (x, 1 - y), (1 - x, 1 - y)]

        def slot(a, dev):
            return outs[a].at[4 * dev[0] + 2 * dev[1] + dev[2]]

        def copy(a, k, block, to, src=None):
            return pltpu.make_async_remote_copy(
                src_ref=slot(a, block) if src is None else src, dst_ref=slot(a, block),
                send_sem=send_sems.at[7 * a + k], recv_sem=recv_sems.at[7 * a + k], device_id=to, device_id_type=MESH)

        mine = [pltpu.make_async_copy(ins[a], slot(a, me), local_sems.at[a]) for a in range(n)]
        for cp in mine:
            cp.start()
        first = []
        for a in range(n):
            first.append(copy(a, 0, me, sibling, src=ins[a]))
            first += [copy(a, 1 + j, me, (*chip, c), src=ins[a]) for j, chip in enumerate(chips)]
        for cp in first:
            cp.start()
        passed = []
        for j, chip in enumerate(chips):
            for a in range(n):
                copy(a, 1 + j, (*chip, c), me).wait_recv()
                fwd = copy(a, 4 + j, (*chip, c), sibling)
                fwd.start()
                passed.append(fwd)
        for a in range(n):
            copy(a, 0, sibling, me).wait_recv()
            for j, chip in enumerate(chips):
                copy(a, 4 + j, (*chip, 1 - c), me).wait_recv()
        for cp in first + passed:
            cp.wait_send()
        for cp in mine:
            cp.wait()

    return pl.pallas_call(
        body, name=name,
        out_shape=[jax.ShapeDtypeStruct((N_DEV,) + s.shape, s.dtype) for s in shards],
        in_specs=[ANY] * n, out_specs=[ANY] * n,
        scratch_shapes=[pltpu.SemaphoreType.DMA((7 * n,)), pltpu.SemaphoreType.DMA((7 * n,)),
                        pltpu.SemaphoreType.DMA((n,))],
    )(*shards)


def _exchange(parts, name):
    n = len(parts)

    def body(*refs):
        ins, outs = refs[:n], refs[n:2 * n]
        send_sems, recv_sems, local_sems = refs[2 * n:]
        x, y, c = lax.axis_index("x"), lax.axis_index("y"), lax.axis_index("c")
        me_idx = 4 * x + 2 * y + c

        def peer(d):
            px = 1 - x if d & 4 else x
            py = 1 - y if d & 2 else y
            pc = 1 - c if d & 1 else c
            return (px, py, pc), 4 * px + 2 * py + pc

        mine = [pltpu.make_async_copy(ins[a].at[me_idx], outs[a].at[me_idx], local_sems.at[a]) for a in range(n)]
        for cp in mine:
            cp.start()
        sends = []
        for d in (1, 2, 4, 3, 5, 6, 7):
            dev, idx = peer(d)
            for a in range(n):
                cp = pltpu.make_async_remote_copy(
                    src_ref=ins[a].at[idx], dst_ref=outs[a].at[me_idx],
                    send_sem=send_sems.at[7 * a + d - 1], recv_sem=recv_sems.at[7 * a + d - 1],
                    device_id=dev, device_id_type=MESH)
                cp.start()
                sends.append(cp)
        for d in range(1, 8):
            dev, idx = peer(d)
            for a in range(n):
                pltpu.make_async_remote_copy(
                    src_ref=ins[a].at[idx], dst_ref=outs[a].at[idx],
                    send_sem=send_sems.at[7 * a + d - 1], recv_sem=recv_sems.at[7 * a + d - 1],
                    device_id=dev, device_id_type=MESH).wait_recv()
        for cp in sends:
            cp.wait_send()
        for cp in mine:
            cp.wait()

    return pl.pallas_call(
        body, name=name,
        out_shape=[jax.ShapeDtypeStruct(p.shape, p.dtype) for p in parts],
        in_specs=[ANY] * n, out_specs=[ANY] * n,
        scratch_shapes=[pltpu.SemaphoreType.DMA((7 * n,)), pltpu.SemaphoreType.DMA((7 * n,)),
                        pltpu.SemaphoreType.DMA((n,))],
    )(*parts)


def _in_proj(xb, w_in, b_in, layer):
    T = xb.shape[0]
    tm, tn = min(T, 1024), SEG

    def body(x_ref, w_ref, b_ref, u_ref):
        u_ref[...] = _dot(x_ref[...], w_ref[...]) + b_ref[...]

    return pl.pallas_call(
        body, name="in_proj", grid=(D_IN // tn, T // tm),
        in_specs=[pl.BlockSpec((tm, D_MODEL), lambda n, m: (m, 0)),
                  pl.BlockSpec((None, D_MODEL, tn), lambda n, m: (layer, 0, n)),
                  pl.BlockSpec((None, 1, tn), lambda n, m: (layer, 0, n))],
        out_specs=pl.BlockSpec((tm, tn), lambda n, m: (m, n)),
        out_shape=jax.ShapeDtypeStruct((T, D_IN), F32),
        compiler_params=_params("parallel", "parallel"),
    )(xb, w_in, b_in)


def _du_specs(tm, row_of, seg_of):
    return [pl.BlockSpec((None, tm, SEG), lambda *g: (jnp.minimum(seg_of(*g), N_QKV_SEG - 1), row_of(*g), 0)),
            pl.BlockSpec((tm, SEG), lambda *g: (row_of(*g), jnp.maximum(seg_of(*g) - N_QKV_SEG, 0)))]


def _in_proj_dx(dqkv, du_rest, w_in, dres, layer):
    T = du_rest.shape[0]
    tm = min(T, 512)

    def body(a1_ref, a2_ref, w_ref, r_ref, o_ref):
        s = pl.program_id(1)

        @pl.when(s == 0)
        def _():
            o_ref[...] = r_ref[...]

        @pl.when(s < N_QKV_SEG)
        def _():
            o_ref[...] += _dot_nt(a1_ref[...].astype(BF16), w_ref[...])

        @pl.when(s >= N_QKV_SEG)
        def _():
            o_ref[...] += _dot_nt(a2_ref[...].astype(BF16), w_ref[...])

    return pl.pallas_call(
        body, name="in_proj_dx", grid=(T // tm, N_SEG),
        in_specs=_du_specs(tm, lambda m, s: m, lambda m, s: s) + [
            pl.BlockSpec((None, D_MODEL, SEG), lambda m, s: (layer, 0, s)),
            pl.BlockSpec((tm, D_MODEL), lambda m, s: (m, 0))],
        out_specs=pl.BlockSpec((tm, D_MODEL), lambda m, s: (m, 0)),
        out_shape=jax.ShapeDtypeStruct((T, D_MODEL), F32),
        compiler_params=_params("parallel", "arbitrary"),
    )(dqkv, du_rest, w_in, dres)


def _in_proj_dw(xb, dqkv, du_rest):
    T = xb.shape[0]
    tk = min(T, 512)

    def body(x_ref, a1_ref, a2_ref, dw_ref, db_ref):
        s, t = pl.program_id(0), pl.program_id(1)

        @pl.when(t == 0)
        def _():
            dw_ref[...] = jnp.zeros_like(dw_ref)
            db_ref[...] = jnp.zeros_like(db_ref)

        def acc(a):
            dw_ref[...] += _dot_tn(x_ref[...], a.astype(BF16))
            db_ref[...] += jnp.sum(a, axis=0, keepdims=True)

        @pl.when(s < N_QKV_SEG)
        def _():
            acc(a1_ref[...])

        @pl.when(s >= N_QKV_SEG)
        def _():
            acc(a2_ref[...])

    return pl.pallas_call(
        body, name="in_proj_dw", grid=(N_SEG, T // tk),
        in_specs=[pl.BlockSpec((tk, D_MODEL), lambda s, t: (t, 0))] + _du_specs(tk, lambda s, t: t, lambda s, t: s),
        out_specs=[pl.BlockSpec((D_MODEL, SEG), lambda s, t: (0, s)),
                   pl.BlockSpec((None, 1, SEG), lambda s, t: (s, 0, 0))],
        out_shape=[jax.ShapeDtypeStruct((D_MODEL, D_IN), F32), jax.ShapeDtypeStruct((N_SEG, 1, SEG), F32)],
        compiler_params=_params("parallel", "arbitrary"),
    )(xb, dqkv, du_rest)


def _split_dot(val, tri):
    hi = val.astype(BF16)
    lo = (val - hi.astype(F32)).astype(BF16)
    return _dot(hi, tri) + _dot(lo, tri)


def _log_sigmoids(z):
    sp = jnp.log(1.0 + jnp.exp(-jnp.abs(z)))
    return jnp.minimum(z, 0.0) - sp, jnp.minimum(-z, 0.0) - sp


def _attn_masks():
    row = lax.broadcasted_iota(jnp.int32, (ATT_BLOCK, ATT_BLOCK), 0)
    col = lax.broadcasted_iota(jnp.int32, (ATT_BLOCK, ATT_BLOCK), 1)
    return col < row, (row > col).astype(BF16), (row >= col).astype(BF16)


EXP_UNDERFLOW = -104.0


def _sweep_keys(i, tile, rsum):
    tile(i, True)

    def live(carry):
        j, top = carry
        return jnp.logical_and(j >= 0, top > EXP_UNDERFLOW)

    def step(carry):
        j, _ = carry
        tile(j, False)
        return j - 1, jnp.max(rsum[...])

    lax.while_loop(live, step, (i - 1, jnp.max(rsum[...])))


def _attn_fwd(u):
    T = u.shape[0]
    B = ATT_BLOCK
    nq = T // B

    def body(q_ref, k_ref, v_ref, o_ref, kb, vb, acc, rsum):
        i = pl.program_id(1)

        @pl.when(i == 0)
        def _():
            for h in range(2):
                kb[h] = k_ref[:, HEAD_DIM * h:HEAD_DIM * (h + 1)].astype(BF16)
                vb[h] = v_ref[:, HEAD_DIM * h:HEAD_DIM * (h + 1)].astype(BF16)

        causal, tri, _ = _attn_masks()
        for h in range(2):
            hs = slice(HEAD_DIM * h, HEAD_DIM * (h + 1))
            q = (q_ref[:, hs] * ATT_SCALE).astype(BF16)
            acc[...] = jnp.zeros_like(acc)
            rsum[...] = jnp.zeros_like(rsum)

            def tile(j, masked):
                rows = pl.ds(pl.multiple_of(j * B, B), B)
                z = _dot_nt(q, kb[h, rows, :])
                a, f = _log_sigmoids(z)
                if masked:
                    f = jnp.where(causal, f, 0.0)
                later = _split_dot(f, tri)
                w = jnp.exp(a + later + rsum[...])
                if masked:
                    w = jnp.where(causal, w, 0.0)
                acc[...] += _split_dot(w, vb[h, rows, :])
                rsum[...] += later[:, 0:1] + f[:, 0:1]

            _sweep_keys(i, tile, rsum)
            o_ref[:, hs] = acc[...]

    return pl.pallas_call(
        body, name="attn_fwd", grid=(D_ATT // 128, nq),
        in_specs=[pl.BlockSpec((B, 128), lambda p, i: (i, p)),
                  pl.BlockSpec((T, 128), lambda p, i: (0, 4 + p)),
                  pl.BlockSpec((T, 128), lambda p, i: (0, 8 + p))],
        out_specs=pl.BlockSpec((B, 128), lambda p, i: (i, p)),
        out_shape=jax.ShapeDtypeStruct((T, D_ATT), F32),
        scratch_shapes=[pltpu.VMEM((2, T, HEAD_DIM), BF16), pltpu.VMEM((2, T, HEAD_DIM), BF16),
                        pltpu.VMEM((B, HEAD_DIM), F32), pltpu.VMEM((B, 1), F32)],
        compiler_params=_params("parallel", "arbitrary"),
    )(u, u, u)


def _attn_bwd(u, att, datt):
    T = u.shape[0]
    B = ATT_BLOCK
    nq = T // B

    def body(q_ref, k_ref, v_ref, o_ref, do_ref, dqkv_ref, kb, vb, dkacc, dvacc, dqacc, rsum, gsum):
        i = pl.program_id(1)

        @pl.when(i == 0)
        def _():
            for h in range(2):
                kb[h] = k_ref[:, HEAD_DIM * h:HEAD_DIM * (h + 1)].astype(BF16)
                vb[h] = v_ref[:, HEAD_DIM * h:HEAD_DIM * (h + 1)].astype(BF16)
            dkacc[...] = jnp.zeros_like(dkacc)
            dvacc[...] = jnp.zeros_like(dvacc)

        causal, tri, tri_incl = _attn_masks()
        qrows = pl.ds(pl.multiple_of(i * B, B), B)
        for h in range(2):
            hs = slice(HEAD_DIM * h, HEAD_DIM * (h + 1))
            q = (q_ref[:, hs] * ATT_SCALE).astype(BF16)
            do = do_ref[:, hs]
            dob = do.astype(BF16)
            total = jnp.sum(dob.astype(F32) * o_ref[:, hs], axis=1, keepdims=True)
            dqacc[...] = jnp.zeros_like(dqacc)
            rsum[...] = jnp.zeros_like(rsum)
            gsum[...] = jnp.zeros_like(gsum)

            def tile(j, masked):
                rows = pl.ds(pl.multiple_of(j * B, B), B)
                kj, vj = kb[h, rows, :], vb[h, rows, :]
                z = _dot_nt(q, kj)
                a, f = _log_sigmoids(z)
                if masked:
                    f = jnp.where(causal, f, 0.0)
                later = _split_dot(f, tri)
                w = jnp.exp(a + later + rsum[...])
                if masked:
                    w = jnp.where(causal, w, 0.0)
                g = _dot_nt(dob, vj) * w
                suffix = _split_dot(g, tri_incl)
                before = total - (suffix + gsum[...])
                beta = jnp.exp(a)
                dz = g * (1.0 - beta) - before * beta
                if masked:
                    dz = jnp.where(causal, dz, 0.0)
                dzb = dz.astype(BF16)
                dqacc[...] += _dot(dzb, kj)
                dkacc[h, rows, :] += _dot_tn(dzb, q)
                dvacc[h, rows, :] += _dot_tn(w.astype(BF16), dob)
                rsum[...] += later[:, 0:1] + f[:, 0:1]
                gsum[...] += suffix[:, 0:1]

            _sweep_keys(i, tile, rsum)
            dqkv_ref[0, qrows, hs] = dqacc[...] * ATT_SCALE

        @pl.when(i == nq - 1)
        def _():
            for h in range(2):
                hs = slice(HEAD_DIM * h, HEAD_DIM * (h + 1))
                dqkv_ref[1, :, hs] = dkacc[h]
                dqkv_ref[2, :, hs] = dvacc[h]

    return pl.pallas_call(
        body, name="attn_bwd", grid=(D_ATT // 128, nq),
        in_specs=[pl.BlockSpec((B, 128), lambda p, i: (i, p)),
                  pl.BlockSpec((T, 128), lambda p, i: (0, 4 + p)),
                  pl.BlockSpec((T, 128), lambda p, i: (0, 8 + p)),
                  pl.BlockSpec((B, 128), lambda p, i: (i, p)),
                  pl.BlockSpec((B, 128), lambda p, i: (i, p))],
        out_specs=pl.BlockSpec((3, T, 128), lambda p, i: (0, 0, p)),
        out_shape=jax.ShapeDtypeStruct((3, T, D_ATT), F32),
        scratch_shapes=[pltpu.VMEM((2, T, HEAD_DIM), BF16), pltpu.VMEM((2, T, HEAD_DIM), BF16),
                        pltpu.VMEM((2, T, HEAD_DIM), F32), pltpu.VMEM((2, T, HEAD_DIM), F32),
                        pltpu.VMEM((B, HEAD_DIM), F32), pltpu.VMEM((B, 1), F32), pltpu.VMEM((B, 1), F32)],
        compiler_params=_params("parallel", "arbitrary"),
    )(u, u, u, att, datt)


MIX_TILE = 128


def _u_specs(tm, tile_of):
    return [pl.BlockSpec((tm, SEG), functools.partial(lambda k, *g: (tile_of(*g), k), k)) for k in range(3, N_SEG)]


def _conv(ext, cw_ref, tm):
    acc = cw_ref[0:1, :] * ext[2:2 + tm, :]
    for k in range(1, CONV_WIDTH):
        acc = acc + cw_ref[k:k + 1, :] * ext[2 + k:2 + k + tm, :]
    return acc


def _norm_stats(v):
    mu = jnp.mean(v, axis=-1, keepdims=True)
    vc = v - mu
    rstd = lax.rsqrt(jnp.mean(vc * vc, axis=-1, keepdims=True) + LN_EPS)
    return vc * rstd, rstd


def _norm_bwd(dy_scaled, xhat, rstd):
    return rstd * (dy_scaled - jnp.mean(dy_scaled, axis=-1, keepdims=True)
                   - xhat * jnp.mean(dy_scaled * xhat, axis=-1, keepdims=True))


def _mix_fwd(x, u, att, wap, wcp, wout, cw, cvec, dvec, layer):
    T = x.shape[0]
    tm = MIX_TILE

    def body(x_ref, za_ref, ga_ref, gb_ref, zc_ref, ga0_ref, ga1_ref, gc0_ref, gc1_ref, att_ref,
             wap_ref, wcp_ref, wout_ref, cw_ref, cvec_ref, dvec_ref, xn_ref, xnb_ref, h_ref, ext):
        i = pl.program_id(0)

        @pl.when(i == 0)
        def _():
            ext[0:CONV_PAD, :] = jnp.zeros((CONV_PAD, D_CONV), F32)

        za = za_ref[...]
        attg = att_ref[...] * (za * _sig(za))
        ab = _dot(attg.astype(BF16), wap_ref[...])

        ext[CONV_PAD:CONV_PAD + tm, :] = ga_ref[...] * _sig(gb_ref[...])
        c1 = _conv(ext, cw_ref, tm) + cvec_ref[0:1, :]
        ext[0:CONV_PAD, :] = ext[tm:tm + CONV_PAD, :]
        xh, _ = _norm_stats(c1)
        c2 = xh * cvec_ref[1:2, :] + cvec_ref[2:3, :]
        zc = zc_ref[...]
        cz = (c2 * _sig(c2)) * (zc * _sig(zc))
        cb = _dot(cz.astype(BF16), wcp_ref[...]) + dvec_ref[0:1, :]

        g_att = jnp.concatenate([ga0_ref[...], ga1_ref[...]], axis=1)
        g_conv = jnp.concatenate([gc0_ref[...], gc1_ref[...]], axis=1)
        merged = _sig(g_att) * ab + _sig(g_conv) * cb
        h = DEEPNORM_ALPHA * x_ref[...] + _dot(merged.astype(BF16), wout_ref[...])
        h_ref[...] = h
        hh, _ = _norm_stats(h)
        xn = hh * dvec_ref[1:2, :] + dvec_ref[2:3, :]
        xn_ref[...] = xn
        xnb_ref[...] = xn.astype(BF16)

    row = pl.BlockSpec((tm, D_MODEL), lambda i: (i, 0))
    return pl.pallas_call(
        body, name="mix_fwd", grid=(T // tm,),
        in_specs=[row] + _u_specs(tm, lambda i: i) + [
            pl.BlockSpec((tm, D_ATT), lambda i: (i, 0)),
            pl.BlockSpec((None, D_ATT, D_MODEL), lambda i: (layer, 0, 0)),
            pl.BlockSpec((None, D_CONV, D_MODEL), lambda i: (layer, 0, 0)),
            pl.BlockSpec((None, D_MODEL, D_MODEL), lambda i: (layer, 0, 0)),
            pl.BlockSpec((None, CONV_PAD, D_CONV), lambda i: (layer, 0, 0)),
            pl.BlockSpec((None, 8, D_CONV), lambda i: (layer, 0, 0)),
            pl.BlockSpec((None, 8, D_MODEL), lambda i: (layer, 0, 0))],
        out_specs=[row, row, row],
        out_shape=[jax.ShapeDtypeStruct((T, D_MODEL), F32), jax.ShapeDtypeStruct((T, D_MODEL), BF16),
                   jax.ShapeDtypeStruct((T, D_MODEL), F32)],
        scratch_shapes=[pltpu.VMEM((tm + CONV_PAD, D_CONV), F32)],
        compiler_params=_params("arbitrary"),
    )(x, *([u] * 8), att, wap, wcp, wout, cw, cvec, dvec)


def _mix_bwd(dxo, h, u, att, wap, wcp, wout, cw, cvec, dvec, layer):
    T = dxo.shape[0]
    tm = MIX_TILE
    nt = T // tm
    halo_per_tile = tm // CONV_PAD

    def body(dxo_ref, h_ref, za_ref, ga_ref, gb_ref, zc_ref, ga0_ref, ga1_ref, gc0_ref, gc1_ref, hga_ref, hgb_ref,
             att_ref, wap_ref, wcp_ref, wout_ref, cw_ref, cvec_ref, dvec_ref,
             datt_ref, dur_ref, dres_ref, dwap_ref, dwcp_ref, dwout_ref, dcw_ref, dcvec_ref, ddvec_ref, ext, dext):
        r = pl.program_id(0)
        i = nt - 1 - r

        @pl.when(r == 0)
        def _():
            for ref in (dwap_ref, dwcp_ref, dwout_ref, dcw_ref, dcvec_ref, ddvec_ref):
                ref[...] = jnp.zeros_like(ref)
            dext[tm:tm + CONV_PAD, :] = jnp.zeros((CONV_PAD, D_CONV), F32)

        za = za_ref[...]
        s_za = _sig(za)
        silu_za = za * s_za
        att_t = att_ref[...]
        attg = (att_t * silu_za).astype(BF16)
        ab = _dot(attg, wap_ref[...])

        ga, gb = ga_ref[...], gb_ref[...]
        s_gb = _sig(gb)
        halo = hga_ref[...] * _sig(hgb_ref[...])
        ext[0:CONV_PAD, :] = jnp.where(i > 0, halo, 0.0)
        ext[CONV_PAD:CONV_PAD + tm, :] = ga * s_gb
        c1 = _conv(ext, cw_ref, tm) + cvec_ref[0:1, :]
        xh, rstd_c = _norm_stats(c1)
        c2 = xh * cvec_ref[1:2, :] + cvec_ref[2:3, :]
        s_c2 = _sig(c2)
        c3 = c2 * s_c2
        zc = zc_ref[...]
        s_zc = _sig(zc)
        cz = (c3 * (zc * s_zc)).astype(BF16)
        cb = _dot(cz, wcp_ref[...]) + dvec_ref[0:1, :]

        s_ga = _sig(jnp.concatenate([ga0_ref[...], ga1_ref[...]], axis=1))
        s_gc = _sig(jnp.concatenate([gc0_ref[...], gc1_ref[...]], axis=1))
        merged = (s_ga * ab + s_gc * cb).astype(BF16)

        dxo_t = dxo_ref[...]
        hh, rstd_h = _norm_stats(h_ref[...])
        ddvec_ref[1:2, :] += jnp.sum(dxo_t * hh, axis=0, keepdims=True)
        ddvec_ref[2:3, :] += jnp.sum(dxo_t, axis=0, keepdims=True)
        dh = _norm_bwd(dxo_t * dvec_ref[1:2, :], hh, rstd_h)
        dres_ref[...] = DEEPNORM_ALPHA * dh
        dy = dh.astype(BF16)

        dwout_ref[...] += _dot_tn(merged, dy)
        dm = _dot_nt(dy, wout_ref[...])
        dab = (dm * s_ga).astype(BF16)
        dcb = dm * s_gc
        dur_ref[:, 4 * SEG:6 * SEG] = dm * ab * (s_ga * (1.0 - s_ga))
        dur_ref[:, 6 * SEG:8 * SEG] = dm * cb * (s_gc * (1.0 - s_gc))

        dwap_ref[...] += _dot_tn(attg, dab)
        dattg = _dot_nt(dab, wap_ref[...])
        datt_ref[...] = dattg * silu_za
        dur_ref[:, 0:SEG] = dattg * att_t * (s_za * (1.0 + za * (1.0 - s_za)))

        ddvec_ref[0:1, :] += jnp.sum(dcb, axis=0, keepdims=True)
        dcbb = dcb.astype(BF16)
        dwcp_ref[...] += _dot_tn(cz, dcbb)
        dcz = _dot_nt(dcbb, wcp_ref[...])
        dur_ref[:, 3 * SEG:4 * SEG] = dcz * c3 * (s_zc * (1.0 + zc * (1.0 - s_zc)))
        dc2 = dcz * (zc * s_zc) * (s_c2 * (1.0 + c2 * (1.0 - s_c2)))
        dcvec_ref[1:2, :] += jnp.sum(dc2 * xh, axis=0, keepdims=True)
        dcvec_ref[2:3, :] += jnp.sum(dc2, axis=0, keepdims=True)
        dc1 = _norm_bwd(dc2 * cvec_ref[1:2, :], xh, rstd_c)
        dcvec_ref[0:1, :] += jnp.sum(dc1, axis=0, keepdims=True)
        for k in range(CONV_WIDTH):
            dcw_ref[k:k + 1, :] += jnp.sum(dc1 * ext[2 + k:2 + k + tm, :], axis=0, keepdims=True)
        dext[0:tm, :] = dc1
        dc0 = cw_ref[0:1, :] * dext[CONV_WIDTH - 1:CONV_WIDTH - 1 + tm, :]
        for k in range(1, CONV_WIDTH):
            dc0 = dc0 + cw_ref[k:k + 1, :] * dext[CONV_WIDTH - 1 - k:CONV_WIDTH - 1 - k + tm, :]
        dext[tm:tm + CONV_PAD, :] = dext[0:CONV_PAD, :]
        dur_ref[:, SEG:2 * SEG] = dc0 * s_gb
        dur_ref[:, 2 * SEG:3 * SEG] = dc0 * ga * (s_gb * (1.0 - s_gb))

    rev = lambda r: nt - 1 - r
    row = pl.BlockSpec((tm, D_MODEL), lambda r: (rev(r), 0))
    halo_spec = [pl.BlockSpec((CONV_PAD, SEG),
                              functools.partial(lambda k, r: (jnp.maximum(rev(r) * halo_per_tile - 1, 0), k), k))
                 for k in (4, 5)]
    const = lambda shape: pl.BlockSpec(shape, lambda r: (0,) * len(shape))
    return pl.pallas_call(
        body, name="mix_bwd", grid=(nt,),
        in_specs=[row, row] + _u_specs(tm, rev) + halo_spec + [
            pl.BlockSpec((tm, D_ATT), lambda r: (rev(r), 0)),
            pl.BlockSpec((None, D_ATT, D_MODEL), lambda r: (layer, 0, 0)),
            pl.BlockSpec((None, D_CONV, D_MODEL), lambda r: (layer, 0, 0)),
            pl.BlockSpec((None, D_MODEL, D_MODEL), lambda r: (layer, 0, 0)),
            pl.BlockSpec((None, CONV_PAD, D_CONV), lambda r: (layer, 0, 0)),
            pl.BlockSpec((None, 8, D_CONV), lambda r: (layer, 0, 0)),
            pl.BlockSpec((None, 8, D_MODEL), lambda r: (layer, 0, 0))],
        out_specs=[pl.BlockSpec((tm, D_ATT), lambda r: (rev(r), 0)),
                   pl.BlockSpec((tm, 8 * SEG), lambda r: (rev(r), 0)),
                   row,
                   const((D_ATT, D_MODEL)), const((D_CONV, D_MODEL)), const((D_MODEL, D_MODEL)),
                   const((CONV_PAD, D_CONV)), const((8, D_CONV)), const((8, D_MODEL))],
        out_shape=[jax.ShapeDtypeStruct((T, D_ATT), F32), jax.ShapeDtypeStruct((T, 8 * SEG), F32),
                   jax.ShapeDtypeStruct((T, D_MODEL), F32),
                   jax.ShapeDtypeStruct((D_ATT, D_MODEL), F32), jax.ShapeDtypeStruct((D_CONV, D_MODEL), F32),
                   jax.ShapeDtypeStruct((D_MODEL, D_MODEL), F32), jax.ShapeDtypeStruct((CONV_PAD, D_CONV), F32),
                   jax.ShapeDtypeStruct((8, D_CONV), F32), jax.ShapeDtypeStruct((8, D_MODEL), F32)],
        scratch_shapes=[pltpu.VMEM((tm + CONV_PAD, D_CONV), F32), pltpu.VMEM((tm + CONV_PAD, D_CONV), F32)],
        compiler_params=_params("arbitrary"),
    )(dxo, h, *([u] * 10), att, wap, wcp, wout, cw, cvec, dvec)


def _loss_head(y, target):
    T = y.shape[0]
    tm = min(T, 512)

    def body(y_ref, t_ref, dy_ref, loss_ref):
        @pl.when(pl.program_id(0) == 0)
        def _():
            loss_ref[...] = jnp.zeros_like(loss_ref)

        err = y_ref[...] - t_ref[...]
        dy_ref[...] = err * (1.0 / D_MODEL)
        loss_ref[...] += 0.5 * jnp.sum(jnp.mean(err * err, axis=-1, keepdims=True))

    row = pl.BlockSpec((tm, D_MODEL), lambda i: (i, 0))
    return pl.pallas_call(
        body, name="loss_head", grid=(T // tm,), in_specs=[row, row],
        out_specs=[row, pl.BlockSpec((8, 128), lambda i: (0, 0))],
        out_shape=[jax.ShapeDtypeStruct((T, D_MODEL), F32), jax.ShapeDtypeStruct((8, 128), F32)],
        compiler_params=_params("arbitrary"),
    )(y, target)


def _adamw(w, g, m, v):
    m = ADAM_B1 * m + (1.0 - ADAM_B1) * g
    v = ADAM_B2 * v + (1.0 - ADAM_B2) * (g * g)
    m_hat = m / (1.0 - ADAM_B1 ** ADAM_STEP)
    v_hat = v / (1.0 - ADAM_B2 ** ADAM_STEP)
    return -ADAM_LR * (m_hat / (jnp.sqrt(v_hat) + ADAM_EPS) + ADAM_WD * w), m, v


def _sum_slots(ref):
    total = ref[0]
    for j in range(1, N_DEV):
        total = total + ref[j].astype(F32)
    return total


def _adamw_sharded(recv, w, m, v, prev, layer, tr, name):
    n_layers, R, C = w.shape

    def body(*refs):
        recv_ref, w_ref, m_ref, v_ref = refs[:4]
        g_ref, d_ref, nm_ref, nv_ref = refs[-4:]
        g = _sum_slots(recv_ref)
        g_ref[...] = g
        d_ref[...], nm_ref[...], nv_ref[...] = _adamw(w_ref[...], g, m_ref[...], v_ref[...])

    slab = pl.BlockSpec((None, tr, C), lambda i: (layer, i, 0))
    n_prev = 0 if prev is None else 4
    return pl.pallas_call(
        body, name=name, grid=(R // tr,),
        in_specs=[pl.BlockSpec((N_DEV, tr, C), lambda i: (0, i, 0)), slab, slab, slab] + [ANY] * n_prev,
        out_specs=[slab] * 4,
        out_shape=[jax.ShapeDtypeStruct(w.shape, F32)] * 4,
        input_output_aliases={4 + k: k for k in range(n_prev)},
        compiler_params=_params("parallel"),
    )(recv, w, m, v, *(prev or ()))


def _sum8(g):
    R = g.shape[1]

    def body(g_ref, o_ref):
        o_ref[...] = _sum_slots(g_ref)

    return pl.pallas_call(body, name="sum_small_grads", out_shape=jax.ShapeDtypeStruct((R, 128), F32))(g)


def _adamw_small(w, g, m, v):
    def body(w_ref, g_ref, m_ref, v_ref, d_ref, nm_ref, nv_ref):
        d_ref[...], nm_ref[...], nv_ref[...] = _adamw(w_ref[...], g_ref[...], m_ref[...], v_ref[...])

    return pl.pallas_call(body, name="adamw_small", out_shape=[jax.ShapeDtypeStruct(w.shape, F32)] * 3)(w, g, m, v)


REPLICATED = (("b_in", D_IN), ("conv_b", D_CONV), ("conv_ln_g", D_CONV), ("conv_ln_b", D_CONV),
              ("b_conv_proj", D_MODEL), ("ln_g", D_MODEL), ("ln_b", D_MODEL))
N_REPLICATED = sum(n for _, n in REPLICATED)
CONV_W_SHARD_PAD = 2048


def _pack_small(rep, conv_w_shard):
    flat = jnp.concatenate([rep[k] for k, _ in REPLICATED], axis=1)
    cws = jnp.pad(conv_w_shard.reshape(DEPTH, -1), ((0, 0), (0, CONV_W_SHARD_PAD - CONV_WIDTH * 64)))
    return jnp.concatenate([flat, cws], axis=1).reshape(-1, 128)


def _unpack_small(packed):
    per = packed.reshape(DEPTH, N_REPLICATED + CONV_W_SHARD_PAD)
    out, off = {}, 0
    for k, n in REPLICATED:
        out[k] = per[:, off:off + n]
        off += n
    out["conv_w"] = per[:, off:off + CONV_WIDTH * 64].reshape(DEPTH, CONV_WIDTH, 64)
    return out


def kernel(x, w_in, b_in, conv_w, conv_b, conv_ln_g, conv_ln_b, w_att_proj, w_conv_proj, b_conv_proj, w_out, ln_g, ln_b, loss_target, m_w_in, m_b_in, m_conv_w, m_conv_b, m_conv_ln_g, m_conv_ln_b, m_w_att_proj, m_w_conv_proj, m_b_conv_proj, m_w_out, m_ln_g, m_ln_b, v_w_in, v_b_in, v_conv_w, v_conv_b, v_conv_ln_g, v_conv_ln_b, v_w_att_proj, v_w_conv_proj, v_b_conv_proj, v_w_out, v_ln_g, v_ln_b):
    x0 = x[0]
    target = loss_target[0]

    g_in, g_ap, g_cp, g_out, g_cw = _all_gather(
        [w_in.astype(BF16), w_att_proj.astype(BF16), w_conv_proj.astype(BF16), w_out.astype(BF16), conv_w],
        "gather_weights")
    w_in_full = g_in.transpose(1, 2, 0, 3).reshape(DEPTH, D_MODEL, D_IN)
    wap = g_ap.transpose(1, 2, 0, 3).reshape(DEPTH, D_ATT, D_MODEL)
    wcp = g_cp.transpose(1, 2, 0, 3).reshape(DEPTH, D_CONV, D_MODEL)
    wout = g_out.transpose(1, 0, 2, 3).reshape(DEPTH, D_MODEL, D_MODEL)
    cw = jnp.pad(g_cw.transpose(1, 2, 0, 3).reshape(DEPTH, CONV_WIDTH, D_CONV), ((0, 0), (0, 1), (0, 0)))
    pad8 = lambda rows: jnp.pad(jnp.stack(rows, axis=1), ((0, 0), (0, 8 - len(rows)), (0, 0)))
    cvec = pad8([conv_b, conv_ln_g, conv_ln_b])
    dvec = pad8([b_conv_proj, ln_g, ln_b])
    b_in3 = b_in.reshape(DEPTH, 1, D_IN)

    xs, xbs, us, atts, hs = [x0], [x0.astype(BF16)], [], [], []
    for l in range(DEPTH):
        u = _in_proj(xbs[l], w_in_full, b_in3, l)
        att = _attn_fwd(u)
        xn, xnb, h = _mix_fwd(xs[l], u, att, wap, wcp, wout, cw, cvec, dvec, l)
        us.append(u), atts.append(att), hs.append(h), xs.append(xn), xbs.append(xnb)
    dxo, loss_part = _loss_head(xs[DEPTH], target)
    loss = lax.psum(loss_part[0, 0], ("x", "y", "c"))

    big = {"w_in": (w_in, m_w_in, v_w_in, 256), "w_att_proj": (w_att_proj, m_w_att_proj, v_w_att_proj, D_ATT),
           "w_conv_proj": (w_conv_proj, m_w_conv_proj, v_w_conv_proj, D_CONV), "w_out": (w_out, m_w_out, v_w_out, 128)}
    results = {k: None for k in big}
    small = [None] * DEPTH
    for l in reversed(range(DEPTH)):
        datt, du_rest, dres, dwap, dwcp, dwout, dcw, dcvec, ddvec = _mix_bwd(
            dxo, hs[l], us[l], atts[l], wap, wcp, wout, cw, cvec, dvec, l)
        dqkv = _attn_bwd(us[l], atts[l], datt)
        dxo = _in_proj_dx(dqkv, du_rest, w_in_full, dres, l)
        dwin, dbin = _in_proj_dw(xbs[l], dqkv, du_rest)
        small[l] = jnp.concatenate([dbin.reshape(-1), dcvec[:3].reshape(-1), ddvec[:3].reshape(-1), dcw.reshape(-1)])
        parts = {"w_in": dwin.reshape(D_MODEL, N_DEV, W_IN_SHARD).transpose(1, 0, 2),
                 "w_att_proj": dwap.reshape(D_ATT, N_DEV, D_MODEL // N_DEV).transpose(1, 0, 2),
                 "w_conv_proj": dwcp.reshape(D_CONV, N_DEV, D_MODEL // N_DEV).transpose(1, 0, 2),
                 "w_out": dwout.reshape(N_DEV, D_MODEL // N_DEV, D_MODEL)}
        names = list(big)
        recv = dict(zip(names, _exchange([parts[k] for k in names], "exchange_grads")))
        for k in names:
            w, m, v, tr = big[k]
            results[k] = _adamw_sharded(recv[k], w, m, v, results[k], l, tr, "adamw_" + k)
    grad_x = dxo[None]

    small_all = jnp.stack(small).reshape(-1, 128)
    (gathered,) = _all_gather([small_all], "gather_small_grads")
    per_layer = _sum8(gathered).reshape(DEPTH, -1)
    g_rep, off = {}, 0
    for k, n in (("b_in", D_IN), ("conv_b", D_CONV), ("conv_ln_g", D_CONV), ("conv_ln_b", D_CONV),
                 ("b_conv_proj", D_MODEL), ("ln_g", D_MODEL), ("ln_b", D_MODEL)):
        g_rep[k] = per_layer[:, off:off + n]
        off += n
    g_cw_full = per_layer[:, off:].reshape(DEPTH, CONV_PAD, D_CONV)[:, :CONV_WIDTH]
    g_cw_shard = lax.dynamic_slice_in_dim(g_cw_full, _my_index() * 64, 64, axis=2)
    rep_w = dict(b_in=b_in, conv_b=conv_b, conv_ln_g=conv_ln_g, conv_ln_b=conv_ln_b, b_conv_proj=b_conv_proj,
                 ln_g=ln_g, ln_b=ln_b)
    rep_m = dict(b_in=m_b_in, conv_b=m_conv_b, conv_ln_g=m_conv_ln_g, conv_ln_b=m_conv_ln_b,
                 b_conv_proj=m_b_conv_proj, ln_g=m_ln_g, ln_b=m_ln_b)
    rep_v = dict(b_in=v_b_in, conv_b=v_conv_b, conv_ln_g=v_conv_ln_g, conv_ln_b=v_conv_ln_b,
                 b_conv_proj=v_b_conv_proj, ln_g=v_ln_g, ln_b=v_ln_b)
    sd, sm, sv = _adamw_small(_pack_small(rep_w, conv_w), _pack_small(g_rep, g_cw_shard),
                              _pack_small(rep_m, m_conv_w), _pack_small(rep_v, v_conv_w))
    small_out = [dict(g_rep, conv_w=g_cw_shard), _unpack_small(sd), _unpack_small(sm), _unpack_small(sv)]

    order = ["w_in", "b_in", "conv_w", "conv_b", "conv_ln_g", "conv_ln_b", "w_att_proj", "w_conv_proj",
             "b_conv_proj", "w_out", "ln_g", "ln_b"]
    outs = [loss, grad_x]
    for kind in range(4):
        outs += [results[k][kind] if k in big else small_out[kind][k] for k in order]
    return tuple(outs)
```

```python
import functools

import jax
import jax.numpy as jnp
from jax import lax
from jax.experimental import pallas as pl
from jax.experimental.pallas import tpu as pltpu

F32, BF16 = jnp.float32, jnp.bfloat16

D_MODEL = 1024
D_ATT = 512
D_CONV = 512
HEAD_DIM = 64
CONV_WIDTH = 31
CONV_PAD = 32
DEPTH = 4
N_DEV = 8
SEG = 512
D_IN = 4 * D_ATT + 3 * D_CONV + 2 * D_MODEL
N_SEG = D_IN // SEG
N_QKV_SEG = 3
W_IN_SHARD = D_IN // N_DEV
LN_EPS = 1e-5
DEEPNORM_ALPHA = (2 * DEPTH) ** 0.25
ATT_SCALE = HEAD_DIM ** -0.5
ATT_BLOCK = 256
ADAM_LR, ADAM_B1, ADAM_B2, ADAM_EPS, ADAM_WD, ADAM_STEP = 0.001, 0.9, 0.999, 1e-08, 0.01, 10
VMEM_LIMIT = 56 * 1024 * 1024
MESH = pl.DeviceIdType.MESH
ANY = pl.BlockSpec(memory_space=pl.ANY)


def _params(*semantics):
    return pltpu.CompilerParams(dimension_semantics=semantics, vmem_limit_bytes=VMEM_LIMIT)


def _sig(x):
    return 1.0 / (1.0 + jnp.exp(-x))


def _dot(a, b):
    return jnp.dot(a, b, preferred_element_type=F32)


def _dot_nt(a, b):
    return lax.dot_general(a, b, (((1,), (1,)), ((), ())), preferred_element_type=F32)


def _dot_tn(a, b):
    return lax.dot_general(a, b, (((0,), (0,)), ((), ())), preferred_element_type=F32)


def _my_index():
    return 4 * lax.axis_index("x") + 2 * lax.axis_index("y") + lax.axis_index("c")


def _all_gather(shards, name):
    n = len(shards)

    def body(*refs):
        ins, outs = refs[:n], refs[n:2 * n]
        send_sems, recv_sems, local_sems = refs[2 * n:]
        x, y, c = lax.axis_index("x"), lax.axis_index("y"), lax.axis_index("c")
        me, sibling = (x, y, c), (x, y, 1 - c)
        chips = [(1 - x, y), (x, 1 - y), (1 - x, 1 - y)]

        def slot(a, dev):
            return outs[a].at[4 * dev[0] + 2 * dev[1] + dev[2]]

        def copy(a, k, block, to, src=None):
            return pltpu.make_async_remote_copy(
                src_ref=slot(a, block) if src is None else src, dst_ref=slot(a, block),
                send_sem=send_sems.at[7 * a + k], recv_sem=recv_sems.at[7 * a + k], device_id=to, device_id_type=MESH)

        mine = [pltpu.make_async_copy(ins[a], slot(a, me), local_sems.at[a]) for a in range(n)]
        for cp in mine:
            cp.start()
        first = []
        for a in range(n):
            first.append(copy(a, 0, me, sibling, src=ins[a]))
            first += [copy(a, 1 + j, me, (*chip, c), src=ins[a]) for j, chip in enumerate(chips)]
        for cp in first:
            cp.start()
        passed = []
        for j, chip in enumerate(chips):
            for a in range(n):
                copy(a, 1 + j, (*chip, c), me).wait_recv()
                fwd = copy(a, 4 + j, (*chip, c), sibling)
                fwd.start()
                passed.append(fwd)
        for a in range(n):
            copy(a, 0, sibling, me).wait_recv()
            for j, chip in enumerate(chips):
                copy(a, 4 + j, (*chip, 1 - c), me).wait_recv()
        for cp in first + passed:
            cp.wait_send()
        for cp in mine:
            cp.wait()

    return pl.pallas_call(
        body, name=name,
        out_shape=[jax.ShapeDtypeStruct((N_DEV,) + s.shape, s.dtype) for s in shards],
        in_specs=[ANY] * n, out_specs=[ANY] * n,
        scratch_shapes=[pltpu.SemaphoreType.DMA((7 * n,)), pltpu.SemaphoreType.DMA((7 * n,)),
                        pltpu.SemaphoreType.DMA((n,))],
    )(*shards)


def _exchange(parts, name):
    n = len(parts)

    def body(*refs):
        ins, outs = refs[:n], refs[n:2 * n]
        send_sems, recv_sems, local_sems = refs[2 * n:]
        x, y, c = lax.axis_index("x"), lax.axis_index("y"), lax.axis_index("c")
        me_idx = 4 * x + 2 * y + c

        def peer(d):
            px = 1 - x if d & 4 else x
            py = 1 - y if d & 2 else y
            pc = 1 - c if d & 1 else c
            return (px, py, pc), 4 * px + 2 * py + pc

        mine = [pltpu.make_async_copy(ins[a].at[me_idx], outs[a].at[me_idx], local_sems.at[a]) for a in range(n)]
        for cp in mine:
            cp.start()
        sends = []
        for d in (1, 2, 4, 3, 5, 6, 7):
            dev, idx = peer(d)
            for a in range(n):
                cp = pltpu.make_async_remote_copy(
                    src_ref=ins[a].at[idx], dst_ref=outs[a].at[me_idx],
                    send_sem=send_sems.at[7 * a + d - 1], recv_sem=recv_sems.at[7 * a + d - 1],
                    device_id=dev, device_id_type=MESH)
                cp.start()
                sends.append(cp)
        for d in range(1, 8):
            dev, idx = peer(d)
            for a in range(n):
                pltpu.make_async_remote_copy(
                    src_ref=ins[a].at[idx], dst_ref=outs[a].at[idx],
                    send_sem=send_sems.at[7 * a + d - 1], recv_sem=recv_sems.at[7 * a + d - 1],
                    device_id=dev, device_id_type=MESH).wait_recv()
        for cp in sends:
            cp.wait_send()
        for cp in mine:
            cp.wait()

    return pl.pallas_call(
        body, name=name,
        out_shape=[jax.ShapeDtypeStruct(p.shape, p.dtype) for p in parts],
        in_specs=[ANY] * n, out_specs=[ANY] * n,
        scratch_shapes=[pltpu.SemaphoreType.DMA((7 * n,)), pltpu.SemaphoreType.DMA((7 * n,)),
                        pltpu.SemaphoreType.DMA((n,))],
    )(*parts)


def _in_proj(xb, w_in, b_in, layer):
    T = xb.shape[0]
    tm, tn = min(T, 1024), SEG

    def body(x_ref, w_ref, b_ref, u_ref):
        u_ref[...] = _dot(x_ref[...], w_ref[...]) + b_ref[...]

    return pl.pallas_call(
        body, name="in_proj", grid=(D_IN // tn, T // tm),
        in_specs=[pl.BlockSpec((tm, D_MODEL), lambda n, m: (m, 0)),
                  pl.BlockSpec((None, D_MODEL, tn), lambda n, m: (layer, 0, n)),
                  pl.BlockSpec((None, 1, tn), lambda n, m: (layer, 0, n))],
        out_specs=pl.BlockSpec((tm, tn), lambda n, m: (m, n)),
        out_shape=jax.ShapeDtypeStruct((T, D_IN), F32),
        compiler_params=_params("parallel", "parallel"),
    )(xb, w_in, b_in)


def _du_specs(tm, row_of, seg_of):
    return [pl.BlockSpec((None, tm, SEG), lambda *g: (jnp.minimum(seg_of(*g), N_QKV_SEG - 1), row_of(*g), 0)),
            pl.BlockSpec((tm, SEG), lambda *g: (row_of(*g), jnp.maximum(seg_of(*g) - N_QKV_SEG, 0)))]


def _in_proj_dx(dqkv, du_rest, w_in, dres, layer):
    T = du_rest.shape[0]
    tm = min(T, 512)

    def body(a1_ref, a2_ref, w_ref, r_ref, o_ref):
        s = pl.program_id(1)

        @pl.when(s == 0)
        def _():
            o_ref[...] = r_ref[...]

        @pl.when(s < N_QKV_SEG)
        def _():
            o_ref[...] += _dot_nt(a1_ref[...].astype(BF16), w_ref[...])

        @pl.when(s >= N_QKV_SEG)
        def _():
            o_ref[...] += _dot_nt(a2_ref[...].astype(BF16), w_ref[...])

    return pl.pallas_call(
        body, name="in_proj_dx", grid=(T // tm, N_SEG),
        in_specs=_du_specs(tm, lambda m, s: m, lambda m, s: s) + [
            pl.BlockSpec((None, D_MODEL, SEG), lambda m, s: (layer, 0, s)),
            pl.BlockSpec((tm, D_MODEL), lambda m, s: (m, 0))],
        out_specs=pl.BlockSpec((tm, D_MODEL), lambda m, s: (m, 0)),
        out_shape=jax.ShapeDtypeStruct((T, D_MODEL), F32),
        compiler_params=_params("parallel", "arbitrary"),
    )(dqkv, du_rest, w_in, dres)


def _in_proj_dw(xb, dqkv, du_rest):
    T = xb.shape[0]
    tk = min(T, 512)

    def body(x_ref, a1_ref, a2_ref, dw_ref, db_ref):
        s, t = pl.program_id(0), pl.program_id(1)

        @pl.when(t == 0)
        def _():
            dw_ref[...] = jnp.zeros_like(dw_ref)
            db_ref[...] = jnp.zeros_like(db_ref)

        def acc(a):
            dw_ref[...] += _dot_tn(x_ref[...], a.astype(BF16))
            db_ref[...] += jnp.sum(a, axis=0, keepdims=True)

        @pl.when(s < N_QKV_SEG)
        def _():
            acc(a1_ref[...])

        @pl.when(s >= N_QKV_SEG)
        def _():
            acc(a2_ref[...])

    return pl.pallas_call(
        body, name="in_proj_dw", grid=(N_SEG, T // tk),
        in_specs=[pl.BlockSpec((tk, D_MODEL), lambda s, t: (t, 0))] + _du_specs(tk, lambda s, t: t, lambda s, t: s),
        out_specs=[pl.BlockSpec((D_MODEL, SEG), lambda s, t: (0, s)),
                   pl.BlockSpec((None, 1, SEG), lambda s, t: (s, 0, 0))],
        out_shape=[jax.ShapeDtypeStruct((D_MODEL, D_IN), F32), jax.ShapeDtypeStruct((N_SEG, 1, SEG), F32)],
        compiler_params=_params("parallel", "arbitrary"),
    )(xb, dqkv, du_rest)


def _split_dot(val, rhs_twice):
    hi = val.astype(BF16)
    lo = (val - hi.astype(F32)).astype(BF16)
    return _dot(jnp.concatenate([hi, lo], axis=1), rhs_twice)


def _twice(rhs):
    return jnp.concatenate([rhs, rhs], axis=0)


def _log_sigmoids(z):
    sp = jnp.log(1.0 + jnp.exp(-jnp.abs(z)))
    return jnp.minimum(z, 0.0) - sp, jnp.minimum(-z, 0.0) - sp


def _attn_masks():
    row = lax.broadcasted_iota(jnp.int32, (ATT_BLOCK, ATT_BLOCK), 0)
    col = lax.broadcasted_iota(jnp.int32, (ATT_BLOCK, ATT_BLOCK), 1)
    return col < row, (row > col).astype(BF16), (row >= col).astype(BF16)


EXP_UNDERFLOW = -104.0


def _sweep_keys(i, tile, rsum):
    tile(i, True)

    def live(carry):
        j, top = carry
        return jnp.logical_and(j >= 0, top > EXP_UNDERFLOW)

    def step(carry):
        j, _ = carry
        tile(j, False)
        return j - 1, jnp.max(rsum[...])

    lax.while_loop(live, step, (i - 1, jnp.max(rsum[...])))


def _attn_fwd(u):
    T = u.shape[0]
    B = ATT_BLOCK
    nq = T // B

    def body(q_ref, k_ref, v_ref, o_ref, kb, vb, acc, rsum):
        i = pl.program_id(1)

        @pl.when(i == 0)
        def _():
            for h in range(2):
                kb[h] = k_ref[:, HEAD_DIM * h:HEAD_DIM * (h + 1)].astype(BF16)
                vb[h] = v_ref[:, HEAD_DIM * h:HEAD_DIM * (h + 1)].astype(BF16)

        causal, tri, _ = _attn_masks()
        tri2 = _twice(tri)
        hs = [slice(HEAD_DIM * h, HEAD_DIM * (h + 1)) for h in range(2)]
        q = [(q_ref[:, hs[h]] * ATT_SCALE).astype(BF16) for h in range(2)]
        acc[...] = jnp.zeros_like(acc)
        rsum[...] = jnp.zeros_like(rsum)

        def tile(j, masked):
            rows = pl.ds(pl.multiple_of(j * B, B), B)
            for h in range(2):
                z = _dot_nt(q[h], kb[h, rows, :])
                a, f = _log_sigmoids(z)
                if masked:
                    f = jnp.where(causal, f, 0.0)
                later = _split_dot(f, tri2)
                w = jnp.exp(a + later + rsum[h])
                if masked:
                    w = jnp.where(causal, w, 0.0)
                acc[h] += _split_dot(w, _twice(vb[h, rows, :]))
                rsum[h] += later[:, 0:1] + f[:, 0:1]

        _sweep_keys(i, tile, rsum)
        for h in range(2):
            o_ref[:, hs[h]] = acc[h]

    return pl.pallas_call(
        body, name="attn_fwd", grid=(D_ATT // 128, nq),
        in_specs=[pl.BlockSpec((B, 128), lambda p, i: (i, p)),
                  pl.BlockSpec((T, 128), lambda p, i: (0, 4 + p)),
                  pl.BlockSpec((T, 128), lambda p, i: (0, 8 + p))],
        out_specs=pl.BlockSpec((B, 128), lambda p, i: (i, p)),
        out_shape=jax.ShapeDtypeStruct((T, D_ATT), F32),
        scratch_shapes=[pltpu.VMEM((2, T, HEAD_DIM), BF16), pltpu.VMEM((2, T, HEAD_DIM), BF16),
                        pltpu.VMEM((2, B, HEAD_DIM), F32), pltpu.VMEM((2, B, 1), F32)],
        compiler_params=_params("parallel", "arbitrary"),
    )(u, u, u)


def _attn_bwd(u, att, datt):
    T = u.shape[0]
    B = ATT_BLOCK
    nq = T // B

    def body(q_ref, k_ref, v_ref, o_ref, do_ref, dqkv_ref, kb, vb, dkacc, dvacc, dqacc, rsum, gsum):
        i = pl.program_id(1)

        @pl.when(i == 0)
        def _():
            for h in range(2):
                kb[h] = k_ref[:, HEAD_DIM * h:HEAD_DIM * (h + 1)].astype(BF16)
                vb[h] = v_ref[:, HEAD_DIM * h:HEAD_DIM * (h + 1)].astype(BF16)
            dkacc[...] = jnp.zeros_like(dkacc)
            dvacc[...] = jnp.zeros_like(dvacc)

        causal, tri, tri_incl = _attn_masks()
        tri2, tri_incl2 = _twice(tri), _twice(tri_incl)
        qrows = pl.ds(pl.multiple_of(i * B, B), B)
        hs = [slice(HEAD_DIM * h, HEAD_DIM * (h + 1)) for h in range(2)]
        q = [(q_ref[:, hs[h]] * ATT_SCALE).astype(BF16) for h in range(2)]
        dob = [do_ref[:, hs[h]].astype(BF16) for h in range(2)]
        total = [jnp.sum(dob[h].astype(F32) * o_ref[:, hs[h]], axis=1, keepdims=True) for h in range(2)]
        dqacc[...] = jnp.zeros_like(dqacc)
        rsum[...] = jnp.zeros_like(rsum)
        gsum[...] = jnp.zeros_like(gsum)

        def tile(j, masked):
            rows = pl.ds(pl.multiple_of(j * B, B), B)
            for h in range(2):
                kj, vj = kb[h, rows, :], vb[h, rows, :]
                z = _dot_nt(q[h], kj)
                a, f = _log_sigmoids(z)
                if masked:
                    f = jnp.where(causal, f, 0.0)
                later = _split_dot(f, tri2)
                w = jnp.exp(a + later + rsum[h])
                if masked:
                    w = jnp.where(causal, w, 0.0)
                g = _dot_nt(dob[h], vj) * w
                suffix = _split_dot(g, tri_incl2)
                before = total[h] - (suffix + gsum[h])
                beta = jnp.exp(a)
                dz = g * (1.0 - beta) - before * beta
                if masked:
                    dz = jnp.where(causal, dz, 0.0)
                dzb = dz.astype(BF16)
                dqacc[h] += _dot(dzb, kj)
                dkacc[h, rows, :] += _dot_tn(dzb, q[h])
                dvacc[h, rows, :] += _dot_tn(w.astype(BF16), dob[h])
                rsum[h] += later[:, 0:1] + f[:, 0:1]
                gsum[h] += suffix[:, 0:1]

        _sweep_keys(i, tile, rsum)
        for h in range(2):
            dqkv_ref[0, qrows, hs[h]] = dqacc[h] * ATT_SCALE

        @pl.when(i == nq - 1)
        def _():
            for h in range(2):
                dqkv_ref[1, :, hs[h]] = dkacc[h]
                dqkv_ref[2, :, hs[h]] = dvacc[h]

    return pl.pallas_call(
        body, name="attn_bwd", grid=(D_ATT // 128, nq),
        in_specs=[pl.BlockSpec((B, 128), lambda p, i: (i, p)),
                  pl.BlockSpec((T, 128), lambda p, i: (0, 4 + p)),
                  pl.BlockSpec((T, 128), lambda p, i: (0, 8 + p)),
                  pl.BlockSpec((B, 128), lambda p, i: (i, p)),
                  pl.BlockSpec((B, 128), lambda p, i: (i, p))],
        out_specs=pl.BlockSpec((3, T, 128), lambda p, i: (0, 0, p)),
        out_shape=jax.ShapeDtypeStruct((3, T, D_ATT), F32),
        scratch_shapes=[pltpu.VMEM((2, T, HEAD_DIM), BF16), pltpu.VMEM((2, T, HEAD_DIM), BF16),
                        pltpu.VMEM((2, T, HEAD_DIM), F32), pltpu.VMEM((2, T, HEAD_DIM), F32),
                        pltpu.VMEM((2, B, HEAD_DIM), F32), pltpu.VMEM((2, B, 1), F32), pltpu.VMEM((2, B, 1), F32)],
        compiler_params=_params("parallel", "arbitrary"),
    )(u, u, u, att, datt)


MIX_TILE = 128


def _u_specs(tm, tile_of):
    return [pl.BlockSpec((tm, SEG), functools.partial(lambda k, *g: (tile_of(*g), k), k)) for k in range(3, N_SEG)]


def _conv(ext, cw_ref, tm):
    acc = cw_ref[0:1, :] * ext[2:2 + tm, :]
    for k in range(1, CONV_WIDTH):
        acc = acc + cw_ref[k:k + 1, :] * ext[2 + k:2 + k + tm, :]
    return acc


def _norm_stats(v):
    mu = jnp.mean(v, axis=-1, keepdims=True)
    vc = v - mu
    rstd = lax.rsqrt(jnp.mean(vc * vc, axis=-1, keepdims=True) + LN_EPS)
    return vc * rstd, rstd


def _norm_bwd(dy_scaled, xhat, rstd):
    return rstd * (dy_scaled - jnp.mean(dy_scaled, axis=-1, keepdims=True)
                   - xhat * jnp.mean(dy_scaled * xhat, axis=-1, keepdims=True))


def _mix_fwd(x, u, att, wap, wcp, wout, cw, cvec, dvec, layer):
    T = x.shape[0]
    tm = MIX_TILE

    def body(x_ref, za_ref, ga_ref, gb_ref, zc_ref, ga0_ref, ga1_ref, gc0_ref, gc1_ref, att_ref,
             wap_ref, wcp_ref, wout_ref, cw_ref, cvec_ref, dvec_ref, xn_ref, xnb_ref, h_ref, ext):
        i = pl.program_id(0)

        @pl.when(i == 0)
        def _():
            ext[0:CONV_PAD, :] = jnp.zeros((CONV_PAD, D_CONV), F32)

        za = za_ref[...]
        attg = att_ref[...] * (za * _sig(za))
        ab = _dot(attg.astype(BF16), wap_ref[...])

        ext[CONV_PAD:CONV_PAD + tm, :] = ga_ref[...] * _sig(gb_ref[...])
        c1 = _conv(ext, cw_ref, tm) + cvec_ref[0:1, :]
        ext[0:CONV_PAD, :] = ext[tm:tm + CONV_PAD, :]
        xh, _ = _norm_stats(c1)
        c2 = xh * cvec_ref[1:2, :] + cvec_ref[2:3, :]
        zc = zc_ref[...]
        cz = (c2 * _sig(c2)) * (zc * _sig(zc))
        cb = _dot(cz.astype(BF16), wcp_ref[...]) + dvec_ref[0:1, :]

        g_att = jnp.concatenate([ga0_ref[...], ga1_ref[...]], axis=1)
        g_conv = jnp.concatenate([gc0_ref[...], gc1_ref[...]], axis=1)
        merged = _sig(g_att) * ab + _sig(g_conv) * cb
        h = DEEPNORM_ALPHA * x_ref[...] + _dot(merged.astype(BF16), wout_ref[...])
        h_ref[...] = h
        hh, _ = _norm_stats(h)
        xn = hh * dvec_ref[1:2, :] + dvec_ref[2:3, :]
        xn_ref[...] = xn
        xnb_ref[...] = xn.astype(BF16)

    row = pl.BlockSpec((tm, D_MODEL), lambda i: (i, 0))
    return pl.pallas_call(
        body, name="mix_fwd", grid=(T // tm,),
        in_specs=[row] + _u_specs(tm, lambda i: i) + [
            pl.BlockSpec((tm, D_ATT), lambda i: (i, 0)),
            pl.BlockSpec((None, D_ATT, D_MODEL), lambda i: (layer, 0, 0)),
            pl.BlockSpec((None, D_CONV, D_MODEL), lambda i: (layer, 0, 0)),
            pl.BlockSpec((None, D_MODEL, D_MODEL), lambda i: (layer, 0, 0)),
            pl.BlockSpec((None, CONV_PAD, D_CONV), lambda i: (layer, 0, 0)),
            pl.BlockSpec((None, 8, D_CONV), lambda i: (layer, 0, 0)),
            pl.BlockSpec((None, 8, D_MODEL), lambda i: (layer, 0, 0))],
        out_specs=[row, row, row],
        out_shape=[jax.ShapeDtypeStruct((T, D_MODEL), F32), jax.ShapeDtypeStruct((T, D_MODEL), BF16),
                   jax.ShapeDtypeStruct((T, D_MODEL), F32)],
        scratch_shapes=[pltpu.VMEM((tm + CONV_PAD, D_CONV), F32)],
        compiler_params=_params("arbitrary"),
    )(x, *([u] * 8), att, wap, wcp, wout, cw, cvec, dvec)


def _mix_bwd(dxo, h, u, att, wap, wcp, wout, cw, cvec, dvec, layer):
    T = dxo.shape[0]
    tm = MIX_TILE
    nt = T // tm
    halo_per_tile = tm // CONV_PAD

    def body(dxo_ref, h_ref, za_ref, ga_ref, gb_ref, zc_ref, ga0_ref, ga1_ref, gc0_ref, gc1_ref, hga_ref, hgb_ref,
             att_ref, wap_ref, wcp_ref, wout_ref, cw_ref, cvec_ref, dvec_ref,
             datt_ref, dur_ref, dres_ref, dwap_ref, dwcp_ref, dwout_ref, dcw_ref, dcvec_ref, ddvec_ref, ext, dext):
        r = pl.program_id(0)
        i = nt - 1 - r

        @pl.when(r == 0)
        def _():
            for ref in (dwap_ref, dwcp_ref, dwout_ref, dcw_ref, dcvec_ref, ddvec_ref):
                ref[...] = jnp.zeros_like(ref)
            dext[tm:tm + CONV_PAD, :] = jnp.zeros((CONV_PAD, D_CONV), F32)

        za = za_ref[...]
        s_za = _sig(za)
        silu_za = za * s_za
        att_t = att_ref[...]
        attg = (att_t * silu_za).astype(BF16)
        ab = _dot(attg, wap_ref[...])

        ga, gb = ga_ref[...], gb_ref[...]
        s_gb = _sig(gb)
        halo = hga_ref[...] * _sig(hgb_ref[...])
        ext[0:CONV_PAD, :] = jnp.where(i > 0, halo, 0.0)
        ext[CONV_PAD:CONV_PAD + tm, :] = ga * s_gb
        c1 = _conv(ext, cw_ref, tm) + cvec_ref[0:1, :]
        xh, rstd_c = _norm_stats(c1)
        c2 = xh * cvec_ref[1:2, :] + cvec_ref[2:3, :]
        s_c2 = _sig(c2)
        c3 = c2 * s_c2
        zc = zc_ref[...]
        s_zc = _sig(zc)
        cz = (c3 * (zc * s_zc)).astype(BF16)
        cb = _dot(cz, wcp_ref[...]) + dvec_ref[0:1, :]

        s_ga = _sig(jnp.concatenate([ga0_ref[...], ga1_ref[...]], axis=1))
        s_gc = _sig(jnp.concatenate([gc0_ref[...], gc1_ref[...]], axis=1))
        merged = (s_ga * ab + s_gc * cb).astype(BF16)

        dxo_t = dxo_ref[...]
        hh, rstd_h = _norm_stats(h_ref[...])
        ddvec_ref[1:2, :] += jnp.sum(dxo_t * hh, axis=0, keepdims=True)
        ddvec_ref[2:3, :] += jnp.sum(dxo_t, axis=0, keepdims=True)
        dh = _norm_bwd(dxo_t * dvec_ref[1:2, :], hh, rstd_h)
        dres_ref[...] = DEEPNORM_ALPHA * dh
        dy = dh.astype(BF16)

        dwout_ref[...] += _dot_tn(merged, dy)
        dm = _dot_nt(dy, wout_ref[...])
        dab = (dm * s_ga).astype(BF16)
        dcb = dm * s_gc
        dur_ref[:, 4 * SEG:6 * SEG] = dm * ab * (s_ga * (1.0 - s_ga))
        dur_ref[:, 6 * SEG:8 * SEG] = dm * cb * (s_gc * (1.0 - s_gc))

        dwap_ref[...] += _dot_tn(attg, dab)
        dattg = _dot_nt(dab, wap_ref[...])
        datt_ref[...] = dattg * silu_za
        dur_ref[:, 0:SEG] = dattg * att_t * (s_za * (1.0 + za * (1.0 - s_za)))

        ddvec_ref[0:1, :] += jnp.sum(dcb, axis=0, keepdims=True)
        dcbb = dcb.astype(BF16)
        dwcp_ref[...] += _dot_tn(cz, dcbb)
        dcz = _dot_nt(dcbb, wcp_ref[...])
        dur_ref[:, 3 * SEG:4 * SEG] = dcz * c3 * (s_zc * (1.0 + zc * (1.0 - s_zc)))
        dc2 = dcz * (zc * s_zc) * (s_c2 * (1.0 + c2 * (1.0 - s_c2)))
        dcvec_ref[1:2, :] += jnp.sum(dc2 * xh, axis=0, keepdims=True)
        dcvec_ref[2:3, :] += jnp.sum(dc2, axis=0, keepdims=True)
        dc1 = _norm_bwd(dc2 * cvec_ref[1:2, :], xh, rstd_c)
        dcvec_ref[0:1, :] += jnp.sum(dc1, axis=0, keepdims=True)
        for k in range(CONV_WIDTH):
            dcw_ref[k:k + 1, :] += jnp.sum(dc1 * ext[2 + k:2 + k + tm, :], axis=0, keepdims=True)
        dext[0:tm, :] = dc1
        dc0 = cw_ref[0:1, :] * dext[CONV_WIDTH - 1:CONV_WIDTH - 1 + tm, :]
        for k in range(1, CONV_WIDTH):
            dc0 = dc0 + cw_ref[k:k + 1, :] * dext[CONV_WIDTH - 1 - k:CONV_WIDTH - 1 - k + tm, :]
        dext[tm:tm + CONV_PAD, :] = dext[0:CONV_PAD, :]
        dur_ref[:, SEG:2 * SEG] = dc0 * s_gb
        dur_ref[:, 2 * SEG:3 * SEG] = dc0 * ga * (s_gb * (1.0 - s_gb))

    rev = lambda r: nt - 1 - r
    row = pl.BlockSpec((tm, D_MODEL), lambda r: (rev(r), 0))
    halo_spec = [pl.BlockSpec((CONV_PAD, SEG),
                              functools.partial(lambda k, r: (jnp.maximum(rev(r) * halo_per_tile - 1, 0), k), k))
                 for k in (4, 5)]
    const = lambda shape: pl.BlockSpec(shape, lambda r: (0,) * len(shape))
    return pl.pallas_call(
        body, name="mix_bwd", grid=(nt,),
        in_specs=[row, row] + _u_specs(tm, rev) + halo_spec + [
            pl.BlockSpec((tm, D_ATT), lambda r: (rev(r), 0)),
            pl.BlockSpec((None, D_ATT, D_MODEL), lambda r: (layer, 0, 0)),
            pl.BlockSpec((None, D_CONV, D_MODEL), lambda r: (layer, 0, 0)),
            pl.BlockSpec((None, D_MODEL, D_MODEL), lambda r: (layer, 0, 0)),
            pl.BlockSpec((None, CONV_PAD, D_CONV), lambda r: (layer, 0, 0)),
            pl.BlockSpec((None, 8, D_CONV), lambda r: (layer, 0, 0)),
            pl.BlockSpec((None, 8, D_MODEL), lambda r: (layer, 0, 0))],
        out_specs=[pl.BlockSpec((tm, D_ATT), lambda r: (rev(r), 0)),
                   pl.BlockSpec((tm, 8 * SEG), lambda r: (rev(r), 0)),
                   row,
                   const((D_ATT, D_MODEL)), const((D_CONV, D_MODEL)), const((D_MODEL, D_MODEL)),
                   const((CONV_PAD, D_CONV)), const((8, D_CONV)), const((8, D_MODEL))],
        out_shape=[jax.ShapeDtypeStruct((T, D_ATT), F32), jax.ShapeDtypeStruct((T, 8 * SEG), F32),
                   jax.ShapeDtypeStruct((T, D_MODEL), F32),
                   jax.ShapeDtypeStruct((D_ATT, D_MODEL), F32), jax.ShapeDtypeStruct((D_CONV, D_MODEL), F32),
                   jax.ShapeDtypeStruct((D_MODEL, D_MODEL), F32), jax.ShapeDtypeStruct((CONV_PAD, D_CONV), F32),
                   jax.ShapeDtypeStruct((8, D_CONV), F32), jax.ShapeDtypeStruct((8, D_MODEL), F32)],
        scratch_shapes=[pltpu.VMEM((tm + CONV_PAD, D_CONV), F32), pltpu.VMEM((tm + CONV_PAD, D_CONV), F32)],
        compiler_params=_params("arbitrary"),
    )(dxo, h, *([u] * 10), att, wap, wcp, wout, cw, cvec, dvec)


def _loss_head(y, target):
    T = y.shape[0]
    tm = min(T, 512)

    def body(y_ref, t_ref, dy_ref, loss_ref):
        @pl.when(pl.program_id(0) == 0)
        def _():
            loss_ref[...] = jnp.zeros_like(loss_ref)

        err = y_ref[...] - t_ref[...]
        dy_ref[...] = err * (1.0 / D_MODEL)
        loss_ref[...] += 0.5 * jnp.sum(jnp.mean(err * err, axis=-1, keepdims=True))

    row = pl.BlockSpec((tm, D_MODEL), lambda i: (i, 0))
    return pl.pallas_call(
        body, name="loss_head", grid=(T // tm,), in_specs=[row, row],
        out_specs=[row, pl.BlockSpec((8, 128), lambda i: (0, 0))],
        out_shape=[jax.ShapeDtypeStruct((T, D_MODEL), F32), jax.ShapeDtypeStruct((8, 128), F32)],
        compiler_params=_params("arbitrary"),
    )(y, target)


def _adamw(w, g, m, v):
    m = ADAM_B1 * m + (1.0 - ADAM_B1) * g
    v = ADAM_B2 * v + (1.0 - ADAM_B2) * (g * g)
    m_hat = m / (1.0 - ADAM_B1 ** ADAM_STEP)
    v_hat = v / (1.0 - ADAM_B2 ** ADAM_STEP)
    return -ADAM_LR * (m_hat / (jnp.sqrt(v_hat) + ADAM_EPS) + ADAM_WD * w), m, v


def _sum_slots(ref):
    total = ref[0].astype(F32)
    for j in range(1, N_DEV):
        total = total + ref[j].astype(F32)
    return total


def _adamw_sharded(recv, w, m, v, prev, layer, tr, name):
    n_layers, R, C = w.shape

    def body(*refs):
        recv_ref, w_ref, m_ref, v_ref = refs[:4]
        g_ref, d_ref, nm_ref, nv_ref = refs[-4:]
        g = _sum_slots(recv_ref)
        g_ref[...] = g
        d_ref[...], nm_ref[...], nv_ref[...] = _adamw(w_ref[...], g, m_ref[...], v_ref[...])

    slab = pl.BlockSpec((None, tr, C), lambda i: (layer, i, 0))
    n_prev = 0 if prev is None else 4
    return pl.pallas_call(
        body, name=name, grid=(R // tr,),
        in_specs=[pl.BlockSpec((N_DEV, tr, C), lambda i: (0, i, 0)), slab, slab, slab] + [ANY] * n_prev,
        out_specs=[slab] * 4,
        out_shape=[jax.ShapeDtypeStruct(w.shape, F32)] * 4,
        input_output_aliases={4 + k: k for k in range(n_prev)},
        compiler_params=_params("parallel"),
    )(recv, w, m, v, *(prev or ()))


def _sum8(g):
    R = g.shape[1]

    def body(g_ref, o_ref):
        o_ref[...] = _sum_slots(g_ref)

    return pl.pallas_call(body, name="sum_small_grads", out_shape=jax.ShapeDtypeStruct((R, 128), F32))(g)


def _adamw_small(w, g, m, v):
    def body(w_ref, g_ref, m_ref, v_ref, d_ref, nm_ref, nv_ref):
        d_ref[...], nm_ref[...], nv_ref[...] = _adamw(w_ref[...], g_ref[...], m_ref[...], v_ref[...])

    return pl.pallas_call(body, name="adamw_small", out_shape=[jax.ShapeDtypeStruct(w.shape, F32)] * 3)(w, g, m, v)


REPLICATED = (("b_in", D_IN), ("conv_b", D_CONV), ("conv_ln_g", D_CONV), ("conv_ln_b", D_CONV),
              ("b_conv_proj", D_MODEL), ("ln_g", D_MODEL), ("ln_b", D_MODEL))
N_REPLICATED = sum(n for _, n in REPLICATED)
CONV_W_SHARD_PAD = 2048


def _pack_small(rep, conv_w_shard):
    flat = jnp.concatenate([rep[k] for k, _ in REPLICATED], axis=1)
    cws = jnp.pad(conv_w_shard.reshape(DEPTH, -1), ((0, 0), (0, CONV_W_SHARD_PAD - CONV_WIDTH * 64)))
    return jnp.concatenate([flat, cws], axis=1).reshape(-1, 128)


def _unpack_small(packed):
    per = packed.reshape(DEPTH, N_REPLICATED + CONV_W_SHARD_PAD)
    out, off = {}, 0
    for k, n in REPLICATED:
        out[k] = per[:, off:off + n]
        off += n
    out["conv_w"] = per[:, off:off + CONV_WIDTH * 64].reshape(DEPTH, CONV_WIDTH, 64)
    return out


def kernel(x, w_in, b_in, conv_w, conv_b, conv_ln_g, conv_ln_b, w_att_proj, w_conv_proj, b_conv_proj, w_out, ln_g, ln_b, loss_target, m_w_in, m_b_in, m_conv_w, m_conv_b, m_conv_ln_g, m_conv_ln_b, m_w_att_proj, m_w_conv_proj, m_b_conv_proj, m_w_out, m_ln_g, m_ln_b, v_w_in, v_b_in, v_conv_w, v_conv_b, v_conv_ln_g, v_conv_ln_b, v_w_att_proj, v_w_conv_proj, v_b_conv_proj, v_w_out, v_ln_g, v_ln_b):
    x0 = x[0]
    target = loss_target[0]

    g_in, g_ap, g_cp, g_out, g_cw = _all_gather(
        [w_in.astype(BF16), w_att_proj.astype(BF16), w_conv_proj.astype(BF16), w_out.astype(BF16), conv_w],
        "gather_weights")
    w_in_full = g_in.transpose(1, 2, 0, 3).reshape(DEPTH, D_MODEL, D_IN)
    wap = g_ap.transpose(1, 2, 0, 3).reshape(DEPTH, D_ATT, D_MODEL)
    wcp = g_cp.transpose(1, 2, 0, 3).reshape(DEPTH, D_CONV, D_MODEL)
    wout = g_out.transpose(1, 0, 2, 3).reshape(DEPTH, D_MODEL, D_MODEL)
    cw = jnp.pad(g_cw.transpose(1, 2, 0, 3).reshape(DEPTH, CONV_WIDTH, D_CONV), ((0, 0), (0, 1), (0, 0)))
    pad8 = lambda rows: jnp.pad(jnp.stack(rows, axis=1), ((0, 0), (0, 8 - len(rows)), (0, 0)))
    cvec = pad8([conv_b, conv_ln_g, conv_ln_b])
    dvec = pad8([b_conv_proj, ln_g, ln_b])
    b_in3 = b_in.reshape(DEPTH, 1, D_IN)

    xs, xbs, us, atts, hs = [x0], [x0.astype(BF16)], [], [], []
    for l in range(DEPTH):
        u = _in_proj(xbs[l], w_in_full, b_in3, l)
        att = _attn_fwd(u)
        xn, xnb, h = _mix_fwd(xs[l], u, att, wap, wcp, wout, cw, cvec, dvec, l)
        us.append(u), atts.append(att), hs.append(h), xs.append(xn), xbs.append(xnb)
    dxo, loss_part = _loss_head(xs[DEPTH], target)
    loss = lax.psum(loss_part[0, 0], ("x", "y", "c"))

    big = {"w_in": (w_in, m_w_in, v_w_in, 256), "w_att_proj": (w_att_proj, m_w_att_proj, v_w_att_proj, D_ATT),
           "w_conv_proj": (w_conv_proj, m_w_conv_proj, v_w_conv_proj, D_CONV), "w_out": (w_out, m_w_out, v_w_out, 128)}
    results = {k: None for k in big}
    small = [None] * DEPTH
    for l in reversed(range(DEPTH)):
        datt, du_rest, dres, dwap, dwcp, dwout, dcw, dcvec, ddvec = _mix_bwd(
            dxo, hs[l], us[l], atts[l], wap, wcp, wout, cw, cvec, dvec, l)
        dqkv = _attn_bwd(us[l], atts[l], datt)
        dxo = _in_proj_dx(dqkv, du_rest, w_in_full, dres, l)
        dwin, dbin = _in_proj_dw(xbs[l], dqkv, du_rest)
        small[l] = jnp.concatenate([dbin.reshape(-1), dcvec[:3].reshape(-1), ddvec[:3].reshape(-1), dcw.reshape(-1)])
        parts = {"w_in": dwin.reshape(D_MODEL, N_DEV, W_IN_SHARD).transpose(1, 0, 2),
                 "w_att_proj": dwap.reshape(D_ATT, N_DEV, D_MODEL // N_DEV).transpose(1, 0, 2),
                 "w_conv_proj": dwcp.reshape(D_CONV, N_DEV, D_MODEL // N_DEV).transpose(1, 0, 2),
                 "w_out": dwout.reshape(N_DEV, D_MODEL // N_DEV, D_MODEL)}
        names = list(big)
        recv = dict(zip(names, _exchange([parts[k].astype(BF16) for k in names], "exchange_grads")))
        for k in names:
            w, m, v, tr = big[k]
            results[k] = _adamw_sharded(recv[k], w, m, v, results[k], l, tr, "adamw_" + k)
    grad_x = dxo[None]

    small_all = jnp.stack(small).reshape(-1, 128)
    (gathered,) = _all_gather([small_all], "gather_small_grads")
    per_layer = _sum8(gathered).reshape(DEPTH, -1)
    g_rep, off = {}, 0
    for k, n in (("b_in", D_IN), ("conv_b", D_CONV), ("conv_ln_g", D_CONV), ("conv_ln_b", D_CONV),
                 ("b_conv_proj", D_MODEL), ("ln_g", D_MODEL), ("ln_b", D_MODEL)):
        g_rep[k] = per_layer[:, off:off + n]
        off += n
    g_cw_full = per_layer[:, off:].reshape(DEPTH, CONV_PAD, D_CONV)[:, :CONV_WIDTH]
    g_cw_shard = lax.dynamic_slice_in_dim(g_cw_full, _my_index() * 64, 64, axis=2)
    rep_w = dict(b_in=b_in, conv_b=conv_b, conv_ln_g=conv_ln_g, conv_ln_b=conv_ln_b, b_conv_proj=b_conv_proj,
                 ln_g=ln_g, ln_b=ln_b)
    rep_m = dict(b_in=m_b_in, conv_b=m_conv_b, conv_ln_g=m_conv_ln_g, conv_ln_b=m_conv_ln_b,
                 b_conv_proj=m_b_conv_proj, ln_g=m_ln_g, ln_b=m_ln_b)
    rep_v = dict(b_in=v_b_in, conv_b=v_conv_b, conv_ln_g=v_conv_ln_g, conv_ln_b=v_conv_ln_b,
                 b_conv_proj=v_b_conv_proj, ln_g=v_ln_g, ln_b=v_ln_b)
    sd, sm, sv = _adamw_small(_pack_small(rep_w, conv_w), _pack_small(g_rep, g_cw_shard),
                              _pack_small(rep_m, m_conv_w), _pack_small(rep_v, v_conv_w))
    small_out = [dict(g_rep, conv_w=g_cw_shard), _unpack_small(sd), _unpack_small(sm), _unpack_small(sv)]

    order = ["w_in", "b_in", "conv_w", "conv_b", "conv_ln_g", "conv_ln_b", "w_att_proj", "w_conv_proj",
             "b_conv_proj", "w_out", "ln_g", "ln_b"]
    outs = [loss, grad_x]
    for kind in range(4):
        outs += [results[k][kind] if k in big else small_out[kind][k] for k in order]
    return tuple(outs)
```

```python
import functools

import jax
import jax.numpy as jnp
from jax import lax
from jax.experimental import pallas as pl
from jax.experimental.pallas import tpu as pltpu

F32, BF16 = jnp.float32, jnp.bfloat16

D_MODEL = 1024
D_ATT = 512
D_CONV = 512
HEAD_DIM = 64
CONV_WIDTH = 31
CONV_PAD = 32
DEPTH = 4
N_DEV = 8
SEG = 512
D_IN = 4 * D_ATT + 3 * D_CONV + 2 * D_MODEL
N_SEG = D_IN // SEG
N_QKV_SEG = 3
W_IN_SHARD = D_IN // N_DEV
LN_EPS = 1e-5
DEEPNORM_ALPHA = (2 * DEPTH) ** 0.25
ATT_SCALE = HEAD_DIM ** -0.5
ATT_BLOCK = 256
ADAM_LR, ADAM_B1, ADAM_B2, ADAM_EPS, ADAM_WD, ADAM_STEP = 0.001, 0.9, 0.999, 1e-08, 0.01, 10
VMEM_LIMIT = 56 * 1024 * 1024
MESH = pl.DeviceIdType.MESH
ANY = pl.BlockSpec(memory_space=pl.ANY)


def _params(*semantics):
    return pltpu.CompilerParams(dimension_semantics=semantics, vmem_limit_bytes=VMEM_LIMIT)


def _sig(x):
    return 1.0 / (1.0 + jnp.exp(-x))


def _dot(a, b):
    return jnp.dot(a, b, preferred_element_type=F32)


def _dot_nt(a, b):
    return lax.dot_general(a, b, (((1,), (1,)), ((), ())), preferred_element_type=F32)


def _dot_tn(a, b):
    return lax.dot_general(a, b, (((0,), (0,)), ((), ())), preferred_element_type=F32)


def _my_index():
    return 4 * lax.axis_index("x") + 2 * lax.axis_index("y") + lax.axis_index("c")


def _all_gather(shards, name):
    n = len(shards)

    def body(*refs):
        ins, outs = refs[:n], refs[n:2 * n]
        send_sems, recv_sems, local_sems = refs[2 * n:]
        x, y, c = lax.axis_index("x"), lax.axis_index("y"), lax.axis_index("c")
        me, sibling = (x, y, c), (x, y, 1 - c)
        chips = [(1 - x, y), (x, 1 - y), (1 - x, 1 - y)]

        def slot(a, dev):
            return outs[a].at[4 * dev[0] + 2 * dev[1] + dev[2]]

        def copy(a, k, block, to, src=None):
            return pltpu.make_async_remote_copy(
                src_ref=slot(a, block) if src is None else src, dst_ref=slot(a, block),
                send_sem=send_sems.at[7 * a + k], recv_sem=recv_sems.at[7 * a + k], device_id=to, device_id_type=MESH)

        mine = [pltpu.make_async_copy(ins[a], slot(a, me), local_sems.at[a]) for a in range(n)]
        for cp in mine:
            cp.start()
        first = []
        for a in range(n):
            first.append(copy(a, 0, me, sibling, src=ins[a]))
            first += [copy(a, 1 + j, me, (*chip, c), src=ins[a]) for j, chip in enumerate(chips)]
        for cp in first:
            cp.start()
        passed = []
        for j, chip in enumerate(chips):
            for a in range(n):
                copy(a, 1 + j, (*chip, c), me).wait_recv()
                fwd = copy(a, 4 + j, (*chip, c), sibling)
                fwd.start()
                passed.append(fwd)
        for a in range(n):
            copy(a, 0, sibling, me).wait_recv()
            for j, chip in enumerate(chips):
                copy(a, 4 + j, (*chip, 1 - c), me).wait_recv()
        for cp in first + passed:
            cp.wait_send()
        for cp in mine:
            cp.wait()

    return pl.pallas_call(
        body, name=name,
        out_shape=[jax.ShapeDtypeStruct((N_DEV,) + s.shape, s.dtype) for s in shards],
        in_specs=[ANY] * n, out_specs=[ANY] * n,
        scratch_shapes=[pltpu.SemaphoreType.DMA((7 * n,)), pltpu.SemaphoreType.DMA((7 * n,)),
                        pltpu.SemaphoreType.DMA((n,))],
    )(*shards)


class _Exchange:
    def __init__(self, ins, outs, send_sems, recv_sems, local_sems):
        n = len(ins)
        x, y, c = lax.axis_index("x"), lax.axis_index("y"), lax.axis_index("c")
        me_idx = 4 * x + 2 * y + c
        self.mine = [pltpu.make_async_copy(ins[a].at[me_idx], outs[a].at[me_idx], local_sems.at[a])
                     for a in range(n)]
        self.sends, self.recvs = [], []
        for d in (1, 2, 4, 3, 5, 6, 7):
            px = 1 - x if d & 4 else x
            py = 1 - y if d & 2 else y
            pc = 1 - c if d & 1 else c
            idx = 4 * px + 2 * py + pc
            for a in range(n):
                sems = dict(send_sem=send_sems.at[7 * a + d - 1], recv_sem=recv_sems.at[7 * a + d - 1],
                            device_id=(px, py, pc), device_id_type=MESH)
                self.sends.append(pltpu.make_async_remote_copy(
                    src_ref=ins[a].at[idx], dst_ref=outs[a].at[me_idx], **sems))
                self.recvs.append(pltpu.make_async_remote_copy(
                    src_ref=ins[a].at[idx], dst_ref=outs[a].at[idx], **sems))

    @staticmethod
    def semaphores(n):
        return [pltpu.SemaphoreType.DMA((7 * n,)), pltpu.SemaphoreType.DMA((7 * n,)), pltpu.SemaphoreType.DMA((n,))]

    def start(self):
        for cp in self.mine + self.sends:
            cp.start()

    def wait(self):
        for cp in self.recvs:
            cp.wait_recv()
        for cp in self.sends:
            cp.wait_send()
        for cp in self.mine:
            cp.wait()


def _host(hosted):
    n = len(hosted)
    return ([ANY] * n, [ANY] * n, [jax.ShapeDtypeStruct(p.shape, p.dtype) for p in hosted],
            _Exchange.semaphores(n) if n else [])


def _split_refs(refs, *counts):
    out, at = [], 0
    for k in counts:
        out.append(refs[at:at + k])
        at += k
    return out + [refs[at:]]


def _in_proj(xb, w_in, b_in, layer):
    T = xb.shape[0]
    tm, tn = min(T, 1024), SEG

    def body(x_ref, w_ref, b_ref, u_ref):
        u_ref[...] = _dot(x_ref[...], w_ref[...]) + b_ref[...]

    return pl.pallas_call(
        body, name="in_proj", grid=(T // tm, D_IN // tn),
        in_specs=[pl.BlockSpec((tm, D_MODEL), lambda m, n: (m, 0)),
                  pl.BlockSpec((None, D_MODEL, tn), lambda m, n: (layer, 0, n)),
                  pl.BlockSpec((None, 1, tn), lambda m, n: (layer, 0, n))],
        out_specs=pl.BlockSpec((tm, tn), lambda m, n: (m, n)),
        out_shape=jax.ShapeDtypeStruct((T, D_IN), F32),
        compiler_params=_params("parallel", "parallel"),
    )(xb, w_in, b_in)


def _du_specs(tm, row_of, seg_of):
    return [pl.BlockSpec((None, tm, SEG), lambda *g: (jnp.minimum(seg_of(*g), N_QKV_SEG - 1), row_of(*g), 0)),
            pl.BlockSpec((tm, SEG), lambda *g: (row_of(*g), jnp.maximum(seg_of(*g) - N_QKV_SEG, 0)))]


def _in_proj_dx(dqkv, du_rest, w_in, dres, layer, hosted=()):
    T = du_rest.shape[0]
    tm = min(T, 512)
    nm = T // tm
    n_h = len(hosted)
    h_in, h_out, h_shape, h_sems = _host(hosted)

    def body(*refs):
        (a1_ref, a2_ref, w_ref, r_ref), h_ins, (o_ref,), h_outs, sems = _split_refs(refs, 4, n_h, 1, n_h)
        m, s = pl.program_id(0), pl.program_id(1)

        if n_h:
            @pl.when(jnp.logical_and(m == 0, s == 0))
            def _():
                _Exchange(h_ins, h_outs, *sems).start()

        @pl.when(s == 0)
        def _():
            o_ref[...] = r_ref[...]

        @pl.when(s < N_QKV_SEG)
        def _():
            o_ref[...] += _dot_nt(a1_ref[...].astype(BF16), w_ref[...])

        @pl.when(s >= N_QKV_SEG)
        def _():
            o_ref[...] += _dot_nt(a2_ref[...].astype(BF16), w_ref[...])

        if n_h:
            @pl.when(jnp.logical_and(m == nm - 1, s == N_SEG - 1))
            def _():
                _Exchange(h_ins, h_outs, *sems).wait()

    out = pl.pallas_call(
        body, name="in_proj_dx", grid=(nm, N_SEG),
        in_specs=_du_specs(tm, lambda m, s: m, lambda m, s: s) + [
            pl.BlockSpec((None, D_MODEL, SEG), lambda m, s: (layer, 0, s)),
            pl.BlockSpec((tm, D_MODEL), lambda m, s: (m, 0))] + h_in,
        out_specs=[pl.BlockSpec((tm, D_MODEL), lambda m, s: (m, 0))] + h_out,
        out_shape=[jax.ShapeDtypeStruct((T, D_MODEL), F32)] + h_shape,
        scratch_shapes=h_sems,
        compiler_params=_params("arbitrary", "arbitrary"),
    )(dqkv, du_rest, w_in, dres, *hosted)
    return out[0], out[1:]


def _in_proj_dw(xb, dqkv, du_rest):
    T = xb.shape[0]
    tk = min(T, 512)

    def body(x_ref, a1_ref, a2_ref, dw_ref, db_ref):
        s, t = pl.program_id(0), pl.program_id(1)

        @pl.when(t == 0)
        def _():
            dw_ref[...] = jnp.zeros_like(dw_ref)
            db_ref[...] = jnp.zeros_like(db_ref)

        def acc(a):
            dw_ref[...] += _dot_tn(x_ref[...], a.astype(BF16))
            db_ref[...] += jnp.sum(a, axis=0, keepdims=True)

        @pl.when(s < N_QKV_SEG)
        def _():
            acc(a1_ref[...])

        @pl.when(s >= N_QKV_SEG)
        def _():
            acc(a2_ref[...])

    return pl.pallas_call(
        body, name="in_proj_dw", grid=(N_SEG, T // tk),
        in_specs=[pl.BlockSpec((tk, D_MODEL), lambda s, t: (t, 0))] + _du_specs(tk, lambda s, t: t, lambda s, t: s),
        out_specs=[pl.BlockSpec((D_MODEL, SEG), lambda s, t: (0, s)),
                   pl.BlockSpec((None, 1, SEG), lambda s, t: (s, 0, 0))],
        out_shape=[jax.ShapeDtypeStruct((D_MODEL, D_IN), F32), jax.ShapeDtypeStruct((N_SEG, 1, SEG), F32)],
        compiler_params=_params("parallel", "arbitrary"),
    )(xb, dqkv, du_rest)


def _split_dot(val, rhs_twice):
    hi = val.astype(BF16)
    lo = (val - hi.astype(F32)).astype(BF16)
    return _dot(jnp.concatenate([hi, lo], axis=1), rhs_twice)


def _twice(rhs):
    return jnp.concatenate([rhs, rhs], axis=0)


def _log_sigmoids(z):
    sp = jnp.log(1.0 + jnp.exp(-jnp.abs(z)))
    return jnp.minimum(z, 0.0) - sp, jnp.minimum(-z, 0.0) - sp


def _attn_masks():
    row = lax.broadcasted_iota(jnp.int32, (ATT_BLOCK, ATT_BLOCK), 0)
    col = lax.broadcasted_iota(jnp.int32, (ATT_BLOCK, ATT_BLOCK), 1)
    return col < row, (row > col).astype(BF16), (row >= col).astype(BF16)


EXP_UNDERFLOW = -104.0


def _sweep_keys(i, tile, rsum):
    tile(i, True)

    def live(carry):
        j, top = carry
        return jnp.logical_and(j >= 0, top > EXP_UNDERFLOW)

    def step(carry):
        j, _ = carry
        tile(j, False)
        return j - 1, jnp.max(rsum[...])

    lax.while_loop(live, step, (i - 1, jnp.max(rsum[...])))


def _attn_fwd(u):
    T = u.shape[0]
    B = ATT_BLOCK
    nq = T // B

    def body(q_ref, k_ref, v_ref, o_ref, kb, vb, acc, rsum):
        i = pl.program_id(1)

        @pl.when(i == 0)
        def _():
            for h in range(2):
                kb[h] = k_ref[:, HEAD_DIM * h:HEAD_DIM * (h + 1)].astype(BF16)
                vb[h] = v_ref[:, HEAD_DIM * h:HEAD_DIM * (h + 1)].astype(BF16)

        causal, tri, _ = _attn_masks()
        tri2 = _twice(tri)
        hs = [slice(HEAD_DIM * h, HEAD_DIM * (h + 1)) for h in range(2)]
        q = [(q_ref[:, hs[h]] * ATT_SCALE).astype(BF16) for h in range(2)]
        acc[...] = jnp.zeros_like(acc)
        rsum[...] = jnp.zeros_like(rsum)

        def tile(j, masked):
            rows = pl.ds(pl.multiple_of(j * B, B), B)
            for h in range(2):
                z = _dot_nt(q[h], kb[h, rows, :])
                a, f = _log_sigmoids(z)
                if masked:
                    f = jnp.where(causal, f, 0.0)
                later = _split_dot(f, tri2)
                w = jnp.exp(a + later + rsum[h])
                if masked:
                    w = jnp.where(causal, w, 0.0)
                acc[h] += _split_dot(w, _twice(vb[h, rows, :]))
                rsum[h] += later[:, 0:1] + f[:, 0:1]

        _sweep_keys(i, tile, rsum)
        for h in range(2):
            o_ref[:, hs[h]] = acc[h]

    return pl.pallas_call(
        body, name="attn_fwd", grid=(D_ATT // 128, nq),
        in_specs=[pl.BlockSpec((B, 128), lambda p, i: (i, p)),
                  pl.BlockSpec((T, 128), lambda p, i: (0, 4 + p)),
                  pl.BlockSpec((T, 128), lambda p, i: (0, 8 + p))],
        out_specs=pl.BlockSpec((B, 128), lambda p, i: (i, p)),
        out_shape=jax.ShapeDtypeStruct((T, D_ATT), F32),
        scratch_shapes=[pltpu.VMEM((2, T, HEAD_DIM), BF16), pltpu.VMEM((2, T, HEAD_DIM), BF16),
                        pltpu.VMEM((2, B, HEAD_DIM), F32), pltpu.VMEM((2, B, 1), F32)],
        compiler_params=_params("parallel", "arbitrary"),
    )(u, u, u)


def _attn_bwd(u, att, datt, hosted=()):
    T = u.shape[0]
    B = ATT_BLOCK
    nq = T // B
    n_pairs = D_ATT // 128
    n_h = len(hosted)
    h_in, h_out, h_shape, h_sems = _host(hosted)

    def body(*refs):
        (q_ref, k_ref, v_ref, o_ref, do_ref), h_ins, (dqkv_ref,), h_outs, scratch = _split_refs(refs, 5, n_h, 1, n_h)
        kb, vb, dkacc, dvacc, dqacc, rsum, gsum = scratch[:7]
        p, i = pl.program_id(0), pl.program_id(1)

        if n_h:
            @pl.when(jnp.logical_and(p == 0, i == 0))
            def _():
                _Exchange(h_ins, h_outs, *scratch[7:]).start()

        @pl.when(i == 0)
        def _():
            for h in range(2):
                kb[h] = k_ref[:, HEAD_DIM * h:HEAD_DIM * (h + 1)].astype(BF16)
                vb[h] = v_ref[:, HEAD_DIM * h:HEAD_DIM * (h + 1)].astype(BF16)
            dkacc[...] = jnp.zeros_like(dkacc)
            dvacc[...] = jnp.zeros_like(dvacc)

        causal, tri, tri_incl = _attn_masks()
        tri2, tri_incl2 = _twice(tri), _twice(tri_incl)
        qrows = pl.ds(pl.multiple_of(i * B, B), B)
        hs = [slice(HEAD_DIM * h, HEAD_DIM * (h + 1)) for h in range(2)]
        q = [(q_ref[:, hs[h]] * ATT_SCALE).astype(BF16) for h in range(2)]
        dob = [do_ref[:, hs[h]].astype(BF16) for h in range(2)]
        total = [jnp.sum(dob[h].astype(F32) * o_ref[:, hs[h]], axis=1, keepdims=True) for h in range(2)]
        dqacc[...] = jnp.zeros_like(dqacc)
        rsum[...] = jnp.zeros_like(rsum)
        gsum[...] = jnp.zeros_like(gsum)

        def tile(j, masked):
            rows = pl.ds(pl.multiple_of(j * B, B), B)
            for h in range(2):
                kj, vj = kb[h, rows, :], vb[h, rows, :]
                z = _dot_nt(q[h], kj)
                a, f = _log_sigmoids(z)
                if masked:
                    f = jnp.where(causal, f, 0.0)
                later = _split_dot(f, tri2)
                w = jnp.exp(a + later + rsum[h])
                if masked:
                    w = jnp.where(causal, w, 0.0)
                g = _dot_nt(dob[h], vj) * w
                suffix = _split_dot(g, tri_incl2)
                before = total[h] - (suffix + gsum[h])
                beta = jnp.exp(a)
                dz = g * (1.0 - beta) - before * beta
                if masked:
                    dz = jnp.where(causal, dz, 0.0)
                dzb = dz.astype(BF16)
                dqacc[h] += _dot(dzb, kj)
                dkacc[h, rows, :] += _dot_tn(dzb, q[h])
                dvacc[h, rows, :] += _dot_tn(w.astype(BF16), dob[h])
                rsum[h] += later[:, 0:1] + f[:, 0:1]
                gsum[h] += suffix[:, 0:1]

        _sweep_keys(i, tile, rsum)
        for h in range(2):
            dqkv_ref[0, qrows, hs[h]] = dqacc[h] * ATT_SCALE

        @pl.when(i == nq - 1)
        def _():
            for h in range(2):
                dqkv_ref[1, :, hs[h]] = dkacc[h]
                dqkv_ref[2, :, hs[h]] = dvacc[h]

        if n_h:
            @pl.when(jnp.logical_and(p == n_pairs - 1, i == nq - 1))
            def _():
                _Exchange(h_ins, h_outs, *scratch[7:]).wait()

    out = pl.pallas_call(
        body, name="attn_bwd", grid=(n_pairs, nq),
        in_specs=[pl.BlockSpec((B, 128), lambda p, i: (i, p)),
                  pl.BlockSpec((T, 128), lambda p, i: (0, 4 + p)),
                  pl.BlockSpec((T, 128), lambda p, i: (0, 8 + p)),
                  pl.BlockSpec((B, 128), lambda p, i: (i, p)),
                  pl.BlockSpec((B, 128), lambda p, i: (i, p))] + h_in,
        out_specs=[pl.BlockSpec((3, T, 128), lambda p, i: (0, 0, p))] + h_out,
        out_shape=[jax.ShapeDtypeStruct((3, T, D_ATT), F32)] + h_shape,
        scratch_shapes=[pltpu.VMEM((2, T, HEAD_DIM), BF16), pltpu.VMEM((2, T, HEAD_DIM), BF16),
                        pltpu.VMEM((2, T, HEAD_DIM), F32), pltpu.VMEM((2, T, HEAD_DIM), F32),
                        pltpu.VMEM((2, B, HEAD_DIM), F32), pltpu.VMEM((2, B, 1), F32),
                        pltpu.VMEM((2, B, 1), F32)] + h_sems,
        compiler_params=_params("arbitrary", "arbitrary"),
    )(u, u, u, att, datt, *hosted)
    return out[0], out[1:]


MIX_TILE = 128


def _u_specs(tm, tile_of):
    return [pl.BlockSpec((tm, SEG), functools.partial(lambda k, *g: (tile_of(*g), k), k)) for k in range(3, N_SEG)]


def _conv(ext, cw_ref, tm):
    acc = cw_ref[0:1, :] * ext[2:2 + tm, :]
    for k in range(1, CONV_WIDTH):
        acc = acc + cw_ref[k:k + 1, :] * ext[2 + k:2 + k + tm, :]
    return acc


def _norm_stats(v):
    mu = jnp.mean(v, axis=-1, keepdims=True)
    vc = v - mu
    rstd = lax.rsqrt(jnp.mean(vc * vc, axis=-1, keepdims=True) + LN_EPS)
    return vc * rstd, rstd


def _norm_bwd(dy_scaled, xhat, rstd):
    return rstd * (dy_scaled - jnp.mean(dy_scaled, axis=-1, keepdims=True)
                   - xhat * jnp.mean(dy_scaled * xhat, axis=-1, keepdims=True))


def _mix_fwd(x, u, att, wap, wcp, wout, cw, cvec, dvec, layer):
    T = x.shape[0]
    tm = MIX_TILE

    def body(x_ref, za_ref, ga_ref, gb_ref, zc_ref, ga0_ref, ga1_ref, gc0_ref, gc1_ref, att_ref,
             wap_ref, wcp_ref, wout_ref, cw_ref, cvec_ref, dvec_ref, xn_ref, xnb_ref, h_ref, ext):
        i = pl.program_id(0)

        @pl.when(i == 0)
        def _():
            ext[0:CONV_PAD, :] = jnp.zeros((CONV_PAD, D_CONV), F32)

        za = za_ref[...]
        attg = att_ref[...] * (za * _sig(za))
        ab = _dot(attg.astype(BF16), wap_ref[...])

        ext[CONV_PAD:CONV_PAD + tm, :] = ga_ref[...] * _sig(gb_ref[...])
        c1 = _conv(ext, cw_ref, tm) + cvec_ref[0:1, :]
        ext[0:CONV_PAD, :] = ext[tm:tm + CONV_PAD, :]
        xh, _ = _norm_stats(c1)
        c2 = xh * cvec_ref[1:2, :] + cvec_ref[2:3, :]
        zc = zc_ref[...]
        cz = (c2 * _sig(c2)) * (zc * _sig(zc))
        cb = _dot(cz.astype(BF16), wcp_ref[...]) + dvec_ref[0:1, :]

        g_att = jnp.concatenate([ga0_ref[...], ga1_ref[...]], axis=1)
        g_conv = jnp.concatenate([gc0_ref[...], gc1_ref[...]], axis=1)
        merged = _sig(g_att) * ab + _sig(g_conv) * cb
        h = DEEPNORM_ALPHA * x_ref[...] + _dot(merged.astype(BF16), wout_ref[...])
        h_ref[...] = h
        hh, _ = _norm_stats(h)
        xn = hh * dvec_ref[1:2, :] + dvec_ref[2:3, :]
        xn_ref[...] = xn
        xnb_ref[...] = xn.astype(BF16)

    row = pl.BlockSpec((tm, D_MODEL), lambda i: (i, 0))
    return pl.pallas_call(
        body, name="mix_fwd", grid=(T // tm,),
        in_specs=[row] + _u_specs(tm, lambda i: i) + [
            pl.BlockSpec((tm, D_ATT), lambda i: (i, 0)),
            pl.BlockSpec((None, D_ATT, D_MODEL), lambda i: (layer, 0, 0)),
            pl.BlockSpec((None, D_CONV, D_MODEL), lambda i: (layer, 0, 0)),
            pl.BlockSpec((None, D_MODEL, D_MODEL), lambda i: (layer, 0, 0)),
            pl.BlockSpec((None, CONV_PAD, D_CONV), lambda i: (layer, 0, 0)),
            pl.BlockSpec((None, 8, D_CONV), lambda i: (layer, 0, 0)),
            pl.BlockSpec((None, 8, D_MODEL), lambda i: (layer, 0, 0))],
        out_specs=[row, row, row],
        out_shape=[jax.ShapeDtypeStruct((T, D_MODEL), F32), jax.ShapeDtypeStruct((T, D_MODEL), BF16),
                   jax.ShapeDtypeStruct((T, D_MODEL), F32)],
        scratch_shapes=[pltpu.VMEM((tm + CONV_PAD, D_CONV), F32)],
        compiler_params=_params("arbitrary"),
    )(x, *([u] * 8), att, wap, wcp, wout, cw, cvec, dvec)


def _mix_bwd(dxo, h, u, att, wap, wcp, wout, cw, cvec, dvec, layer, hosted=()):
    T = dxo.shape[0]
    tm = MIX_TILE
    nt = T // tm
    halo_per_tile = tm // CONV_PAD
    n_h = len(hosted)
    h_in, h_out, h_shape, h_sems = _host(hosted)

    def body(*refs):
        ins, h_ins, outs, h_outs, scratch = _split_refs(refs, 19, n_h, 9, n_h)
        (dxo_ref, h_ref, za_ref, ga_ref, gb_ref, zc_ref, ga0_ref, ga1_ref, gc0_ref, gc1_ref, hga_ref, hgb_ref,
         att_ref, wap_ref, wcp_ref, wout_ref, cw_ref, cvec_ref, dvec_ref) = ins
        datt_ref, dur_ref, dres_ref, dwap_ref, dwcp_ref, dwout_ref, dcw_ref, dcvec_ref, ddvec_ref = outs
        ext, dext = scratch[:2]
        r = pl.program_id(0)
        i = nt - 1 - r

        if n_h:
            @pl.when(r == 0)
            def _():
                _Exchange(h_ins, h_outs, *scratch[2:]).start()

        @pl.when(r == 0)
        def _():
            for ref in (dwap_ref, dwcp_ref, dwout_ref, dcw_ref, dcvec_ref, ddvec_ref):
                ref[...] = jnp.zeros_like(ref)
            dext[tm:tm + CONV_PAD, :] = jnp.zeros((CONV_PAD, D_CONV), F32)

        za = za_ref[...]
        s_za = _sig(za)
        silu_za = za * s_za
        att_t = att_ref[...]
        attg = (att_t * silu_za).astype(BF16)
        ab = _dot(attg, wap_ref[...])

        ga, gb = ga_ref[...], gb_ref[...]
        s_gb = _sig(gb)
        halo = hga_ref[...] * _sig(hgb_ref[...])
        ext[0:CONV_PAD, :] = jnp.where(i > 0, halo, 0.0)
        ext[CONV_PAD:CONV_PAD + tm, :] = ga * s_gb
        c1 = _conv(ext, cw_ref, tm) + cvec_ref[0:1, :]
        xh, rstd_c = _norm_stats(c1)
        c2 = xh * cvec_ref[1:2, :] + cvec_ref[2:3, :]
        s_c2 = _sig(c2)
        c3 = c2 * s_c2
        zc = zc_ref[...]
        s_zc = _sig(zc)
        cz = (c3 * (zc * s_zc)).astype(BF16)
        cb = _dot(cz, wcp_ref[...]) + dvec_ref[0:1, :]

        s_ga = _sig(jnp.concatenate([ga0_ref[...], ga1_ref[...]], axis=1))
        s_gc = _sig(jnp.concatenate([gc0_ref[...], gc1_ref[...]], axis=1))
        merged = (s_ga * ab + s_gc * cb).astype(BF16)

        dxo_t = dxo_ref[...]
        hh, rstd_h = _norm_stats(h_ref[...])
        ddvec_ref[1:2, :] += jnp.sum(dxo_t * hh, axis=0, keepdims=True)
        ddvec_ref[2:3, :] += jnp.sum(dxo_t, axis=0, keepdims=True)
        dh = _norm_bwd(dxo_t * dvec_ref[1:2, :], hh, rstd_h)
        dres_ref[...] = DEEPNORM_ALPHA * dh
        dy = dh.astype(BF16)

        dwout_ref[...] += _dot_tn(merged, dy)
        dm = _dot_nt(dy, wout_ref[...])
        dab = (dm * s_ga).astype(BF16)
        dcb = dm * s_gc
        dur_ref[:, 4 * SEG:6 * SEG] = dm * ab * (s_ga * (1.0 - s_ga))
        dur_ref[:, 6 * SEG:8 * SEG] = dm * cb * (s_gc * (1.0 - s_gc))

        dwap_ref[...] += _dot_tn(attg, dab)
        dattg = _dot_nt(dab, wap_ref[...])
        datt_ref[...] = dattg * silu_za
        dur_ref[:, 0:SEG] = dattg * att_t * (s_za * (1.0 + za * (1.0 - s_za)))

        ddvec_ref[0:1, :] += jnp.sum(dcb, axis=0, keepdims=True)
        dcbb = dcb.astype(BF16)
        dwcp_ref[...] += _dot_tn(cz, dcbb)
        dcz = _dot_nt(dcbb, wcp_ref[...])
        dur_ref[:, 3 * SEG:4 * SEG] = dcz * c3 * (s_zc * (1.0 + zc * (1.0 - s_zc)))
        dc2 = dcz * (zc * s_zc) * (s_c2 * (1.0 + c2 * (1.0 - s_c2)))
        dcvec_ref[1:2, :] += jnp.sum(dc2 * xh, axis=0, keepdims=True)
        dcvec_ref[2:3, :] += jnp.sum(dc2, axis=0, keepdims=True)
        dc1 = _norm_bwd(dc2 * cvec_ref[1:2, :], xh, rstd_c)
        dcvec_ref[0:1, :] += jnp.sum(dc1, axis=0, keepdims=True)
        for k in range(CONV_WIDTH):
            dcw_ref[k:k + 1, :] += jnp.sum(dc1 * ext[2 + k:2 + k + tm, :], axis=0, keepdims=True)
        dext[0:tm, :] = dc1
        dc0 = cw_ref[0:1, :] * dext[CONV_WIDTH - 1:CONV_WIDTH - 1 + tm, :]
        for k in range(1, CONV_WIDTH):
            dc0 = dc0 + cw_ref[k:k + 1, :] * dext[CONV_WIDTH - 1 - k:CONV_WIDTH - 1 - k + tm, :]
        dext[tm:tm + CONV_PAD, :] = dext[0:CONV_PAD, :]
        dur_ref[:, SEG:2 * SEG] = dc0 * s_gb
        dur_ref[:, 2 * SEG:3 * SEG] = dc0 * ga * (s_gb * (1.0 - s_gb))

        if n_h:
            @pl.when(r == nt - 1)
            def _():
                _Exchange(h_ins, h_outs, *scratch[2:]).wait()

    rev = lambda r: nt - 1 - r
    row = pl.BlockSpec((tm, D_MODEL), lambda r: (rev(r), 0))
    halo_spec = [pl.BlockSpec((CONV_PAD, SEG),
                              functools.partial(lambda k, r: (jnp.maximum(rev(r) * halo_per_tile - 1, 0), k), k))
                 for k in (4, 5)]
    const = lambda shape: pl.BlockSpec(shape, lambda r: (0,) * len(shape))
    out = pl.pallas_call(
        body, name="mix_bwd", grid=(nt,),
        in_specs=[row, row] + _u_specs(tm, rev) + halo_spec + [
            pl.BlockSpec((tm, D_ATT), lambda r: (rev(r), 0)),
            pl.BlockSpec((None, D_ATT, D_MODEL), lambda r: (layer, 0, 0)),
            pl.BlockSpec((None, D_CONV, D_MODEL), lambda r: (layer, 0, 0)),
            pl.BlockSpec((None, D_MODEL, D_MODEL), lambda r: (layer, 0, 0)),
            pl.BlockSpec((None, CONV_PAD, D_CONV), lambda r: (layer, 0, 0)),
            pl.BlockSpec((None, 8, D_CONV), lambda r: (layer, 0, 0)),
            pl.BlockSpec((None, 8, D_MODEL), lambda r: (layer, 0, 0))] + h_in,
        out_specs=[pl.BlockSpec((tm, D_ATT), lambda r: (rev(r), 0)),
                   pl.BlockSpec((tm, 8 * SEG), lambda r: (rev(r), 0)),
                   row,
                   const((D_ATT, D_MODEL)), const((D_CONV, D_MODEL)), const((D_MODEL, D_MODEL)),
                   const((CONV_PAD, D_CONV)), const((8, D_CONV)), const((8, D_MODEL))] + h_out,
        out_shape=[jax.ShapeDtypeStruct((T, D_ATT), F32), jax.ShapeDtypeStruct((T, 8 * SEG), F32),
                   jax.ShapeDtypeStruct((T, D_MODEL), F32),
                   jax.ShapeDtypeStruct((D_ATT, D_MODEL), F32), jax.ShapeDtypeStruct((D_CONV, D_MODEL), F32),
                   jax.ShapeDtypeStruct((D_MODEL, D_MODEL), F32), jax.ShapeDtypeStruct((CONV_PAD, D_CONV), F32),
                   jax.ShapeDtypeStruct((8, D_CONV), F32), jax.ShapeDtypeStruct((8, D_MODEL), F32)] + h_shape,
        scratch_shapes=[pltpu.VMEM((tm + CONV_PAD, D_CONV), F32),
                        pltpu.VMEM((tm + CONV_PAD, D_CONV), F32)] + h_sems,
        compiler_params=_params("arbitrary"),
    )(dxo, h, *([u] * 10), att, wap, wcp, wout, cw, cvec, dvec, *hosted)
    return out[:9], out[9:]


def _loss_head(y, target):
    T = y.shape[0]
    tm = min(T, 512)

    def body(y_ref, t_ref, dy_ref, loss_ref):
        @pl.when(pl.program_id(0) == 0)
        def _():
            loss_ref[...] = jnp.zeros_like(loss_ref)

        err = y_ref[...] - t_ref[...]
        dy_ref[...] = err * (1.0 / D_MODEL)
        loss_ref[...] += 0.5 * jnp.sum(jnp.mean(err * err, axis=-1, keepdims=True))

    row = pl.BlockSpec((tm, D_MODEL), lambda i: (i, 0))
    return pl.pallas_call(
        body, name="loss_head", grid=(T // tm,), in_specs=[row, row],
        out_specs=[row, pl.BlockSpec((8, 128), lambda i: (0, 0))],
        out_shape=[jax.ShapeDtypeStruct((T, D_MODEL), F32), jax.ShapeDtypeStruct((8, 128), F32)],
        compiler_params=_params("arbitrary"),
    )(y, target)


def _adamw(w, g, m, v):
    m = ADAM_B1 * m + (1.0 - ADAM_B1) * g
    v = ADAM_B2 * v + (1.0 - ADAM_B2) * (g * g)
    m_hat = m / (1.0 - ADAM_B1 ** ADAM_STEP)
    v_hat = v / (1.0 - ADAM_B2 ** ADAM_STEP)
    return -ADAM_LR * (m_hat / (jnp.sqrt(v_hat) + ADAM_EPS) + ADAM_WD * w), m, v


def _sum_slots(ref):
    total = ref[0].astype(F32)
    for j in range(1, N_DEV):
        total = total + ref[j].astype(F32)
    return total


def _adamw_sharded(recv, w, m, v, prev, layer, tr, name):
    n_layers, R, C = w.shape

    def body(*refs):
        recv_ref, w_ref, m_ref, v_ref = refs[:4]
        g_ref, d_ref, nm_ref, nv_ref = refs[-4:]
        g = _sum_slots(recv_ref)
        g_ref[...] = g
        d_ref[...], nm_ref[...], nv_ref[...] = _adamw(w_ref[...], g, m_ref[...], v_ref[...])

    slab = pl.BlockSpec((None, tr, C), lambda i: (layer, i, 0))
    n_prev = 0 if prev is None else 4
    return pl.pallas_call(
        body, name=name, grid=(R // tr,),
        in_specs=[pl.BlockSpec((N_DEV, tr, C), lambda i: (0, i, 0)), slab, slab, slab] + [ANY] * n_prev,
        out_specs=[slab] * 4,
        out_shape=[jax.ShapeDtypeStruct(w.shape, F32)] * 4,
        input_output_aliases={4 + k: k for k in range(n_prev)},
        compiler_params=_params("parallel"),
    )(recv, w, m, v, *(prev or ()))


def _sum8(g):
    R = g.shape[1]

    def body(g_ref, o_ref):
        o_ref[...] = _sum_slots(g_ref)

    return pl.pallas_call(body, name="sum_small_grads", out_shape=jax.ShapeDtypeStruct((R, 128), F32))(g)


def _adamw_small(w, g, m, v):
    def body(w_ref, g_ref, m_ref, v_ref, d_ref, nm_ref, nv_ref):
        d_ref[...], nm_ref[...], nv_ref[...] = _adamw(w_ref[...], g_ref[...], m_ref[...], v_ref[...])

    return pl.pallas_call(body, name="adamw_small", out_shape=[jax.ShapeDtypeStruct(w.shape, F32)] * 3)(w, g, m, v)


REPLICATED = (("b_in", D_IN), ("conv_b", D_CONV), ("conv_ln_g", D_CONV), ("conv_ln_b", D_CONV),
              ("b_conv_proj", D_MODEL), ("ln_g", D_MODEL), ("ln_b", D_MODEL))
N_REPLICATED = sum(n for _, n in REPLICATED)
CONV_W_SHARD_PAD = 2048


def _pack_small(rep, conv_w_shard):
    flat = jnp.concatenate([rep[k] for k, _ in REPLICATED], axis=1)
    cws = jnp.pad(conv_w_shard.reshape(DEPTH, -1), ((0, 0), (0, CONV_W_SHARD_PAD - CONV_WIDTH * 64)))
    return jnp.concatenate([flat, cws], axis=1).reshape(-1, 128)


def _unpack_small(packed):
    per = packed.reshape(DEPTH, N_REPLICATED + CONV_W_SHARD_PAD)
    out, off = {}, 0
    for k, n in REPLICATED:
        out[k] = per[:, off:off + n]
        off += n
    out["conv_w"] = per[:, off:off + CONV_WIDTH * 64].reshape(DEPTH, CONV_WIDTH, 64)
    return out


def kernel(x, w_in, b_in, conv_w, conv_b, conv_ln_g, conv_ln_b, w_att_proj, w_conv_proj, b_conv_proj, w_out, ln_g, ln_b, loss_target, m_w_in, m_b_in, m_conv_w, m_conv_b, m_conv_ln_g, m_conv_ln_b, m_w_att_proj, m_w_conv_proj, m_b_conv_proj, m_w_out, m_ln_g, m_ln_b, v_w_in, v_b_in, v_conv_w, v_conv_b, v_conv_ln_g, v_conv_ln_b, v_w_att_proj, v_w_conv_proj, v_b_conv_proj, v_w_out, v_ln_g, v_ln_b):
    x0 = x[0]
    target = loss_target[0]

    g_in, g_ap, g_cp, g_out, g_cw = _all_gather(
        [w_in.astype(BF16), w_att_proj.astype(BF16), w_conv_proj.astype(BF16), w_out.astype(BF16), conv_w],
        "gather_weights")
    w_in_full = g_in.transpose(1, 2, 0, 3).reshape(DEPTH, D_MODEL, D_IN)
    wap = g_ap.transpose(1, 2, 0, 3).reshape(DEPTH, D_ATT, D_MODEL)
    wcp = g_cp.transpose(1, 2, 0, 3).reshape(DEPTH, D_CONV, D_MODEL)
    wout = g_out.transpose(1, 0, 2, 3).reshape(DEPTH, D_MODEL, D_MODEL)
    cw = jnp.pad(g_cw.transpose(1, 2, 0, 3).reshape(DEPTH, CONV_WIDTH, D_CONV), ((0, 0), (0, 1), (0, 0)))
    pad8 = lambda rows: jnp.pad(jnp.stack(rows, axis=1), ((0, 0), (0, 8 - len(rows)), (0, 0)))
    cvec = pad8([conv_b, conv_ln_g, conv_ln_b])
    dvec = pad8([b_conv_proj, ln_g, ln_b])
    b_in3 = b_in.reshape(DEPTH, 1, D_IN)

    xs, xbs, us, atts, hs = [x0], [x0.astype(BF16)], [], [], []
    for l in range(DEPTH):
        u = _in_proj(xbs[l], w_in_full, b_in3, l)
        att = _attn_fwd(u)
        xn, xnb, h = _mix_fwd(xs[l], u, att, wap, wcp, wout, cw, cvec, dvec, l)
        us.append(u), atts.append(att), hs.append(h), xs.append(xn), xbs.append(xnb)
    dxo, loss_part = _loss_head(xs[DEPTH], target)
    loss = lax.psum(loss_part[0, 0], ("x", "y", "c"))

    big = {"w_in": (w_in, m_w_in, v_w_in, 256), "w_att_proj": (w_att_proj, m_w_att_proj, v_w_att_proj, D_ATT),
           "w_conv_proj": (w_conv_proj, m_w_conv_proj, v_w_conv_proj, D_CONV), "w_out": (w_out, m_w_out, v_w_out, 128)}
    results = {k: None for k in big}
    small = [None] * DEPTH

    def update(k, recv, l):
        w, m, v, tr = big[k]
        results[k] = _adamw_sharded(recv, w, m, v, results[k], l, tr, "adamw_" + k)

    pending_w_in = ()
    for l in reversed(range(DEPTH)):
        (datt, du_rest, dres, dwap, dwcp, dwout, dcw, dcvec, ddvec), got = _mix_bwd(
            dxo, hs[l], us[l], atts[l], wap, wcp, wout, cw, cvec, dvec, l, hosted=pending_w_in)
        if got:
            update("w_in", got[0], l + 1)
        proj = [dwap.reshape(D_ATT, N_DEV, D_MODEL // N_DEV).transpose(1, 0, 2).astype(BF16),
                dwcp.reshape(D_CONV, N_DEV, D_MODEL // N_DEV).transpose(1, 0, 2).astype(BF16),
                dwout.reshape(N_DEV, D_MODEL // N_DEV, D_MODEL).astype(BF16)]
        dqkv, got = _attn_bwd(us[l], atts[l], datt, hosted=proj)
        for k, r in zip(("w_att_proj", "w_conv_proj", "w_out"), got):
            update(k, r, l)
        dwin, dbin = _in_proj_dw(xbs[l], dqkv, du_rest)
        small[l] = jnp.concatenate([dbin.reshape(-1), dcvec[:3].reshape(-1), ddvec[:3].reshape(-1), dcw.reshape(-1)])
        pending_w_in = (dwin.reshape(D_MODEL, N_DEV, W_IN_SHARD).transpose(1, 0, 2).astype(BF16),)
        dxo, got = _in_proj_dx(dqkv, du_rest, w_in_full, dres, l, hosted=pending_w_in if l == 0 else ())
        if got:
            update("w_in", got[0], l)
    grad_x = dxo[None]

    small_all = jnp.stack(small).reshape(-1, 128)
    (gathered,) = _all_gather([small_all], "gather_small_grads")
    per_layer = _sum8(gathered).reshape(DEPTH, -1)
    g_rep, off = {}, 0
    for k, n in (("b_in", D_IN), ("conv_b", D_CONV), ("conv_ln_g", D_CONV), ("conv_ln_b", D_CONV),
                 ("b_conv_proj", D_MODEL), ("ln_g", D_MODEL), ("ln_b", D_MODEL)):
        g_rep[k] = per_layer[:, off:off + n]
        off += n
    g_cw_full = per_layer[:, off:].reshape(DEPTH, CONV_PAD, D_CONV)[:, :CONV_WIDTH]
    g_cw_shard = lax.dynamic_slice_in_dim(g_cw_full, _my_index() * 64, 64, axis=2)
    rep_w = dict(b_in=b_in, conv_b=conv_b, conv_ln_g=conv_ln_g, conv_ln_b=conv_ln_b, b_conv_proj=b_conv_proj,
                 ln_g=ln_g, ln_b=ln_b)
    rep_m = dict(b_in=m_b_in, conv_b=m_conv_b, conv_ln_g=m_conv_ln_g, conv_ln_b=m_conv_ln_b,
                 b_conv_proj=m_b_conv_proj, ln_g=m_ln_g, ln_b=m_ln_b)
    rep_v = dict(b_in=v_b_in, conv_b=v_conv_b, conv_ln_g=v_conv_ln_g, conv_ln_b=v_conv_ln_b,
                 b_conv_proj=v_b_conv_proj, ln_g=v_ln_g, ln_b=v_ln_b)
    sd, sm, sv = _adamw_small(_pack_small(rep_w, conv_w), _pack_small(g_rep, g_cw_shard),
                              _pack_small(rep_m, m_conv_w), _pack_small(rep_v, v_conv_w))
    small_out = [dict(g_rep, conv_w=g_cw_shard), _unpack_small(sd), _unpack_small(sm), _unpack_small(sv)]

    order = ["w_in", "b_in", "conv_w", "conv_b", "conv_ln_g", "conv_ln_b", "w_att_proj", "w_conv_proj",
             "b_conv_proj", "w_out", "ln_g", "ln_b"]
    outs = [loss, grad_x]
    for kind in range(4):
        outs += [results[k][kind] if k in big else small_out[kind][k] for k in order]
    return tuple(outs)
```

```python
import functools

import jax
import jax.numpy as jnp
from jax import lax
from jax.experimental import pallas as pl
from jax.experimental.pallas import tpu as pltpu

F32, BF16 = jnp.float32, jnp.bfloat16

D_MODEL = 1024
D_ATT = 512
D_CONV = 512
HEAD_DIM = 64
CONV_WIDTH = 31
CONV_PAD = 32
DEPTH = 4
N_DEV = 8
SEG = 512
D_IN = 4 * D_ATT + 3 * D_CONV + 2 * D_MODEL
N_SEG = D_IN // SEG
N_QKV_SEG = 3
W_IN_SHARD = D_IN // N_DEV
LN_EPS = 1e-5
DEEPNORM_ALPHA = (2 * DEPTH) ** 0.25
ATT_SCALE = HEAD_DIM ** -0.5
ATT_BLOCK = 256
ADAM_LR, ADAM_B1, ADAM_B2, ADAM_EPS, ADAM_WD, ADAM_STEP = 0.001, 0.9, 0.999, 1e-08, 0.01, 10
VMEM_LIMIT = 56 * 1024 * 1024
MESH = pl.DeviceIdType.MESH
ANY = pl.BlockSpec(memory_space=pl.ANY)


def _params(*semantics):
    return pltpu.CompilerParams(dimension_semantics=semantics, vmem_limit_bytes=VMEM_LIMIT)


def _sig(x):
    return 1.0 / (1.0 + jnp.exp(-x))


def _dot(a, b):
    return jnp.dot(a, b, preferred_element_type=F32)


def _dot_nt(a, b):
    return lax.dot_general(a, b, (((1,), (1,)), ((), ())), preferred_element_type=F32)


def _dot_tn(a, b):
    return lax.dot_general(a, b, (((0,), (0,)), ((), ())), preferred_element_type=F32)


def _my_index():
    return 4 * lax.axis_index("x") + 2 * lax.axis_index("y") + lax.axis_index("c")


class _Gather:
    def __init__(self, ins, outs, send_sems, recv_sems, local_sems):
        self.n = n = len(ins)
        x, y, c = lax.axis_index("x"), lax.axis_index("y"), lax.axis_index("c")
        me, sibling = (x, y, c), (x, y, 1 - c)
        chips = [(1 - x, y), (x, 1 - y), (1 - x, 1 - y)]

        def slot(a, dev):
            return outs[a].at[4 * dev[0] + 2 * dev[1] + dev[2]]

        def copy(a, k, block, to, src=None):
            return pltpu.make_async_remote_copy(
                src_ref=slot(a, block) if src is None else src, dst_ref=slot(a, block),
                send_sem=send_sems.at[7 * a + k], recv_sem=recv_sems.at[7 * a + k], device_id=to, device_id_type=MESH)

        self.mine = [pltpu.make_async_copy(ins[a], slot(a, me), local_sems.at[a]) for a in range(n)]
        self.first, self.landed, self.passed, self.last = [], [], [], []
        for a in range(n):
            self.first.append(copy(a, 0, me, sibling, src=ins[a]))
            self.first += [copy(a, 1 + j, me, (*chip, c), src=ins[a]) for j, chip in enumerate(chips)]
        for j, chip in enumerate(chips):
            for a in range(n):
                self.landed.append(copy(a, 1 + j, (*chip, c), me))
                self.passed.append(copy(a, 4 + j, (*chip, c), sibling))
        for a in range(n):
            self.last.append(copy(a, 0, sibling, me))
            self.last += [copy(a, 4 + j, (*chip, 1 - c), me) for j, chip in enumerate(chips)]

    @staticmethod
    def semaphores(n):
        return [pltpu.SemaphoreType.DMA((7 * n,)), pltpu.SemaphoreType.DMA((7 * n,)), pltpu.SemaphoreType.DMA((n,))]

    def start(self):
        for cp in self.mine + self.first:
            cp.start()

    def forward(self):
        for landed, passed in zip(self.landed, self.passed):
            landed.wait_recv()
            passed.start()

    def wait(self):
        for cp in self.last:
            cp.wait_recv()
        for cp in self.first + self.passed:
            cp.wait_send()
        for cp in self.mine:
            cp.wait()


def _gathered_shapes(shards):
    return [jax.ShapeDtypeStruct((N_DEV,) + s.shape, s.dtype) for s in shards]


def _all_gather(shards, name):
    n = len(shards)

    def body(*refs):
        g = _Gather(refs[:n], refs[n:2 * n], *refs[2 * n:])
        g.start()
        g.forward()
        g.wait()

    return pl.pallas_call(
        body, name=name, out_shape=_gathered_shapes(shards),
        in_specs=[ANY] * n, out_specs=[ANY] * n, scratch_shapes=_Gather.semaphores(n),
    )(*shards)


class _Exchange:
    def __init__(self, ins, outs, send_sems, recv_sems, local_sems):
        n = len(ins)
        x, y, c = lax.axis_index("x"), lax.axis_index("y"), lax.axis_index("c")
        me_idx = 4 * x + 2 * y + c
        self.mine = [pltpu.make_async_copy(ins[a].at[me_idx], outs[a].at[me_idx], local_sems.at[a])
                     for a in range(n)]
        self.sends, self.recvs = [], []
        for d in (1, 2, 4, 3, 5, 6, 7):
            px = 1 - x if d & 4 else x
            py = 1 - y if d & 2 else y
            pc = 1 - c if d & 1 else c
            idx = 4 * px + 2 * py + pc
            for a in range(n):
                sems = dict(send_sem=send_sems.at[7 * a + d - 1], recv_sem=recv_sems.at[7 * a + d - 1],
                            device_id=(px, py, pc), device_id_type=MESH)
                self.sends.append(pltpu.make_async_remote_copy(
                    src_ref=ins[a].at[idx], dst_ref=outs[a].at[me_idx], **sems))
                self.recvs.append(pltpu.make_async_remote_copy(
                    src_ref=ins[a].at[idx], dst_ref=outs[a].at[idx], **sems))

    @staticmethod
    def semaphores(n):
        return [pltpu.SemaphoreType.DMA((7 * n,)), pltpu.SemaphoreType.DMA((7 * n,)), pltpu.SemaphoreType.DMA((n,))]

    def start(self):
        for cp in self.mine + self.sends:
            cp.start()

    def wait(self):
        for cp in self.recvs:
            cp.wait_recv()
        for cp in self.sends:
            cp.wait_send()
        for cp in self.mine:
            cp.wait()


def _host(hosted):
    n = len(hosted)
    return ([ANY] * n, [ANY] * n, [jax.ShapeDtypeStruct(p.shape, p.dtype) for p in hosted],
            _Exchange.semaphores(n) if n else [])


def _split_refs(refs, *counts):
    out, at = [], 0
    for k in counts:
        out.append(refs[at:at + k])
        at += k
    return out + [refs[at:]]


def _in_proj(xb, w_in, b_in, layer):
    T = xb.shape[0]
    tm, tn = min(T, 1024), SEG

    def body(x_ref, w_ref, b_ref, u_ref):
        u_ref[...] = _dot(x_ref[...], w_ref[...]) + b_ref[...]

    return pl.pallas_call(
        body, name="in_proj", grid=(T // tm, D_IN // tn),
        in_specs=[pl.BlockSpec((tm, D_MODEL), lambda m, n: (m, 0)),
                  pl.BlockSpec((D_MODEL, tn), lambda m, n: (0, n)),
                  pl.BlockSpec((None, 1, tn), lambda m, n: (layer, 0, n))],
        out_specs=pl.BlockSpec((tm, tn), lambda m, n: (m, n)),
        out_shape=jax.ShapeDtypeStruct((T, D_IN), F32),
        compiler_params=_params("parallel", "parallel"),
    )(xb, w_in, b_in)


def _du_specs(tm, row_of, seg_of):
    return [pl.BlockSpec((None, tm, SEG), lambda *g: (jnp.minimum(seg_of(*g), N_QKV_SEG - 1), row_of(*g), 0)),
            pl.BlockSpec((tm, SEG), lambda *g: (row_of(*g), jnp.maximum(seg_of(*g) - N_QKV_SEG, 0)))]


def _in_proj_dx(dqkv, du_rest, w_in, dres, hosted=()):
    T = du_rest.shape[0]
    tm = min(T, 512)
    nm = T // tm
    n_h = len(hosted)
    h_in, h_out, h_shape, h_sems = _host(hosted)

    def body(*refs):
        (a1_ref, a2_ref, w_ref, r_ref), h_ins, (o_ref,), h_outs, sems = _split_refs(refs, 4, n_h, 1, n_h)
        m, s = pl.program_id(0), pl.program_id(1)

        if n_h:
            @pl.when(jnp.logical_and(m == 0, s == 0))
            def _():
                _Exchange(h_ins, h_outs, *sems).start()

        @pl.when(s == 0)
        def _():
            o_ref[...] = r_ref[...]

        @pl.when(s < N_QKV_SEG)
        def _():
            o_ref[...] += _dot_nt(a1_ref[...].astype(BF16), w_ref[...])

        @pl.when(s >= N_QKV_SEG)
        def _():
            o_ref[...] += _dot_nt(a2_ref[...].astype(BF16), w_ref[...])

        if n_h:
            @pl.when(jnp.logical_and(m == nm - 1, s == N_SEG - 1))
            def _():
                _Exchange(h_ins, h_outs, *sems).wait()

    out = pl.pallas_call(
        body, name="in_proj_dx", grid=(nm, N_SEG),
        in_specs=_du_specs(tm, lambda m, s: m, lambda m, s: s) + [
            pl.BlockSpec((D_MODEL, SEG), lambda m, s: (0, s)),
            pl.BlockSpec((tm, D_MODEL), lambda m, s: (m, 0))] + h_in,
        out_specs=[pl.BlockSpec((tm, D_MODEL), lambda m, s: (m, 0))] + h_out,
        out_shape=[jax.ShapeDtypeStruct((T, D_MODEL), F32)] + h_shape,
        scratch_shapes=h_sems,
        compiler_params=_params("arbitrary", "arbitrary"),
    )(dqkv, du_rest, w_in, dres, *hosted)
    return out[0], out[1:]


def _in_proj_dw(xb, dqkv, du_rest):
    T = xb.shape[0]
    tk = min(T, 512)

    def body(x_ref, a1_ref, a2_ref, dw_ref, db_ref):
        s, t = pl.program_id(0), pl.program_id(1)

        @pl.when(t == 0)
        def _():
            dw_ref[...] = jnp.zeros_like(dw_ref)
            db_ref[...] = jnp.zeros_like(db_ref)

        def acc(a):
            dw_ref[...] += _dot_tn(x_ref[...], a.astype(BF16))
            db_ref[...] += jnp.sum(a, axis=0, keepdims=True)

        @pl.when(s < N_QKV_SEG)
        def _():
            acc(a1_ref[...])

        @pl.when(s >= N_QKV_SEG)
        def _():
            acc(a2_ref[...])

    return pl.pallas_call(
        body, name="in_proj_dw", grid=(N_SEG, T // tk),
        in_specs=[pl.BlockSpec((tk, D_MODEL), lambda s, t: (t, 0))] + _du_specs(tk, lambda s, t: t, lambda s, t: s),
        out_specs=[pl.BlockSpec((D_MODEL, SEG), lambda s, t: (0, s)),
                   pl.BlockSpec((None, 1, SEG), lambda s, t: (s, 0, 0))],
        out_shape=[jax.ShapeDtypeStruct((D_MODEL, D_IN), F32), jax.ShapeDtypeStruct((N_SEG, 1, SEG), F32)],
        compiler_params=_params("parallel", "arbitrary"),
    )(xb, dqkv, du_rest)


def _split_dot(val, rhs_twice):
    hi = val.astype(BF16)
    lo = (val - hi.astype(F32)).astype(BF16)
    return _dot(jnp.concatenate([hi, lo], axis=1), rhs_twice)


def _twice(rhs):
    return jnp.concatenate([rhs, rhs], axis=0)


def _log_sigmoids(z):
    sp = jnp.log(1.0 + jnp.exp(-jnp.abs(z)))
    return jnp.minimum(z, 0.0) - sp, jnp.minimum(-z, 0.0) - sp


def _attn_masks():
    row = lax.broadcasted_iota(jnp.int32, (ATT_BLOCK, ATT_BLOCK), 0)
    col = lax.broadcasted_iota(jnp.int32, (ATT_BLOCK, ATT_BLOCK), 1)
    return col < row, (row > col).astype(BF16), (row >= col).astype(BF16)


EXP_UNDERFLOW = -104.0


def _sweep_keys(i, tile, rsum):
    tile(i, True)

    def live(carry):
        j, top = carry
        return jnp.logical_and(j >= 0, top > EXP_UNDERFLOW)

    def step(carry):
        j, _ = carry
        tile(j, False)
        return j - 1, jnp.max(rsum[...])

    lax.while_loop(live, step, (i - 1, jnp.max(rsum[...])))


def _attn_fwd(u, gather=()):
    T = u.shape[0]
    B = ATT_BLOCK
    nq = T // B
    n_pairs = D_ATT // 128
    n_g = len(gather)

    def body(*refs):
        (q_ref, k_ref, v_ref), g_ins, (o_ref,), g_outs, scratch = _split_refs(refs, 3, n_g, 1, n_g)
        kb, vb, acc, rsum = scratch[:4]
        p, i = pl.program_id(0), pl.program_id(1)

        if n_g:
            @pl.when(jnp.logical_and(p == 0, i == 0))
            def _():
                _Gather(g_ins, g_outs, *scratch[4:]).start()

            @pl.when(jnp.logical_and(p == n_pairs // 2, i == 0))
            def _():
                _Gather(g_ins, g_outs, *scratch[4:]).forward()

        @pl.when(i == 0)
        def _():
            for h in range(2):
                kb[h] = k_ref[:, HEAD_DIM * h:HEAD_DIM * (h + 1)].astype(BF16)
                vb[h] = v_ref[:, HEAD_DIM * h:HEAD_DIM * (h + 1)].astype(BF16)

        causal, tri, _ = _attn_masks()
        tri2 = _twice(tri)
        hs = [slice(HEAD_DIM * h, HEAD_DIM * (h + 1)) for h in range(2)]
        q = [(q_ref[:, hs[h]] * ATT_SCALE).astype(BF16) for h in range(2)]
        acc[...] = jnp.zeros_like(acc)
        rsum[...] = jnp.zeros_like(rsum)

        def tile(j, masked):
            rows = pl.ds(pl.multiple_of(j * B, B), B)
            for h in range(2):
                z = _dot_nt(q[h], kb[h, rows, :])
                a, f = _log_sigmoids(z)
                if masked:
                    f = jnp.where(causal, f, 0.0)
                later = _split_dot(f, tri2)
                w = jnp.exp(a + later + rsum[h])
                if masked:
                    w = jnp.where(causal, w, 0.0)
                acc[h] += _split_dot(w, _twice(vb[h, rows, :]))
                rsum[h] += later[:, 0:1] + f[:, 0:1]

        _sweep_keys(i, tile, rsum)
        for h in range(2):
            o_ref[:, hs[h]] = acc[h]

        if n_g:
            @pl.when(jnp.logical_and(p == n_pairs - 1, i == nq - 1))
            def _():
                _Gather(g_ins, g_outs, *scratch[4:]).wait()

    out = pl.pallas_call(
        body, name="attn_fwd", grid=(n_pairs, nq),
        in_specs=[pl.BlockSpec((B, 128), lambda p, i: (i, p)),
                  pl.BlockSpec((T, 128), lambda p, i: (0, 4 + p)),
                  pl.BlockSpec((T, 128), lambda p, i: (0, 8 + p))] + [ANY] * n_g,
        out_specs=[pl.BlockSpec((B, 128), lambda p, i: (i, p))] + [ANY] * n_g,
        out_shape=[jax.ShapeDtypeStruct((T, D_ATT), F32)] + _gathered_shapes(gather),
        scratch_shapes=[pltpu.VMEM((2, T, HEAD_DIM), BF16), pltpu.VMEM((2, T, HEAD_DIM), BF16),
                        pltpu.VMEM((2, B, HEAD_DIM), F32), pltpu.VMEM((2, B, 1), F32)]
        + (_Gather.semaphores(n_g) if n_g else []),
        compiler_params=_params("arbitrary", "arbitrary"),
    )(u, u, u, *gather)
    return out[0], out[1:]


def _attn_bwd(u, att, datt, hosted=()):
    T = u.shape[0]
    B = ATT_BLOCK
    nq = T // B
    n_pairs = D_ATT // 128
    n_h = len(hosted)
    h_in, h_out, h_shape, h_sems = _host(hosted)

    def body(*refs):
        (q_ref, k_ref, v_ref, o_ref, do_ref), h_ins, (dqkv_ref,), h_outs, scratch = _split_refs(refs, 5, n_h, 1, n_h)
        kb, vb, dkacc, dvacc, dqacc, rsum, gsum = scratch[:7]
        p, i = pl.program_id(0), pl.program_id(1)

        if n_h:
            @pl.when(jnp.logical_and(p == 0, i == 0))
            def _():
                _Exchange(h_ins, h_outs, *scratch[7:]).start()

        @pl.when(i == 0)
        def _():
            for h in range(2):
                kb[h] = k_ref[:, HEAD_DIM * h:HEAD_DIM * (h + 1)].astype(BF16)
                vb[h] = v_ref[:, HEAD_DIM * h:HEAD_DIM * (h + 1)].astype(BF16)
            dkacc[...] = jnp.zeros_like(dkacc)
            dvacc[...] = jnp.zeros_like(dvacc)

        causal, tri, tri_incl = _attn_masks()
        tri2, tri_incl2 = _twice(tri), _twice(tri_incl)
        qrows = pl.ds(pl.multiple_of(i * B, B), B)
        hs = [slice(HEAD_DIM * h, HEAD_DIM * (h + 1)) for h in range(2)]
        q = [(q_ref[:, hs[h]] * ATT_SCALE).astype(BF16) for h in range(2)]
        dob = [do_ref[:, hs[h]].astype(BF16) for h in range(2)]
        total = [jnp.sum(dob[h].astype(F32) * o_ref[:, hs[h]], axis=1, keepdims=True) for h in range(2)]
        dqacc[...] = jnp.zeros_like(dqacc)
        rsum[...] = jnp.zeros_like(rsum)
        gsum[...] = jnp.zeros_like(gsum)

        def tile(j, masked):
            rows = pl.ds(pl.multiple_of(j * B, B), B)
            for h in range(2):
                kj, vj = kb[h, rows, :], vb[h, rows, :]
                z = _dot_nt(q[h], kj)
                a, f = _log_sigmoids(z)
                if masked:
                    f = jnp.where(causal, f, 0.0)
                later = _split_dot(f, tri2)
                w = jnp.exp(a + later + rsum[h])
                if masked:
                    w = jnp.where(causal, w, 0.0)
                g = _dot_nt(dob[h], vj) * w
                suffix = _split_dot(g, tri_incl2)
                before = total[h] - (suffix + gsum[h])
                beta = jnp.exp(a)
                dz = g * (1.0 - beta) - before * beta
                if masked:
                    dz = jnp.where(causal, dz, 0.0)
                dzb = dz.astype(BF16)
                dqacc[h] += _dot(dzb, kj)
                dkacc[h, rows, :] += _dot_tn(dzb, q[h])
                dvacc[h, rows, :] += _dot_tn(w.astype(BF16), dob[h])
                rsum[h] += later[:, 0:1] + f[:, 0:1]
                gsum[h] += suffix[:, 0:1]

        _sweep_keys(i, tile, rsum)
        for h in range(2):
            dqkv_ref[0, qrows, hs[h]] = dqacc[h] * ATT_SCALE

        @pl.when(i == nq - 1)
        def _():
            for h in range(2):
                dqkv_ref[1, :, hs[h]] = dkacc[h]
                dqkv_ref[2, :, hs[h]] = dvacc[h]

        if n_h:
            @pl.when(jnp.logical_and(p == n_pairs - 1, i == nq - 1))
            def _():
                _Exchange(h_ins, h_outs, *scratch[7:]).wait()

    out = pl.pallas_call(
        body, name="attn_bwd", grid=(n_pairs, nq),
        in_specs=[pl.BlockSpec((B, 128), lambda p, i: (i, p)),
                  pl.BlockSpec((T, 128), lambda p, i: (0, 4 + p)),
                  pl.BlockSpec((T, 128), lambda p, i: (0, 8 + p)),
                  pl.BlockSpec((B, 128), lambda p, i: (i, p)),
                  pl.BlockSpec((B, 128), lambda p, i: (i, p))] + h_in,
        out_specs=[pl.BlockSpec((3, T, 128), lambda p, i: (0, 0, p))] + h_out,
        out_shape=[jax.ShapeDtypeStruct((3, T, D_ATT), F32)] + h_shape,
        scratch_shapes=[pltpu.VMEM((2, T, HEAD_DIM), BF16), pltpu.VMEM((2, T, HEAD_DIM), BF16),
                        pltpu.VMEM((2, T, HEAD_DIM), F32), pltpu.VMEM((2, T, HEAD_DIM), F32),
                        pltpu.VMEM((2, B, HEAD_DIM), F32), pltpu.VMEM((2, B, 1), F32),
                        pltpu.VMEM((2, B, 1), F32)] + h_sems,
        compiler_params=_params("arbitrary", "arbitrary"),
    )(u, u, u, att, datt, *hosted)
    return out[0], out[1:]


MIX_TILE = 128


def _u_specs(tm, tile_of):
    return [pl.BlockSpec((tm, SEG), functools.partial(lambda k, *g: (tile_of(*g), k), k)) for k in range(3, N_SEG)]


SHIFT_ROWS = 24


def _shifted_scratch(tm):
    return pltpu.VMEM((7, tm + SHIFT_ROWS, D_CONV), F32)


def _make_shifted(base, shifted, tm):
    for b in range(1, 8):
        shifted[b - 1] = base[b:b + tm + SHIFT_ROWS, :]


def _tap_window(base, shifted, start, tm):
    whole, b = divmod(start, 8)
    if b == 0:
        return base[start:start + tm, :]
    return shifted[b - 1, 8 * whole:8 * whole + tm, :]


def _conv(ext, ext_shifted, cw_ref, tm):
    acc = cw_ref[0:1, :] * _tap_window(ext, ext_shifted, 2, tm)
    for k in range(1, CONV_WIDTH):
        acc = acc + cw_ref[k:k + 1, :] * _tap_window(ext, ext_shifted, 2 + k, tm)
    return acc


def _norm_stats(v):
    mu = jnp.mean(v, axis=-1, keepdims=True)
    vc = v - mu
    rstd = lax.rsqrt(jnp.mean(vc * vc, axis=-1, keepdims=True) + LN_EPS)
    return vc * rstd, rstd


def _norm_bwd(dy_scaled, xhat, rstd):
    return rstd * (dy_scaled - jnp.mean(dy_scaled, axis=-1, keepdims=True)
                   - xhat * jnp.mean(dy_scaled * xhat, axis=-1, keepdims=True))


def _mix_fwd(x, u, att, wap, wcp, wout, cw, cvec, dvec, layer):
    T = x.shape[0]
    tm = MIX_TILE

    def body(x_ref, za_ref, ga_ref, gb_ref, zc_ref, ga0_ref, ga1_ref, gc0_ref, gc1_ref, att_ref,
             wap_ref, wcp_ref, wout_ref, cw_ref, cvec_ref, dvec_ref, xn_ref, xnb_ref, h_ref, ext, ext_shifted):
        i = pl.program_id(0)

        @pl.when(i == 0)
        def _():
            ext[0:CONV_PAD, :] = jnp.zeros((CONV_PAD, D_CONV), F32)

        za = za_ref[...]
        attg = att_ref[...] * (za * _sig(za))
        ab = _dot(attg.astype(BF16), wap_ref[...])

        ext[CONV_PAD:CONV_PAD + tm, :] = ga_ref[...] * _sig(gb_ref[...])
        _make_shifted(ext, ext_shifted, tm)
        c1 = _conv(ext, ext_shifted, cw_ref, tm) + cvec_ref[0:1, :]
        ext[0:CONV_PAD, :] = ext[tm:tm + CONV_PAD, :]
        xh, _ = _norm_stats(c1)
        c2 = xh * cvec_ref[1:2, :] + cvec_ref[2:3, :]
        zc = zc_ref[...]
        cz = (c2 * _sig(c2)) * (zc * _sig(zc))
        cb = _dot(cz.astype(BF16), wcp_ref[...]) + dvec_ref[0:1, :]

        g_att = jnp.concatenate([ga0_ref[...], ga1_ref[...]], axis=1)
        g_conv = jnp.concatenate([gc0_ref[...], gc1_ref[...]], axis=1)
        merged = _sig(g_att) * ab + _sig(g_conv) * cb
        h = DEEPNORM_ALPHA * x_ref[...] + _dot(merged.astype(BF16), wout_ref[...])
        h_ref[...] = h
        hh, _ = _norm_stats(h)
        xn = hh * dvec_ref[1:2, :] + dvec_ref[2:3, :]
        xn_ref[...] = xn
        xnb_ref[...] = xn.astype(BF16)

    row = pl.BlockSpec((tm, D_MODEL), lambda i: (i, 0))
    return pl.pallas_call(
        body, name="mix_fwd", grid=(T // tm,),
        in_specs=[row] + _u_specs(tm, lambda i: i) + [
            pl.BlockSpec((tm, D_ATT), lambda i: (i, 0)),
            pl.BlockSpec((D_ATT, D_MODEL), lambda i: (0, 0)),
            pl.BlockSpec((D_CONV, D_MODEL), lambda i: (0, 0)),
            pl.BlockSpec((D_MODEL, D_MODEL), lambda i: (0, 0)),
            pl.BlockSpec((CONV_PAD, D_CONV), lambda i: (0, 0)),
            pl.BlockSpec((None, 8, D_CONV), lambda i: (layer, 0, 0)),
            pl.BlockSpec((None, 8, D_MODEL), lambda i: (layer, 0, 0))],
        out_specs=[row, row, row],
        out_shape=[jax.ShapeDtypeStruct((T, D_MODEL), F32), jax.ShapeDtypeStruct((T, D_MODEL), BF16),
                   jax.ShapeDtypeStruct((T, D_MODEL), F32)],
        scratch_shapes=[pltpu.VMEM((tm + CONV_PAD, D_CONV), F32), _shifted_scratch(tm)],
        compiler_params=_params("arbitrary"),
    )(x, *([u] * 8), att, wap, wcp, wout, cw, cvec, dvec)


def _mix_bwd(dxo, h, u, att, wap, wcp, wout, cw, cvec, dvec, layer, hosted=()):
    T = dxo.shape[0]
    tm = MIX_TILE
    nt = T // tm
    halo_per_tile = tm // CONV_PAD
    n_h = len(hosted)
    h_in, h_out, h_shape, h_sems = _host(hosted)

    def body(*refs):
        ins, h_ins, outs, h_outs, scratch = _split_refs(refs, 19, n_h, 9, n_h)
        (dxo_ref, h_ref, za_ref, ga_ref, gb_ref, zc_ref, ga0_ref, ga1_ref, gc0_ref, gc1_ref, hga_ref, hgb_ref,
         att_ref, wap_ref, wcp_ref, wout_ref, cw_ref, cvec_ref, dvec_ref) = ins
        datt_ref, dur_ref, dres_ref, dwap_ref, dwcp_ref, dwout_ref, dcw_ref, dcvec_ref, ddvec_ref = outs
        ext, dext, ext_shifted, dext_shifted = scratch[:4]
        r = pl.program_id(0)
        i = nt - 1 - r

        if n_h:
            @pl.when(r == 0)
            def _():
                _Exchange(h_ins, h_outs, *scratch[4:]).start()

        @pl.when(r == 0)
        def _():
            for ref in (dwap_ref, dwcp_ref, dwout_ref, dcw_ref, dcvec_ref, ddvec_ref):
                ref[...] = jnp.zeros_like(ref)
            dext[tm:tm + CONV_PAD, :] = jnp.zeros((CONV_PAD, D_CONV), F32)

        za = za_ref[...]
        s_za = _sig(za)
        silu_za = za * s_za
        att_t = att_ref[...]
        attg = (att_t * silu_za).astype(BF16)
        ab = _dot(attg, wap_ref[...])

        ga, gb = ga_ref[...], gb_ref[...]
        s_gb = _sig(gb)
        halo = hga_ref[...] * _sig(hgb_ref[...])
        ext[0:CONV_PAD, :] = jnp.where(i > 0, halo, 0.0)
        ext[CONV_PAD:CONV_PAD + tm, :] = ga * s_gb
        _make_shifted(ext, ext_shifted, tm)
        c1 = _conv(ext, ext_shifted, cw_ref, tm) + cvec_ref[0:1, :]
        xh, rstd_c = _norm_stats(c1)
        c2 = xh * cvec_ref[1:2, :] + cvec_ref[2:3, :]
        s_c2 = _sig(c2)
        c3 = c2 * s_c2
        zc = zc_ref[...]
        s_zc = _sig(zc)
        cz = (c3 * (zc * s_zc)).astype(BF16)
        cb = _dot(cz, wcp_ref[...]) + dvec_ref[0:1, :]

        s_ga = _sig(jnp.concatenate([ga0_ref[...], ga1_ref[...]], axis=1))
        s_gc = _sig(jnp.concatenate([gc0_ref[...], gc1_ref[...]], axis=1))
        merged = (s_ga * ab + s_gc * cb).astype(BF16)

        dxo_t = dxo_ref[...]
        hh, rstd_h = _norm_stats(h_ref[...])
        ddvec_ref[1:2, :] += jnp.sum(dxo_t * hh, axis=0, keepdims=True)
        ddvec_ref[2:3, :] += jnp.sum(dxo_t, axis=0, keepdims=True)
        dh = _norm_bwd(dxo_t * dvec_ref[1:2, :], hh, rstd_h)
        dres_ref[...] = DEEPNORM_ALPHA * dh
        dy = dh.astype(BF16)

        dwout_ref[...] += _dot_tn(merged, dy)
        dm = _dot_nt(dy, wout_ref[...])
        dab = (dm * s_ga).astype(BF16)
        dcb = dm * s_gc
        dur_ref[:, 4 * SEG:6 * SEG] = dm * ab * (s_ga * (1.0 - s_ga))
        dur_ref[:, 6 * SEG:8 * SEG] = dm * cb * (s_gc * (1.0 - s_gc))

        dwap_ref[...] += _dot_tn(attg, dab)
        dattg = _dot_nt(dab, wap_ref[...])
        datt_ref[...] = dattg * silu_za
        dur_ref[:, 0:SEG] = dattg * att_t * (s_za * (1.0 + za * (1.0 - s_za)))

        ddvec_ref[0:1, :] += jnp.sum(dcb, axis=0, keepdims=True)
        dcbb = dcb.astype(BF16)
        dwcp_ref[...] += _dot_tn(cz, dcbb)
        dcz = _dot_nt(dcbb, wcp_ref[...])
        dur_ref[:, 3 * SEG:4 * SEG] = dcz * c3 * (s_zc * (1.0 + zc * (1.0 - s_zc)))
        dc2 = dcz * (zc * s_zc) * (s_c2 * (1.0 + c2 * (1.0 - s_c2)))
        dcvec_ref[1:2, :] += jnp.sum(dc2 * xh, axis=0, keepdims=True)
        dcvec_ref[2:3, :] += jnp.sum(dc2, axis=0, keepdims=True)
        dc1 = _norm_bwd(dc2 * cvec_ref[1:2, :], xh, rstd_c)
        dcvec_ref[0:1, :] += jnp.sum(dc1, axis=0, keepdims=True)
        for k in range(CONV_WIDTH):
            dcw_ref[k:k + 1, :] += jnp.sum(dc1 * _tap_window(ext, ext_shifted, 2 + k, tm), axis=0, keepdims=True)
        dext[0:tm, :] = dc1
        _make_shifted(dext, dext_shifted, tm)
        dc0 = cw_ref[0:1, :] * _tap_window(dext, dext_shifted, CONV_WIDTH - 1, tm)
        for k in range(1, CONV_WIDTH):
            dc0 = dc0 + cw_ref[k:k + 1, :] * _tap_window(dext, dext_shifted, CONV_WIDTH - 1 - k, tm)
        dext[tm:tm + CONV_PAD, :] = dext[0:CONV_PAD, :]
        dur_ref[:, SEG:2 * SEG] = dc0 * s_gb
        dur_ref[:, 2 * SEG:3 * SEG] = dc0 * ga * (s_gb * (1.0 - s_gb))

        if n_h:
            @pl.when(r == nt - 1)
            def _():
                _Exchange(h_ins, h_outs, *scratch[4:]).wait()

    rev = lambda r: nt - 1 - r
    row = pl.BlockSpec((tm, D_MODEL), lambda r: (rev(r), 0))
    halo_spec = [pl.BlockSpec((CONV_PAD, SEG),
                              functools.partial(lambda k, r: (jnp.maximum(rev(r) * halo_per_tile - 1, 0), k), k))
                 for k in (4, 5)]
    const = lambda shape: pl.BlockSpec(shape, lambda r: (0,) * len(shape))
    out = pl.pallas_call(
        body, name="mix_bwd", grid=(nt,),
        in_specs=[row, row] + _u_specs(tm, rev) + halo_spec + [
            pl.BlockSpec((tm, D_ATT), lambda r: (rev(r), 0)),
            pl.BlockSpec((D_ATT, D_MODEL), lambda r: (0, 0)),
            pl.BlockSpec((D_CONV, D_MODEL), lambda r: (0, 0)),
            pl.BlockSpec((D_MODEL, D_MODEL), lambda r: (0, 0)),
            pl.BlockSpec((CONV_PAD, D_CONV), lambda r: (0, 0)),
            pl.BlockSpec((None, 8, D_CONV), lambda r: (layer, 0, 0)),
            pl.BlockSpec((None, 8, D_MODEL), lambda r: (layer, 0, 0))] + h_in,
        out_specs=[pl.BlockSpec((tm, D_ATT), lambda r: (rev(r), 0)),
                   pl.BlockSpec((tm, 8 * SEG), lambda r: (rev(r), 0)),
                   row,
                   const((D_ATT, D_MODEL)), const((D_CONV, D_MODEL)), const((D_MODEL, D_MODEL)),
                   const((CONV_PAD, D_CONV)), const((8, D_CONV)), const((8, D_MODEL))] + h_out,
        out_shape=[jax.ShapeDtypeStruct((T, D_ATT), F32), jax.ShapeDtypeStruct((T, 8 * SEG), F32),
                   jax.ShapeDtypeStruct((T, D_MODEL), F32),
                   jax.ShapeDtypeStruct((D_ATT, D_MODEL), F32), jax.ShapeDtypeStruct((D_CONV, D_MODEL), F32),
                   jax.ShapeDtypeStruct((D_MODEL, D_MODEL), F32), jax.ShapeDtypeStruct((CONV_PAD, D_CONV), F32),
                   jax.ShapeDtypeStruct((8, D_CONV), F32), jax.ShapeDtypeStruct((8, D_MODEL), F32)] + h_shape,
        scratch_shapes=[pltpu.VMEM((tm + CONV_PAD, D_CONV), F32), pltpu.VMEM((tm + CONV_PAD, D_CONV), F32),
                        _shifted_scratch(tm), _shifted_scratch(tm)] + h_sems,
        compiler_params=_params("arbitrary"),
    )(dxo, h, *([u] * 10), att, wap, wcp, wout, cw, cvec, dvec, *hosted)
    return out[:9], out[9:]


def _loss_head(y, target):
    T = y.shape[0]
    tm = min(T, 512)

    def body(y_ref, t_ref, dy_ref, loss_ref):
        @pl.when(pl.program_id(0) == 0)
        def _():
            loss_ref[...] = jnp.zeros_like(loss_ref)

        err = y_ref[...] - t_ref[...]
        dy_ref[...] = err * (1.0 / D_MODEL)
        loss_ref[...] += 0.5 * jnp.sum(jnp.mean(err * err, axis=-1, keepdims=True))

    row = pl.BlockSpec((tm, D_MODEL), lambda i: (i, 0))
    return pl.pallas_call(
        body, name="loss_head", grid=(T // tm,), in_specs=[row, row],
        out_specs=[row, pl.BlockSpec((8, 128), lambda i: (0, 0))],
        out_shape=[jax.ShapeDtypeStruct((T, D_MODEL), F32), jax.ShapeDtypeStruct((8, 128), F32)],
        compiler_params=_params("arbitrary"),
    )(y, target)


def _adamw(w, g, m, v):
    m = ADAM_B1 * m + (1.0 - ADAM_B1) * g
    v = ADAM_B2 * v + (1.0 - ADAM_B2) * (g * g)
    m_hat = m / (1.0 - ADAM_B1 ** ADAM_STEP)
    v_hat = v / (1.0 - ADAM_B2 ** ADAM_STEP)
    return -ADAM_LR * (m_hat / (jnp.sqrt(v_hat) + ADAM_EPS) + ADAM_WD * w), m, v


def _sum_slots(ref):
    total = ref[0].astype(F32)
    for j in range(1, N_DEV):
        total = total + ref[j].astype(F32)
    return total


def _adamw_sharded(recv, w, m, v, prev, layer, tr, name):
    n_layers, R, C = w.shape

    def body(*refs):
        recv_ref, w_ref, m_ref, v_ref = refs[:4]
        g_ref, d_ref, nm_ref, nv_ref = refs[-4:]
        g = _sum_slots(recv_ref)
        g_ref[...] = g
        d_ref[...], nm_ref[...], nv_ref[...] = _adamw(w_ref[...], g, m_ref[...], v_ref[...])

    slab = pl.BlockSpec((None, tr, C), lambda i: (layer, i, 0))
    n_prev = 0 if prev is None else 4
    return pl.pallas_call(
        body, name=name, grid=(R // tr,),
        in_specs=[pl.BlockSpec((N_DEV, tr, C), lambda i: (0, i, 0)), slab, slab, slab] + [ANY] * n_prev,
        out_specs=[slab] * 4,
        out_shape=[jax.ShapeDtypeStruct(w.shape, F32)] * 4,
        input_output_aliases={4 + k: k for k in range(n_prev)},
        compiler_params=_params("parallel"),
    )(recv, w, m, v, *(prev or ()))


def _sum8(g):
    R = g.shape[1]

    def body(g_ref, o_ref):
        o_ref[...] = _sum_slots(g_ref)

    return pl.pallas_call(body, name="sum_small_grads", out_shape=jax.ShapeDtypeStruct((R, 128), F32))(g)


def _adamw_small(w, g, m, v):
    def body(w_ref, g_ref, m_ref, v_ref, d_ref, nm_ref, nv_ref):
        d_ref[...], nm_ref[...], nv_ref[...] = _adamw(w_ref[...], g_ref[...], m_ref[...], v_ref[...])

    return pl.pallas_call(body, name="adamw_small", out_shape=[jax.ShapeDtypeStruct(w.shape, F32)] * 3)(w, g, m, v)


REPLICATED = (("b_in", D_IN), ("conv_b", D_CONV), ("conv_ln_g", D_CONV), ("conv_ln_b", D_CONV),
              ("b_conv_proj", D_MODEL), ("ln_g", D_MODEL), ("ln_b", D_MODEL))
N_REPLICATED = sum(n for _, n in REPLICATED)
CONV_W_SHARD_PAD = 2048


def _pack_small(rep, conv_w_shard):
    flat = jnp.concatenate([rep[k] for k, _ in REPLICATED], axis=1)
    cws = jnp.pad(conv_w_shard.reshape(DEPTH, -1), ((0, 0), (0, CONV_W_SHARD_PAD - CONV_WIDTH * 64)))
    return jnp.concatenate([flat, cws], axis=1).reshape(-1, 128)


def _unpack_small(packed):
    per = packed.reshape(DEPTH, N_REPLICATED + CONV_W_SHARD_PAD)
    out, off = {}, 0
    for k, n in REPLICATED:
        out[k] = per[:, off:off + n]
        off += n
    out["conv_w"] = per[:, off:off + CONV_WIDTH * 64].reshape(DEPTH, CONV_WIDTH, 64)
    return out


def kernel(x, w_in, b_in, conv_w, conv_b, conv_ln_g, conv_ln_b, w_att_proj, w_conv_proj, b_conv_proj, w_out, ln_g, ln_b, loss_target, m_w_in, m_b_in, m_conv_w, m_conv_b, m_conv_ln_g, m_conv_ln_b, m_w_att_proj, m_w_conv_proj, m_b_conv_proj, m_w_out, m_ln_g, m_ln_b, v_w_in, v_b_in, v_conv_w, v_conv_b, v_conv_ln_g, v_conv_ln_b, v_w_att_proj, v_w_conv_proj, v_b_conv_proj, v_w_out, v_ln_g, v_ln_b):
    x0 = x[0]
    target = loss_target[0]

    def shards(l):
        return [w_in[l].astype(BF16), w_att_proj[l].astype(BF16), w_conv_proj[l].astype(BF16),
                w_out[l].astype(BF16), conv_w[l]]

    def full_weights(g_in, g_ap, g_cp, g_out, g_cw):
        return (g_in.transpose(1, 0, 2).reshape(D_MODEL, D_IN), g_ap.transpose(1, 0, 2).reshape(D_ATT, D_MODEL),
                g_cp.transpose(1, 0, 2).reshape(D_CONV, D_MODEL), g_out.reshape(D_MODEL, D_MODEL),
                jnp.pad(g_cw.transpose(1, 0, 2).reshape(CONV_WIDTH, D_CONV), ((0, 1), (0, 0))))

    pad8 = lambda rows: jnp.pad(jnp.stack(rows, axis=1), ((0, 0), (0, 8 - len(rows)), (0, 0)))
    cvec = pad8([conv_b, conv_ln_g, conv_ln_b])
    dvec = pad8([b_conv_proj, ln_g, ln_b])
    b_in3 = b_in.reshape(DEPTH, 1, D_IN)

    weights = [full_weights(*_all_gather(shards(0), "gather_weights"))]
    xs, xbs, us, atts, hs = [x0], [x0.astype(BF16)], [], [], []
    for l in range(DEPTH):
        w_in_l, wap, wcp, wout, cw = weights[l]
        u = _in_proj(xbs[l], w_in_l, b_in3, l)
        att, gathered = _attn_fwd(u, gather=shards(l + 1) if l + 1 < DEPTH else ())
        if gathered:
            weights.append(full_weights(*gathered))
        xn, xnb, h = _mix_fwd(xs[l], u, att, wap, wcp, wout, cw, cvec, dvec, l)
        us.append(u), atts.append(att), hs.append(h), xs.append(xn), xbs.append(xnb)
    dxo, loss_part = _loss_head(xs[DEPTH], target)
    loss = lax.psum(loss_part[0, 0], ("x", "y", "c"))

    big = {"w_in": (w_in, m_w_in, v_w_in, 256), "w_att_proj": (w_att_proj, m_w_att_proj, v_w_att_proj, D_ATT),
           "w_conv_proj": (w_conv_proj, m_w_conv_proj, v_w_conv_proj, D_CONV), "w_out": (w_out, m_w_out, v_w_out, 128)}
    results = {k: None for k in big}
    small = [None] * DEPTH

    def update(k, recv, l):
        w, m, v, tr = big[k]
        results[k] = _adamw_sharded(recv, w, m, v, results[k], l, tr, "adamw_" + k)

    pending_w_in = ()
    for l in reversed(range(DEPTH)):
        w_in_l, wap, wcp, wout, cw = weights[l]
        (datt, du_rest, dres, dwap, dwcp, dwout, dcw, dcvec, ddvec), got = _mix_bwd(
            dxo, hs[l], us[l], atts[l], wap, wcp, wout, cw, cvec, dvec, l, hosted=pending_w_in)
        if got:
            update("w_in", got[0], l + 1)
        proj = [dwap.reshape(D_ATT, N_DEV, D_MODEL // N_DEV).transpose(1, 0, 2).astype(BF16),
                dwcp.reshape(D_CONV, N_DEV, D_MODEL // N_DEV).transpose(1, 0, 2).astype(BF16),
                dwout.reshape(N_DEV, D_MODEL // N_DEV, D_MODEL).astype(BF16)]
        dqkv, got = _attn_bwd(us[l], atts[l], datt, hosted=proj)
        for k, r in zip(("w_att_proj", "w_conv_proj", "w_out"), got):
            update(k, r, l)
        dwin, dbin = _in_proj_dw(xbs[l], dqkv, du_rest)
        small[l] = jnp.concatenate([dbin.reshape(-1), dcvec[:3].reshape(-1), ddvec[:3].reshape(-1), dcw.reshape(-1)])
        pending_w_in = (dwin.reshape(D_MODEL, N_DEV, W_IN_SHARD).transpose(1, 0, 2).astype(BF16),)
        dxo, got = _in_proj_dx(dqkv, du_rest, w_in_l, dres, hosted=pending_w_in if l == 0 else ())
        if got:
            update("w_in", got[0], l)
    grad_x = dxo[None]

    small_all = jnp.stack(small).reshape(-1, 128)
    (gathered,) = _all_gather([small_all], "gather_small_grads")
    per_layer = _sum8(gathered).reshape(DEPTH, -1)
    g_rep, off = {}, 0
    for k, n in (("b_in", D_IN), ("conv_b", D_CONV), ("conv_ln_g", D_CONV), ("conv_ln_b", D_CONV),
                 ("b_conv_proj", D_MODEL), ("ln_g", D_MODEL), ("ln_b", D_MODEL)):
        g_rep[k] = per_layer[:, off:off + n]
        off += n
    g_cw_full = per_layer[:, off:].reshape(DEPTH, CONV_PAD, D_CONV)[:, :CONV_WIDTH]
    g_cw_shard = lax.dynamic_slice_in_dim(g_cw_full, _my_index() * 64, 64, axis=2)
    rep_w = dict(b_in=b_in, conv_b=conv_b, conv_ln_g=conv_ln_g, conv_ln_b=conv_ln_b, b_conv_proj=b_conv_proj,
                 ln_g=ln_g, ln_b=ln_b)
    rep_m = dict(b_in=m_b_in, conv_b=m_conv_b, conv_ln_g=m_conv_ln_g, conv_ln_b=m_conv_ln_b,
                 b_conv_proj=m_b_conv_proj, ln_g=m_ln_g, ln_b=m_ln_b)
    rep_v = dict(b_in=v_b_in, conv_b=v_conv_b, conv_ln_g=v_conv_ln_g, conv_ln_b=v_conv_ln_b,
                 b_conv_proj=v_b_conv_proj, ln_g=v_ln_g, ln_b=v_ln_b)
    sd, sm, sv = _adamw_small(_pack_small(rep_w, conv_w), _pack_small(g_rep, g_cw_shard),
                              _pack_small(rep_m, m_conv_w), _pack_small(rep_v, v_conv_w))
    small_out = [dict(g_rep, conv_w=g_cw_shard), _unpack_small(sd), _unpack_small(sm), _unpack_small(sv)]

    order = ["w_in", "b_in", "conv_w", "conv_b", "conv_ln_g", "conv_ln_b", "w_att_proj", "w_conv_proj",
             "b_conv_proj", "w_out", "ln_g", "ln_b"]
    outs = [loss, grad_x]
    for kind in range(4):
        outs += [results[k][kind] if k in big else small_out[kind][k] for k in order]
    return tuple(outs)
```

```python
import functools

import jax
import jax.numpy as jnp
from jax import lax
from jax.experimental import pallas as pl
from jax.experimental.pallas import tpu as pltpu

F32, BF16 = jnp.float32, jnp.bfloat16

D_MODEL = 1024
D_ATT = 512
D_CONV = 512
HEAD_DIM = 64
CONV_WIDTH = 31
CONV_PAD = 32
DEPTH = 4
N_DEV = 8
SEG = 512
D_IN = 4 * D_ATT + 3 * D_CONV + 2 * D_MODEL
N_SEG = D_IN // SEG
N_QKV_SEG = 3
W_IN_SHARD = D_IN // N_DEV
LN_EPS = 1e-5
DEEPNORM_ALPHA = (2 * DEPTH) ** 0.25
ATT_SCALE = HEAD_DIM ** -0.5
ATT_BLOCK = 256
ADAM_LR, ADAM_B1, ADAM_B2, ADAM_EPS, ADAM_WD, ADAM_STEP = 0.001, 0.9, 0.999, 1e-08, 0.01, 10
VMEM_LIMIT = 56 * 1024 * 1024
MESH = pl.DeviceIdType.MESH
ANY = pl.BlockSpec(memory_space=pl.ANY)


def _params(*semantics):
    return pltpu.CompilerParams(dimension_semantics=semantics, vmem_limit_bytes=VMEM_LIMIT)


def _sig(x):
    return 1.0 / (1.0 + jnp.exp(-x))


def _dot(a, b):
    return jnp.dot(a, b, preferred_element_type=F32)


def _dot_nt(a, b):
    return lax.dot_general(a, b, (((1,), (1,)), ((), ())), preferred_element_type=F32)


def _dot_tn(a, b):
    return lax.dot_general(a, b, (((0,), (0,)), ((), ())), preferred_element_type=F32)


def _my_index():
    return 4 * lax.axis_index("x") + 2 * lax.axis_index("y") + lax.axis_index("c")


class _Gather:
    def __init__(self, ins, outs, send_sems, recv_sems, local_sems):
        self.n = n = len(ins)
        x, y, c = lax.axis_index("x"), lax.axis_index("y"), lax.axis_index("c")
        me, sibling = (x, y, c), (x, y, 1 - c)
        chips = [(1 - x, y), (x, 1 - y), (1 - x, 1 - y)]

        def slot(a, dev):
            return outs[a].at[4 * dev[0] + 2 * dev[1] + dev[2]]

        def copy(a, k, block, to, src=None):
            return pltpu.make_async_remote_copy(
                src_ref=slot(a, block) if src is None else src, dst_ref=slot(a, block),
                send_sem=send_sems.at[7 * a + k], recv_sem=recv_sems.at[7 * a + k], device_id=to, device_id_type=MESH)

        self.mine = [pltpu.make_async_copy(ins[a], slot(a, me), local_sems.at[a]) for a in range(n)]
        self.first, self.landed, self.passed, self.last = [], [], [], []
        for a in range(n):
            self.first.append(copy(a, 0, me, sibling, src=ins[a]))
            self.first += [copy(a, 1 + j, me, (*chip, c), src=ins[a]) for j, chip in enumerate(chips)]
        for j, chip in enumerate(chips):
            for a in range(n):
                self.landed.append(copy(a, 1 + j, (*chip, c), me))
                self.passed.append(copy(a, 4 + j, (*chip, c), sibling))
        for a in range(n):
            self.last.append(copy(a, 0, sibling, me))
            self.last += [copy(a, 4 + j, (*chip, 1 - c), me) for j, chip in enumerate(chips)]

    @staticmethod
    def semaphores(n):
        return [pltpu.SemaphoreType.DMA((7 * n,)), pltpu.SemaphoreType.DMA((7 * n,)), pltpu.SemaphoreType.DMA((n,))]

    def start(self):
        for cp in self.mine + self.first:
            cp.start()

    def forward(self):
        for landed, passed in zip(self.landed, self.passed):
            landed.wait_recv()
            passed.start()

    def wait(self):
        for cp in self.last:
            cp.wait_recv()
        for cp in self.first + self.passed:
            cp.wait_send()
        for cp in self.mine:
            cp.wait()


def _gathered_shapes(shards):
    return [jax.ShapeDtypeStruct((N_DEV,) + s.shape, s.dtype) for s in shards]


def _all_gather(shards, name):
    n = len(shards)

    def body(*refs):
        g = _Gather(refs[:n], refs[n:2 * n], *refs[2 * n:])
        g.start()
        g.forward()
        g.wait()

    return pl.pallas_call(
        body, name=name, out_shape=_gathered_shapes(shards),
        in_specs=[ANY] * n, out_specs=[ANY] * n, scratch_shapes=_Gather.semaphores(n),
    )(*shards)


class _Exchange:
    def __init__(self, ins, outs, send_sems, recv_sems, local_sems):
        n = len(ins)
        x, y, c = lax.axis_index("x"), lax.axis_index("y"), lax.axis_index("c")
        me_idx = 4 * x + 2 * y + c
        self.mine = [pltpu.make_async_copy(ins[a].at[me_idx], outs[a].at[me_idx], local_sems.at[a])
                     for a in range(n)]
        self.sends, self.recvs = [], []
        for d in (1, 2, 4, 3, 5, 6, 7):
            px = 1 - x if d & 4 else x
            py = 1 - y if d & 2 else y
            pc = 1 - c if d & 1 else c
            idx = 4 * px + 2 * py + pc
            for a in range(n):
                sems = dict(send_sem=send_sems.at[7 * a + d - 1], recv_sem=recv_sems.at[7 * a + d - 1],
                            device_id=(px, py, pc), device_id_type=MESH)
                self.sends.append(pltpu.make_async_remote_copy(
                    src_ref=ins[a].at[idx], dst_ref=outs[a].at[me_idx], **sems))
                self.recvs.append(pltpu.make_async_remote_copy(
                    src_ref=ins[a].at[idx], dst_ref=outs[a].at[idx], **sems))

    @staticmethod
    def semaphores(n):
        return [pltpu.SemaphoreType.DMA((7 * n,)), pltpu.SemaphoreType.DMA((7 * n,)), pltpu.SemaphoreType.DMA((n,))]

    def start(self):
        for cp in self.mine + self.sends:
            cp.start()

    def wait(self):
        for cp in self.recvs:
            cp.wait_recv()
        for cp in self.sends:
            cp.wait_send()
        for cp in self.mine:
            cp.wait()


def _host(hosted):
    n = len(hosted)
    return ([ANY] * n, [ANY] * n, [jax.ShapeDtypeStruct(p.shape, p.dtype) for p in hosted],
            _Exchange.semaphores(n) if n else [])


def _split_refs(refs, *counts):
    out, at = [], 0
    for k in counts:
        out.append(refs[at:at + k])
        at += k
    return out + [refs[at:]]


def _in_proj(xb, w_in, b_in, layer):
    T = xb.shape[0]
    tm, tn = min(T, 1024), SEG

    def body(x_ref, w_ref, b_ref, u_ref):
        u_ref[...] = _dot(x_ref[...], w_ref[...]) + b_ref[...]

    return pl.pallas_call(
        body, name="in_proj", grid=(T // tm, D_IN // tn),
        in_specs=[pl.BlockSpec((tm, D_MODEL), lambda m, n: (m, 0)),
                  pl.BlockSpec((D_MODEL, tn), lambda m, n: (0, n)),
                  pl.BlockSpec((None, 1, tn), lambda m, n: (layer, 0, n))],
        out_specs=pl.BlockSpec((tm, tn), lambda m, n: (m, n)),
        out_shape=jax.ShapeDtypeStruct((T, D_IN), F32),
        compiler_params=_params("parallel", "parallel"),
    )(xb, w_in, b_in)


def _du_specs(tm, row_of, seg_of):
    return [pl.BlockSpec((None, tm, SEG), lambda *g: (jnp.minimum(seg_of(*g), N_QKV_SEG - 1), row_of(*g), 0)),
            pl.BlockSpec((tm, SEG), lambda *g: (row_of(*g), jnp.maximum(seg_of(*g) - N_QKV_SEG, 0)))]


def _in_proj_dx(dqkv, du_rest, w_in, dres, hosted=()):
    T = du_rest.shape[0]
    tm = min(T, 512)
    nm = T // tm
    n_h = len(hosted)
    h_in, h_out, h_shape, h_sems = _host(hosted)

    def body(*refs):
        (a1_ref, a2_ref, w_ref, r_ref), h_ins, (o_ref,), h_outs, sems = _split_refs(refs, 4, n_h, 1, n_h)
        m, s = pl.program_id(0), pl.program_id(1)

        if n_h:
            @pl.when(jnp.logical_and(m == 0, s == 0))
            def _():
                _Exchange(h_ins, h_outs, *sems).start()

        @pl.when(s == 0)
        def _():
            o_ref[...] = r_ref[...]

        @pl.when(s < N_QKV_SEG)
        def _():
            o_ref[...] += _dot_nt(a1_ref[...].astype(BF16), w_ref[...])

        @pl.when(s >= N_QKV_SEG)
        def _():
            o_ref[...] += _dot_nt(a2_ref[...].astype(BF16), w_ref[...])

        if n_h:
            @pl.when(jnp.logical_and(m == nm - 1, s == N_SEG - 1))
            def _():
                _Exchange(h_ins, h_outs, *sems).wait()

    out = pl.pallas_call(
        body, name="in_proj_dx", grid=(nm, N_SEG),
        in_specs=_du_specs(tm, lambda m, s: m, lambda m, s: s) + [
            pl.BlockSpec((D_MODEL, SEG), lambda m, s: (0, s)),
            pl.BlockSpec((tm, D_MODEL), lambda m, s: (m, 0))] + h_in,
        out_specs=[pl.BlockSpec((tm, D_MODEL), lambda m, s: (m, 0))] + h_out,
        out_shape=[jax.ShapeDtypeStruct((T, D_MODEL), F32)] + h_shape,
        scratch_shapes=h_sems,
        compiler_params=_params("arbitrary", "arbitrary"),
    )(dqkv, du_rest, w_in, dres, *hosted)
    return out[0], out[1:]


def _in_proj_dw(xb, dqkv, du_rest):
    T = xb.shape[0]
    tk = min(T, 512)

    def body(x_ref, a1_ref, a2_ref, dw_ref, db_ref):
        s, t = pl.program_id(0), pl.program_id(1)

        @pl.when(t == 0)
        def _():
            dw_ref[...] = jnp.zeros_like(dw_ref)
            db_ref[...] = jnp.zeros_like(db_ref)

        def acc(a):
            dw_ref[...] += _dot_tn(x_ref[...], a.astype(BF16))
            db_ref[...] += jnp.sum(a, axis=0, keepdims=True)

        @pl.when(s < N_QKV_SEG)
        def _():
            acc(a1_ref[...])

        @pl.when(s >= N_QKV_SEG)
        def _():
            acc(a2_ref[...])

    return pl.pallas_call(
        body, name="in_proj_dw", grid=(N_SEG, T // tk),
        in_specs=[pl.BlockSpec((tk, D_MODEL), lambda s, t: (t, 0))] + _du_specs(tk, lambda s, t: t, lambda s, t: s),
        out_specs=[pl.BlockSpec((D_MODEL, SEG), lambda s, t: (0, s)),
                   pl.BlockSpec((None, 1, SEG), lambda s, t: (s, 0, 0))],
        out_shape=[jax.ShapeDtypeStruct((D_MODEL, D_IN), F32), jax.ShapeDtypeStruct((N_SEG, 1, SEG), F32)],
        compiler_params=_params("parallel", "arbitrary"),
    )(xb, dqkv, du_rest)


ATT_CHUNK = 32


def _chunks():
    return [slice(r, r + ATT_CHUNK) for r in range(0, ATT_BLOCK, ATT_CHUNK)]


def _twice(rhs):
    return jnp.concatenate([rhs, rhs], axis=0)


def _fail_and_log_beta(z):
    t = jnp.maximum(z, 0.0) + jnp.log(1.0 + jnp.exp(-jnp.abs(z)))
    return t, z - t


def _store_split(dst, rows, val):
    hi = val.astype(BF16)
    dst[rows, 0:ATT_BLOCK] = hi
    dst[rows, ATT_BLOCK:2 * ATT_BLOCK] = (val - hi.astype(F32)).astype(BF16)


def _causal_rows(rows):
    row = lax.broadcasted_iota(jnp.int32, (ATT_CHUNK, ATT_BLOCK), 0) + rows.start
    col = lax.broadcasted_iota(jnp.int32, (ATT_CHUNK, ATT_BLOCK), 1)
    return col < row


def _triangles():
    row = lax.broadcasted_iota(jnp.int32, (ATT_BLOCK, ATT_BLOCK), 0)
    col = lax.broadcasted_iota(jnp.int32, (ATT_BLOCK, ATT_BLOCK), 1)
    return _twice((row > col).astype(BF16)), _twice((row >= col).astype(BF16))


def _staggered(stages):
    stages[0](0)
    stages[0](1)
    for k in range(1, len(stages), 2):
        for h in range(2):
            stages[k](h)
            stages[k + 1](h)


EXP_UNDERFLOW = 104.0


def _sweep_keys(i, tile, rsum):
    tile(i, True)

    def live(carry):
        j, low = carry
        return jnp.logical_and(j >= 0, low < EXP_UNDERFLOW)

    def step(carry):
        j, _ = carry
        tile(j, False)
        return j - 1, jnp.min(rsum[...])

    lax.while_loop(live, step, (i - 1, jnp.min(rsum[...])))


def _attn_fwd(u, gather=()):
    T = u.shape[0]
    B = ATT_BLOCK
    nq = T // B
    n_pairs = D_ATT // 128
    n_g = len(gather)

    def body(*refs):
        (q_ref, k_ref, v_ref), g_ins, (o_ref,), g_outs, scratch = _split_refs(refs, 3, n_g, 1, n_g)
        kb, vb, acc, rsum, t0, zs, lt, cat = scratch[:8]
        p, i = pl.program_id(0), pl.program_id(1)

        if n_g:
            @pl.when(jnp.logical_and(p == 0, i == 0))
            def _():
                _Gather(g_ins, g_outs, *scratch[8:]).start()

            @pl.when(jnp.logical_and(p == n_pairs - 1, i == 0))
            def _():
                _Gather(g_ins, g_outs, *scratch[8:]).forward()

        @pl.when(i == 0)
        def _():
            for h in range(2):
                kb[h] = k_ref[:, HEAD_DIM * h:HEAD_DIM * (h + 1)].astype(BF16)
                vb[h] = v_ref[:, HEAD_DIM * h:HEAD_DIM * (h + 1)].astype(BF16)

        tri2, _ = _triangles()
        hs = [slice(HEAD_DIM * h, HEAD_DIM * (h + 1)) for h in range(2)]
        q = [(q_ref[:, hs[h]] * ATT_SCALE).astype(BF16) for h in range(2)]
        acc[...] = jnp.zeros_like(acc)
        rsum[...] = jnp.zeros_like(rsum)

        def tile(j, masked):
            keys = pl.ds(pl.multiple_of(j * B, B), B)

            def scores(h):
                zs[h] = _dot_nt(q[h], kb[h, keys, :])

            def fails(h):
                for r in _chunks():
                    t, a = _fail_and_log_beta(zs[h, r, :])
                    if masked:
                        t = jnp.where(_causal_rows(r), t, 0.0)
                    _store_split(cat.at[h], r, t)
                    zs[h, r, :] = a
                    t0[h, r, :] = t[:, 0:1]

            def later_sums(h):
                lt[h] = _dot(cat[h], tri2)

            def weights(h):
                for r in _chunks():
                    later = lt[h, r, :]
                    w = jnp.exp(zs[h, r, :] - later - rsum[h, r, :])
                    if masked:
                        w = jnp.where(_causal_rows(r), w, 0.0)
                    _store_split(cat.at[h], r, w)
                    rsum[h, r, :] += later[:, 0:1] + t0[h, r, :]

            def values(h):
                acc[h] += _dot(cat[h], _twice(vb[h, keys, :]))

            _staggered([scores, fails, later_sums, weights, values])

        _sweep_keys(i, tile, rsum)
        for h in range(2):
            o_ref[:, hs[h]] = acc[h]

        if n_g:
            @pl.when(jnp.logical_and(p == n_pairs - 1, i == nq - 1))
            def _():
                _Gather(g_ins, g_outs, *scratch[8:]).wait()

    out = pl.pallas_call(
        body, name="attn_fwd", grid=(n_pairs, nq),
        in_specs=[pl.BlockSpec((B, 128), lambda p, i: (i, p)),
                  pl.BlockSpec((T, 128), lambda p, i: (0, 4 + p)),
                  pl.BlockSpec((T, 128), lambda p, i: (0, 8 + p))] + [ANY] * n_g,
        out_specs=[pl.BlockSpec((B, 128), lambda p, i: (i, p))] + [ANY] * n_g,
        out_shape=[jax.ShapeDtypeStruct((T, D_ATT), F32)] + _gathered_shapes(gather),
        scratch_shapes=[pltpu.VMEM((2, T, HEAD_DIM), BF16), pltpu.VMEM((2, T, HEAD_DIM), BF16),
                        pltpu.VMEM((2, B, HEAD_DIM), F32), pltpu.VMEM((2, B, 1), F32), pltpu.VMEM((2, B, 1), F32),
                        pltpu.VMEM((2, B, B), F32), pltpu.VMEM((2, B, B), F32), pltpu.VMEM((2, B, 2 * B), BF16)]
        + (_Gather.semaphores(n_g) if n_g else []),
        compiler_params=_params("arbitrary", "arbitrary"),
    )(u, u, u, *gather)
    return out[0], out[1:]


def _attn_bwd(u, att, datt, hosted=()):
    T = u.shape[0]
    B = ATT_BLOCK
    nq = T // B
    n_pairs = D_ATT // 128
    n_h = len(hosted)
    h_in, h_out, h_shape, h_sems = _host(hosted)

    def body(*refs):
        (q_ref, k_ref, v_ref, o_ref, do_ref), h_ins, (dqkv_ref,), h_outs, scratch = _split_refs(refs, 5, n_h, 1, n_h)
        n_scratch = 16
        (kb, vb, dkacc, dvacc, dqacc, rsum, gsum, total, t0, lt0, zs, gs, lt, cat, wb, dzb) = scratch[:n_scratch]
        p, i = pl.program_id(0), pl.program_id(1)

        if n_h:
            @pl.when(jnp.logical_and(p == 0, i == 0))
            def _():
                _Exchange(h_ins, h_outs, *scratch[n_scratch:]).start()

        @pl.when(i == 0)
        def _():
            for h in range(2):
                kb[h] = k_ref[:, HEAD_DIM * h:HEAD_DIM * (h + 1)].astype(BF16)
                vb[h] = v_ref[:, HEAD_DIM * h:HEAD_DIM * (h + 1)].astype(BF16)
            dkacc[...] = jnp.zeros_like(dkacc)
            dvacc[...] = jnp.zeros_like(dvacc)

        tri2, tri_incl2 = _triangles()
        qrows = pl.ds(pl.multiple_of(i * B, B), B)
        hs = [slice(HEAD_DIM * h, HEAD_DIM * (h + 1)) for h in range(2)]
        q = [(q_ref[:, hs[h]] * ATT_SCALE).astype(BF16) for h in range(2)]
        dob = [do_ref[:, hs[h]].astype(BF16) for h in range(2)]
        for h in range(2):
            total[h] = jnp.sum(dob[h].astype(F32) * o_ref[:, hs[h]], axis=1, keepdims=True)
        dqacc[...] = jnp.zeros_like(dqacc)
        rsum[...] = jnp.zeros_like(rsum)
        gsum[...] = jnp.zeros_like(gsum)

        def tile(j, masked):
            keys = pl.ds(pl.multiple_of(j * B, B), B)

            def scores(h):
                zs[h] = _dot_nt(q[h], kb[h, keys, :])
                gs[h] = _dot_nt(dob[h], vb[h, keys, :])

            def fails(h):
                for r in _chunks():
                    t, a = _fail_and_log_beta(zs[h, r, :])
                    if masked:
                        t = jnp.where(_causal_rows(r), t, 0.0)
                    _store_split(cat.at[h], r, t)
                    zs[h, r, :] = a
                    t0[h, r, :] = t[:, 0:1]

            def later_sums(h):
                lt[h] = _dot(cat[h], tri2)

            def weights(h):
                for r in _chunks():
                    later = lt[h, r, :]
                    w = jnp.exp(zs[h, r, :] - later - rsum[h, r, :])
                    if masked:
                        w = jnp.where(_causal_rows(r), w, 0.0)
                    g = gs[h, r, :] * w
                    gs[h, r, :] = g
                    _store_split(cat.at[h], r, g)
                    wb[h, r, :] = w.astype(BF16)
                    lt0[h, r, :] = later[:, 0:1]

            def suffix_sums(h):
                lt[h] = _dot(cat[h], tri_incl2)

            def score_grads(h):
                for r in _chunks():
                    suffix = lt[h, r, :]
                    g = gs[h, r, :]
                    before = total[h, r, :] - (suffix + gsum[h, r, :])
                    dz = g - jnp.exp(zs[h, r, :]) * (g + before)
                    if masked:
                        dz = jnp.where(_causal_rows(r), dz, 0.0)
                    dzb[h, r, :] = dz.astype(BF16)
                    rsum[h, r, :] += lt0[h, r, :] + t0[h, r, :]
                    gsum[h, r, :] += suffix[:, 0:1]

            def input_grads(h):
                dqacc[h] += _dot(dzb[h], kb[h, keys, :])
                dkacc[h, keys, :] += _dot_tn(dzb[h], q[h])
                dvacc[h, keys, :] += _dot_tn(wb[h], dob[h])

            _staggered([scores, fails, later_sums, weights, suffix_sums, score_grads, input_grads])

        _sweep_keys(i, tile, rsum)
        for h in range(2):
            dqkv_ref[0, qrows, hs[h]] = dqacc[h] * ATT_SCALE

        @pl.when(i == nq - 1)
        def _():
            for h in range(2):
                dqkv_ref[1, :, hs[h]] = dkacc[h]
                dqkv_ref[2, :, hs[h]] = dvacc[h]

        if n_h:
            @pl.when(jnp.logical_and(p == n_pairs - 1, i == nq - 1))
            def _():
                _Exchange(h_ins, h_outs, *scratch[n_scratch:]).wait()

    out = pl.pallas_call(
        body, name="attn_bwd", grid=(n_pairs, nq),
        in_specs=[pl.BlockSpec((B, 128), lambda p, i: (i, p)),
                  pl.BlockSpec((T, 128), lambda p, i: (0, 4 + p)),
                  pl.BlockSpec((T, 128), lambda p, i: (0, 8 + p)),
                  pl.BlockSpec((B, 128), lambda p, i: (i, p)),
                  pl.BlockSpec((B, 128), lambda p, i: (i, p))] + h_in,
        out_specs=[pl.BlockSpec((3, T, 128), lambda p, i: (0, 0, p))] + h_out,
        out_shape=[jax.ShapeDtypeStruct((3, T, D_ATT), F32)] + h_shape,
        scratch_shapes=[pltpu.VMEM((2, T, HEAD_DIM), BF16), pltpu.VMEM((2, T, HEAD_DIM), BF16),
                        pltpu.VMEM((2, T, HEAD_DIM), F32), pltpu.VMEM((2, T, HEAD_DIM), F32),
                        pltpu.VMEM((2, B, HEAD_DIM), F32)]
        + [pltpu.VMEM((2, B, 1), F32)] * 5
        + [pltpu.VMEM((2, B, B), F32)] * 3
        + [pltpu.VMEM((2, B, 2 * B), BF16), pltpu.VMEM((2, B, B), BF16), pltpu.VMEM((2, B, B), BF16)]
        + h_sems,
        compiler_params=_params("arbitrary", "arbitrary"),
    )(u, u, u, att, datt, *hosted)
    return out[0], out[1:]


MIX_TILE = 128


def _u_specs(tm, tile_of):
    return [pl.BlockSpec((tm, SEG), functools.partial(lambda k, *g: (tile_of(*g), k), k)) for k in range(3, N_SEG)]


SHIFT_ROWS = 24


def _shifted_scratch(tm):
    return pltpu.VMEM((7, tm + SHIFT_ROWS, D_CONV), F32)


def _make_shifted(base, shifted, tm):
    for b in range(1, 8):
        shifted[b - 1] = base[b:b + tm + SHIFT_ROWS, :]


def _tap_window(base, shifted, start, tm):
    whole, b = divmod(start, 8)
    if b == 0:
        return base[start:start + tm, :]
    return shifted[b - 1, 8 * whole:8 * whole + tm, :]


def _conv(ext, ext_shifted, cw_ref, tm):
    acc = cw_ref[0:1, :] * _tap_window(ext, ext_shifted, 2, tm)
    for k in range(1, CONV_WIDTH):
        acc = acc + cw_ref[k:k + 1, :] * _tap_window(ext, ext_shifted, 2 + k, tm)
    return acc


def _norm_stats(v):
    mu = jnp.mean(v, axis=-1, keepdims=True)
    vc = v - mu
    rstd = lax.rsqrt(jnp.mean(vc * vc, axis=-1, keepdims=True) + LN_EPS)
    return vc * rstd, rstd


def _norm_bwd(dy_scaled, xhat, rstd):
    return rstd * (dy_scaled - jnp.mean(dy_scaled, axis=-1, keepdims=True)
                   - xhat * jnp.mean(dy_scaled * xhat, axis=-1, keepdims=True))


def _mix_fwd(x, u, att, wap, wcp, wout, cw, cvec, dvec, layer):
    T = x.shape[0]
    tm = MIX_TILE

    def body(x_ref, za_ref, ga_ref, gb_ref, zc_ref, ga0_ref, ga1_ref, gc0_ref, gc1_ref, att_ref,
             wap_ref, wcp_ref, wout_ref, cw_ref, cvec_ref, dvec_ref, xn_ref, xnb_ref, h_ref, ext, ext_shifted):
        i = pl.program_id(0)

        @pl.when(i == 0)
        def _():
            ext[0:CONV_PAD, :] = jnp.zeros((CONV_PAD, D_CONV), F32)

        za = za_ref[...]
        attg = att_ref[...] * (za * _sig(za))
        ab = _dot(attg.astype(BF16), wap_ref[...])

        ext[CONV_PAD:CONV_PAD + tm, :] = ga_ref[...] * _sig(gb_ref[...])
        _make_shifted(ext, ext_shifted, tm)
        c1 = _conv(ext, ext_shifted, cw_ref, tm) + cvec_ref[0:1, :]
        ext[0:CONV_PAD, :] = ext[tm:tm + CONV_PAD, :]
        xh, _ = _norm_stats(c1)
        c2 = xh * cvec_ref[1:2, :] + cvec_ref[2:3, :]
        zc = zc_ref[...]
        cz = (c2 * _sig(c2)) * (zc * _sig(zc))
        cb = _dot(cz.astype(BF16), wcp_ref[...]) + dvec_ref[0:1, :]

        g_att = jnp.concatenate([ga0_ref[...], ga1_ref[...]], axis=1)
        g_conv = jnp.concatenate([gc0_ref[...], gc1_ref[...]], axis=1)
        merged = _sig(g_att) * ab + _sig(g_conv) * cb
        h = DEEPNORM_ALPHA * x_ref[...] + _dot(merged.astype(BF16), wout_ref[...])
        h_ref[...] = h
        hh, _ = _norm_stats(h)
        xn = hh * dvec_ref[1:2, :] + dvec_ref[2:3, :]
        xn_ref[...] = xn
        xnb_ref[...] = xn.astype(BF16)

    row = pl.BlockSpec((tm, D_MODEL), lambda i: (i, 0))
    return pl.pallas_call(
        body, name="mix_fwd", grid=(T // tm,),
        in_specs=[row] + _u_specs(tm, lambda i: i) + [
            pl.BlockSpec((tm, D_ATT), lambda i: (i, 0)),
            pl.BlockSpec((D_ATT, D_MODEL), lambda i: (0, 0)),
            pl.BlockSpec((D_CONV, D_MODEL), lambda i: (0, 0)),
            pl.BlockSpec((D_MODEL, D_MODEL), lambda i: (0, 0)),
            pl.BlockSpec((CONV_PAD, D_CONV), lambda i: (0, 0)),
            pl.BlockSpec((None, 8, D_CONV), lambda i: (layer, 0, 0)),
            pl.BlockSpec((None, 8, D_MODEL), lambda i: (layer, 0, 0))],
        out_specs=[row, row, row],
        out_shape=[jax.ShapeDtypeStruct((T, D_MODEL), F32), jax.ShapeDtypeStruct((T, D_MODEL), BF16),
                   jax.ShapeDtypeStruct((T, D_MODEL), F32)],
        scratch_shapes=[pltpu.VMEM((tm + CONV_PAD, D_CONV), F32), _shifted_scratch(tm)],
        compiler_params=_params("arbitrary"),
    )(x, *([u] * 8), att, wap, wcp, wout, cw, cvec, dvec)


def _mix_bwd(dxo, h, u, att, wap, wcp, wout, cw, cvec, dvec, layer, hosted=()):
    T = dxo.shape[0]
    tm = MIX_TILE
    nt = T // tm
    halo_per_tile = tm // CONV_PAD
    n_h = len(hosted)
    h_in, h_out, h_shape, h_sems = _host(hosted)

    def body(*refs):
        ins, h_ins, outs, h_outs, scratch = _split_refs(refs, 19, n_h, 9, n_h)
        (dxo_ref, h_ref, za_ref, ga_ref, gb_ref, zc_ref, ga0_ref, ga1_ref, gc0_ref, gc1_ref, hga_ref, hgb_ref,
         att_ref, wap_ref, wcp_ref, wout_ref, cw_ref, cvec_ref, dvec_ref) = ins
        datt_ref, dur_ref, dres_ref, dwap_ref, dwcp_ref, dwout_ref, dcw_ref, dcvec_ref, ddvec_ref = outs
        ext, dext, ext_shifted, dext_shifted = scratch[:4]
        r = pl.program_id(0)
        i = nt - 1 - r

        if n_h:
            @pl.when(r == 0)
            def _():
                _Exchange(h_ins, h_outs, *scratch[4:]).start()

        @pl.when(r == 0)
        def _():
            for ref in (dwap_ref, dwcp_ref, dwout_ref, dcw_ref, dcvec_ref, ddvec_ref):
                ref[...] = jnp.zeros_like(ref)
            dext[tm:tm + CONV_PAD, :] = jnp.zeros((CONV_PAD, D_CONV), F32)

        za = za_ref[...]
        s_za = _sig(za)
        silu_za = za * s_za
        att_t = att_ref[...]
        attg = (att_t * silu_za).astype(BF16)
        ab = _dot(attg, wap_ref[...])

        ga, gb = ga_ref[...], gb_ref[...]
        s_gb = _sig(gb)
        halo = hga_ref[...] * _sig(hgb_ref[...])
        ext[0:CONV_PAD, :] = jnp.where(i > 0, halo, 0.0)
        ext[CONV_PAD:CONV_PAD + tm, :] = ga * s_gb
        _make_shifted(ext, ext_shifted, tm)
        c1 = _conv(ext, ext_shifted, cw_ref, tm) + cvec_ref[0:1, :]
        xh, rstd_c = _norm_stats(c1)
        c2 = xh * cvec_ref[1:2, :] + cvec_ref[2:3, :]
        s_c2 = _sig(c2)
        c3 = c2 * s_c2
        zc = zc_ref[...]
        s_zc = _sig(zc)
        cz = (c3 * (zc * s_zc)).astype(BF16)
        cb = _dot(cz, wcp_ref[...]) + dvec_ref[0:1, :]

        s_ga = _sig(jnp.concatenate([ga0_ref[...], ga1_ref[...]], axis=1))
        s_gc = _sig(jnp.concatenate([gc0_ref[...], gc1_ref[...]], axis=1))
        merged = (s_ga * ab + s_gc * cb).astype(BF16)

        dxo_t = dxo_ref[...]
        hh, rstd_h = _norm_stats(h_ref[...])
        ddvec_ref[1:2, :] += jnp.sum(dxo_t * hh, axis=0, keepdims=True)
        ddvec_ref[2:3, :] += jnp.sum(dxo_t, axis=0, keepdims=True)
        dh = _norm_bwd(dxo_t * dvec_ref[1:2, :], hh, rstd_h)
        dres_ref[...] = DEEPNORM_ALPHA * dh
        dy = dh.astype(BF16)

        dwout_ref[...] += _dot_tn(merged, dy)
        dm = _dot_nt(dy, wout_ref[...])
        dab = (dm * s_ga).astype(BF16)
        dcb = dm * s_gc
        dur_ref[:, 4 * SEG:6 * SEG] = dm * ab * (s_ga * (1.0 - s_ga))
        dur_ref[:, 6 * SEG:8 * SEG] = dm * cb * (s_gc * (1.0 - s_gc))

        dwap_ref[...] += _dot_tn(attg, dab)
        dattg = _dot_nt(dab, wap_ref[...])
        datt_ref[...] = dattg * silu_za
        dur_ref[:, 0:SEG] = dattg * att_t * (s_za * (1.0 + za * (1.0 - s_za)))

        ddvec_ref[0:1, :] += jnp.sum(dcb, axis=0, keepdims=True)
        dcbb = dcb.astype(BF16)
        dwcp_ref[...] += _dot_tn(cz, dcbb)
        dcz = _dot_nt(dcbb, wcp_ref[...])
        dur_ref[:, 3 * SEG:4 * SEG] = dcz * c3 * (s_zc * (1.0 + zc * (1.0 - s_zc)))
        dc2 = dcz * (zc * s_zc) * (s_c2 * (1.0 + c2 * (1.0 - s_c2)))
        dcvec_ref[1:2, :] += jnp.sum(dc2 * xh, axis=0, keepdims=True)
        dcvec_ref[2:3, :] += jnp.sum(dc2, axis=0, keepdims=True)
        dc1 = _norm_bwd(dc2 * cvec_ref[1:2, :], xh, rstd_c)
        dcvec_ref[0:1, :] += jnp.sum(dc1, axis=0, keepdims=True)
        for k in range(CONV_WIDTH):
            dcw_ref[k:k + 1, :] += jnp.sum(dc1 * _tap_window(ext, ext_shifted, 2 + k, tm), axis=0, keepdims=True)
        dext[0:tm, :] = dc1
        _make_shifted(dext, dext_shifted, tm)
        dc0 = cw_ref[0:1, :] * _tap_window(dext, dext_shifted, CONV_WIDTH - 1, tm)
        for k in range(1, CONV_WIDTH):
            dc0 = dc0 + cw_ref[k:k + 1, :] * _tap_window(dext, dext_shifted, CONV_WIDTH - 1 - k, tm)
        dext[tm:tm + CONV_PAD, :] = dext[0:CONV_PAD, :]
        dur_ref[:, SEG:2 * SEG] = dc0 * s_gb
        dur_ref[:, 2 * SEG:3 * SEG] = dc0 * ga * (s_gb * (1.0 - s_gb))

        if n_h:
            @pl.when(r == nt - 1)
            def _():
                _Exchange(h_ins, h_outs, *scratch[4:]).wait()

    rev = lambda r: nt - 1 - r
    row = pl.BlockSpec((tm, D_MODEL), lambda r: (rev(r), 0))
    halo_spec = [pl.BlockSpec((CONV_PAD, SEG),
                              functools.partial(lambda k, r: (jnp.maximum(rev(r) * halo_per_tile - 1, 0), k), k))
                 for k in (4, 5)]
    const = lambda shape: pl.BlockSpec(shape, lambda r: (0,) * len(shape))
    out = pl.pallas_call(
        body, name="mix_bwd", grid=(nt,),
        in_specs=[row, row] + _u_specs(tm, rev) + halo_spec + [
            pl.BlockSpec((tm, D_ATT), lambda r: (rev(r), 0)),
            pl.BlockSpec((D_ATT, D_MODEL), lambda r: (0, 0)),
            pl.BlockSpec((D_CONV, D_MODEL), lambda r: (0, 0)),
            pl.BlockSpec((D_MODEL, D_MODEL), lambda r: (0, 0)),
            pl.BlockSpec((CONV_PAD, D_CONV), lambda r: (0, 0)),
            pl.BlockSpec((None, 8, D_CONV), lambda r: (layer, 0, 0)),
            pl.BlockSpec((None, 8, D_MODEL), lambda r: (layer, 0, 0))] + h_in,
        out_specs=[pl.BlockSpec((tm, D_ATT), lambda r: (rev(r), 0)),
                   pl.BlockSpec((tm, 8 * SEG), lambda r: (rev(r), 0)),
                   row,
                   const((D_ATT, D_MODEL)), const((D_CONV, D_MODEL)), const((D_MODEL, D_MODEL)),
                   const((CONV_PAD, D_CONV)), const((8, D_CONV)), const((8, D_MODEL))] + h_out,
        out_shape=[jax.ShapeDtypeStruct((T, D_ATT), F32), jax.ShapeDtypeStruct((T, 8 * SEG), F32),
                   jax.ShapeDtypeStruct((T, D_MODEL), F32),
                   jax.ShapeDtypeStruct((D_ATT, D_MODEL), F32), jax.ShapeDtypeStruct((D_CONV, D_MODEL), F32),
                   jax.ShapeDtypeStruct((D_MODEL, D_MODEL), F32), jax.ShapeDtypeStruct((CONV_PAD, D_CONV), F32),
                   jax.ShapeDtypeStruct((8, D_CONV), F32), jax.ShapeDtypeStruct((8, D_MODEL), F32)] + h_shape,
        scratch_shapes=[pltpu.VMEM((tm + CONV_PAD, D_CONV), F32), pltpu.VMEM((tm + CONV_PAD, D_CONV), F32),
                        _shifted_scratch(tm), _shifted_scratch(tm)] + h_sems,
        compiler_params=_params("arbitrary"),
    )(dxo, h, *([u] * 10), att, wap, wcp, wout, cw, cvec, dvec, *hosted)
    return out[:9], out[9:]


def _loss_head(y, target):
    T = y.shape[0]
    tm = min(T, 512)

    def body(y_ref, t_ref, dy_ref, loss_ref):
        @pl.when(pl.program_id(0) == 0)
        def _():
            loss_ref[...] = jnp.zeros_like(loss_ref)

        err = y_ref[...] - t_ref[...]
        dy_ref[...] = err * (1.0 / D_MODEL)
        loss_ref[...] += 0.5 * jnp.sum(jnp.mean(err * err, axis=-1, keepdims=True))

    row = pl.BlockSpec((tm, D_MODEL), lambda i: (i, 0))
    return pl.pallas_call(
        body, name="loss_head", grid=(T // tm,), in_specs=[row, row],
        out_specs=[row, pl.BlockSpec((8, 128), lambda i: (0, 0))],
        out_shape=[jax.ShapeDtypeStruct((T, D_MODEL), F32), jax.ShapeDtypeStruct((8, 128), F32)],
        compiler_params=_params("arbitrary"),
    )(y, target)


def _adamw(w, g, m, v):
    m = ADAM_B1 * m + (1.0 - ADAM_B1) * g
    v = ADAM_B2 * v + (1.0 - ADAM_B2) * (g * g)
    m_hat = m / (1.0 - ADAM_B1 ** ADAM_STEP)
    v_hat = v / (1.0 - ADAM_B2 ** ADAM_STEP)
    return -ADAM_LR * (m_hat / (jnp.sqrt(v_hat) + ADAM_EPS) + ADAM_WD * w), m, v


def _sum_slots(ref):
    total = ref[0].astype(F32)
    for j in range(1, N_DEV):
        total = total + ref[j].astype(F32)
    return total


def _adamw_sharded(recv, w, m, v, prev, layer, tr, name):
    n_layers, R, C = w.shape

    def body(*refs):
        recv_ref, w_ref, m_ref, v_ref = refs[:4]
        g_ref, d_ref, nm_ref, nv_ref = refs[-4:]
        g = _sum_slots(recv_ref)
        g_ref[...] = g
        d_ref[...], nm_ref[...], nv_ref[...] = _adamw(w_ref[...], g, m_ref[...], v_ref[...])

    slab = pl.BlockSpec((None, tr, C), lambda i: (layer, i, 0))
    n_prev = 0 if prev is None else 4
    return pl.pallas_call(
        body, name=name, grid=(R // tr,),
        in_specs=[pl.BlockSpec((N_DEV, tr, C), lambda i: (0, i, 0)), slab, slab, slab] + [ANY] * n_prev,
        out_specs=[slab] * 4,
        out_shape=[jax.ShapeDtypeStruct(w.shape, F32)] * 4,
        input_output_aliases={4 + k: k for k in range(n_prev)},
        compiler_params=_params("parallel"),
    )(recv, w, m, v, *(prev or ()))


def _sum8(g):
    R = g.shape[1]

    def body(g_ref, o_ref):
        o_ref[...] = _sum_slots(g_ref)

    return pl.pallas_call(body, name="sum_small_grads", out_shape=jax.ShapeDtypeStruct((R, 128), F32))(g)


def _adamw_small(w, g, m, v):
    def body(w_ref, g_ref, m_ref, v_ref, d_ref, nm_ref, nv_ref):
        d_ref[...], nm_ref[...], nv_ref[...] = _adamw(w_ref[...], g_ref[...], m_ref[...], v_ref[...])

    return pl.pallas_call(body, name="adamw_small", out_shape=[jax.ShapeDtypeStruct(w.shape, F32)] * 3)(w, g, m, v)


REPLICATED = (("b_in", D_IN), ("conv_b", D_CONV), ("conv_ln_g", D_CONV), ("conv_ln_b", D_CONV),
              ("b_conv_proj", D_MODEL), ("ln_g", D_MODEL), ("ln_b", D_MODEL))
N_REPLICATED = sum(n for _, n in REPLICATED)
CONV_W_SHARD_PAD = 2048


def _pack_small(rep, conv_w_shard):
    flat = jnp.concatenate([rep[k] for k, _ in REPLICATED], axis=1)
    cws = jnp.pad(conv_w_shard.reshape(DEPTH, -1), ((0, 0), (0, CONV_W_SHARD_PAD - CONV_WIDTH * 64)))
    return jnp.concatenate([flat, cws], axis=1).reshape(-1, 128)


def _unpack_small(packed):
    per = packed.reshape(DEPTH, N_REPLICATED + CONV_W_SHARD_PAD)
    out, off = {}, 0
    for k, n in REPLICATED:
        out[k] = per[:, off:off + n]
        off += n
    out["conv_w"] = per[:, off:off + CONV_WIDTH * 64].reshape(DEPTH, CONV_WIDTH, 64)
    return out


def kernel(x, w_in, b_in, conv_w, conv_b, conv_ln_g, conv_ln_b, w_att_proj, w_conv_proj, b_conv_proj, w_out, ln_g, ln_b, loss_target, m_w_in, m_b_in, m_conv_w, m_conv_b, m_conv_ln_g, m_conv_ln_b, m_w_att_proj, m_w_conv_proj, m_b_conv_proj, m_w_out, m_ln_g, m_ln_b, v_w_in, v_b_in, v_conv_w, v_conv_b, v_conv_ln_g, v_conv_ln_b, v_w_att_proj, v_w_conv_proj, v_b_conv_proj, v_w_out, v_ln_g, v_ln_b):
    x0 = x[0]
    target = loss_target[0]

    def shards(l):
        return [w_in[l].astype(BF16), w_att_proj[l].astype(BF16), w_conv_proj[l].astype(BF16),
                w_out[l].astype(BF16), conv_w[l]]

    def full_weights(g_in, g_ap, g_cp, g_out, g_cw):
        return (g_in.transpose(1, 0, 2).reshape(D_MODEL, D_IN), g_ap.transpose(1, 0, 2).reshape(D_ATT, D_MODEL),
                g_cp.transpose(1, 0, 2).reshape(D_CONV, D_MODEL), g_out.reshape(D_MODEL, D_MODEL),
                jnp.pad(g_cw.transpose(1, 0, 2).reshape(CONV_WIDTH, D_CONV), ((0, 1), (0, 0))))

    pad8 = lambda rows: jnp.pad(jnp.stack(rows, axis=1), ((0, 0), (0, 8 - len(rows)), (0, 0)))
    cvec = pad8([conv_b, conv_ln_g, conv_ln_b])
    dvec = pad8([b_conv_proj, ln_g, ln_b])
    b_in3 = b_in.reshape(DEPTH, 1, D_IN)

    weights = [full_weights(*_all_gather(shards(0), "gather_weights"))]
    xs, xbs, us, atts, hs = [x0], [x0.astype(BF16)], [], [], []
    for l in range(DEPTH):
        w_in_l, wap, wcp, wout, cw = weights[l]
        u = _in_proj(xbs[l], w_in_l, b_in3, l)
        att, gathered = _attn_fwd(u, gather=shards(l + 1) if l + 1 < DEPTH else ())
        if gathered:
            weights.append(full_weights(*gathered))
        xn, xnb, h = _mix_fwd(xs[l], u, att, wap, wcp, wout, cw, cvec, dvec, l)
        us.append(u), atts.append(att), hs.append(h), xs.append(xn), xbs.append(xnb)
    dxo, loss_part = _loss_head(xs[DEPTH], target)
    loss = lax.psum(loss_part[0, 0], ("x", "y", "c"))

    big = {"w_in": (w_in, m_w_in, v_w_in, 256), "w_att_proj": (w_att_proj, m_w_att_proj, v_w_att_proj, D_ATT),
           "w_conv_proj": (w_conv_proj, m_w_conv_proj, v_w_conv_proj, D_CONV), "w_out": (w_out, m_w_out, v_w_out, 128)}
    results = {k: None for k in big}
    small = [None] * DEPTH

    def update(k, recv, l):
        w, m, v, tr = big[k]
        results[k] = _adamw_sharded(recv, w, m, v, results[k], l, tr, "adamw_" + k)

    pending_w_in = ()
    for l in reversed(range(DEPTH)):
        w_in_l, wap, wcp, wout, cw = weights[l]
        (datt, du_rest, dres, dwap, dwcp, dwout, dcw, dcvec, ddvec), got = _mix_bwd(
            dxo, hs[l], us[l], atts[l], wap, wcp, wout, cw, cvec, dvec, l, hosted=pending_w_in)
        if got:
            update("w_in", got[0], l + 1)
        proj = [dwap.reshape(D_ATT, N_DEV, D_MODEL // N_DEV).transpose(1, 0, 2).astype(BF16),
                dwcp.reshape(D_CONV, N_DEV, D_MODEL // N_DEV).transpose(1, 0, 2).astype(BF16),
                dwout.reshape(N_DEV, D_MODEL // N_DEV, D_MODEL).astype(BF16)]
        dqkv, got = _attn_bwd(us[l], atts[l], datt, hosted=proj)
        for k, r in zip(("w_att_proj", "w_conv_proj", "w_out"), got):
            update(k, r, l)
        dwin, dbin = _in_proj_dw(xbs[l], dqkv, du_rest)
        small[l] = jnp.concatenate([dbin.reshape(-1), dcvec[:3].reshape(-1), ddvec[:3].reshape(-1), dcw.reshape(-1)])
        pending_w_in = (dwin.reshape(D_MODEL, N_DEV, W_IN_SHARD).transpose(1, 0, 2).astype(BF16),)
        dxo, got = _in_proj_dx(dqkv, du_rest, w_in_l, dres, hosted=pending_w_in if l == 0 else ())
        if got:
            update("w_in", got[0], l)
    grad_x = dxo[None]

    small_all = jnp.stack(small).reshape(-1, 128)
    (gathered,) = _all_gather([small_all], "gather_small_grads")
    per_layer = _sum8(gathered).reshape(DEPTH, -1)
    g_rep, off = {}, 0
    for k, n in (("b_in", D_IN), ("conv_b", D_CONV), ("conv_ln_g", D_CONV), ("conv_ln_b", D_CONV),
                 ("b_conv_proj", D_MODEL), ("ln_g", D_MODEL), ("ln_b", D_MODEL)):
        g_rep[k] = per_layer[:, off:off + n]
        off += n
    g_cw_full = per_layer[:, off:].reshape(DEPTH, CONV_PAD, D_CONV)[:, :CONV_WIDTH]
    g_cw_shard = lax.dynamic_slice_in_dim(g_cw_full, _my_index() * 64, 64, axis=2)
    rep_w = dict(b_in=b_in, conv_b=conv_b, conv_ln_g=conv_ln_g, conv_ln_b=conv_ln_b, b_conv_proj=b_conv_proj,
                 ln_g=ln_g, ln_b=ln_b)
    rep_m = dict(b_in=m_b_in, conv_b=m_conv_b, conv_ln_g=m_conv_ln_g, conv_ln_b=m_conv_ln_b,
                 b_conv_proj=m_b_conv_proj, ln_g=m_ln_g, ln_b=m_ln_b)
    rep_v = dict(b_in=v_b_in, conv_b=v_conv_b, conv_ln_g=v_conv_ln_g, conv_ln_b=v_conv_ln_b,
                 b_conv_proj=v_b_conv_proj, ln_g=v_ln_g, ln_b=v_ln_b)
    sd, sm, sv = _adamw_small(_pack_small(rep_w, conv_w), _pack_small(g_rep, g_cw_shard),
                              _pack_small(rep_m, m_conv_w), _pack_small(rep_v, v_conv_w))
    small_out = [dict(g_rep, conv_w=g_cw_shard), _unpack_small(sd), _unpack_small(sm), _unpack_small(sv)]

    order = ["w_in", "b_in", "conv_w", "conv_b", "conv_ln_g", "conv_ln_b", "w_att_proj", "w_conv_proj",
             "b_conv_proj", "w_out", "ln_g", "ln_b"]
    outs = [loss, grad_x]
    for kind in range(4):
        outs += [results[k][kind] if k in big else small_out[kind][k] for k in order]
    return tuple(outs)
```

```python
import functools

import jax
import jax.numpy as jnp
from jax import lax
from jax.experimental import pallas as pl
from jax.experimental.pallas import tpu as pltpu

F32, BF16 = jnp.float32, jnp.bfloat16

D_MODEL = 1024
D_ATT = 512
D_CONV = 512
HEAD_DIM = 64
CONV_WIDTH = 31
CONV_PAD = 32
DEPTH = 4
N_DEV = 8
SEG = 512
D_IN = 4 * D_ATT + 3 * D_CONV + 2 * D_MODEL
N_SEG = D_IN // SEG
N_QKV_SEG = 3
W_IN_SHARD = D_IN // N_DEV
LN_EPS = 1e-5
DEEPNORM_ALPHA = (2 * DEPTH) ** 0.25
ATT_SCALE = HEAD_DIM ** -0.5
ATT_BLOCK = 256
ADAM_LR, ADAM_B1, ADAM_B2, ADAM_EPS, ADAM_WD, ADAM_STEP = 0.001, 0.9, 0.999, 1e-08, 0.01, 10
VMEM_LIMIT = 56 * 1024 * 1024
MESH = pl.DeviceIdType.MESH
ANY = pl.BlockSpec(memory_space=pl.ANY)


def _params(*semantics):
    return pltpu.CompilerParams(dimension_semantics=semantics, vmem_limit_bytes=VMEM_LIMIT)


def _sig(x):
    return 1.0 / (1.0 + jnp.exp(-x))


def _dot(a, b):
    return jnp.dot(a, b, preferred_element_type=F32)


def _dot_nt(a, b):
    return lax.dot_general(a, b, (((1,), (1,)), ((), ())), preferred_element_type=F32)


def _dot_tn(a, b):
    return lax.dot_general(a, b, (((0,), (0,)), ((), ())), preferred_element_type=F32)


def _my_index():
    return 4 * lax.axis_index("x") + 2 * lax.axis_index("y") + lax.axis_index("c")


class _Gather:
    def __init__(self, ins, outs, send_sems, recv_sems, local_sems):
        self.n = n = len(ins)
        x, y, c = lax.axis_index("x"), lax.axis_index("y"), lax.axis_index("c")
        me, sibling = (x, y, c), (x, y, 1 - c)
        chips = [(1 - x, y), (x, 1 - y), (1 - x, 1 - y)]

        def slot(a, dev):
            return outs[a].at[4 * dev[0] + 2 * dev[1] + dev[2]]

        def copy(a, k, block, to, src=None):
            return pltpu.make_async_remote_copy(
                src_ref=slot(a, block) if src is None else src, dst_ref=slot(a, block),
                send_sem=send_sems.at[7 * a + k], recv_sem=recv_sems.at[7 * a + k], device_id=to, device_id_type=MESH)

        self.mine = [pltpu.make_async_copy(ins[a], slot(a, me), local_sems.at[a]) for a in range(n)]
        self.first, self.landed, self.passed, self.last = [], [], [], []
        for a in range(n):
            self.first.append(copy(a, 0, me, sibling, src=ins[a]))
            self.first += [copy(a, 1 + j, me, (*chip, c), src=ins[a]) for j, chip in enumerate(chips)]
        for j, chip in enumerate(chips):
            for a in range(n):
                self.landed.append(copy(a, 1 + j, (*chip, c), me))
                self.passed.append(copy(a, 4 + j, (*chip, c), sibling))
        for a in range(n):
            self.last.append(copy(a, 0, sibling, me))
            self.last += [copy(a, 4 + j, (*chip, 1 - c), me) for j, chip in enumerate(chips)]

    @staticmethod
    def semaphores(n):
        return [pltpu.SemaphoreType.DMA((7 * n,)), pltpu.SemaphoreType.DMA((7 * n,)), pltpu.SemaphoreType.DMA((n,))]

    def start(self):
        for cp in self.mine + self.first:
            cp.start()

    def forward(self):
        for landed, passed in zip(self.landed, self.passed):
            landed.wait_recv()
            passed.start()

    def wait(self):
        for cp in self.last:
            cp.wait_recv()
        for cp in self.first + self.passed:
            cp.wait_send()
        for cp in self.mine:
            cp.wait()


def _gathered_shapes(shards):
    return [jax.ShapeDtypeStruct((N_DEV,) + s.shape, s.dtype) for s in shards]


def _all_gather(shards, name):
    n = len(shards)

    def body(*refs):
        g = _Gather(refs[:n], refs[n:2 * n], *refs[2 * n:])
        g.start()
        g.forward()
        g.wait()

    return pl.pallas_call(
        body, name=name, out_shape=_gathered_shapes(shards),
        in_specs=[ANY] * n, out_specs=[ANY] * n, scratch_shapes=_Gather.semaphores(n),
    )(*shards)


class _Exchange:
    def __init__(self, ins, outs, send_sems, recv_sems, local_sems):
        n = len(ins)
        x, y, c = lax.axis_index("x"), lax.axis_index("y"), lax.axis_index("c")
        me_idx = 4 * x + 2 * y + c
        self.mine = [pltpu.make_async_copy(ins[a].at[me_idx], outs[a].at[me_idx], local_sems.at[a])
                     for a in range(n)]
        self.sends, self.recvs = [], []
        for d in (1, 2, 4, 3, 5, 6, 7):
            px = 1 - x if d & 4 else x
            py = 1 - y if d & 2 else y
            pc = 1 - c if d & 1 else c
            idx = 4 * px + 2 * py + pc
            for a in range(n):
                sems = dict(send_sem=send_sems.at[7 * a + d - 1], recv_sem=recv_sems.at[7 * a + d - 1],
                            device_id=(px, py, pc), device_id_type=MESH)
                self.sends.append(pltpu.make_async_remote_copy(
                    src_ref=ins[a].at[idx], dst_ref=outs[a].at[me_idx], **sems))
                self.recvs.append(pltpu.make_async_remote_copy(
                    src_ref=ins[a].at[idx], dst_ref=outs[a].at[idx], **sems))

    @staticmethod
    def semaphores(n):
        return [pltpu.SemaphoreType.DMA((7 * n,)), pltpu.SemaphoreType.DMA((7 * n,)), pltpu.SemaphoreType.DMA((n,))]

    def start(self):
        for cp in self.mine + self.sends:
            cp.start()

    def wait(self):
        for cp in self.recvs:
            cp.wait_recv()
        for cp in self.sends:
            cp.wait_send()
        for cp in self.mine:
            cp.wait()


def _host(hosted):
    n = len(hosted)
    return ([ANY] * n, [ANY] * n, [jax.ShapeDtypeStruct(p.shape, p.dtype) for p in hosted],
            _Exchange.semaphores(n) if n else [])


def _split_refs(refs, *counts):
    out, at = [], 0
    for k in counts:
        out.append(refs[at:at + k])
        at += k
    return out + [refs[at:]]


def _in_proj(xb, w_in, b_in, layer):
    T = xb.shape[0]
    tm, tn = min(T, 1024), SEG

    def body(x_ref, w_ref, b_ref, u_ref):
        u_ref[...] = _dot(x_ref[...], w_ref[...]) + b_ref[...]

    return pl.pallas_call(
        body, name="in_proj", grid=(T // tm, D_IN // tn),
        in_specs=[pl.BlockSpec((tm, D_MODEL), lambda m, n: (m, 0)),
                  pl.BlockSpec((D_MODEL, tn), lambda m, n: (0, n)),
                  pl.BlockSpec((None, 1, tn), lambda m, n: (layer, 0, n))],
        out_specs=pl.BlockSpec((tm, tn), lambda m, n: (m, n)),
        out_shape=jax.ShapeDtypeStruct((T, D_IN), F32),
        compiler_params=_params("parallel", "parallel"),
    )(xb, w_in, b_in)


def _du_specs(tm, row_of, seg_of):
    return [pl.BlockSpec((None, tm, SEG), lambda *g: (jnp.minimum(seg_of(*g), N_QKV_SEG - 1), row_of(*g), 0)),
            pl.BlockSpec((tm, SEG), lambda *g: (row_of(*g), jnp.maximum(seg_of(*g) - N_QKV_SEG, 0)))]


def _in_proj_dx(dqkv, du_rest, w_in, dres, hosted=()):
    T = du_rest.shape[0]
    tm = min(T, 2048)
    nm = T // tm
    n_h = len(hosted)
    h_in, h_out, h_shape, h_sems = _host(hosted)

    def body(*refs):
        (a1_ref, a2_ref, w_ref, r_ref), h_ins, (o_ref,), h_outs, sems = _split_refs(refs, 4, n_h, 1, n_h)
        m, s = pl.program_id(0), pl.program_id(1)

        if n_h:
            @pl.when(jnp.logical_and(m == 0, s == 0))
            def _():
                _Exchange(h_ins, h_outs, *sems).start()

        @pl.when(s == 0)
        def _():
            o_ref[...] = r_ref[...]

        @pl.when(s < N_QKV_SEG)
        def _():
            o_ref[...] += _dot_nt(a1_ref[...], w_ref[...])

        @pl.when(s >= N_QKV_SEG)
        def _():
            o_ref[...] += _dot_nt(a2_ref[...], w_ref[...])

        if n_h:
            @pl.when(jnp.logical_and(m == nm - 1, s == N_SEG - 1))
            def _():
                _Exchange(h_ins, h_outs, *sems).wait()

    out = pl.pallas_call(
        body, name="in_proj_dx", grid=(nm, N_SEG),
        in_specs=_du_specs(tm, lambda m, s: m, lambda m, s: s) + [
            pl.BlockSpec((D_MODEL, SEG), lambda m, s: (0, s)),
            pl.BlockSpec((tm, D_MODEL), lambda m, s: (m, 0))] + h_in,
        out_specs=[pl.BlockSpec((tm, D_MODEL), lambda m, s: (m, 0))] + h_out,
        out_shape=[jax.ShapeDtypeStruct((T, D_MODEL), F32)] + h_shape,
        scratch_shapes=h_sems,
        compiler_params=_params("arbitrary", "arbitrary"),
    )(dqkv, du_rest, w_in, dres, *hosted)
    return out[0], out[1:]


def _in_proj_dw(xb, dqkv, du_rest):
    T = xb.shape[0]
    tk = min(T, 1024)

    def body(x_ref, a1_ref, a2_ref, dw_ref, db_ref):
        s, t = pl.program_id(0), pl.program_id(1)

        @pl.when(t == 0)
        def _():
            dw_ref[...] = jnp.zeros_like(dw_ref)
            db_ref[...] = jnp.zeros_like(db_ref)

        def acc(a):
            dw_ref[...] += _dot_tn(x_ref[...], a)
            db_ref[...] += jnp.sum(a.astype(F32), axis=0, keepdims=True)

        @pl.when(s < N_QKV_SEG)
        def _():
            acc(a1_ref[...])

        @pl.when(s >= N_QKV_SEG)
        def _():
            acc(a2_ref[...])

    return pl.pallas_call(
        body, name="in_proj_dw", grid=(N_SEG, T // tk),
        in_specs=[pl.BlockSpec((tk, D_MODEL), lambda s, t: (t, 0))] + _du_specs(tk, lambda s, t: t, lambda s, t: s),
        out_specs=[pl.BlockSpec((D_MODEL, SEG), lambda s, t: (0, s)),
                   pl.BlockSpec((None, 1, SEG), lambda s, t: (s, 0, 0))],
        out_shape=[jax.ShapeDtypeStruct((D_MODEL, D_IN), F32), jax.ShapeDtypeStruct((N_SEG, 1, SEG), F32)],
        compiler_params=_params("parallel", "arbitrary"),
    )(xb, dqkv, du_rest)


ATT_CHUNK = 32


def _chunks():
    return [slice(r, r + ATT_CHUNK) for r in range(0, ATT_BLOCK, ATT_CHUNK)]


def _twice(rhs):
    return jnp.concatenate([rhs, rhs], axis=0)


def _fail_and_log_beta(z):
    t = jnp.maximum(z, 0.0) + jnp.log(1.0 + jnp.exp(-jnp.abs(z)))
    return t, z - t


def _store_split(dst, rows, val):
    hi = val.astype(BF16)
    dst[rows, 0:ATT_BLOCK] = hi
    dst[rows, ATT_BLOCK:2 * ATT_BLOCK] = (val - hi.astype(F32)).astype(BF16)


def _fill_causal(keep):
    row = lax.broadcasted_iota(jnp.int32, (ATT_BLOCK, ATT_BLOCK), 0)
    col = lax.broadcasted_iota(jnp.int32, (ATT_BLOCK, ATT_BLOCK), 1)
    keep[...] = (col < row).astype(F32)


def _triangles():
    row = lax.broadcasted_iota(jnp.int32, (ATT_BLOCK, ATT_BLOCK), 0)
    col = lax.broadcasted_iota(jnp.int32, (ATT_BLOCK, ATT_BLOCK), 1)
    return _twice((row > col).astype(BF16)), _twice((row >= col).astype(BF16))


def _staggered(stages):
    stages[0](0)
    stages[0](1)
    for k in range(1, len(stages), 2):
        for h in range(2):
            stages[k](h)
            stages[k + 1](h)


EXP_UNDERFLOW = 104.0


def _sweep_keys(i, tile, rsum):
    tile(i, True)

    def live(carry):
        j, low = carry
        return jnp.logical_and(j >= 0, low < EXP_UNDERFLOW)

    def step(carry):
        j, _ = carry
        tile(j, False)
        return j - 1, jnp.min(rsum[...])

    lax.while_loop(live, step, (i - 1, jnp.min(rsum[...])))


def _attn_fwd(u, gather=()):
    T = u.shape[0]
    B = ATT_BLOCK
    nq = T // B
    n_pairs = D_ATT // 128
    n_g = len(gather)

    def body(*refs):
        (q_ref, k_ref, v_ref), g_ins, (o_ref,), g_outs, scratch = _split_refs(refs, 3, n_g, 1, n_g)
        kb, vb, acc, rsum, t0, zs, lt, cat, keep = scratch[:9]
        p, i = pl.program_id(0), pl.program_id(1)

        if n_g:
            @pl.when(jnp.logical_and(p == 0, i == 0))
            def _():
                _Gather(g_ins, g_outs, *scratch[9:]).start()

            @pl.when(jnp.logical_and(p == n_pairs - 1, i == 0))
            def _():
                _Gather(g_ins, g_outs, *scratch[9:]).forward()

        @pl.when(i == 0)
        def _():
            _fill_causal(keep)
            for h in range(2):
                kb[h] = k_ref[:, HEAD_DIM * h:HEAD_DIM * (h + 1)].astype(BF16)
                vb[h] = v_ref[:, HEAD_DIM * h:HEAD_DIM * (h + 1)].astype(BF16)

        tri2, _ = _triangles()
        hs = [slice(HEAD_DIM * h, HEAD_DIM * (h + 1)) for h in range(2)]
        q = [(q_ref[:, hs[h]] * ATT_SCALE).astype(BF16) for h in range(2)]
        acc[...] = jnp.zeros_like(acc)
        rsum[...] = jnp.zeros_like(rsum)

        def tile(j, masked):
            keys = pl.ds(pl.multiple_of(j * B, B), B)

            def scores(h):
                zs[h] = _dot_nt(q[h], kb[h, keys, :])

            def fails(h):
                for r in _chunks():
                    t, a = _fail_and_log_beta(zs[h, r, :])
                    if masked:
                        t = t * keep[r, :]
                    _store_split(cat.at[h], r, t)
                    zs[h, r, :] = a
                    t0[h, r, :] = t[:, 0:1]

            def later_sums(h):
                lt[h] = _dot(cat[h], tri2)

            def weights(h):
                for r in _chunks():
                    later = lt[h, r, :]
                    w = jnp.exp(zs[h, r, :] - later - rsum[h, r, :])
                    if masked:
                        w = w * keep[r, :]
                    _store_split(cat.at[h], r, w)
                    rsum[h, r, :] += later[:, 0:1] + t0[h, r, :]

            def values(h):
                acc[h] += _dot(cat[h], _twice(vb[h, keys, :]))

            _staggered([scores, fails, later_sums, weights, values])

        _sweep_keys(i, tile, rsum)
        for h in range(2):
            o_ref[:, hs[h]] = acc[h]

        if n_g:
            @pl.when(jnp.logical_and(p == n_pairs - 1, i == nq - 1))
            def _():
                _Gather(g_ins, g_outs, *scratch[9:]).wait()

    out = pl.pallas_call(
        body, name="attn_fwd", grid=(n_pairs, nq),
        in_specs=[pl.BlockSpec((B, 128), lambda p, i: (i, p)),
                  pl.BlockSpec((T, 128), lambda p, i: (0, 4 + p)),
                  pl.BlockSpec((T, 128), lambda p, i: (0, 8 + p))] + [ANY] * n_g,
        out_specs=[pl.BlockSpec((B, 128), lambda p, i: (i, p))] + [ANY] * n_g,
        out_shape=[jax.ShapeDtypeStruct((T, D_ATT), F32)] + _gathered_shapes(gather),
        scratch_shapes=[pltpu.VMEM((2, T, HEAD_DIM), BF16), pltpu.VMEM((2, T, HEAD_DIM), BF16),
                        pltpu.VMEM((2, B, HEAD_DIM), F32), pltpu.VMEM((2, B, 1), F32), pltpu.VMEM((2, B, 1), F32),
                        pltpu.VMEM((2, B, B), F32), pltpu.VMEM((2, B, B), F32), pltpu.VMEM((2, B, 2 * B), BF16),
                        pltpu.VMEM((B, B), F32)]
        + (_Gather.semaphores(n_g) if n_g else []),
        compiler_params=_params("arbitrary", "arbitrary"),
    )(u, u, u, *gather)
    return out[0], out[1:]


def _attn_bwd(u, att, datt, hosted=()):
    T = u.shape[0]
    B = ATT_BLOCK
    nq = T // B
    n_pairs = D_ATT // 128
    n_h = len(hosted)
    h_in, h_out, h_shape, h_sems = _host(hosted)

    def body(*refs):
        (q_ref, k_ref, v_ref, o_ref, do_ref), h_ins, (dqkv_ref,), h_outs, scratch = _split_refs(refs, 5, n_h, 1, n_h)
        n_scratch = 17
        (kb, vb, dkacc, dvacc, dqacc, rsum, gsum, total, t0, lt0, zs, gs, lt, cat, wb, dzb,
         keep) = scratch[:n_scratch]
        p, i = pl.program_id(0), pl.program_id(1)

        if n_h:
            @pl.when(jnp.logical_and(p == 0, i == 0))
            def _():
                _Exchange(h_ins, h_outs, *scratch[n_scratch:]).start()

        @pl.when(i == 0)
        def _():
            _fill_causal(keep)
            for h in range(2):
                kb[h] = k_ref[:, HEAD_DIM * h:HEAD_DIM * (h + 1)].astype(BF16)
                vb[h] = v_ref[:, HEAD_DIM * h:HEAD_DIM * (h + 1)].astype(BF16)
            dkacc[...] = jnp.zeros_like(dkacc)
            dvacc[...] = jnp.zeros_like(dvacc)

        tri2, tri_incl2 = _triangles()
        qrows = pl.ds(pl.multiple_of(i * B, B), B)
        hs = [slice(HEAD_DIM * h, HEAD_DIM * (h + 1)) for h in range(2)]
        q = [(q_ref[:, hs[h]] * ATT_SCALE).astype(BF16) for h in range(2)]
        dob = [do_ref[:, hs[h]].astype(BF16) for h in range(2)]
        for h in range(2):
            total[h] = jnp.sum(dob[h].astype(F32) * o_ref[:, hs[h]], axis=1, keepdims=True)
        dqacc[...] = jnp.zeros_like(dqacc)
        rsum[...] = jnp.zeros_like(rsum)
        gsum[...] = jnp.zeros_like(gsum)

        def tile(j, masked):
            keys = pl.ds(pl.multiple_of(j * B, B), B)

            def scores(h):
                zs[h] = _dot_nt(q[h], kb[h, keys, :])
                gs[h] = _dot_nt(dob[h], vb[h, keys, :])

            def fails(h):
                for r in _chunks():
                    t, a = _fail_and_log_beta(zs[h, r, :])
                    if masked:
                        t = t * keep[r, :]
                    _store_split(cat.at[h], r, t)
                    zs[h, r, :] = a
                    t0[h, r, :] = t[:, 0:1]

            def later_sums(h):
                lt[h] = _dot(cat[h], tri2)

            def weights(h):
                for r in _chunks():
                    later = lt[h, r, :]
                    w = jnp.exp(zs[h, r, :] - later - rsum[h, r, :])
                    if masked:
                        w = w * keep[r, :]
                    g = gs[h, r, :] * w
                    gs[h, r, :] = g
                    _store_split(cat.at[h], r, g)
                    wb[h, r, :] = w.astype(BF16)
                    lt0[h, r, :] = later[:, 0:1]

            def suffix_sums(h):
                lt[h] = _dot(cat[h], tri_incl2)

            def score_grads(h):
                for r in _chunks():
                    suffix = lt[h, r, :]
                    g = gs[h, r, :]
                    before = total[h, r, :] - (suffix + gsum[h, r, :])
                    dz = g - jnp.exp(zs[h, r, :]) * (g + before)
                    if masked:
                        dz = dz * keep[r, :]
                    dzb[h, r, :] = dz.astype(BF16)
                    rsum[h, r, :] += lt0[h, r, :] + t0[h, r, :]
                    gsum[h, r, :] += suffix[:, 0:1]

            def input_grads(h):
                dqacc[h] += _dot(dzb[h], kb[h, keys, :])
                dkacc[h, keys, :] += _dot_tn(dzb[h], q[h])
                dvacc[h, keys, :] += _dot_tn(wb[h], dob[h])

            _staggered([scores, fails, later_sums, weights, suffix_sums, score_grads, input_grads])

        _sweep_keys(i, tile, rsum)
        both = lambda acc2: jnp.concatenate([acc2[0], acc2[1]], axis=1)
        dqkv_ref[0, qrows, :] = (both(dqacc) * ATT_SCALE).astype(BF16)

        @pl.when(i == nq - 1)
        def _():
            dqkv_ref[1] = both(dkacc).astype(BF16)
            dqkv_ref[2] = both(dvacc).astype(BF16)

        if n_h:
            @pl.when(jnp.logical_and(p == n_pairs - 1, i == nq - 1))
            def _():
                _Exchange(h_ins, h_outs, *scratch[n_scratch:]).wait()

    out = pl.pallas_call(
        body, name="attn_bwd", grid=(n_pairs, nq),
        in_specs=[pl.BlockSpec((B, 128), lambda p, i: (i, p)),
                  pl.BlockSpec((T, 128), lambda p, i: (0, 4 + p)),
                  pl.BlockSpec((T, 128), lambda p, i: (0, 8 + p)),
                  pl.BlockSpec((B, 128), lambda p, i: (i, p)),
                  pl.BlockSpec((B, 128), lambda p, i: (i, p))] + h_in,
        out_specs=[pl.BlockSpec((3, T, 128), lambda p, i: (0, 0, p))] + h_out,
        out_shape=[jax.ShapeDtypeStruct((3, T, D_ATT), BF16)] + h_shape,
        scratch_shapes=[pltpu.VMEM((2, T, HEAD_DIM), BF16), pltpu.VMEM((2, T, HEAD_DIM), BF16),
                        pltpu.VMEM((2, T, HEAD_DIM), F32), pltpu.VMEM((2, T, HEAD_DIM), F32),
                        pltpu.VMEM((2, B, HEAD_DIM), F32)]
        + [pltpu.VMEM((2, B, 1), F32)] * 5
        + [pltpu.VMEM((2, B, B), F32)] * 3
        + [pltpu.VMEM((2, B, 2 * B), BF16), pltpu.VMEM((2, B, B), BF16), pltpu.VMEM((2, B, B), BF16)]
        + [pltpu.VMEM((B, B), F32)]
        + h_sems,
        compiler_params=_params("arbitrary", "arbitrary"),
    )(u, u, u, att, datt, *hosted)
    return out[0], out[1:]


MIX_TILE = 128


def _u_specs(tm, tile_of):
    return [pl.BlockSpec((tm, SEG), functools.partial(lambda k, *g: (tile_of(*g), k), k)) for k in range(3, N_SEG)]


SHIFT_ROWS = 24


def _shifted_scratch(tm):
    return pltpu.VMEM((7, tm + SHIFT_ROWS, D_CONV), F32)


def _make_shifted(base, shifted, tm):
    for b in range(1, 8):
        shifted[b - 1] = base[b:b + tm + SHIFT_ROWS, :]


def _tap_window(base, shifted, start, tm):
    whole, b = divmod(start, 8)
    if b == 0:
        return base[start:start + tm, :]
    return shifted[b - 1, 8 * whole:8 * whole + tm, :]


def _conv(ext, ext_shifted, cw_ref, tm):
    acc = cw_ref[0:1, :] * _tap_window(ext, ext_shifted, 2, tm)
    for k in range(1, CONV_WIDTH):
        acc = acc + cw_ref[k:k + 1, :] * _tap_window(ext, ext_shifted, 2 + k, tm)
    return acc


def _norm_stats(v):
    mu = jnp.mean(v, axis=-1, keepdims=True)
    vc = v - mu
    rstd = lax.rsqrt(jnp.mean(vc * vc, axis=-1, keepdims=True) + LN_EPS)
    return vc * rstd, rstd


def _norm_bwd(dy_scaled, xhat, rstd):
    return rstd * (dy_scaled - jnp.mean(dy_scaled, axis=-1, keepdims=True)
                   - xhat * jnp.mean(dy_scaled * xhat, axis=-1, keepdims=True))


def _mix_fwd(x, u, att, wap, wcp, wout, cw, cvec, dvec, layer):
    T = x.shape[0]
    tm = MIX_TILE

    def body(x_ref, za_ref, ga_ref, gb_ref, zc_ref, ga0_ref, ga1_ref, gc0_ref, gc1_ref, att_ref,
             wap_ref, wcp_ref, wout_ref, cw_ref, cvec_ref, dvec_ref, xn_ref, xnb_ref, h_ref, ext, ext_shifted):
        i = pl.program_id(0)

        @pl.when(i == 0)
        def _():
            ext[0:CONV_PAD, :] = jnp.zeros((CONV_PAD, D_CONV), F32)

        za = za_ref[...]
        attg = att_ref[...] * (za * _sig(za))
        ab = _dot(attg.astype(BF16), wap_ref[...])

        ext[CONV_PAD:CONV_PAD + tm, :] = ga_ref[...] * _sig(gb_ref[...])
        _make_shifted(ext, ext_shifted, tm)
        c1 = _conv(ext, ext_shifted, cw_ref, tm) + cvec_ref[0:1, :]
        ext[0:CONV_PAD, :] = ext[tm:tm + CONV_PAD, :]
        xh, _ = _norm_stats(c1)
        c2 = xh * cvec_ref[1:2, :] + cvec_ref[2:3, :]
        zc = zc_ref[...]
        cz = (c2 * _sig(c2)) * (zc * _sig(zc))
        cb = _dot(cz.astype(BF16), wcp_ref[...]) + dvec_ref[0:1, :]

        g_att = jnp.concatenate([ga0_ref[...], ga1_ref[...]], axis=1)
        g_conv = jnp.concatenate([gc0_ref[...], gc1_ref[...]], axis=1)
        merged = _sig(g_att) * ab + _sig(g_conv) * cb
        h = DEEPNORM_ALPHA * x_ref[...] + _dot(merged.astype(BF16), wout_ref[...])
        h_ref[...] = h
        hh, _ = _norm_stats(h)
        xn = hh * dvec_ref[1:2, :] + dvec_ref[2:3, :]
        xn_ref[...] = xn
        xnb_ref[...] = xn.astype(BF16)

    row = pl.BlockSpec((tm, D_MODEL), lambda i: (i, 0))
    return pl.pallas_call(
        body, name="mix_fwd", grid=(T // tm,),
        in_specs=[row] + _u_specs(tm, lambda i: i) + [
            pl.BlockSpec((tm, D_ATT), lambda i: (i, 0)),
            pl.BlockSpec((D_ATT, D_MODEL), lambda i: (0, 0)),
            pl.BlockSpec((D_CONV, D_MODEL), lambda i: (0, 0)),
            pl.BlockSpec((D_MODEL, D_MODEL), lambda i: (0, 0)),
            pl.BlockSpec((CONV_PAD, D_CONV), lambda i: (0, 0)),
            pl.BlockSpec((None, 8, D_CONV), lambda i: (layer, 0, 0)),
            pl.BlockSpec((None, 8, D_MODEL), lambda i: (layer, 0, 0))],
        out_specs=[row, row, row],
        out_shape=[jax.ShapeDtypeStruct((T, D_MODEL), F32), jax.ShapeDtypeStruct((T, D_MODEL), BF16),
                   jax.ShapeDtypeStruct((T, D_MODEL), F32)],
        scratch_shapes=[pltpu.VMEM((tm + CONV_PAD, D_CONV), F32), _shifted_scratch(tm)],
        compiler_params=_params("arbitrary"),
    )(x, *([u] * 8), att, wap, wcp, wout, cw, cvec, dvec)


def _mix_bwd(dxo, h, u, att, wap, wcp, wout, cw, cvec, dvec, layer, hosted=()):
    T = dxo.shape[0]
    tm = MIX_TILE
    nt = T // tm
    halo_per_tile = tm // CONV_PAD
    n_h = len(hosted)
    h_in, h_out, h_shape, h_sems = _host(hosted)

    def body(*refs):
        ins, h_ins, outs, h_outs, scratch = _split_refs(refs, 19, n_h, 9, n_h)
        (dxo_ref, h_ref, za_ref, ga_ref, gb_ref, zc_ref, ga0_ref, ga1_ref, gc0_ref, gc1_ref, hga_ref, hgb_ref,
         att_ref, wap_ref, wcp_ref, wout_ref, cw_ref, cvec_ref, dvec_ref) = ins
        datt_ref, dur_ref, dres_ref, dwap_ref, dwcp_ref, dwout_ref, dcw_ref, dcvec_ref, ddvec_ref = outs
        ext, dext, ext_shifted, dext_shifted = scratch[:4]
        r = pl.program_id(0)
        i = nt - 1 - r

        if n_h:
            @pl.when(r == 0)
            def _():
                _Exchange(h_ins, h_outs, *scratch[4:]).start()

        @pl.when(r == 0)
        def _():
            for ref in (dwap_ref, dwcp_ref, dwout_ref, dcw_ref, dcvec_ref, ddvec_ref):
                ref[...] = jnp.zeros_like(ref)
            dext[tm:tm + CONV_PAD, :] = jnp.zeros((CONV_PAD, D_CONV), F32)

        za = za_ref[...]
        s_za = _sig(za)
        silu_za = za * s_za
        att_t = att_ref[...]
        attg = (att_t * silu_za).astype(BF16)
        ab = _dot(attg, wap_ref[...])

        ga, gb = ga_ref[...], gb_ref[...]
        s_gb = _sig(gb)
        halo = hga_ref[...] * _sig(hgb_ref[...])
        ext[0:CONV_PAD, :] = jnp.where(i > 0, halo, 0.0)
        ext[CONV_PAD:CONV_PAD + tm, :] = ga * s_gb
        _make_shifted(ext, ext_shifted, tm)
        c1 = _conv(ext, ext_shifted, cw_ref, tm) + cvec_ref[0:1, :]
        xh, rstd_c = _norm_stats(c1)
        c2 = xh * cvec_ref[1:2, :] + cvec_ref[2:3, :]
        s_c2 = _sig(c2)
        c3 = c2 * s_c2
        zc = zc_ref[...]
        s_zc = _sig(zc)
        cz = (c3 * (zc * s_zc)).astype(BF16)
        cb = _dot(cz, wcp_ref[...]) + dvec_ref[0:1, :]

        s_ga = _sig(jnp.concatenate([ga0_ref[...], ga1_ref[...]], axis=1))
        s_gc = _sig(jnp.concatenate([gc0_ref[...], gc1_ref[...]], axis=1))
        merged = (s_ga * ab + s_gc * cb).astype(BF16)

        dxo_t = dxo_ref[...]
        hh, rstd_h = _norm_stats(h_ref[...])
        ddvec_ref[1:2, :] += jnp.sum(dxo_t * hh, axis=0, keepdims=True)
        ddvec_ref[2:3, :] += jnp.sum(dxo_t, axis=0, keepdims=True)
        dh = _norm_bwd(dxo_t * dvec_ref[1:2, :], hh, rstd_h)
        dres_ref[...] = DEEPNORM_ALPHA * dh
        dy = dh.astype(BF16)

        dwout_ref[...] += _dot_tn(merged, dy)
        dm = _dot_nt(dy, wout_ref[...])
        dab = (dm * s_ga).astype(BF16)
        dcb = dm * s_gc
        dur_ref[:, 4 * SEG:6 * SEG] = (dm * ab * (s_ga * (1.0 - s_ga))).astype(BF16)
        dur_ref[:, 6 * SEG:8 * SEG] = (dm * cb * (s_gc * (1.0 - s_gc))).astype(BF16)

        dwap_ref[...] += _dot_tn(attg, dab)
        dattg = _dot_nt(dab, wap_ref[...])
        datt_ref[...] = dattg * silu_za
        dur_ref[:, 0:SEG] = (dattg * att_t * (s_za * (1.0 + za * (1.0 - s_za)))).astype(BF16)

        ddvec_ref[0:1, :] += jnp.sum(dcb, axis=0, keepdims=True)
        dcbb = dcb.astype(BF16)
        dwcp_ref[...] += _dot_tn(cz, dcbb)
        dcz = _dot_nt(dcbb, wcp_ref[...])
        dur_ref[:, 3 * SEG:4 * SEG] = (dcz * c3 * (s_zc * (1.0 + zc * (1.0 - s_zc)))).astype(BF16)
        dc2 = dcz * (zc * s_zc) * (s_c2 * (1.0 + c2 * (1.0 - s_c2)))
        dcvec_ref[1:2, :] += jnp.sum(dc2 * xh, axis=0, keepdims=True)
        dcvec_ref[2:3, :] += jnp.sum(dc2, axis=0, keepdims=True)
        dc1 = _norm_bwd(dc2 * cvec_ref[1:2, :], xh, rstd_c)
        dcvec_ref[0:1, :] += jnp.sum(dc1, axis=0, keepdims=True)
        for k in range(CONV_WIDTH):
            dcw_ref[k:k + 1, :] += jnp.sum(dc1 * _tap_window(ext, ext_shifted, 2 + k, tm), axis=0, keepdims=True)
        dext[0:tm, :] = dc1
        _make_shifted(dext, dext_shifted, tm)
        dc0 = cw_ref[0:1, :] * _tap_window(dext, dext_shifted, CONV_WIDTH - 1, tm)
        for k in range(1, CONV_WIDTH):
            dc0 = dc0 + cw_ref[k:k + 1, :] * _tap_window(dext, dext_shifted, CONV_WIDTH - 1 - k, tm)
        dext[tm:tm + CONV_PAD, :] = dext[0:CONV_PAD, :]
        dur_ref[:, SEG:2 * SEG] = (dc0 * s_gb).astype(BF16)
        dur_ref[:, 2 * SEG:3 * SEG] = (dc0 * ga * (s_gb * (1.0 - s_gb))).astype(BF16)

        if n_h:
            @pl.when(r == nt - 1)
            def _():
                _Exchange(h_ins, h_outs, *scratch[4:]).wait()

    rev = lambda r: nt - 1 - r
    row = pl.BlockSpec((tm, D_MODEL), lambda r: (rev(r), 0))
    halo_spec = [pl.BlockSpec((CONV_PAD, SEG),
                              functools.partial(lambda k, r: (jnp.maximum(rev(r) * halo_per_tile - 1, 0), k), k))
                 for k in (4, 5)]
    const = lambda shape: pl.BlockSpec(shape, lambda r: (0,) * len(shape))
    out = pl.pallas_call(
        body, name="mix_bwd", grid=(nt,),
        in_specs=[row, row] + _u_specs(tm, rev) + halo_spec + [
            pl.BlockSpec((tm, D_ATT), lambda r: (rev(r), 0)),
            pl.BlockSpec((D_ATT, D_MODEL), lambda r: (0, 0)),
            pl.BlockSpec((D_CONV, D_MODEL), lambda r: (0, 0)),
            pl.BlockSpec((D_MODEL, D_MODEL), lambda r: (0, 0)),
            pl.BlockSpec((CONV_PAD, D_CONV), lambda r: (0, 0)),
            pl.BlockSpec((None, 8, D_CONV), lambda r: (layer, 0, 0)),
            pl.BlockSpec((None, 8, D_MODEL), lambda r: (layer, 0, 0))] + h_in,
        out_specs=[pl.BlockSpec((tm, D_ATT), lambda r: (rev(r), 0)),
                   pl.BlockSpec((tm, 8 * SEG), lambda r: (rev(r), 0)),
                   row,
                   const((D_ATT, D_MODEL)), const((D_CONV, D_MODEL)), const((D_MODEL, D_MODEL)),
                   const((CONV_PAD, D_CONV)), const((8, D_CONV)), const((8, D_MODEL))] + h_out,
        out_shape=[jax.ShapeDtypeStruct((T, D_ATT), F32), jax.ShapeDtypeStruct((T, 8 * SEG), BF16),
                   jax.ShapeDtypeStruct((T, D_MODEL), F32),
                   jax.ShapeDtypeStruct((D_ATT, D_MODEL), F32), jax.ShapeDtypeStruct((D_CONV, D_MODEL), F32),
                   jax.ShapeDtypeStruct((D_MODEL, D_MODEL), F32), jax.ShapeDtypeStruct((CONV_PAD, D_CONV), F32),
                   jax.ShapeDtypeStruct((8, D_CONV), F32), jax.ShapeDtypeStruct((8, D_MODEL), F32)] + h_shape,
        scratch_shapes=[pltpu.VMEM((tm + CONV_PAD, D_CONV), F32), pltpu.VMEM((tm + CONV_PAD, D_CONV), F32),
                        _shifted_scratch(tm), _shifted_scratch(tm)] + h_sems,
        compiler_params=_params("arbitrary"),
    )(dxo, h, *([u] * 10), att, wap, wcp, wout, cw, cvec, dvec, *hosted)
    return out[:9], out[9:]


def _loss_head(y, target):
    T = y.shape[0]
    tm = min(T, 512)

    def body(y_ref, t_ref, dy_ref, loss_ref):
        @pl.when(pl.program_id(0) == 0)
        def _():
            loss_ref[...] = jnp.zeros_like(loss_ref)

        err = y_ref[...] - t_ref[...]
        dy_ref[...] = err * (1.0 / D_MODEL)
        loss_ref[...] += 0.5 * jnp.sum(jnp.mean(err * err, axis=-1, keepdims=True))

    row = pl.BlockSpec((tm, D_MODEL), lambda i: (i, 0))
    return pl.pallas_call(
        body, name="loss_head", grid=(T // tm,), in_specs=[row, row],
        out_specs=[row, pl.BlockSpec((8, 128), lambda i: (0, 0))],
        out_shape=[jax.ShapeDtypeStruct((T, D_MODEL), F32), jax.ShapeDtypeStruct((8, 128), F32)],
        compiler_params=_params("arbitrary"),
    )(y, target)


def _adamw(w, g, m, v):
    m = ADAM_B1 * m + (1.0 - ADAM_B1) * g
    v = ADAM_B2 * v + (1.0 - ADAM_B2) * (g * g)
    m_hat = m / (1.0 - ADAM_B1 ** ADAM_STEP)
    v_hat = v / (1.0 - ADAM_B2 ** ADAM_STEP)
    return -ADAM_LR * (m_hat / (jnp.sqrt(v_hat) + ADAM_EPS) + ADAM_WD * w), m, v


def _sum_slots(ref):
    total = ref[0].astype(F32)
    for j in range(1, N_DEV):
        total = total + ref[j].astype(F32)
    return total


def _adamw_sharded(recv, w, m, v, prev, layer, tr, name):
    n_layers, R, C = w.shape

    def body(*refs):
        recv_ref, w_ref, m_ref, v_ref = refs[:4]
        g_ref, d_ref, nm_ref, nv_ref = refs[-4:]
        g = _sum_slots(recv_ref)
        g_ref[...] = g
        d_ref[...], nm_ref[...], nv_ref[...] = _adamw(w_ref[...], g, m_ref[...], v_ref[...])

    slab = pl.BlockSpec((None, tr, C), lambda i: (layer, i, 0))
    n_prev = 0 if prev is None else 4
    return pl.pallas_call(
        body, name=name, grid=(R // tr,),
        in_specs=[pl.BlockSpec((N_DEV, tr, C), lambda i: (0, i, 0)), slab, slab, slab] + [ANY] * n_prev,
        out_specs=[slab] * 4,
        out_shape=[jax.ShapeDtypeStruct(w.shape, F32)] * 4,
        input_output_aliases={4 + k: k for k in range(n_prev)},
        compiler_params=_params("parallel"),
    )(recv, w, m, v, *(prev or ()))


def _sum8(g):
    R = g.shape[1]

    def body(g_ref, o_ref):
        o_ref[...] = _sum_slots(g_ref)

    return pl.pallas_call(body, name="sum_small_grads", out_shape=jax.ShapeDtypeStruct((R, 128), F32))(g)


def _adamw_small(w, g, m, v):
    def body(w_ref, g_ref, m_ref, v_ref, d_ref, nm_ref, nv_ref):
        d_ref[...], nm_ref[...], nv_ref[...] = _adamw(w_ref[...], g_ref[...], m_ref[...], v_ref[...])

    return pl.pallas_call(body, name="adamw_small", out_shape=[jax.ShapeDtypeStruct(w.shape, F32)] * 3)(w, g, m, v)


REPLICATED = (("b_in", D_IN), ("conv_b", D_CONV), ("conv_ln_g", D_CONV), ("conv_ln_b", D_CONV),
              ("b_conv_proj", D_MODEL), ("ln_g", D_MODEL), ("ln_b", D_MODEL))
N_REPLICATED = sum(n for _, n in REPLICATED)
CONV_W_SHARD_PAD = 2048


def _pack_small(rep, conv_w_shard):
    flat = jnp.concatenate([rep[k] for k, _ in REPLICATED], axis=1)
    cws = jnp.pad(conv_w_shard.reshape(DEPTH, -1), ((0, 0), (0, CONV_W_SHARD_PAD - CONV_WIDTH * 64)))
    return jnp.concatenate([flat, cws], axis=1).reshape(-1, 128)


def _unpack_small(packed):
    per = packed.reshape(DEPTH, N_REPLICATED + CONV_W_SHARD_PAD)
    out, off = {}, 0
    for k, n in REPLICATED:
        out[k] = per[:, off:off + n]
        off += n
    out["conv_w"] = per[:, off:off + CONV_WIDTH * 64].reshape(DEPTH, CONV_WIDTH, 64)
    return out


def kernel(x, w_in, b_in, conv_w, conv_b, conv_ln_g, conv_ln_b, w_att_proj, w_conv_proj, b_conv_proj, w_out, ln_g, ln_b, loss_target, m_w_in, m_b_in, m_conv_w, m_conv_b, m_conv_ln_g, m_conv_ln_b, m_w_att_proj, m_w_conv_proj, m_b_conv_proj, m_w_out, m_ln_g, m_ln_b, v_w_in, v_b_in, v_conv_w, v_conv_b, v_conv_ln_g, v_conv_ln_b, v_w_att_proj, v_w_conv_proj, v_b_conv_proj, v_w_out, v_ln_g, v_ln_b):
    x0 = x[0]
    target = loss_target[0]

    def shards(l):
        return [w_in[l].astype(BF16), w_att_proj[l].astype(BF16), w_conv_proj[l].astype(BF16),
                w_out[l].astype(BF16), conv_w[l]]

    def full_weights(g_in, g_ap, g_cp, g_out, g_cw):
        return (g_in.transpose(1, 0, 2).reshape(D_MODEL, D_IN), g_ap.transpose(1, 0, 2).reshape(D_ATT, D_MODEL),
                g_cp.transpose(1, 0, 2).reshape(D_CONV, D_MODEL), g_out.reshape(D_MODEL, D_MODEL),
                jnp.pad(g_cw.transpose(1, 0, 2).reshape(CONV_WIDTH, D_CONV), ((0, 1), (0, 0))))

    pad8 = lambda rows: jnp.pad(jnp.stack(rows, axis=1), ((0, 0), (0, 8 - len(rows)), (0, 0)))
    cvec = pad8([conv_b, conv_ln_g, conv_ln_b])
    dvec = pad8([b_conv_proj, ln_g, ln_b])
    b_in3 = b_in.reshape(DEPTH, 1, D_IN)

    weights = [full_weights(*_all_gather(shards(0), "gather_weights"))]
    xs, xbs, us, atts, hs = [x0], [x0.astype(BF16)], [], [], []
    for l in range(DEPTH):
        w_in_l, wap, wcp, wout, cw = weights[l]
        u = _in_proj(xbs[l], w_in_l, b_in3, l)
        att, gathered = _attn_fwd(u, gather=shards(l + 1) if l + 1 < DEPTH else ())
        if gathered:
            weights.append(full_weights(*gathered))
        xn, xnb, h = _mix_fwd(xs[l], u, att, wap, wcp, wout, cw, cvec, dvec, l)
        us.append(u), atts.append(att), hs.append(h), xs.append(xn), xbs.append(xnb)
    dxo, loss_part = _loss_head(xs[DEPTH], target)
    loss = lax.psum(loss_part[0, 0], ("x", "y", "c"))

    big = {"w_in": (w_in, m_w_in, v_w_in, 256), "w_att_proj": (w_att_proj, m_w_att_proj, v_w_att_proj, D_ATT),
           "w_conv_proj": (w_conv_proj, m_w_conv_proj, v_w_conv_proj, D_CONV), "w_out": (w_out, m_w_out, v_w_out, 128)}
    results = {k: None for k in big}
    small = [None] * DEPTH

    def update(k, recv, l):
        w, m, v, tr = big[k]
        results[k] = _adamw_sharded(recv, w, m, v, results[k], l, tr, "adamw_" + k)

    pending_w_in = ()
    for l in reversed(range(DEPTH)):
        w_in_l, wap, wcp, wout, cw = weights[l]
        (datt, du_rest, dres, dwap, dwcp, dwout, dcw, dcvec, ddvec), got = _mix_bwd(
            dxo, hs[l], us[l], atts[l], wap, wcp, wout, cw, cvec, dvec, l, hosted=pending_w_in)
        if got:
            update("w_in", got[0], l + 1)
        proj = [dwap.reshape(D_ATT, N_DEV, D_MODEL // N_DEV).transpose(1, 0, 2).astype(BF16),
                dwcp.reshape(D_CONV, N_DEV, D_MODEL // N_DEV).transpose(1, 0, 2).astype(BF16),
                dwout.reshape(N_DEV, D_MODEL // N_DEV, D_MODEL).astype(BF16)]
        dqkv, got = _attn_bwd(us[l], atts[l], datt, hosted=proj)
        for k, r in zip(("w_att_proj", "w_conv_proj", "w_out"), got):
            update(k, r, l)
        dwin, dbin = _in_proj_dw(xbs[l], dqkv, du_rest)
        small[l] = jnp.concatenate([dbin.reshape(-1), dcvec[:3].reshape(-1), ddvec[:3].reshape(-1), dcw.reshape(-1)])
        pending_w_in = (dwin.reshape(D_MODEL, N_DEV, W_IN_SHARD).transpose(1, 0, 2).astype(BF16),)
        dxo, got = _in_proj_dx(dqkv, du_rest, w_in_l, dres, hosted=pending_w_in if l == 0 else ())
        if got:
            update("w_in", got[0], l)
    grad_x = dxo[None]

    small_all = jnp.stack(small).reshape(-1, 128)
    (gathered,) = _all_gather([small_all], "gather_small_grads")
    per_layer = _sum8(gathered).reshape(DEPTH, -1)
    g_rep, off = {}, 0
    for k, n in (("b_in", D_IN), ("conv_b", D_CONV), ("conv_ln_g", D_CONV), ("conv_ln_b", D_CONV),
                 ("b_conv_proj", D_MODEL), ("ln_g", D_MODEL), ("ln_b", D_MODEL)):
        g_rep[k] = per_layer[:, off:off + n]
        off += n
    g_cw_full = per_layer[:, off:].reshape(DEPTH, CONV_PAD, D_CONV)[:, :CONV_WIDTH]
    g_cw_shard = lax.dynamic_slice_in_dim(g_cw_full, _my_index() * 64, 64, axis=2)
    rep_w = dict(b_in=b_in, conv_b=conv_b, conv_ln_g=conv_ln_g, conv_ln_b=conv_ln_b, b_conv_proj=b_conv_proj,
                 ln_g=ln_g, ln_b=ln_b)
    rep_m = dict(b_in=m_b_in, conv_b=m_conv_b, conv_ln_g=m_conv_ln_g, conv_ln_b=m_conv_ln_b,
                 b_conv_proj=m_b_conv_proj, ln_g=m_ln_g, ln_b=m_ln_b)
    rep_v = dict(b_in=v_b_in, conv_b=v_conv_b, conv_ln_g=v_conv_ln_g, conv_ln_b=v_conv_ln_b,
                 b_conv_proj=v_b_conv_proj, ln_g=v_ln_g, ln_b=v_ln_b)
    sd, sm, sv = _adamw_small(_pack_small(rep_w, conv_w), _pack_small(g_rep, g_cw_shard),
                              _pack_small(rep_m, m_conv_w), _pack_small(rep_v, v_conv_w))
    small_out = [dict(g_rep, conv_w=g_cw_shard), _unpack_small(sd), _unpack_small(sm), _unpack_small(sv)]

    order = ["w_in", "b_in", "conv_w", "conv_b", "conv_ln_g", "conv_ln_b", "w_att_proj", "w_conv_proj",
             "b_conv_proj", "w_out", "ln_g", "ln_b"]
    outs = [loss, grad_x]
    for kind in range(4):
        outs += [results[k][kind] if k in big else small_out[kind][k] for k in order]
    return tuple(outs)
```

```python
import functools

import jax
import jax.numpy as jnp
from jax import lax
from jax.experimental import pallas as pl
from jax.experimental.pallas import tpu as pltpu

F32, BF16 = jnp.float32, jnp.bfloat16

D_MODEL = 1024
D_ATT = 512
D_CONV = 512
HEAD_DIM = 64
CONV_WIDTH = 31
CONV_PAD = 32
DEPTH = 4
N_DEV = 8
SEG = 512
D_IN = 4 * D_ATT + 3 * D_CONV + 2 * D_MODEL
N_SEG = D_IN // SEG
N_QKV_SEG = 3
W_IN_SHARD = D_IN // N_DEV
LN_EPS = 1e-5
DEEPNORM_ALPHA = (2 * DEPTH) ** 0.25
ATT_SCALE = HEAD_DIM ** -0.5
ATT_BLOCK = 256
ADAM_LR, ADAM_B1, ADAM_B2, ADAM_EPS, ADAM_WD, ADAM_STEP = 0.001, 0.9, 0.999, 1e-08, 0.01, 10
VMEM_LIMIT = 56 * 1024 * 1024
MESH = pl.DeviceIdType.MESH
ANY = pl.BlockSpec(memory_space=pl.ANY)


def _params(*semantics):
    return pltpu.CompilerParams(dimension_semantics=semantics, vmem_limit_bytes=VMEM_LIMIT)


def _sig(x):
    return 1.0 / (1.0 + jnp.exp(-x))


def _dot(a, b):
    return jnp.dot(a, b, preferred_element_type=F32)


def _dot_nt(a, b):
    return lax.dot_general(a, b, (((1,), (1,)), ((), ())), preferred_element_type=F32)


def _dot_tn(a, b):
    return lax.dot_general(a, b, (((0,), (0,)), ((), ())), preferred_element_type=F32)


def _my_index():
    return 4 * lax.axis_index("x") + 2 * lax.axis_index("y") + lax.axis_index("c")


class _Gather:
    def __init__(self, ins, outs, send_sems, recv_sems, local_sems):
        self.n = n = len(ins)
        x, y, c = lax.axis_index("x"), lax.axis_index("y"), lax.axis_index("c")
        me, sibling = (x, y, c), (x, y, 1 - c)
        chips = [(1 - x, y), (x, 1 - y), (1 - x, 1 - y)]

        def slot(a, dev):
            return outs[a].at[4 * dev[0] + 2 * dev[1] + dev[2]]

        def copy(a, k, block, to, src=None):
            return pltpu.make_async_remote_copy(
                src_ref=slot(a, block) if src is None else src, dst_ref=slot(a, block),
                send_sem=send_sems.at[7 * a + k], recv_sem=recv_sems.at[7 * a + k], device_id=to, device_id_type=MESH)

        self.mine = [pltpu.make_async_copy(ins[a], slot(a, me), local_sems.at[a]) for a in range(n)]
        self.first, self.landed, self.passed, self.last = [], [], [], []
        for a in range(n):
            self.first.append(copy(a, 0, me, sibling, src=ins[a]))
            self.first += [copy(a, 1 + j, me, (*chip, c), src=ins[a]) for j, chip in enumerate(chips)]
        for j, chip in enumerate(chips):
            for a in range(n):
                self.landed.append(copy(a, 1 + j, (*chip, c), me))
                self.passed.append(copy(a, 4 + j, (*chip, c), sibling))
        for a in range(n):
            self.last.append(copy(a, 0, sibling, me))
            self.last += [copy(a, 4 + j, (*chip, 1 - c), me) for j, chip in enumerate(chips)]

    @staticmethod
    def semaphores(n):
        return [pltpu.SemaphoreType.DMA((7 * n,)), pltpu.SemaphoreType.DMA((7 * n,)), pltpu.SemaphoreType.DMA((n,))]

    def start(self):
        for cp in self.mine + self.first:
            cp.start()

    def forward(self):
        for landed, passed in zip(self.landed, self.passed):
            landed.wait_recv()
            passed.start()

    def wait(self):
        for cp in self.last:
            cp.wait_recv()
        for cp in self.first + self.passed:
            cp.wait_send()
        for cp in self.mine:
            cp.wait()


def _gathered_shapes(shards):
    return [jax.ShapeDtypeStruct((N_DEV,) + s.shape, s.dtype) for s in shards]


def _all_gather(shards, name):
    n = len(shards)

    def body(*refs):
        g = _Gather(refs[:n], refs[n:2 * n], *refs[2 * n:])
        g.start()
        g.forward()
        g.wait()

    return pl.pallas_call(
        body, name=name, out_shape=_gathered_shapes(shards),
        in_specs=[ANY] * n, out_specs=[ANY] * n, scratch_shapes=_Gather.semaphores(n),
    )(*shards)


class _Exchange:
    def __init__(self, ins, outs, send_sems, recv_sems, local_sems):
        n = len(ins)
        x, y, c = lax.axis_index("x"), lax.axis_index("y"), lax.axis_index("c")
        me_idx = 4 * x + 2 * y + c
        self.mine = [pltpu.make_async_copy(ins[a].at[me_idx], outs[a].at[me_idx], local_sems.at[a])
                     for a in range(n)]
        self.sends, self.recvs = [], []
        for d in (1, 2, 4, 3, 5, 6, 7):
            px = 1 - x if d & 4 else x
            py = 1 - y if d & 2 else y
            pc = 1 - c if d & 1 else c
            idx = 4 * px + 2 * py + pc
            for a in range(n):
                sems = dict(send_sem=send_sems.at[7 * a + d - 1], recv_sem=recv_sems.at[7 * a + d - 1],
                            device_id=(px, py, pc), device_id_type=MESH)
                self.sends.append(pltpu.make_async_remote_copy(
                    src_ref=ins[a].at[idx], dst_ref=outs[a].at[me_idx], **sems))
                self.recvs.append(pltpu.make_async_remote_copy(
                    src_ref=ins[a].at[idx], dst_ref=outs[a].at[idx], **sems))

    @staticmethod
    def semaphores(n):
        return [pltpu.SemaphoreType.DMA((7 * n,)), pltpu.SemaphoreType.DMA((7 * n,)), pltpu.SemaphoreType.DMA((n,))]

    def start(self):
        for cp in self.mine + self.sends:
            cp.start()

    def wait(self):
        for cp in self.recvs:
            cp.wait_recv()
        for cp in self.sends:
            cp.wait_send()
        for cp in self.mine:
            cp.wait()


def _host(hosted):
    n = len(hosted)
    return ([ANY] * n, [ANY] * n, [jax.ShapeDtypeStruct(p.shape, p.dtype) for p in hosted],
            _Exchange.semaphores(n) if n else [])


def _split_refs(refs, *counts):
    out, at = [], 0
    for k in counts:
        out.append(refs[at:at + k])
        at += k
    return out + [refs[at:]]


def _in_proj(xb, w_in, b_in, layer):
    T = xb.shape[0]
    tm, tn = min(T, 2048), SEG

    def body(x_ref, w_ref, b_ref, u_ref):
        u_ref[...] = _dot(x_ref[...], w_ref[...]) + b_ref[...]

    return pl.pallas_call(
        body, name="in_proj", grid=(T // tm, D_IN // tn),
        in_specs=[pl.BlockSpec((tm, D_MODEL), lambda m, n: (m, 0)),
                  pl.BlockSpec((D_MODEL, tn), lambda m, n: (0, n)),
                  pl.BlockSpec((None, 1, tn), lambda m, n: (layer, 0, n))],
        out_specs=pl.BlockSpec((tm, tn), lambda m, n: (m, n)),
        out_shape=jax.ShapeDtypeStruct((T, D_IN), F32),
        compiler_params=_params("parallel", "parallel"),
    )(xb, w_in, b_in)


def _du_specs(tm, row_of, seg_of):
    return [pl.BlockSpec((None, tm, SEG), lambda *g: (jnp.minimum(seg_of(*g), N_QKV_SEG - 1), row_of(*g), 0)),
            pl.BlockSpec((tm, SEG), lambda *g: (row_of(*g), jnp.maximum(seg_of(*g) - N_QKV_SEG, 0)))]


def _in_proj_dx(dqkv, du_rest, w_in, dres, hosted=()):
    T = du_rest.shape[0]
    tm = min(T, 2048)
    nm = T // tm
    n_h = len(hosted)
    h_in, h_out, h_shape, h_sems = _host(hosted)

    def body(*refs):
        (a1_ref, a2_ref, w_ref, r_ref), h_ins, (o_ref,), h_outs, sems = _split_refs(refs, 4, n_h, 1, n_h)
        m, s = pl.program_id(0), pl.program_id(1)

        if n_h:
            @pl.when(jnp.logical_and(m == 0, s == 0))
            def _():
                _Exchange(h_ins, h_outs, *sems).start()

        @pl.when(s == 0)
        def _():
            o_ref[...] = r_ref[...]

        @pl.when(s < N_QKV_SEG)
        def _():
            o_ref[...] += _dot_nt(a1_ref[...], w_ref[...])

        @pl.when(s >= N_QKV_SEG)
        def _():
            o_ref[...] += _dot_nt(a2_ref[...], w_ref[...])

        if n_h:
            @pl.when(jnp.logical_and(m == nm - 1, s == N_SEG - 1))
            def _():
                _Exchange(h_ins, h_outs, *sems).wait()

    out = pl.pallas_call(
        body, name="in_proj_dx", grid=(nm, N_SEG),
        in_specs=_du_specs(tm, lambda m, s: m, lambda m, s: s) + [
            pl.BlockSpec((D_MODEL, SEG), lambda m, s: (0, s)),
            pl.BlockSpec((tm, D_MODEL), lambda m, s: (m, 0))] + h_in,
        out_specs=[pl.BlockSpec((tm, D_MODEL), lambda m, s: (m, 0))] + h_out,
        out_shape=[jax.ShapeDtypeStruct((T, D_MODEL), F32)] + h_shape,
        scratch_shapes=h_sems,
        compiler_params=_params("arbitrary", "arbitrary"),
    )(dqkv, du_rest, w_in, dres, *hosted)
    return out[0], out[1:]


def _in_proj_dw(xb, dqkv, du_rest):
    T = xb.shape[0]
    tk = min(T, 1024)
    nt = T // tk

    def body(x_ref, a1_ref, a2_ref, dw_ref, db_ref, acc_ref):
        s, t = pl.program_id(0), pl.program_id(1)

        @pl.when(t == 0)
        def _():
            acc_ref[...] = jnp.zeros_like(acc_ref)
            db_ref[...] = jnp.zeros_like(db_ref)

        def acc(a):
            acc_ref[...] += _dot_tn(x_ref[...], a)
            db_ref[...] += jnp.sum(a.astype(F32), axis=0, keepdims=True)

        @pl.when(s < N_QKV_SEG)
        def _():
            acc(a1_ref[...])

        @pl.when(s >= N_QKV_SEG)
        def _():
            acc(a2_ref[...])

        @pl.when(t == nt - 1)
        def _():
            dw_ref[...] = acc_ref[...].astype(BF16)

    return pl.pallas_call(
        body, name="in_proj_dw", grid=(N_SEG, nt),
        in_specs=[pl.BlockSpec((tk, D_MODEL), lambda s, t: (t, 0))] + _du_specs(tk, lambda s, t: t, lambda s, t: s),
        out_specs=[pl.BlockSpec((D_MODEL, SEG), lambda s, t: (0, s)),
                   pl.BlockSpec((None, 1, SEG), lambda s, t: (s, 0, 0))],
        out_shape=[jax.ShapeDtypeStruct((D_MODEL, D_IN), BF16), jax.ShapeDtypeStruct((N_SEG, 1, SEG), F32)],
        scratch_shapes=[pltpu.VMEM((D_MODEL, SEG), F32)],
        compiler_params=_params("parallel", "arbitrary"),
    )(xb, dqkv, du_rest)


ATT_CHUNK = 32


def _chunks():
    return [slice(r, r + ATT_CHUNK) for r in range(0, ATT_BLOCK, ATT_CHUNK)]


def _twice(rhs):
    return jnp.concatenate([rhs, rhs], axis=0)


def _fail_and_log_beta(z):
    t = jnp.maximum(z, 0.0) + jnp.log(1.0 + jnp.exp(-jnp.abs(z)))
    return t, z - t


def _store_split(dst, rows, val):
    hi = val.astype(BF16)
    dst[rows, 0:ATT_BLOCK] = hi
    dst[rows, ATT_BLOCK:2 * ATT_BLOCK] = (val - hi.astype(F32)).astype(BF16)


def _fill_causal(keep):
    row = lax.broadcasted_iota(jnp.int32, (ATT_BLOCK, ATT_BLOCK), 0)
    col = lax.broadcasted_iota(jnp.int32, (ATT_BLOCK, ATT_BLOCK), 1)
    keep[...] = (col < row).astype(F32)


def _triangles():
    row = lax.broadcasted_iota(jnp.int32, (ATT_BLOCK, ATT_BLOCK), 0)
    col = lax.broadcasted_iota(jnp.int32, (ATT_BLOCK, ATT_BLOCK), 1)
    return _twice((row > col).astype(BF16)), _twice((row >= col).astype(BF16))


def _staggered(stages):
    stages[0](0)
    stages[0](1)
    for k in range(1, len(stages), 2):
        for h in range(2):
            stages[k](h)
            stages[k + 1](h)


EXP_UNDERFLOW = 104.0


def _sweep_keys(i, tile, rsum):
    tile(i, True)

    def live(carry):
        j, low = carry
        return jnp.logical_and(j >= 0, low < EXP_UNDERFLOW)

    def step(carry):
        j, _ = carry
        tile(j, False)
        return j - 1, jnp.min(rsum[...])

    lax.while_loop(live, step, (i - 1, jnp.min(rsum[...])))


def _attn_fwd(u, gather=()):
    T = u.shape[0]
    B = ATT_BLOCK
    nq = T // B
    n_pairs = D_ATT // 128
    n_g = len(gather)

    def body(*refs):
        (q_ref, k_ref, v_ref), g_ins, (o_ref,), g_outs, scratch = _split_refs(refs, 3, n_g, 1, n_g)
        kb, vb, acc, rsum, t0, zs, lt, cat, keep = scratch[:9]
        p, i = pl.program_id(0), pl.program_id(1)

        if n_g:
            @pl.when(jnp.logical_and(p == 0, i == 0))
            def _():
                _Gather(g_ins, g_outs, *scratch[9:]).start()

            @pl.when(jnp.logical_and(p == n_pairs - 1, i == 0))
            def _():
                _Gather(g_ins, g_outs, *scratch[9:]).forward()

        @pl.when(i == 0)
        def _():
            _fill_causal(keep)
            for h in range(2):
                kb[h] = k_ref[:, HEAD_DIM * h:HEAD_DIM * (h + 1)].astype(BF16)
                vb[h] = v_ref[:, HEAD_DIM * h:HEAD_DIM * (h + 1)].astype(BF16)

        tri2, _ = _triangles()
        hs = [slice(HEAD_DIM * h, HEAD_DIM * (h + 1)) for h in range(2)]
        q = [(q_ref[:, hs[h]] * ATT_SCALE).astype(BF16) for h in range(2)]
        acc[...] = jnp.zeros_like(acc)
        rsum[...] = jnp.zeros_like(rsum)

        def tile(j, masked):
            keys = pl.ds(pl.multiple_of(j * B, B), B)

            def scores(h):
                zs[h] = _dot_nt(q[h], kb[h, keys, :])

            def fails(h):
                for r in _chunks():
                    t, a = _fail_and_log_beta(zs[h, r, :])
                    if masked:
                        t = t * keep[r, :]
                    _store_split(cat.at[h], r, t)
                    zs[h, r, :] = a
                    t0[h, r, :] = t[:, 0:1]

            def later_sums(h):
                lt[h] = _dot(cat[h], tri2)

            def weights(h):
                for r in _chunks():
                    later = lt[h, r, :]
                    w = jnp.exp(zs[h, r, :] - later - rsum[h, r, :])
                    if masked:
                        w = w * keep[r, :]
                    _store_split(cat.at[h], r, w)
                    rsum[h, r, :] += later[:, 0:1] + t0[h, r, :]

            def values(h):
                acc[h] += _dot(cat[h], _twice(vb[h, keys, :]))

            _staggered([scores, fails, later_sums, weights, values])

        _sweep_keys(i, tile, rsum)
        for h in range(2):
            o_ref[:, hs[h]] = acc[h]

        if n_g:
            @pl.when(jnp.logical_and(p == n_pairs - 1, i == nq - 1))
            def _():
                _Gather(g_ins, g_outs, *scratch[9:]).wait()

    out = pl.pallas_call(
        body, name="attn_fwd", grid=(n_pairs, nq),
        in_specs=[pl.BlockSpec((B, 128), lambda p, i: (i, p)),
                  pl.BlockSpec((T, 128), lambda p, i: (0, 4 + p)),
                  pl.BlockSpec((T, 128), lambda p, i: (0, 8 + p))] + [ANY] * n_g,
        out_specs=[pl.BlockSpec((B, 128), lambda p, i: (i, p))] + [ANY] * n_g,
        out_shape=[jax.ShapeDtypeStruct((T, D_ATT), F32)] + _gathered_shapes(gather),
        scratch_shapes=[pltpu.VMEM((2, T, HEAD_DIM), BF16), pltpu.VMEM((2, T, HEAD_DIM), BF16),
                        pltpu.VMEM((2, B, HEAD_DIM), F32), pltpu.VMEM((2, B, 1), F32), pltpu.VMEM((2, B, 1), F32),
                        pltpu.VMEM((2, B, B), F32), pltpu.VMEM((2, B, B), F32), pltpu.VMEM((2, B, 2 * B), BF16),
                        pltpu.VMEM((B, B), F32)]
        + (_Gather.semaphores(n_g) if n_g else []),
        compiler_params=_params("arbitrary", "arbitrary"),
    )(u, u, u, *gather)
    return out[0], out[1:]


def _attn_bwd(u, att, datt, hosted=()):
    T = u.shape[0]
    B = ATT_BLOCK
    nq = T // B
    n_pairs = D_ATT // 128
    n_h = len(hosted)
    h_in, h_out, h_shape, h_sems = _host(hosted)

    def body(*refs):
        (q_ref, k_ref, v_ref, o_ref, do_ref), h_ins, (dqkv_ref,), h_outs, scratch = _split_refs(refs, 5, n_h, 1, n_h)
        n_scratch = 17
        (kb, vb, dkacc, dvacc, dqacc, rsum, gsum, total, t0, lt0, zs, gs, lt, cat, wb, dzb,
         keep) = scratch[:n_scratch]
        p, i = pl.program_id(0), pl.program_id(1)

        if n_h:
            @pl.when(jnp.logical_and(p == 0, i == 0))
            def _():
                _Exchange(h_ins, h_outs, *scratch[n_scratch:]).start()

        @pl.when(i == 0)
        def _():
            _fill_causal(keep)
            for h in range(2):
                kb[h] = k_ref[:, HEAD_DIM * h:HEAD_DIM * (h + 1)].astype(BF16)
                vb[h] = v_ref[:, HEAD_DIM * h:HEAD_DIM * (h + 1)].astype(BF16)
            dkacc[...] = jnp.zeros_like(dkacc)
            dvacc[...] = jnp.zeros_like(dvacc)

        tri2, tri_incl2 = _triangles()
        qrows = pl.ds(pl.multiple_of(i * B, B), B)
        hs = [slice(HEAD_DIM * h, HEAD_DIM * (h + 1)) for h in range(2)]
        q = [(q_ref[:, hs[h]] * ATT_SCALE).astype(BF16) for h in range(2)]
        dob = [do_ref[:, hs[h]].astype(BF16) for h in range(2)]
        for h in range(2):
            total[h] = jnp.sum(dob[h].astype(F32) * o_ref[:, hs[h]], axis=1, keepdims=True)
        dqacc[...] = jnp.zeros_like(dqacc)
        rsum[...] = jnp.zeros_like(rsum)
        gsum[...] = jnp.zeros_like(gsum)

        def tile(j, masked):
            keys = pl.ds(pl.multiple_of(j * B, B), B)

            def scores(h):
                zs[h] = _dot_nt(q[h], kb[h, keys, :])
                gs[h] = _dot_nt(dob[h], vb[h, keys, :])

            def fails(h):
                for r in _chunks():
                    t, a = _fail_and_log_beta(zs[h, r, :])
                    if masked:
                        t = t * keep[r, :]
                    _store_split(cat.at[h], r, t)
                    zs[h, r, :] = a
                    t0[h, r, :] = t[:, 0:1]

            def later_sums(h):
                lt[h] = _dot(cat[h], tri2)

            def weights(h):
                for r in _chunks():
                    later = lt[h, r, :]
                    w = jnp.exp(zs[h, r, :] - later - rsum[h, r, :])
                    if masked:
                        w = w * keep[r, :]
                    g = gs[h, r, :] * w
                    gs[h, r, :] = g
                    _store_split(cat.at[h], r, g)
                    wb[h, r, :] = w.astype(BF16)
                    lt0[h, r, :] = later[:, 0:1]

            def suffix_sums(h):
                lt[h] = _dot(cat[h], tri_incl2)

            def score_grads(h):
                for r in _chunks():
                    suffix = lt[h, r, :]
                    g = gs[h, r, :]
                    before = total[h, r, :] - (suffix + gsum[h, r, :])
                    dz = g - jnp.exp(zs[h, r, :]) * (g + before)
                    if masked:
                        dz = dz * keep[r, :]
                    dzb[h, r, :] = dz.astype(BF16)
                    rsum[h, r, :] += lt0[h, r, :] + t0[h, r, :]
                    gsum[h, r, :] += suffix[:, 0:1]

            def input_grads(h):
                dqacc[h] += _dot(dzb[h], kb[h, keys, :])
                dkacc[h, keys, :] += _dot_tn(dzb[h], q[h])
                dvacc[h, keys, :] += _dot_tn(wb[h], dob[h])

            _staggered([scores, fails, later_sums, weights, suffix_sums, score_grads, input_grads])

        _sweep_keys(i, tile, rsum)
        both = lambda acc2: jnp.concatenate([acc2[0], acc2[1]], axis=1)
        dqkv_ref[0, qrows, :] = (both(dqacc) * ATT_SCALE).astype(BF16)

        @pl.when(i == nq - 1)
        def _():
            dqkv_ref[1] = both(dkacc).astype(BF16)
            dqkv_ref[2] = both(dvacc).astype(BF16)

        if n_h:
            @pl.when(jnp.logical_and(p == n_pairs - 1, i == nq - 1))
            def _():
                _Exchange(h_ins, h_outs, *scratch[n_scratch:]).wait()

    out = pl.pallas_call(
        body, name="attn_bwd", grid=(n_pairs, nq),
        in_specs=[pl.BlockSpec((B, 128), lambda p, i: (i, p)),
                  pl.BlockSpec((T, 128), lambda p, i: (0, 4 + p)),
                  pl.BlockSpec((T, 128), lambda p, i: (0, 8 + p)),
                  pl.BlockSpec((B, 128), lambda p, i: (i, p)),
                  pl.BlockSpec((B, 128), lambda p, i: (i, p))] + h_in,
        out_specs=[pl.BlockSpec((3, T, 128), lambda p, i: (0, 0, p))] + h_out,
        out_shape=[jax.ShapeDtypeStruct((3, T, D_ATT), BF16)] + h_shape,
        scratch_shapes=[pltpu.VMEM((2, T, HEAD_DIM), BF16), pltpu.VMEM((2, T, HEAD_DIM), BF16),
                        pltpu.VMEM((2, T, HEAD_DIM), F32), pltpu.VMEM((2, T, HEAD_DIM), F32),
                        pltpu.VMEM((2, B, HEAD_DIM), F32)]
        + [pltpu.VMEM((2, B, 1), F32)] * 5
        + [pltpu.VMEM((2, B, B), F32)] * 3
        + [pltpu.VMEM((2, B, 2 * B), BF16), pltpu.VMEM((2, B, B), BF16), pltpu.VMEM((2, B, B), BF16)]
        + [pltpu.VMEM((B, B), F32)]
        + h_sems,
        compiler_params=_params("arbitrary", "arbitrary"),
    )(u, u, u, att, datt, *hosted)
    return out[0], out[1:]


MIX_TILE = 128
MIX_CHUNK = 16


def _u_specs(tm, tile_of):
    return [pl.BlockSpec((tm, SEG), functools.partial(lambda k, *g: (tile_of(*g), k), k)) for k in range(3, N_SEG)]


SHIFT_ROWS = 24


def _shifted_scratch(tm):
    return pltpu.VMEM((7, tm + SHIFT_ROWS, D_CONV), F32)


def _make_shifted(base, shifted, tm):
    for b in range(1, 8):
        shifted[b - 1] = base[b:b + tm + SHIFT_ROWS, :]


def _tap_window(base, shifted, start, rows):
    whole, b = divmod(start, 8)
    if b == 0:
        return base[start + rows.start:start + rows.stop, :]
    return shifted[b - 1, 8 * whole + rows.start:8 * whole + rows.stop, :]


def _conv(ext, ext_shifted, cw_ref, rows):
    acc = cw_ref[0:1, :] * _tap_window(ext, ext_shifted, 2, rows)
    for k in range(1, CONV_WIDTH):
        acc = acc + cw_ref[k:k + 1, :] * _tap_window(ext, ext_shifted, 2 + k, rows)
    return acc


def _norm_stats(v):
    mu = jnp.mean(v, axis=-1, keepdims=True)
    vc = v - mu
    rstd = lax.rsqrt(jnp.mean(vc * vc, axis=-1, keepdims=True) + LN_EPS)
    return vc * rstd, rstd


def _norm_bwd(dy_scaled, xhat, rstd):
    return rstd * (dy_scaled - jnp.mean(dy_scaled, axis=-1, keepdims=True)
                   - xhat * jnp.mean(dy_scaled * xhat, axis=-1, keepdims=True))


def _mix_fwd(x, u, att, wap, wcp, wout, cw, cvec, dvec, layer):
    T = x.shape[0]
    tm = MIX_TILE

    def body(x_ref, za_ref, ga_ref, gb_ref, zc_ref, ga0_ref, ga1_ref, gc0_ref, gc1_ref, att_ref,
             wap_ref, wcp_ref, wout_ref, cw_ref, cvec_ref, dvec_ref, xn_ref, xnb_ref, h_ref, ext, ext_shifted):
        i = pl.program_id(0)

        @pl.when(i == 0)
        def _():
            ext[0:CONV_PAD, :] = jnp.zeros((CONV_PAD, D_CONV), F32)

        za = za_ref[...]
        attg = att_ref[...] * (za * _sig(za))
        ab = _dot(attg.astype(BF16), wap_ref[...])

        ext[CONV_PAD:CONV_PAD + tm, :] = ga_ref[...] * _sig(gb_ref[...])
        _make_shifted(ext, ext_shifted, tm)
        c1 = _conv(ext, ext_shifted, cw_ref, slice(0, tm)) + cvec_ref[0:1, :]
        ext[0:CONV_PAD, :] = ext[tm:tm + CONV_PAD, :]
        xh, _ = _norm_stats(c1)
        c2 = xh * cvec_ref[1:2, :] + cvec_ref[2:3, :]
        zc = zc_ref[...]
        cz = (c2 * _sig(c2)) * (zc * _sig(zc))
        cb = _dot(cz.astype(BF16), wcp_ref[...]) + dvec_ref[0:1, :]

        g_att = jnp.concatenate([ga0_ref[...], ga1_ref[...]], axis=1)
        g_conv = jnp.concatenate([gc0_ref[...], gc1_ref[...]], axis=1)
        merged = _sig(g_att) * ab + _sig(g_conv) * cb
        h = DEEPNORM_ALPHA * x_ref[...] + _dot(merged.astype(BF16), wout_ref[...])
        h_ref[...] = h
        hh, _ = _norm_stats(h)
        xn = hh * dvec_ref[1:2, :] + dvec_ref[2:3, :]
        xn_ref[...] = xn
        xnb_ref[...] = xn.astype(BF16)

    row = pl.BlockSpec((tm, D_MODEL), lambda i: (i, 0))
    return pl.pallas_call(
        body, name="mix_fwd", grid=(T // tm,),
        in_specs=[row] + _u_specs(tm, lambda i: i) + [
            pl.BlockSpec((tm, D_ATT), lambda i: (i, 0)),
            pl.BlockSpec((D_ATT, D_MODEL), lambda i: (0, 0)),
            pl.BlockSpec((D_CONV, D_MODEL), lambda i: (0, 0)),
            pl.BlockSpec((D_MODEL, D_MODEL), lambda i: (0, 0)),
            pl.BlockSpec((CONV_PAD, D_CONV), lambda i: (0, 0)),
            pl.BlockSpec((None, 8, D_CONV), lambda i: (layer, 0, 0)),
            pl.BlockSpec((None, 8, D_MODEL), lambda i: (layer, 0, 0))],
        out_specs=[row, row, row],
        out_shape=[jax.ShapeDtypeStruct((T, D_MODEL), F32), jax.ShapeDtypeStruct((T, D_MODEL), BF16),
                   jax.ShapeDtypeStruct((T, D_MODEL), F32)],
        scratch_shapes=[pltpu.VMEM((tm + CONV_PAD, D_CONV), F32), _shifted_scratch(tm)],
        compiler_params=_params("arbitrary"),
    )(x, *([u] * 8), att, wap, wcp, wout, cw, cvec, dvec)


def _mix_bwd(dxo, h, u, att, wap, wcp, wout, cw, cvec, dvec, layer, hosted=()):
    T = dxo.shape[0]
    tm = MIX_TILE
    nt = T // tm
    halo_per_tile = tm // CONV_PAD
    n_h = len(hosted)
    h_in, h_out, h_shape, h_sems = _host(hosted)

    def body(*refs):
        ins, h_ins, outs, h_outs, scratch = _split_refs(refs, 19, n_h, 9, n_h)
        (dxo_ref, h_ref, za_ref, ga_ref, gb_ref, zc_ref, ga0_ref, ga1_ref, gc0_ref, gc1_ref, hga_ref, hgb_ref,
         att_ref, wap_ref, wcp_ref, wout_ref, cw_ref, cvec_ref, dvec_ref) = ins
        datt_ref, dur_ref, dres_ref, dwap_ref, dwcp_ref, dwout_ref, dcw_ref, dcvec_ref, ddvec_ref = outs
        n_scratch = 19
        (ext, dext, ext_shifted, dext_shifted, attg_s, cz_s, merged_s, dy_s, dab_s, dcb_s,
         xh_s, dattg_s, dcz_s, ab_s, cb_s, sga_s, sgc_s, dm_s, rstd_s) = scratch[:n_scratch]
        r = pl.program_id(0)
        i = nt - 1 - r

        if n_h:
            @pl.when(r == 0)
            def _():
                _Exchange(h_ins, h_outs, *scratch[n_scratch:]).start()

        @pl.when(r == 0)
        def _():
            for ref in (dwap_ref, dwcp_ref, dwout_ref, dcw_ref, dcvec_ref, ddvec_ref):
                ref[...] = jnp.zeros_like(ref)
            dext[tm:tm + CONV_PAD, :] = jnp.zeros((CONV_PAD, D_CONV), F32)

        chunks = [slice(c, c + MIX_CHUNK) for c in range(0, tm, MIX_CHUNK)]
        whole = slice(0, tm)
        ln_g_c, ln_b_c = cvec_ref[1:2, :], cvec_ref[2:3, :]

        halo = hga_ref[...] * _sig(hgb_ref[...])
        ext[0:CONV_PAD, :] = jnp.where(i > 0, halo, 0.0)
        for c in chunks:
            za = za_ref[c, :]
            attg_s[c, :] = (att_ref[c, :] * (za * _sig(za))).astype(BF16)
            ext[CONV_PAD + c.start:CONV_PAD + c.stop, :] = ga_ref[c, :] * _sig(gb_ref[c, :])
        _make_shifted(ext, ext_shifted, tm)
        for c in chunks:
            c1 = _conv(ext, ext_shifted, cw_ref, c) + cvec_ref[0:1, :]
            xh, rstd_c = _norm_stats(c1)
            xh_s[c, :] = xh
            rstd_s[c, :] = rstd_c
            c2 = xh * ln_g_c + ln_b_c
            zc = zc_ref[c, :]
            cz_s[c, :] = ((c2 * _sig(c2)) * (zc * _sig(zc))).astype(BF16)
        ab_s[...] = _dot(attg_s[...], wap_ref[...])
        cb_s[...] = _dot(cz_s[...], wcp_ref[...]) + dvec_ref[0:1, :]

        for c in chunks:
            s_ga = _sig(jnp.concatenate([ga0_ref[c, :], ga1_ref[c, :]], axis=1))
            s_gc = _sig(jnp.concatenate([gc0_ref[c, :], gc1_ref[c, :]], axis=1))
            sga_s[c, :] = s_ga
            sgc_s[c, :] = s_gc
            merged_s[c, :] = (s_ga * ab_s[c, :] + s_gc * cb_s[c, :]).astype(BF16)
            dxo_c = dxo_ref[c, :]
            hh, rstd_h = _norm_stats(h_ref[c, :])
            ddvec_ref[1:2, :] += jnp.sum(dxo_c * hh, axis=0, keepdims=True)
            ddvec_ref[2:3, :] += jnp.sum(dxo_c, axis=0, keepdims=True)
            dh = _norm_bwd(dxo_c * dvec_ref[1:2, :], hh, rstd_h)
            dres_ref[c, :] = DEEPNORM_ALPHA * dh
            dy_s[c, :] = dh.astype(BF16)
        dwout_ref[...] += _dot_tn(merged_s[...], dy_s[...])
        dm_s[...] = _dot_nt(dy_s[...], wout_ref[...])

        for c in chunks:
            dm, s_ga, s_gc = dm_s[c, :], sga_s[c, :], sgc_s[c, :]
            dab_s[c, :] = (dm * s_ga).astype(BF16)
            dcb = dm * s_gc
            dcb_s[c, :] = dcb.astype(BF16)
            ddvec_ref[0:1, :] += jnp.sum(dcb, axis=0, keepdims=True)
            dur_ref[c, 4 * SEG:6 * SEG] = (dm * ab_s[c, :] * (s_ga * (1.0 - s_ga))).astype(BF16)
            dur_ref[c, 6 * SEG:8 * SEG] = (dm * cb_s[c, :] * (s_gc * (1.0 - s_gc))).astype(BF16)
        dwap_ref[...] += _dot_tn(attg_s[...], dab_s[...])
        dattg_s[...] = _dot_nt(dab_s[...], wap_ref[...])
        dwcp_ref[...] += _dot_tn(cz_s[...], dcb_s[...])
        dcz_s[...] = _dot_nt(dcb_s[...], wcp_ref[...])

        for c in chunks:
            za = za_ref[c, :]
            s_za = _sig(za)
            dattg = dattg_s[c, :]
            datt_ref[c, :] = dattg * (za * s_za)
            dur_ref[c, 0:SEG] = (dattg * att_ref[c, :] * (s_za * (1.0 + za * (1.0 - s_za)))).astype(BF16)
            xh = xh_s[c, :]
            c2 = xh * ln_g_c + ln_b_c
            s_c2 = _sig(c2)
            zc = zc_ref[c, :]
            s_zc = _sig(zc)
            dcz = dcz_s[c, :]
            dur_ref[c, 3 * SEG:4 * SEG] = (dcz * (c2 * s_c2) * (s_zc * (1.0 + zc * (1.0 - s_zc)))).astype(BF16)
            dc2 = dcz * (zc * s_zc) * (s_c2 * (1.0 + c2 * (1.0 - s_c2)))
            dcvec_ref[1:2, :] += jnp.sum(dc2 * xh, axis=0, keepdims=True)
            dcvec_ref[2:3, :] += jnp.sum(dc2, axis=0, keepdims=True)
            dc1 = _norm_bwd(dc2 * ln_g_c, xh, rstd_s[c, :])
            dcvec_ref[0:1, :] += jnp.sum(dc1, axis=0, keepdims=True)
            dext[c, :] = dc1

        dc1_tile = dext[0:tm, :]
        for k in range(CONV_WIDTH):
            dcw_ref[k:k + 1, :] += jnp.sum(dc1_tile * _tap_window(ext, ext_shifted, 2 + k, whole),
                                           axis=0, keepdims=True)
        _make_shifted(dext, dext_shifted, tm)
        for c in chunks:
            dc0 = cw_ref[0:1, :] * _tap_window(dext, dext_shifted, CONV_WIDTH - 1, c)
            for k in range(1, CONV_WIDTH):
                dc0 = dc0 + cw_ref[k:k + 1, :] * _tap_window(dext, dext_shifted, CONV_WIDTH - 1 - k, c)
            s_gb = _sig(gb_ref[c, :])
            dur_ref[c, SEG:2 * SEG] = (dc0 * s_gb).astype(BF16)
            dur_ref[c, 2 * SEG:3 * SEG] = (dc0 * ga_ref[c, :] * (s_gb * (1.0 - s_gb))).astype(BF16)
        dext[tm:tm + CONV_PAD, :] = dext[0:CONV_PAD, :]

        if n_h:
            @pl.when(r == nt - 1)
            def _():
                _Exchange(h_ins, h_outs, *scratch[n_scratch:]).wait()

    rev = lambda r: nt - 1 - r
    row = pl.BlockSpec((tm, D_MODEL), lambda r: (rev(r), 0))
    halo_spec = [pl.BlockSpec((CONV_PAD, SEG),
                              functools.partial(lambda k, r: (jnp.maximum(rev(r) * halo_per_tile - 1, 0), k), k))
                 for k in (4, 5)]
    const = lambda shape: pl.BlockSpec(shape, lambda r: (0,) * len(shape))
    out = pl.pallas_call(
        body, name="mix_bwd", grid=(nt,),
        in_specs=[row, row] + _u_specs(tm, rev) + halo_spec + [
            pl.BlockSpec((tm, D_ATT), lambda r: (rev(r), 0)),
            pl.BlockSpec((D_ATT, D_MODEL), lambda r: (0, 0)),
            pl.BlockSpec((D_CONV, D_MODEL), lambda r: (0, 0)),
            pl.BlockSpec((D_MODEL, D_MODEL), lambda r: (0, 0)),
            pl.BlockSpec((CONV_PAD, D_CONV), lambda r: (0, 0)),
            pl.BlockSpec((None, 8, D_CONV), lambda r: (layer, 0, 0)),
            pl.BlockSpec((None, 8, D_MODEL), lambda r: (layer, 0, 0))] + h_in,
        out_specs=[pl.BlockSpec((tm, D_ATT), lambda r: (rev(r), 0)),
                   pl.BlockSpec((tm, 8 * SEG), lambda r: (rev(r), 0)),
                   row,
                   const((D_ATT, D_MODEL)), const((D_CONV, D_MODEL)), const((D_MODEL, D_MODEL)),
                   const((CONV_PAD, D_CONV)), const((8, D_CONV)), const((8, D_MODEL))] + h_out,
        out_shape=[jax.ShapeDtypeStruct((T, D_ATT), F32), jax.ShapeDtypeStruct((T, 8 * SEG), BF16),
                   jax.ShapeDtypeStruct((T, D_MODEL), F32),
                   jax.ShapeDtypeStruct((D_ATT, D_MODEL), F32), jax.ShapeDtypeStruct((D_CONV, D_MODEL), F32),
                   jax.ShapeDtypeStruct((D_MODEL, D_MODEL), F32), jax.ShapeDtypeStruct((CONV_PAD, D_CONV), F32),
                   jax.ShapeDtypeStruct((8, D_CONV), F32), jax.ShapeDtypeStruct((8, D_MODEL), F32)] + h_shape,
        scratch_shapes=[pltpu.VMEM((tm + CONV_PAD, D_CONV), F32), pltpu.VMEM((tm + CONV_PAD, D_CONV), F32),
                        _shifted_scratch(tm), _shifted_scratch(tm),
                        pltpu.VMEM((tm, D_ATT), BF16), pltpu.VMEM((tm, D_CONV), BF16)]
        + [pltpu.VMEM((tm, D_MODEL), BF16)] * 4
        + [pltpu.VMEM((tm, D_CONV), F32)] * 3
        + [pltpu.VMEM((tm, D_MODEL), F32)] * 5
        + [pltpu.VMEM((tm, 1), F32)]
        + h_sems,
        compiler_params=_params("arbitrary"),
    )(dxo, h, *([u] * 10), att, wap, wcp, wout, cw, cvec, dvec, *hosted)
    return out[:9], out[9:]


def _loss_head(y, target):
    T = y.shape[0]
    tm = min(T, 512)

    def body(y_ref, t_ref, dy_ref, loss_ref):
        @pl.when(pl.program_id(0) == 0)
        def _():
            loss_ref[...] = jnp.zeros_like(loss_ref)

        err = y_ref[...] - t_ref[...]
        dy_ref[...] = err * (1.0 / D_MODEL)
        loss_ref[...] += 0.5 * jnp.sum(jnp.mean(err * err, axis=-1, keepdims=True))

    row = pl.BlockSpec((tm, D_MODEL), lambda i: (i, 0))
    return pl.pallas_call(
        body, name="loss_head", grid=(T // tm,), in_specs=[row, row],
        out_specs=[row, pl.BlockSpec((8, 128), lambda i: (0, 0))],
        out_shape=[jax.ShapeDtypeStruct((T, D_MODEL), F32), jax.ShapeDtypeStruct((8, 128), F32)],
        compiler_params=_params("arbitrary"),
    )(y, target)


def _adamw(w, g, m, v):
    m = ADAM_B1 * m + (1.0 - ADAM_B1) * g
    v = ADAM_B2 * v + (1.0 - ADAM_B2) * (g * g)
    m_hat = m / (1.0 - ADAM_B1 ** ADAM_STEP)
    v_hat = v / (1.0 - ADAM_B2 ** ADAM_STEP)
    return -ADAM_LR * (m_hat / (jnp.sqrt(v_hat) + ADAM_EPS) + ADAM_WD * w), m, v


def _sum_slots(ref):
    total = ref[0].astype(F32)
    for j in range(1, N_DEV):
        total = total + ref[j].astype(F32)
    return total


def _adamw_sharded(recv, w, m, v, prev, layer, tr, name):
    n_layers, R, C = w.shape

    def body(*refs):
        recv_ref, w_ref, m_ref, v_ref = refs[:4]
        g_ref, d_ref, nm_ref, nv_ref = refs[-4:]
        g = _sum_slots(recv_ref)
        g_ref[...] = g
        d_ref[...], nm_ref[...], nv_ref[...] = _adamw(w_ref[...], g, m_ref[...], v_ref[...])

    slab = pl.BlockSpec((None, tr, C), lambda i: (layer, i, 0))
    n_prev = 0 if prev is None else 4
    return pl.pallas_call(
        body, name=name, grid=(R // tr,),
        in_specs=[pl.BlockSpec((N_DEV, tr, C), lambda i: (0, i, 0)), slab, slab, slab] + [ANY] * n_prev,
        out_specs=[slab] * 4,
        out_shape=[jax.ShapeDtypeStruct(w.shape, F32)] * 4,
        input_output_aliases={4 + k: k for k in range(n_prev)},
        compiler_params=_params("parallel"),
    )(recv, w, m, v, *(prev or ()))


def _sum8(g):
    R = g.shape[1]

    def body(g_ref, o_ref):
        o_ref[...] = _sum_slots(g_ref)

    return pl.pallas_call(body, name="sum_small_grads", out_shape=jax.ShapeDtypeStruct((R, 128), F32))(g)


def _adamw_small(w, g, m, v):
    def body(w_ref, g_ref, m_ref, v_ref, d_ref, nm_ref, nv_ref):
        d_ref[...], nm_ref[...], nv_ref[...] = _adamw(w_ref[...], g_ref[...], m_ref[...], v_ref[...])

    return pl.pallas_call(body, name="adamw_small", out_shape=[jax.ShapeDtypeStruct(w.shape, F32)] * 3)(w, g, m, v)


REPLICATED = (("b_in", D_IN), ("conv_b", D_CONV), ("conv_ln_g", D_CONV), ("conv_ln_b", D_CONV),
              ("b_conv_proj", D_MODEL), ("ln_g", D_MODEL), ("ln_b", D_MODEL))
N_REPLICATED = sum(n for _, n in REPLICATED)
CONV_W_SHARD_PAD = 2048


def _pack_small(rep, conv_w_shard):
    flat = jnp.concatenate([rep[k] for k, _ in REPLICATED], axis=1)
    cws = jnp.pad(conv_w_shard.reshape(DEPTH, -1), ((0, 0), (0, CONV_W_SHARD_PAD - CONV_WIDTH * 64)))
    return jnp.concatenate([flat, cws], axis=1).reshape(-1, 128)


def _unpack_small(packed):
    per = packed.reshape(DEPTH, N_REPLICATED + CONV_W_SHARD_PAD)
    out, off = {}, 0
    for k, n in REPLICATED:
        out[k] = per[:, off:off + n]
        off += n
    out["conv_w"] = per[:, off:off + CONV_WIDTH * 64].reshape(DEPTH, CONV_WIDTH, 64)
    return out


def kernel(x, w_in, b_in, conv_w, conv_b, conv_ln_g, conv_ln_b, w_att_proj, w_conv_proj, b_conv_proj, w_out, ln_g, ln_b, loss_target, m_w_in, m_b_in, m_conv_w, m_conv_b, m_conv_ln_g, m_conv_ln_b, m_w_att_proj, m_w_conv_proj, m_b_conv_proj, m_w_out, m_ln_g, m_ln_b, v_w_in, v_b_in, v_conv_w, v_conv_b, v_conv_ln_g, v_conv_ln_b, v_w_att_proj, v_w_conv_proj, v_b_conv_proj, v_w_out, v_ln_g, v_ln_b):
    x0 = x[0]
    target = loss_target[0]

    def shards(l):
        return [w_in[l].astype(BF16), w_att_proj[l].astype(BF16), w_conv_proj[l].astype(BF16),
                w_out[l].astype(BF16), conv_w[l]]

    def full_weights(g_in, g_ap, g_cp, g_out, g_cw):
        return (g_in.transpose(1, 0, 2).reshape(D_MODEL, D_IN), g_ap.transpose(1, 0, 2).reshape(D_ATT, D_MODEL),
                g_cp.transpose(1, 0, 2).reshape(D_CONV, D_MODEL), g_out.reshape(D_MODEL, D_MODEL),
                jnp.pad(g_cw.transpose(1, 0, 2).reshape(CONV_WIDTH, D_CONV), ((0, 1), (0, 0))))

    pad8 = lambda rows: jnp.pad(jnp.stack(rows, axis=1), ((0, 0), (0, 8 - len(rows)), (0, 0)))
    cvec = pad8([conv_b, conv_ln_g, conv_ln_b])
    dvec = pad8([b_conv_proj, ln_g, ln_b])
    b_in3 = b_in.reshape(DEPTH, 1, D_IN)

    weights = [full_weights(*_all_gather(shards(0), "gather_weights"))]
    xs, xbs, us, atts, hs = [x0], [x0.astype(BF16)], [], [], []
    for l in range(DEPTH):
        w_in_l, wap, wcp, wout, cw = weights[l]
        u = _in_proj(xbs[l], w_in_l, b_in3, l)
        att, gathered = _attn_fwd(u, gather=shards(l + 1) if l + 1 < DEPTH else ())
        if gathered:
            weights.append(full_weights(*gathered))
        xn, xnb, h = _mix_fwd(xs[l], u, att, wap, wcp, wout, cw, cvec, dvec, l)
        us.append(u), atts.append(att), hs.append(h), xs.append(xn), xbs.append(xnb)
    dxo, loss_part = _loss_head(xs[DEPTH], target)
    loss = lax.psum(loss_part[0, 0], ("x", "y", "c"))

    big = {"w_in": (w_in, m_w_in, v_w_in, 256), "w_att_proj": (w_att_proj, m_w_att_proj, v_w_att_proj, D_ATT),
           "w_conv_proj": (w_conv_proj, m_w_conv_proj, v_w_conv_proj, D_CONV), "w_out": (w_out, m_w_out, v_w_out, 128)}
    results = {k: None for k in big}
    small = [None] * DEPTH

    def update(k, recv, l):
        w, m, v, tr = big[k]
        results[k] = _adamw_sharded(recv, w, m, v, results[k], l, tr, "adamw_" + k)

    pending_w_in = ()
    for l in reversed(range(DEPTH)):
        w_in_l, wap, wcp, wout, cw = weights[l]
        (datt, du_rest, dres, dwap, dwcp, dwout, dcw, dcvec, ddvec), got = _mix_bwd(
            dxo, hs[l], us[l], atts[l], wap, wcp, wout, cw, cvec, dvec, l, hosted=pending_w_in)
        if got:
            update("w_in", got[0], l + 1)
        proj = [dwap.reshape(D_ATT, N_DEV, D_MODEL // N_DEV).transpose(1, 0, 2).astype(BF16),
                dwcp.reshape(D_CONV, N_DEV, D_MODEL // N_DEV).transpose(1, 0, 2).astype(BF16),
                dwout.reshape(N_DEV, D_MODEL // N_DEV, D_MODEL).astype(BF16)]
        dqkv, got = _attn_bwd(us[l], atts[l], datt, hosted=proj)
        for k, r in zip(("w_att_proj", "w_conv_proj", "w_out"), got):
            update(k, r, l)
        dwin, dbin = _in_proj_dw(xbs[l], dqkv, du_rest)
        small[l] = jnp.concatenate([dbin.reshape(-1), dcvec[:3].reshape(-1), ddvec[:3].reshape(-1), dcw.reshape(-1)])
        pending_w_in = (dwin.reshape(D_MODEL, N_DEV, W_IN_SHARD).transpose(1, 0, 2).astype(BF16),)
        dxo, got = _in_proj_dx(dqkv, du_rest, w_in_l, dres, hosted=pending_w_in if l == 0 else ())
        if got:
            update("w_in", got[0], l)
    grad_x = dxo[None]

    small_all = jnp.stack(small).reshape(-1, 128)
    (gathered,) = _all_gather([small_all], "gather_small_grads")
    per_layer = _sum8(gathered).reshape(DEPTH, -1)
    g_rep, off = {}, 0
    for k, n in (("b_in", D_IN), ("conv_b", D_CONV), ("conv_ln_g", D_CONV), ("conv_ln_b", D_CONV),
                 ("b_conv_proj", D_MODEL), ("ln_g", D_MODEL), ("ln_b", D_MODEL)):
        g_rep[k] = per_layer[:, off:off + n]
        off += n
    g_cw_full = per_layer[:, off:].reshape(DEPTH, CONV_PAD, D_CONV)[:, :CONV_WIDTH]
    g_cw_shard = lax.dynamic_slice_in_dim(g_cw_full, _my_index() * 64, 64, axis=2)
    rep_w = dict(b_in=b_in, conv_b=conv_b, conv_ln_g=conv_ln_g, conv_ln_b=conv_ln_b, b_conv_proj=b_conv_proj,
                 ln_g=ln_g, ln_b=ln_b)
    rep_m = dict(b_in=m_b_in, conv_b=m_conv_b, conv_ln_g=m_conv_ln_g, conv_ln_b=m_conv_ln_b,
                 b_conv_proj=m_b_conv_proj, ln_g=m_ln_g, ln_b=m_ln_b)
    rep_v = dict(b_in=v_b_in, conv_b=v_conv_b, conv_ln_g=v_conv_ln_g, conv_ln_b=v_conv_ln_b,
                 b_conv_proj=v_b_conv_proj, ln_g=v_ln_g, ln_b=v_ln_b)
    sd, sm, sv = _adamw_small(_pack_small(rep_w, conv_w), _pack_small(g_rep, g_cw_shard),
                              _pack_small(rep_m, m_conv_w), _pack_small(rep_v, v_conv_w))
    small_out = [dict(g_rep, conv_w=g_cw_shard), _unpack_small(sd), _unpack_small(sm), _unpack_small(sv)]

    order = ["w_in", "b_in", "conv_w", "conv_b", "conv_ln_g", "conv_ln_b", "w_att_proj", "w_conv_proj",
             "b_conv_proj", "w_out", "ln_g", "ln_b"]
    outs = [loss, grad_x]
    for kind in range(4):
        outs += [results[k][kind] if k in big else small_out[kind][k] for k in order]
    return tuple(outs)
```

```python
import functools

import jax
import jax.numpy as jnp
from jax import lax
from jax.experimental import pallas as pl
from jax.experimental.pallas import tpu as pltpu

F32, BF16 = jnp.float32, jnp.bfloat16

D_MODEL = 1024
D_ATT = 512
D_CONV = 512
HEAD_DIM = 64
CONV_WIDTH = 31
CONV_PAD = 32
DEPTH = 4
N_DEV = 8
SEG = 512
D_IN = 4 * D_ATT + 3 * D_CONV + 2 * D_MODEL
N_SEG = D_IN // SEG
N_QKV_SEG = 3
W_IN_SHARD = D_IN // N_DEV
LN_EPS = 1e-5
DEEPNORM_ALPHA = (2 * DEPTH) ** 0.25
ATT_SCALE = HEAD_DIM ** -0.5
ATT_BLOCK = 256
ADAM_LR, ADAM_B1, ADAM_B2, ADAM_EPS, ADAM_WD, ADAM_STEP = 0.001, 0.9, 0.999, 1e-08, 0.01, 10
VMEM_LIMIT = 56 * 1024 * 1024
MESH = pl.DeviceIdType.MESH
ANY = pl.BlockSpec(memory_space=pl.ANY)


def _resident(shape, index_map):
    return pl.BlockSpec(shape, index_map, pipeline_mode=pl.Buffered(1))


def _params(*semantics):
    return pltpu.CompilerParams(dimension_semantics=semantics, vmem_limit_bytes=VMEM_LIMIT)


def _sig(x):
    return 1.0 / (1.0 + jnp.exp(-x))


def _dot(a, b):
    return jnp.dot(a, b, preferred_element_type=F32)


def _dot_nt(a, b):
    return lax.dot_general(a, b, (((1,), (1,)), ((), ())), preferred_element_type=F32)


def _dot_tn(a, b):
    return lax.dot_general(a, b, (((0,), (0,)), ((), ())), preferred_element_type=F32)


def _my_index():
    return 4 * lax.axis_index("x") + 2 * lax.axis_index("y") + lax.axis_index("c")


class _Gather:
    def __init__(self, ins, outs, send_sems, recv_sems, local_sems):
        self.n = n = len(ins)
        x, y, c = lax.axis_index("x"), lax.axis_index("y"), lax.axis_index("c")
        me, sibling = (x, y, c), (x, y, 1 - c)
        chips = [(1 - x, y), (x, 1 - y), (1 - x, 1 - y)]

        def slot(a, dev):
            return outs[a].at[4 * dev[0] + 2 * dev[1] + dev[2]]

        def copy(a, k, block, to, src=None):
            return pltpu.make_async_remote_copy(
                src_ref=slot(a, block) if src is None else src, dst_ref=slot(a, block),
                send_sem=send_sems.at[7 * a + k], recv_sem=recv_sems.at[7 * a + k], device_id=to, device_id_type=MESH)

        self.mine = [pltpu.make_async_copy(ins[a], slot(a, me), local_sems.at[a]) for a in range(n)]
        self.first, self.landed, self.passed, self.last = [], [], [], []
        for a in range(n):
            self.first.append(copy(a, 0, me, sibling, src=ins[a]))
            self.first += [copy(a, 1 + j, me, (*chip, c), src=ins[a]) for j, chip in enumerate(chips)]
        for j, chip in enumerate(chips):
            for a in range(n):
                self.landed.append(copy(a, 1 + j, (*chip, c), me))
                self.passed.append(copy(a, 4 + j, (*chip, c), sibling))
        for a in range(n):
            self.last.append(copy(a, 0, sibling, me))
            self.last += [copy(a, 4 + j, (*chip, 1 - c), me) for j, chip in enumerate(chips)]

    @staticmethod
    def semaphores(n):
        return [pltpu.SemaphoreType.DMA((7 * n,)), pltpu.SemaphoreType.DMA((7 * n,)), pltpu.SemaphoreType.DMA((n,))]

    def start(self):
        for cp in self.mine + self.first:
            cp.start()

    def forward(self):
        for landed, passed in zip(self.landed, self.passed):
            landed.wait_recv()
            passed.start()

    def wait(self):
        for cp in self.last:
            cp.wait_recv()
        for cp in self.first + self.passed:
            cp.wait_send()
        for cp in self.mine:
            cp.wait()


def _gathered_shapes(shards):
    return [jax.ShapeDtypeStruct((N_DEV,) + s.shape, s.dtype) for s in shards]


def _all_gather(shards, name):
    n = len(shards)

    def body(*refs):
        g = _Gather(refs[:n], refs[n:2 * n], *refs[2 * n:])
        g.start()
        g.forward()
        g.wait()

    return pl.pallas_call(
        body, name=name, out_shape=_gathered_shapes(shards),
        in_specs=[ANY] * n, out_specs=[ANY] * n, scratch_shapes=_Gather.semaphores(n),
    )(*shards)


class _Exchange:
    def __init__(self, ins, outs, send_sems, recv_sems, local_sems):
        n = len(ins)
        x, y, c = lax.axis_index("x"), lax.axis_index("y"), lax.axis_index("c")
        me_idx = 4 * x + 2 * y + c
        self.mine = [pltpu.make_async_copy(ins[a].at[me_idx], outs[a].at[me_idx], local_sems.at[a])
                     for a in range(n)]
        self.sends, self.recvs = [], []
        for d in (1, 2, 4, 3, 5, 6, 7):
            px = 1 - x if d & 4 else x
            py = 1 - y if d & 2 else y
            pc = 1 - c if d & 1 else c
            idx = 4 * px + 2 * py + pc
            for a in range(n):
                sems = dict(send_sem=send_sems.at[7 * a + d - 1], recv_sem=recv_sems.at[7 * a + d - 1],
                            device_id=(px, py, pc), device_id_type=MESH)
                self.sends.append(pltpu.make_async_remote_copy(
                    src_ref=ins[a].at[idx], dst_ref=outs[a].at[me_idx], **sems))
                self.recvs.append(pltpu.make_async_remote_copy(
                    src_ref=ins[a].at[idx], dst_ref=outs[a].at[idx], **sems))

    @staticmethod
    def semaphores(n):
        return [pltpu.SemaphoreType.DMA((7 * n,)), pltpu.SemaphoreType.DMA((7 * n,)), pltpu.SemaphoreType.DMA((n,))]

    def start(self):
        for cp in self.mine + self.sends:
            cp.start()

    def wait(self):
        for cp in self.recvs:
            cp.wait_recv()
        for cp in self.sends:
            cp.wait_send()
        for cp in self.mine:
            cp.wait()


def _host(hosted):
    n = len(hosted)
    return ([ANY] * n, [ANY] * n, [jax.ShapeDtypeStruct(p.shape, p.dtype) for p in hosted],
            _Exchange.semaphores(n) if n else [])


def _split_refs(refs, *counts):
    out, at = [], 0
    for k in counts:
        out.append(refs[at:at + k])
        at += k
    return out + [refs[at:]]


def _in_proj(xb, w_in, b_in, layer):
    T = xb.shape[0]
    tm, tn = min(T, 2048), SEG

    def body(x_ref, w_ref, b_ref, u_ref):
        u_ref[...] = _dot(x_ref[...], w_ref[...]) + b_ref[...]

    return pl.pallas_call(
        body, name="in_proj", grid=(T // tm, D_IN // tn),
        in_specs=[pl.BlockSpec((tm, D_MODEL), lambda m, n: (m, 0)),
                  pl.BlockSpec((D_MODEL, tn), lambda m, n: (0, n)),
                  pl.BlockSpec((None, 1, tn), lambda m, n: (layer, 0, n))],
        out_specs=pl.BlockSpec((tm, tn), lambda m, n: (m, n)),
        out_shape=jax.ShapeDtypeStruct((T, D_IN), F32),
        compiler_params=_params("parallel", "parallel"),
    )(xb, w_in, b_in)


def _du_specs(tm, row_of, seg_of):
    return [pl.BlockSpec((None, tm, SEG), lambda *g: (jnp.minimum(seg_of(*g), N_QKV_SEG - 1), row_of(*g), 0)),
            pl.BlockSpec((tm, SEG), lambda *g: (row_of(*g), jnp.maximum(seg_of(*g) - N_QKV_SEG, 0)))]


def _in_proj_dx(dqkv, du_rest, w_in, dres, hosted=()):
    T = du_rest.shape[0]
    tm = min(T, 2048)
    nm = T // tm
    n_h = len(hosted)
    h_in, h_out, h_shape, h_sems = _host(hosted)

    def body(*refs):
        (a1_ref, a2_ref, w_ref, r_ref), h_ins, (o_ref,), h_outs, sems = _split_refs(refs, 4, n_h, 1, n_h)
        m, s = pl.program_id(0), pl.program_id(1)

        if n_h:
            @pl.when(jnp.logical_and(m == 0, s == 0))
            def _():
                _Exchange(h_ins, h_outs, *sems).start()

        @pl.when(s == 0)
        def _():
            o_ref[...] = r_ref[...]

        @pl.when(s < N_QKV_SEG)
        def _():
            o_ref[...] += _dot_nt(a1_ref[...], w_ref[...])

        @pl.when(s >= N_QKV_SEG)
        def _():
            o_ref[...] += _dot_nt(a2_ref[...], w_ref[...])

        if n_h:
            @pl.when(jnp.logical_and(m == nm - 1, s == N_SEG - 1))
            def _():
                _Exchange(h_ins, h_outs, *sems).wait()

    out = pl.pallas_call(
        body, name="in_proj_dx", grid=(nm, N_SEG),
        in_specs=_du_specs(tm, lambda m, s: m, lambda m, s: s) + [
            pl.BlockSpec((D_MODEL, SEG), lambda m, s: (0, s)),
            pl.BlockSpec((tm, D_MODEL), lambda m, s: (m, 0))] + h_in,
        out_specs=[pl.BlockSpec((tm, D_MODEL), lambda m, s: (m, 0))] + h_out,
        out_shape=[jax.ShapeDtypeStruct((T, D_MODEL), F32)] + h_shape,
        scratch_shapes=h_sems,
        compiler_params=_params("arbitrary", "arbitrary"),
    )(dqkv, du_rest, w_in, dres, *hosted)
    return out[0], out[1:]


def _in_proj_dw(xb, dqkv, du_rest):
    T = xb.shape[0]
    tk = min(T, 1024)
    nt = T // tk

    def body(x_ref, a1_ref, a2_ref, dw_ref, db_ref, acc_ref):
        s, t = pl.program_id(0), pl.program_id(1)

        @pl.when(t == 0)
        def _():
            acc_ref[...] = jnp.zeros_like(acc_ref)
            db_ref[...] = jnp.zeros_like(db_ref)

        def acc(a):
            acc_ref[...] += _dot_tn(x_ref[...], a)
            db_ref[...] += jnp.sum(a.astype(F32), axis=0, keepdims=True)

        @pl.when(s < N_QKV_SEG)
        def _():
            acc(a1_ref[...])

        @pl.when(s >= N_QKV_SEG)
        def _():
            acc(a2_ref[...])

        @pl.when(t == nt - 1)
        def _():
            dw_ref[...] = acc_ref[...].astype(BF16)

    return pl.pallas_call(
        body, name="in_proj_dw", grid=(N_SEG, nt),
        in_specs=[pl.BlockSpec((tk, D_MODEL), lambda s, t: (t, 0))] + _du_specs(tk, lambda s, t: t, lambda s, t: s),
        out_specs=[pl.BlockSpec((D_MODEL, SEG), lambda s, t: (0, s)),
                   pl.BlockSpec((None, 1, SEG), lambda s, t: (s, 0, 0))],
        out_shape=[jax.ShapeDtypeStruct((D_MODEL, D_IN), BF16), jax.ShapeDtypeStruct((N_SEG, 1, SEG), F32)],
        scratch_shapes=[pltpu.VMEM((D_MODEL, SEG), F32)],
        compiler_params=_params("parallel", "arbitrary"),
    )(xb, dqkv, du_rest)


ATT_CHUNK = 32


def _chunks():
    return [slice(r, r + ATT_CHUNK) for r in range(0, ATT_BLOCK, ATT_CHUNK)]


def _twice(rhs):
    return jnp.concatenate([rhs, rhs], axis=0)


def _fail_and_log_beta(z):
    t = jnp.maximum(z, 0.0) + jnp.log(1.0 + jnp.exp(-jnp.abs(z)))
    return t, z - t


def _store_split(dst, rows, val):
    hi = val.astype(BF16)
    dst[rows, 0:ATT_BLOCK] = hi
    dst[rows, ATT_BLOCK:2 * ATT_BLOCK] = (val - hi.astype(F32)).astype(BF16)


def _fill_causal(keep):
    row = lax.broadcasted_iota(jnp.int32, (ATT_BLOCK, ATT_BLOCK), 0)
    col = lax.broadcasted_iota(jnp.int32, (ATT_BLOCK, ATT_BLOCK), 1)
    keep[...] = (col < row).astype(F32)


def _triangles():
    row = lax.broadcasted_iota(jnp.int32, (ATT_BLOCK, ATT_BLOCK), 0)
    col = lax.broadcasted_iota(jnp.int32, (ATT_BLOCK, ATT_BLOCK), 1)
    return _twice((row > col).astype(BF16)), _twice((row >= col).astype(BF16))


def _staggered(stages):
    stages[0](0)
    stages[0](1)
    for k in range(1, len(stages), 2):
        for h in range(2):
            stages[k](h)
            stages[k + 1](h)


EXP_UNDERFLOW = 104.0


def _sweep_keys(i, tile, rsum):
    tile(i, True)

    def live(carry):
        j, low = carry
        return jnp.logical_and(j >= 0, low < EXP_UNDERFLOW)

    def step(carry):
        j, _ = carry
        tile(j, False)
        return j - 1, jnp.min(rsum[...])

    lax.while_loop(live, step, (i - 1, jnp.min(rsum[...])))


def _attn_fwd(u, gather=()):
    T = u.shape[0]
    B = ATT_BLOCK
    nq = T // B
    n_pairs = D_ATT // 128
    n_g = len(gather)

    def body(*refs):
        (q_ref, k_ref, v_ref), g_ins, (o_ref,), g_outs, scratch = _split_refs(refs, 3, n_g, 1, n_g)
        kb, vb, acc, rsum, t0, zs, lt, cat, keep = scratch[:9]
        p, i = pl.program_id(0), pl.program_id(1)

        if n_g:
            @pl.when(jnp.logical_and(p == 0, i == 0))
            def _():
                _Gather(g_ins, g_outs, *scratch[9:]).start()

            @pl.when(jnp.logical_and(p == n_pairs - 1, i == 0))
            def _():
                _Gather(g_ins, g_outs, *scratch[9:]).forward()

        @pl.when(i == 0)
        def _():
            _fill_causal(keep)
            for h in range(2):
                kb[h] = k_ref[:, HEAD_DIM * h:HEAD_DIM * (h + 1)].astype(BF16)
                vb[h] = v_ref[:, HEAD_DIM * h:HEAD_DIM * (h + 1)].astype(BF16)

        tri2, _ = _triangles()
        hs = [slice(HEAD_DIM * h, HEAD_DIM * (h + 1)) for h in range(2)]
        q = [(q_ref[:, hs[h]] * ATT_SCALE).astype(BF16) for h in range(2)]
        acc[...] = jnp.zeros_like(acc)
        rsum[...] = jnp.zeros_like(rsum)

        def tile(j, masked):
            keys = pl.ds(pl.multiple_of(j * B, B), B)

            def scores(h):
                zs[h] = _dot_nt(q[h], kb[h, keys, :])

            def fails(h):
                for r in _chunks():
                    t, a = _fail_and_log_beta(zs[h, r, :])
                    if masked:
                        t = t * keep[r, :]
                    _store_split(cat.at[h], r, t)
                    zs[h, r, :] = a
                    t0[h, r, :] = t[:, 0:1]

            def later_sums(h):
                lt[h] = _dot(cat[h], tri2)

            def weights(h):
                for r in _chunks():
                    later = lt[h, r, :]
                    w = jnp.exp(zs[h, r, :] - later - rsum[h, r, :])
                    if masked:
                        w = w * keep[r, :]
                    _store_split(cat.at[h], r, w)
                    rsum[h, r, :] += later[:, 0:1] + t0[h, r, :]

            def values(h):
                acc[h] += _dot(cat[h], _twice(vb[h, keys, :]))

            _staggered([scores, fails, later_sums, weights, values])

        _sweep_keys(i, tile, rsum)
        for h in range(2):
            o_ref[:, hs[h]] = acc[h]

        if n_g:
            @pl.when(jnp.logical_and(p == n_pairs - 1, i == nq - 1))
            def _():
                _Gather(g_ins, g_outs, *scratch[9:]).wait()

    out = pl.pallas_call(
        body, name="attn_fwd", grid=(n_pairs, nq),
        in_specs=[pl.BlockSpec((B, 128), lambda p, i: (i, p)),
                  pl.BlockSpec((T, 128), lambda p, i: (0, 4 + p)),
                  pl.BlockSpec((T, 128), lambda p, i: (0, 8 + p))] + [ANY] * n_g,
        out_specs=[pl.BlockSpec((B, 128), lambda p, i: (i, p))] + [ANY] * n_g,
        out_shape=[jax.ShapeDtypeStruct((T, D_ATT), F32)] + _gathered_shapes(gather),
        scratch_shapes=[pltpu.VMEM((2, T, HEAD_DIM), BF16), pltpu.VMEM((2, T, HEAD_DIM), BF16),
                        pltpu.VMEM((2, B, HEAD_DIM), F32), pltpu.VMEM((2, B, 1), F32), pltpu.VMEM((2, B, 1), F32),
                        pltpu.VMEM((2, B, B), F32), pltpu.VMEM((2, B, B), F32), pltpu.VMEM((2, B, 2 * B), BF16),
                        pltpu.VMEM((B, B), F32)]
        + (_Gather.semaphores(n_g) if n_g else []),
        compiler_params=_params("arbitrary", "arbitrary"),
    )(u, u, u, *gather)
    return out[0], out[1:]


def _attn_bwd(u, att, datt, hosted=()):
    T = u.shape[0]
    B = ATT_BLOCK
    nq = T // B
    n_pairs = D_ATT // 128
    n_h = len(hosted)
    h_in, h_out, h_shape, h_sems = _host(hosted)

    def body(*refs):
        (q_ref, k_ref, v_ref, o_ref, do_ref), h_ins, (dqkv_ref,), h_outs, scratch = _split_refs(refs, 5, n_h, 1, n_h)
        n_scratch = 17
        (kb, vb, dkacc, dvacc, dqacc, rsum, gsum, total, t0, lt0, zs, gs, lt, cat, wb, dzb,
         keep) = scratch[:n_scratch]
        p, i = pl.program_id(0), pl.program_id(1)

        if n_h:
            @pl.when(jnp.logical_and(p == 0, i == 0))
            def _():
                _Exchange(h_ins, h_outs, *scratch[n_scratch:]).start()

        @pl.when(i == 0)
        def _():
            _fill_causal(keep)
            for h in range(2):
                kb[h] = k_ref[:, HEAD_DIM * h:HEAD_DIM * (h + 1)].astype(BF16)
                vb[h] = v_ref[:, HEAD_DIM * h:HEAD_DIM * (h + 1)].astype(BF16)
            dkacc[...] = jnp.zeros_like(dkacc)
            dvacc[...] = jnp.zeros_like(dvacc)

        tri2, tri_incl2 = _triangles()
        qrows = pl.ds(pl.multiple_of(i * B, B), B)
        hs = [slice(HEAD_DIM * h, HEAD_DIM * (h + 1)) for h in range(2)]
        q = [(q_ref[:, hs[h]] * ATT_SCALE).astype(BF16) for h in range(2)]
        dob = [do_ref[:, hs[h]].astype(BF16) for h in range(2)]
        for h in range(2):
            total[h] = jnp.sum(dob[h].astype(F32) * o_ref[:, hs[h]], axis=1, keepdims=True)
        dqacc[...] = jnp.zeros_like(dqacc)
        rsum[...] = jnp.zeros_like(rsum)
        gsum[...] = jnp.zeros_like(gsum)

        def tile(j, masked):
            keys = pl.ds(pl.multiple_of(j * B, B), B)

            def scores(h):
                zs[h] = _dot_nt(q[h], kb[h, keys, :])
                gs[h] = _dot_nt(dob[h], vb[h, keys, :])

            def fails(h):
                for r in _chunks():
                    t, a = _fail_and_log_beta(zs[h, r, :])
                    if masked:
                        t = t * keep[r, :]
                    _store_split(cat.at[h], r, t)
                    zs[h, r, :] = a
                    t0[h, r, :] = t[:, 0:1]

            def later_sums(h):
                lt[h] = _dot(cat[h], tri2)

            def weights(h):
                for r in _chunks():
                    later = lt[h, r, :]
                    w = jnp.exp(zs[h, r, :] - later - rsum[h, r, :])
                    if masked:
                        w = w * keep[r, :]
                    g = gs[h, r, :] * w
                    gs[h, r, :] = g
                    _store_split(cat.at[h], r, g)
                    wb[h, r, :] = w.astype(BF16)
                    lt0[h, r, :] = later[:, 0:1]

            def suffix_sums(h):
                lt[h] = _dot(cat[h], tri_incl2)

            def score_grads(h):
                for r in _chunks():
                    suffix = lt[h, r, :]
                    g = gs[h, r, :]
                    before = total[h, r, :] - (suffix + gsum[h, r, :])
                    dz = g - jnp.exp(zs[h, r, :]) * (g + before)
                    if masked:
                        dz = dz * keep[r, :]
                    dzb[h, r, :] = dz.astype(BF16)
                    rsum[h, r, :] += lt0[h, r, :] + t0[h, r, :]
                    gsum[h, r, :] += suffix[:, 0:1]

            def input_grads(h):
                dqacc[h] += _dot(dzb[h], kb[h, keys, :])
                dkacc[h, keys, :] += _dot_tn(dzb[h], q[h])
                dvacc[h, keys, :] += _dot_tn(wb[h], dob[h])

            _staggered([scores, fails, later_sums, weights, suffix_sums, score_grads, input_grads])

        _sweep_keys(i, tile, rsum)
        both = lambda acc2: jnp.concatenate([acc2[0], acc2[1]], axis=1)
        dqkv_ref[0, qrows, :] = (both(dqacc) * ATT_SCALE).astype(BF16)

        @pl.when(i == nq - 1)
        def _():
            dqkv_ref[1] = both(dkacc).astype(BF16)
            dqkv_ref[2] = both(dvacc).astype(BF16)

        if n_h:
            @pl.when(jnp.logical_and(p == n_pairs - 1, i == nq - 1))
            def _():
                _Exchange(h_ins, h_outs, *scratch[n_scratch:]).wait()

    out = pl.pallas_call(
        body, name="attn_bwd", grid=(n_pairs, nq),
        in_specs=[pl.BlockSpec((B, 128), lambda p, i: (i, p)),
                  pl.BlockSpec((T, 128), lambda p, i: (0, 4 + p)),
                  pl.BlockSpec((T, 128), lambda p, i: (0, 8 + p)),
                  pl.BlockSpec((B, 128), lambda p, i: (i, p)),
                  pl.BlockSpec((B, 128), lambda p, i: (i, p))] + h_in,
        out_specs=[pl.BlockSpec((3, T, 128), lambda p, i: (0, 0, p))] + h_out,
        out_shape=[jax.ShapeDtypeStruct((3, T, D_ATT), BF16)] + h_shape,
        scratch_shapes=[pltpu.VMEM((2, T, HEAD_DIM), BF16), pltpu.VMEM((2, T, HEAD_DIM), BF16),
                        pltpu.VMEM((2, T, HEAD_DIM), F32), pltpu.VMEM((2, T, HEAD_DIM), F32),
                        pltpu.VMEM((2, B, HEAD_DIM), F32)]
        + [pltpu.VMEM((2, B, 1), F32)] * 5
        + [pltpu.VMEM((2, B, B), F32)] * 3
        + [pltpu.VMEM((2, B, 2 * B), BF16), pltpu.VMEM((2, B, B), BF16), pltpu.VMEM((2, B, B), BF16)]
        + [pltpu.VMEM((B, B), F32)]
        + h_sems,
        compiler_params=_params("arbitrary", "arbitrary"),
    )(u, u, u, att, datt, *hosted)
    return out[0], out[1:]


MIX_TILE = 256
MIX_CHUNK = 16


def _u_specs(tm, tile_of):
    return [pl.BlockSpec((tm, SEG), functools.partial(lambda k, *g: (tile_of(*g), k), k)) for k in range(3, N_SEG)]


SHIFT_ROWS = 24


def _shifted_scratch(tm):
    return pltpu.VMEM((7, tm + SHIFT_ROWS, D_CONV), F32)


def _make_shifted(base, shifted, tm):
    for b in range(1, 8):
        shifted[b - 1] = base[b:b + tm + SHIFT_ROWS, :]


def _tap_window(base, shifted, start, rows):
    whole, b = divmod(start, 8)
    if b == 0:
        return base[start + rows.start:start + rows.stop, :]
    return shifted[b - 1, 8 * whole + rows.start:8 * whole + rows.stop, :]


def _conv(ext, ext_shifted, cw_ref, rows):
    acc = cw_ref[0:1, :] * _tap_window(ext, ext_shifted, 2, rows)
    for k in range(1, CONV_WIDTH):
        acc = acc + cw_ref[k:k + 1, :] * _tap_window(ext, ext_shifted, 2 + k, rows)
    return acc


def _norm_stats(v):
    mu = jnp.mean(v, axis=-1, keepdims=True)
    vc = v - mu
    rstd = lax.rsqrt(jnp.mean(vc * vc, axis=-1, keepdims=True) + LN_EPS)
    return vc * rstd, rstd


def _norm_bwd(dy_scaled, xhat, rstd):
    return rstd * (dy_scaled - jnp.mean(dy_scaled, axis=-1, keepdims=True)
                   - xhat * jnp.mean(dy_scaled * xhat, axis=-1, keepdims=True))


def _mix_fwd(x, u, att, wap, wcp, wout, cw, cvec, dvec, layer):
    T = x.shape[0]
    tm = MIX_TILE

    def body(x_ref, za_ref, ga_ref, gb_ref, zc_ref, ga0_ref, ga1_ref, gc0_ref, gc1_ref, att_ref,
             wap_ref, wcp_ref, wout_ref, cw_ref, cvec_ref, dvec_ref, xn_ref, xnb_ref, h_ref, ext, ext_shifted):
        i = pl.program_id(0)

        @pl.when(i == 0)
        def _():
            ext[0:CONV_PAD, :] = jnp.zeros((CONV_PAD, D_CONV), F32)

        za = za_ref[...]
        attg = att_ref[...] * (za * _sig(za))
        ab = _dot(attg.astype(BF16), wap_ref[...])

        ext[CONV_PAD:CONV_PAD + tm, :] = ga_ref[...] * _sig(gb_ref[...])
        _make_shifted(ext, ext_shifted, tm)
        c1 = _conv(ext, ext_shifted, cw_ref, slice(0, tm)) + cvec_ref[0:1, :]
        ext[0:CONV_PAD, :] = ext[tm:tm + CONV_PAD, :]
        xh, _ = _norm_stats(c1)
        c2 = xh * cvec_ref[1:2, :] + cvec_ref[2:3, :]
        zc = zc_ref[...]
        cz = (c2 * _sig(c2)) * (zc * _sig(zc))
        cb = _dot(cz.astype(BF16), wcp_ref[...]) + dvec_ref[0:1, :]

        g_att = jnp.concatenate([ga0_ref[...], ga1_ref[...]], axis=1)
        g_conv = jnp.concatenate([gc0_ref[...], gc1_ref[...]], axis=1)
        merged = _sig(g_att) * ab + _sig(g_conv) * cb
        h = DEEPNORM_ALPHA * x_ref[...] + _dot(merged.astype(BF16), wout_ref[...])
        h_ref[...] = h
        hh, _ = _norm_stats(h)
        xn = hh * dvec_ref[1:2, :] + dvec_ref[2:3, :]
        xn_ref[...] = xn
        xnb_ref[...] = xn.astype(BF16)

    row = pl.BlockSpec((tm, D_MODEL), lambda i: (i, 0))
    return pl.pallas_call(
        body, name="mix_fwd", grid=(T // tm,),
        in_specs=[row] + _u_specs(tm, lambda i: i) + [
            pl.BlockSpec((tm, D_ATT), lambda i: (i, 0)),
            _resident((D_ATT, D_MODEL), lambda i: (0, 0)),
            _resident((D_CONV, D_MODEL), lambda i: (0, 0)),
            _resident((D_MODEL, D_MODEL), lambda i: (0, 0)),
            _resident((CONV_PAD, D_CONV), lambda i: (0, 0)),
            pl.BlockSpec((None, 8, D_CONV), lambda i: (layer, 0, 0)),
            pl.BlockSpec((None, 8, D_MODEL), lambda i: (layer, 0, 0))],
        out_specs=[row, row, row],
        out_shape=[jax.ShapeDtypeStruct((T, D_MODEL), F32), jax.ShapeDtypeStruct((T, D_MODEL), BF16),
                   jax.ShapeDtypeStruct((T, D_MODEL), F32)],
        scratch_shapes=[pltpu.VMEM((tm + CONV_PAD, D_CONV), F32), _shifted_scratch(tm)],
        compiler_params=_params("arbitrary"),
    )(x, *([u] * 8), att, wap, wcp, wout, cw, cvec, dvec)


def _mix_bwd(dxo, h, u, att, wap, wcp, wout, cw, cvec, dvec, layer, hosted=()):
    T = dxo.shape[0]
    tm = MIX_TILE
    nt = T // tm
    halo_per_tile = tm // CONV_PAD
    n_h = len(hosted)
    h_in, h_out, h_shape, h_sems = _host(hosted)

    def body(*refs):
        ins, h_ins, outs, h_outs, scratch = _split_refs(refs, 19, n_h, 9, n_h)
        (dxo_ref, h_ref, za_ref, ga_ref, gb_ref, zc_ref, ga0_ref, ga1_ref, gc0_ref, gc1_ref, hga_ref, hgb_ref,
         att_ref, wap_ref, wcp_ref, wout_ref, cw_ref, cvec_ref, dvec_ref) = ins
        datt_ref, dur_ref, dres_ref, dwap_ref, dwcp_ref, dwout_ref, dcw_ref, dcvec_ref, ddvec_ref = outs
        n_scratch = 19
        (ext, dext, ext_shifted, dext_shifted, attg_s, cz_s, merged_s, dy_s, dab_s, dcb_s,
         xh_s, dattg_s, dcz_s, ab_s, cb_s, sga_s, sgc_s, dm_s, rstd_s) = scratch[:n_scratch]
        r = pl.program_id(0)
        i = nt - 1 - r

        if n_h:
            @pl.when(r == 0)
            def _():
                _Exchange(h_ins, h_outs, *scratch[n_scratch:]).start()

        @pl.when(r == 0)
        def _():
            for ref in (dwap_ref, dwcp_ref, dwout_ref, dcw_ref, dcvec_ref, ddvec_ref):
                ref[...] = jnp.zeros_like(ref)
            dext[tm:tm + CONV_PAD, :] = jnp.zeros((CONV_PAD, D_CONV), F32)

        chunks = [slice(c, c + MIX_CHUNK) for c in range(0, tm, MIX_CHUNK)]
        whole = slice(0, tm)
        ln_g_c, ln_b_c = cvec_ref[1:2, :], cvec_ref[2:3, :]

        halo = hga_ref[...] * _sig(hgb_ref[...])
        ext[0:CONV_PAD, :] = jnp.where(i > 0, halo, 0.0)
        for c in chunks:
            za = za_ref[c, :]
            attg_s[c, :] = (att_ref[c, :] * (za * _sig(za))).astype(BF16)
            ext[CONV_PAD + c.start:CONV_PAD + c.stop, :] = ga_ref[c, :] * _sig(gb_ref[c, :])
        _make_shifted(ext, ext_shifted, tm)
        for c in chunks:
            c1 = _conv(ext, ext_shifted, cw_ref, c) + cvec_ref[0:1, :]
            xh, rstd_c = _norm_stats(c1)
            xh_s[c, :] = xh
            rstd_s[c, :] = rstd_c
            c2 = xh * ln_g_c + ln_b_c
            zc = zc_ref[c, :]
            cz_s[c, :] = ((c2 * _sig(c2)) * (zc * _sig(zc))).astype(BF16)
        ab_s[...] = _dot(attg_s[...], wap_ref[...])
        cb_s[...] = _dot(cz_s[...], wcp_ref[...]) + dvec_ref[0:1, :]

        for c in chunks:
            s_ga = _sig(jnp.concatenate([ga0_ref[c, :], ga1_ref[c, :]], axis=1))
            s_gc = _sig(jnp.concatenate([gc0_ref[c, :], gc1_ref[c, :]], axis=1))
            sga_s[c, :] = s_ga
            sgc_s[c, :] = s_gc
            merged_s[c, :] = (s_ga * ab_s[c, :] + s_gc * cb_s[c, :]).astype(BF16)
            dxo_c = dxo_ref[c, :]
            hh, rstd_h = _norm_stats(h_ref[c, :])
            ddvec_ref[1:2, :] += jnp.sum(dxo_c * hh, axis=0, keepdims=True)
            ddvec_ref[2:3, :] += jnp.sum(dxo_c, axis=0, keepdims=True)
            dh = _norm_bwd(dxo_c * dvec_ref[1:2, :], hh, rstd_h)
            dres_ref[c, :] = DEEPNORM_ALPHA * dh
            dy_s[c, :] = dh.astype(BF16)
        dwout_ref[...] += _dot_tn(merged_s[...], dy_s[...])
        dm_s[...] = _dot_nt(dy_s[...], wout_ref[...])

        for c in chunks:
            dm, s_ga, s_gc = dm_s[c, :], sga_s[c, :], sgc_s[c, :]
            dab_s[c, :] = (dm * s_ga).astype(BF16)
            dcb = dm * s_gc
            dcb_s[c, :] = dcb.astype(BF16)
            ddvec_ref[0:1, :] += jnp.sum(dcb, axis=0, keepdims=True)
            dur_ref[c, 4 * SEG:6 * SEG] = (dm * ab_s[c, :] * (s_ga * (1.0 - s_ga))).astype(BF16)
            dur_ref[c, 6 * SEG:8 * SEG] = (dm * cb_s[c, :] * (s_gc * (1.0 - s_gc))).astype(BF16)
        dwap_ref[...] += _dot_tn(attg_s[...], dab_s[...])
        dattg_s[...] = _dot_nt(dab_s[...], wap_ref[...])
        dwcp_ref[...] += _dot_tn(cz_s[...], dcb_s[...])
        dcz_s[...] = _dot_nt(dcb_s[...], wcp_ref[...])

        for c in chunks:
            za = za_ref[c, :]
            s_za = _sig(za)
            dattg = dattg_s[c, :]
            datt_ref[c, :] = dattg * (za * s_za)
            dur_ref[c, 0:SEG] = (dattg * att_ref[c, :] * (s_za * (1.0 + za * (1.0 - s_za)))).astype(BF16)
            xh = xh_s[c, :]
            c2 = xh * ln_g_c + ln_b_c
            s_c2 = _sig(c2)
            zc = zc_ref[c, :]
            s_zc = _sig(zc)
            dcz = dcz_s[c, :]
            dur_ref[c, 3 * SEG:4 * SEG] = (dcz * (c2 * s_c2) * (s_zc * (1.0 + zc * (1.0 - s_zc)))).astype(BF16)
            dc2 = dcz * (zc * s_zc) * (s_c2 * (1.0 + c2 * (1.0 - s_c2)))
            dcvec_ref[1:2, :] += jnp.sum(dc2 * xh, axis=0, keepdims=True)
            dcvec_ref[2:3, :] += jnp.sum(dc2, axis=0, keepdims=True)
            dc1 = _norm_bwd(dc2 * ln_g_c, xh, rstd_s[c, :])
            dcvec_ref[0:1, :] += jnp.sum(dc1, axis=0, keepdims=True)
            dext[c, :] = dc1

        dc1_tile = dext[0:tm, :]
        for k in range(CONV_WIDTH):
            dcw_ref[k:k + 1, :] += jnp.sum(dc1_tile * _tap_window(ext, ext_shifted, 2 + k, whole),
                                           axis=0, keepdims=True)
        _make_shifted(dext, dext_shifted, tm)
        for c in chunks:
            dc0 = cw_ref[0:1, :] * _tap_window(dext, dext_shifted, CONV_WIDTH - 1, c)
            for k in range(1, CONV_WIDTH):
                dc0 = dc0 + cw_ref[k:k + 1, :] * _tap_window(dext, dext_shifted, CONV_WIDTH - 1 - k, c)
            s_gb = _sig(gb_ref[c, :])
            dur_ref[c, SEG:2 * SEG] = (dc0 * s_gb).astype(BF16)
            dur_ref[c, 2 * SEG:3 * SEG] = (dc0 * ga_ref[c, :] * (s_gb * (1.0 - s_gb))).astype(BF16)
        dext[tm:tm + CONV_PAD, :] = dext[0:CONV_PAD, :]

        if n_h:
            @pl.when(r == nt - 1)
            def _():
                _Exchange(h_ins, h_outs, *scratch[n_scratch:]).wait()

    rev = lambda r: nt - 1 - r
    row = pl.BlockSpec((tm, D_MODEL), lambda r: (rev(r), 0))
    halo_spec = [pl.BlockSpec((CONV_PAD, SEG),
                              functools.partial(lambda k, r: (jnp.maximum(rev(r) * halo_per_tile - 1, 0), k), k))
                 for k in (4, 5)]
    const = lambda shape: _resident(shape, lambda r: (0,) * len(shape))
    out = pl.pallas_call(
        body, name="mix_bwd", grid=(nt,),
        in_specs=[row, row] + _u_specs(tm, rev) + halo_spec + [
            pl.BlockSpec((tm, D_ATT), lambda r: (rev(r), 0)),
            _resident((D_ATT, D_MODEL), lambda r: (0, 0)),
            _resident((D_CONV, D_MODEL), lambda r: (0, 0)),
            _resident((D_MODEL, D_MODEL), lambda r: (0, 0)),
            _resident((CONV_PAD, D_CONV), lambda r: (0, 0)),
            pl.BlockSpec((None, 8, D_CONV), lambda r: (layer, 0, 0)),
            pl.BlockSpec((None, 8, D_MODEL), lambda r: (layer, 0, 0))] + h_in,
        out_specs=[pl.BlockSpec((tm, D_ATT), lambda r: (rev(r), 0)),
                   pl.BlockSpec((tm, 8 * SEG), lambda r: (rev(r), 0)),
                   row,
                   const((D_ATT, D_MODEL)), const((D_CONV, D_MODEL)), const((D_MODEL, D_MODEL)),
                   const((CONV_PAD, D_CONV)), const((8, D_CONV)), const((8, D_MODEL))] + h_out,
        out_shape=[jax.ShapeDtypeStruct((T, D_ATT), F32), jax.ShapeDtypeStruct((T, 8 * SEG), BF16),
                   jax.ShapeDtypeStruct((T, D_MODEL), F32),
                   jax.ShapeDtypeStruct((D_ATT, D_MODEL), F32), jax.ShapeDtypeStruct((D_CONV, D_MODEL), F32),
                   jax.ShapeDtypeStruct((D_MODEL, D_MODEL), F32), jax.ShapeDtypeStruct((CONV_PAD, D_CONV), F32),
                   jax.ShapeDtypeStruct((8, D_CONV), F32), jax.ShapeDtypeStruct((8, D_MODEL), F32)] + h_shape,
        scratch_shapes=[pltpu.VMEM((tm + CONV_PAD, D_CONV), F32), pltpu.VMEM((tm + CONV_PAD, D_CONV), F32),
                        _shifted_scratch(tm), _shifted_scratch(tm),
                        pltpu.VMEM((tm, D_ATT), BF16), pltpu.VMEM((tm, D_CONV), BF16)]
        + [pltpu.VMEM((tm, D_MODEL), BF16)] * 4
        + [pltpu.VMEM((tm, D_CONV), F32)] * 3
        + [pltpu.VMEM((tm, D_MODEL), F32)] * 5
        + [pltpu.VMEM((tm, 1), F32)]
        + h_sems,
        compiler_params=_params("arbitrary"),
    )(dxo, h, *([u] * 10), att, wap, wcp, wout, cw, cvec, dvec, *hosted)
    return out[:9], out[9:]


def _loss_head(y, target):
    T = y.shape[0]
    tm = min(T, 512)

    def body(y_ref, t_ref, dy_ref, loss_ref):
        @pl.when(pl.program_id(0) == 0)
        def _():
            loss_ref[...] = jnp.zeros_like(loss_ref)

        err = y_ref[...] - t_ref[...]
        dy_ref[...] = err * (1.0 / D_MODEL)
        loss_ref[...] += 0.5 * jnp.sum(jnp.mean(err * err, axis=-1, keepdims=True))

    row = pl.BlockSpec((tm, D_MODEL), lambda i: (i, 0))
    return pl.pallas_call(
        body, name="loss_head", grid=(T // tm,), in_specs=[row, row],
        out_specs=[row, pl.BlockSpec((8, 128), lambda i: (0, 0))],
        out_shape=[jax.ShapeDtypeStruct((T, D_MODEL), F32), jax.ShapeDtypeStruct((8, 128), F32)],
        compiler_params=_params("arbitrary"),
    )(y, target)


def _adamw(w, g, m, v):
    m = ADAM_B1 * m + (1.0 - ADAM_B1) * g
    v = ADAM_B2 * v + (1.0 - ADAM_B2) * (g * g)
    m_hat = m / (1.0 - ADAM_B1 ** ADAM_STEP)
    v_hat = v / (1.0 - ADAM_B2 ** ADAM_STEP)
    return -ADAM_LR * (m_hat / (jnp.sqrt(v_hat) + ADAM_EPS) + ADAM_WD * w), m, v


def _sum_slots(ref):
    total = ref[0].astype(F32)
    for j in range(1, N_DEV):
        total = total + ref[j].astype(F32)
    return total


def _adamw_sharded(recv, w, m, v, prev, layer, tr, name):
    n_layers, R, C = w.shape

    def body(*refs):
        recv_ref, w_ref, m_ref, v_ref = refs[:4]
        g_ref, d_ref, nm_ref, nv_ref = refs[-4:]
        g = _sum_slots(recv_ref)
        g_ref[...] = g
        d_ref[...], nm_ref[...], nv_ref[...] = _adamw(w_ref[...], g, m_ref[...], v_ref[...])

    slab = pl.BlockSpec((None, tr, C), lambda i: (layer, i, 0))
    n_prev = 0 if prev is None else 4
    return pl.pallas_call(
        body, name=name, grid=(R // tr,),
        in_specs=[pl.BlockSpec((N_DEV, tr, C), lambda i: (0, i, 0)), slab, slab, slab] + [ANY] * n_prev,
        out_specs=[slab] * 4,
        out_shape=[jax.ShapeDtypeStruct(w.shape, F32)] * 4,
        input_output_aliases={4 + k: k for k in range(n_prev)},
        compiler_params=_params("parallel"),
    )(recv, w, m, v, *(prev or ()))


def _sum8(g):
    R = g.shape[1]

    def body(g_ref, o_ref):
        o_ref[...] = _sum_slots(g_ref)

    return pl.pallas_call(body, name="sum_small_grads", out_shape=jax.ShapeDtypeStruct((R, 128), F32))(g)


def _adamw_small(w, g, m, v):
    def body(w_ref, g_ref, m_ref, v_ref, d_ref, nm_ref, nv_ref):
        d_ref[...], nm_ref[...], nv_ref[...] = _adamw(w_ref[...], g_ref[...], m_ref[...], v_ref[...])

    return pl.pallas_call(body, name="adamw_small", out_shape=[jax.ShapeDtypeStruct(w.shape, F32)] * 3)(w, g, m, v)


REPLICATED = (("b_in", D_IN), ("conv_b", D_CONV), ("conv_ln_g", D_CONV), ("conv_ln_b", D_CONV),
              ("b_conv_proj", D_MODEL), ("ln_g", D_MODEL), ("ln_b", D_MODEL))
N_REPLICATED = sum(n for _, n in REPLICATED)
CONV_W_SHARD_PAD = 2048


def _pack_small(rep, conv_w_shard):
    flat = jnp.concatenate([rep[k] for k, _ in REPLICATED], axis=1)
    cws = jnp.pad(conv_w_shard.reshape(DEPTH, -1), ((0, 0), (0, CONV_W_SHARD_PAD - CONV_WIDTH * 64)))
    return jnp.concatenate([flat, cws], axis=1).reshape(-1, 128)


def _unpack_small(packed):
    per = packed.reshape(DEPTH, N_REPLICATED + CONV_W_SHARD_PAD)
    out, off = {}, 0
    for k, n in REPLICATED:
        out[k] = per[:, off:off + n]
        off += n
    out["conv_w"] = per[:, off:off + CONV_WIDTH * 64].reshape(DEPTH, CONV_WIDTH, 64)
    return out


def kernel(x, w_in, b_in, conv_w, conv_b, conv_ln_g, conv_ln_b, w_att_proj, w_conv_proj, b_conv_proj, w_out, ln_g, ln_b, loss_target, m_w_in, m_b_in, m_conv_w, m_conv_b, m_conv_ln_g, m_conv_ln_b, m_w_att_proj, m_w_conv_proj, m_b_conv_proj, m_w_out, m_ln_g, m_ln_b, v_w_in, v_b_in, v_conv_w, v_conv_b, v_conv_ln_g, v_conv_ln_b, v_w_att_proj, v_w_conv_proj, v_b_conv_proj, v_w_out, v_ln_g, v_ln_b):
    x0 = x[0]
    target = loss_target[0]

    def shards(l):
        return [w_in[l].astype(BF16), w_att_proj[l].astype(BF16), w_conv_proj[l].astype(BF16),
                w_out[l].astype(BF16), conv_w[l]]

    def full_weights(g_in, g_ap, g_cp, g_out, g_cw):
        return (g_in.transpose(1, 0, 2).reshape(D_MODEL, D_IN), g_ap.transpose(1, 0, 2).reshape(D_ATT, D_MODEL),
                g_cp.transpose(1, 0, 2).reshape(D_CONV, D_MODEL), g_out.reshape(D_MODEL, D_MODEL),
                jnp.pad(g_cw.transpose(1, 0, 2).reshape(CONV_WIDTH, D_CONV), ((0, 1), (0, 0))))

    pad8 = lambda rows: jnp.pad(jnp.stack(rows, axis=1), ((0, 0), (0, 8 - len(rows)), (0, 0)))
    cvec = pad8([conv_b, conv_ln_g, conv_ln_b])
    dvec = pad8([b_conv_proj, ln_g, ln_b])
    b_in3 = b_in.reshape(DEPTH, 1, D_IN)

    weights = [full_weights(*_all_gather(shards(0), "gather_weights"))]
    xs, xbs, us, atts, hs = [x0], [x0.astype(BF16)], [], [], []
    for l in range(DEPTH):
        w_in_l, wap, wcp, wout, cw = weights[l]
        u = _in_proj(xbs[l], w_in_l, b_in3, l)
        att, gathered = _attn_fwd(u, gather=shards(l + 1) if l + 1 < DEPTH else ())
        if gathered:
            weights.append(full_weights(*gathered))
        xn, xnb, h = _mix_fwd(xs[l], u, att, wap, wcp, wout, cw, cvec, dvec, l)
        us.append(u), atts.append(att), hs.append(h), xs.append(xn), xbs.append(xnb)
    dxo, loss_part = _loss_head(xs[DEPTH], target)
    loss = lax.psum(loss_part[0, 0], ("x", "y", "c"))

    big = {"w_in": (w_in, m_w_in, v_w_in, 256), "w_att_proj": (w_att_proj, m_w_att_proj, v_w_att_proj, D_ATT),
           "w_conv_proj": (w_conv_proj, m_w_conv_proj, v_w_conv_proj, D_CONV), "w_out": (w_out, m_w_out, v_w_out, 128)}
    results = {k: None for k in big}
    small = [None] * DEPTH

    def update(k, recv, l):
        w, m, v, tr = big[k]
        results[k] = _adamw_sharded(recv, w, m, v, results[k], l, tr, "adamw_" + k)

    pending_w_in = ()
    for l in reversed(range(DEPTH)):
        w_in_l, wap, wcp, wout, cw = weights[l]
        (datt, du_rest, dres, dwap, dwcp, dwout, dcw, dcvec, ddvec), got = _mix_bwd(
            dxo, hs[l], us[l], atts[l], wap, wcp, wout, cw, cvec, dvec, l, hosted=pending_w_in)
        if got:
            update("w_in", got[0], l + 1)
        proj = [dwap.reshape(D_ATT, N_DEV, D_MODEL // N_DEV).transpose(1, 0, 2).astype(BF16),
                dwcp.reshape(D_CONV, N_DEV, D_MODEL // N_DEV).transpose(1, 0, 2).astype(BF16),
                dwout.reshape(N_DEV, D_MODEL // N_DEV, D_MODEL).astype(BF16)]
        dqkv, got = _attn_bwd(us[l], atts[l], datt, hosted=proj)
        for k, r in zip(("w_att_proj", "w_conv_proj", "w_out"), got):
            update(k, r, l)
        dwin, dbin = _in_proj_dw(xbs[l], dqkv, du_rest)
        small[l] = jnp.concatenate([dbin.reshape(-1), dcvec[:3].reshape(-1), ddvec[:3].reshape(-1), dcw.reshape(-1)])
        pending_w_in = (dwin.reshape(D_MODEL, N_DEV, W_IN_SHARD).transpose(1, 0, 2).astype(BF16),)
        dxo, got = _in_proj_dx(dqkv, du_rest, w_in_l, dres, hosted=pending_w_in if l == 0 else ())
        if got:
            update("w_in", got[0], l)
    grad_x = dxo[None]

    small_all = jnp.stack(small).reshape(-1, 128)
    (gathered,) = _all_gather([small_all], "gather_small_grads")
    per_layer = _sum8(gathered).reshape(DEPTH, -1)
    g_rep, off = {}, 0
    for k, n in (("b_in", D_IN), ("conv_b", D_CONV), ("conv_ln_g", D_CONV), ("conv_ln_b", D_CONV),
                 ("b_conv_proj", D_MODEL), ("ln_g", D_MODEL), ("ln_b", D_MODEL)):
        g_rep[k] = per_layer[:, off:off + n]
        off += n
    g_cw_full = per_layer[:, off:].reshape(DEPTH, CONV_PAD, D_CONV)[:, :CONV_WIDTH]
    g_cw_shard = lax.dynamic_slice_in_dim(g_cw_full, _my_index() * 64, 64, axis=2)
    rep_w = dict(b_in=b_in, conv_b=conv_b, conv_ln_g=conv_ln_g, conv_ln_b=conv_ln_b, b_conv_proj=b_conv_proj,
                 ln_g=ln_g, ln_b=ln_b)
    rep_m = dict(b_in=m_b_in, conv_b=m_conv_b, conv_ln_g=m_conv_ln_g, conv_ln_b=m_conv_ln_b,
                 b_conv_proj=m_b_conv_proj, ln_g=m_ln_g, ln_b=m_ln_b)
    rep_v = dict(b_in=v_b_in, conv_b=v_conv_b, conv_ln_g=v_conv_ln_g, conv_ln_b=v_conv_ln_b,
                 b_conv_proj=v_b_conv_proj, ln_g=v_ln_g, ln_b=v_ln_b)
    sd, sm, sv = _adamw_small(_pack_small(rep_w, conv_w), _pack_small(g_rep, g_cw_shard),
                              _pack_small(rep_m, m_conv_w), _pack_small(rep_v, v_conv_w))
    small_out = [dict(g_rep, conv_w=g_cw_shard), _unpack_small(sd), _unpack_small(sm), _unpack_small(sv)]

    order = ["w_in", "b_in", "conv_w", "conv_b", "conv_ln_g", "conv_ln_b", "w_att_proj", "w_conv_proj",
             "b_conv_proj", "w_out", "ln_g", "ln_b"]
    outs = [loss, grad_x]
    for kind in range(4):
        outs += [results[k][kind] if k in big else small_out[kind][k] for k in order]
    return tuple(outs)
```

```python
import functools

import jax
import jax.numpy as jnp
from jax import lax
from jax.experimental import pallas as pl
from jax.experimental.pallas import tpu as pltpu

F32, BF16 = jnp.float32, jnp.bfloat16

D_MODEL = 1024
D_ATT = 512
D_CONV = 512
HEAD_DIM = 64
CONV_WIDTH = 31
CONV_PAD = 32
DEPTH = 4
N_DEV = 8
SEG = 512
D_IN = 4 * D_ATT + 3 * D_CONV + 2 * D_MODEL
N_SEG = D_IN // SEG
N_QKV_SEG = 3
W_IN_SHARD = D_IN // N_DEV
LN_EPS = 1e-5
DEEPNORM_ALPHA = (2 * DEPTH) ** 0.25
ATT_SCALE = HEAD_DIM ** -0.5
ATT_BLOCK = 256
ADAM_LR, ADAM_B1, ADAM_B2, ADAM_EPS, ADAM_WD, ADAM_STEP = 0.001, 0.9, 0.999, 1e-08, 0.01, 10
VMEM_LIMIT = 56 * 1024 * 1024
MESH = pl.DeviceIdType.MESH
ANY = pl.BlockSpec(memory_space=pl.ANY)


def _resident(shape, index_map):
    return pl.BlockSpec(shape, index_map, pipeline_mode=pl.Buffered(1))


def _params(*semantics):
    return pltpu.CompilerParams(dimension_semantics=semantics, vmem_limit_bytes=VMEM_LIMIT)


def _sig(x):
    return 1.0 / (1.0 + jnp.exp(-x))


def _dot(a, b):
    return jnp.dot(a, b, preferred_element_type=F32)


def _dot_nt(a, b):
    return lax.dot_general(a, b, (((1,), (1,)), ((), ())), preferred_element_type=F32)


def _dot_tn(a, b):
    return lax.dot_general(a, b, (((0,), (0,)), ((), ())), preferred_element_type=F32)


def _my_index():
    return 4 * lax.axis_index("x") + 2 * lax.axis_index("y") + lax.axis_index("c")


class _Gather:
    def __init__(self, ins, outs, send_sems, recv_sems, local_sems):
        self.n = n = len(ins)
        x, y, c = lax.axis_index("x"), lax.axis_index("y"), lax.axis_index("c")
        me, sibling = (x, y, c), (x, y, 1 - c)
        chips = [(1 - x, y), (x, 1 - y), (1 - x, 1 - y)]

        def slot(a, dev):
            return outs[a].at[4 * dev[0] + 2 * dev[1] + dev[2]]

        def copy(a, k, block, to, src=None):
            return pltpu.make_async_remote_copy(
                src_ref=slot(a, block) if src is None else src, dst_ref=slot(a, block),
                send_sem=send_sems.at[7 * a + k], recv_sem=recv_sems.at[7 * a + k], device_id=to, device_id_type=MESH)

        self.mine = [pltpu.make_async_copy(ins[a], slot(a, me), local_sems.at[a]) for a in range(n)]
        self.first, self.landed, self.passed, self.last = [], [], [], []
        for a in range(n):
            self.first.append(copy(a, 0, me, sibling, src=ins[a]))
            self.first += [copy(a, 1 + j, me, (*chip, c), src=ins[a]) for j, chip in enumerate(chips)]
        for j, chip in enumerate(chips):
            for a in range(n):
                self.landed.append(copy(a, 1 + j, (*chip, c), me))
                self.passed.append(copy(a, 4 + j, (*chip, c), sibling))
        for a in range(n):
            self.last.append(copy(a, 0, sibling, me))
            self.last += [copy(a, 4 + j, (*chip, 1 - c), me) for j, chip in enumerate(chips)]

    @staticmethod
    def semaphores(n):
        return [pltpu.SemaphoreType.DMA((7 * n,)), pltpu.SemaphoreType.DMA((7 * n,)), pltpu.SemaphoreType.DMA((n,))]

    def start(self):
        for cp in self.mine + self.first:
            cp.start()

    def forward(self):
        for landed, passed in zip(self.landed, self.passed):
            landed.wait_recv()
            passed.start()

    def wait(self):
        for cp in self.last:
            cp.wait_recv()
        for cp in self.first + self.passed:
            cp.wait_send()
        for cp in self.mine:
            cp.wait()


def _gathered_shapes(shards):
    return [jax.ShapeDtypeStruct((N_DEV,) + s.shape, s.dtype) for s in shards]


def _all_gather(shards, name):
    n = len(shards)

    def body(*refs):
        g = _Gather(refs[:n], refs[n:2 * n], *refs[2 * n:])
        g.start()
        g.forward()
        g.wait()

    return pl.pallas_call(
        body, name=name, out_shape=_gathered_shapes(shards),
        in_specs=[ANY] * n, out_specs=[ANY] * n, scratch_shapes=_Gather.semaphores(n),
    )(*shards)


class _Exchange:
    def __init__(self, ins, outs, send_sems, recv_sems, local_sems):
        n = len(ins)
        x, y, c = lax.axis_index("x"), lax.axis_index("y"), lax.axis_index("c")
        me_idx = 4 * x + 2 * y + c
        self.mine = [pltpu.make_async_copy(ins[a].at[me_idx], outs[a].at[me_idx], local_sems.at[a])
                     for a in range(n)]
        self.sends, self.recvs = [], []
        for d in (1, 2, 4, 3, 5, 6, 7):
            px = 1 - x if d & 4 else x
            py = 1 - y if d & 2 else y
            pc = 1 - c if d & 1 else c
            idx = 4 * px + 2 * py + pc
            for a in range(n):
                sems = dict(send_sem=send_sems.at[7 * a + d - 1], recv_sem=recv_sems.at[7 * a + d - 1],
                            device_id=(px, py, pc), device_id_type=MESH)
                self.sends.append(pltpu.make_async_remote_copy(
                    src_ref=ins[a].at[idx], dst_ref=outs[a].at[me_idx], **sems))
                self.recvs.append(pltpu.make_async_remote_copy(
                    src_ref=ins[a].at[idx], dst_ref=outs[a].at[idx], **sems))

    @staticmethod
    def semaphores(n):
        return [pltpu.SemaphoreType.DMA((7 * n,)), pltpu.SemaphoreType.DMA((7 * n,)), pltpu.SemaphoreType.DMA((n,))]

    def start(self):
        for cp in self.mine + self.sends:
            cp.start()

    def wait(self):
        for cp in self.recvs:
            cp.wait_recv()
        for cp in self.sends:
            cp.wait_send()
        for cp in self.mine:
            cp.wait()


def _host(hosted):
    n = len(hosted)
    return ([ANY] * n, [ANY] * n, [jax.ShapeDtypeStruct(p.shape, p.dtype) for p in hosted],
            _Exchange.semaphores(n) if n else [])


def _split_refs(refs, *counts):
    out, at = [], 0
    for k in counts:
        out.append(refs[at:at + k])
        at += k
    return out + [refs[at:]]


def _in_proj(xb, w_in, b_in, layer):
    T = xb.shape[0]
    tm, tn = min(T, 2048), SEG

    def body(x_ref, w_ref, b_ref, u_ref):
        u_ref[...] = _dot(x_ref[...], w_ref[...]) + b_ref[...]

    return pl.pallas_call(
        body, name="in_proj", grid=(T // tm, D_IN // tn),
        in_specs=[pl.BlockSpec((tm, D_MODEL), lambda m, n: (m, 0)),
                  pl.BlockSpec((D_MODEL, tn), lambda m, n: (0, n)),
                  pl.BlockSpec((None, 1, tn), lambda m, n: (layer, 0, n))],
        out_specs=pl.BlockSpec((tm, tn), lambda m, n: (m, n)),
        out_shape=jax.ShapeDtypeStruct((T, D_IN), F32),
        compiler_params=_params("parallel", "parallel"),
    )(xb, w_in, b_in)


def _du_specs(tm, row_of, seg_of):
    return [pl.BlockSpec((None, tm, SEG), lambda *g: (jnp.minimum(seg_of(*g), N_QKV_SEG - 1), row_of(*g), 0)),
            pl.BlockSpec((tm, SEG), lambda *g: (row_of(*g), jnp.maximum(seg_of(*g) - N_QKV_SEG, 0)))]


def _in_proj_dx(dqkv, du_rest, w_in, dres, hosted=()):
    T = du_rest.shape[0]
    tm = min(T, 2048)
    nm = T // tm
    n_h = len(hosted)
    h_in, h_out, h_shape, h_sems = _host(hosted)

    def body(*refs):
        (a1_ref, a2_ref, w_ref, r_ref), h_ins, (o_ref,), h_outs, sems = _split_refs(refs, 4, n_h, 1, n_h)
        m, s = pl.program_id(0), pl.program_id(1)

        if n_h:
            @pl.when(jnp.logical_and(m == 0, s == 0))
            def _():
                _Exchange(h_ins, h_outs, *sems).start()

        @pl.when(s == 0)
        def _():
            o_ref[...] = r_ref[...]

        @pl.when(s < N_QKV_SEG)
        def _():
            o_ref[...] += _dot_nt(a1_ref[...], w_ref[...])

        @pl.when(s >= N_QKV_SEG)
        def _():
            o_ref[...] += _dot_nt(a2_ref[...], w_ref[...])

        if n_h:
            @pl.when(jnp.logical_and(m == nm - 1, s == N_SEG - 1))
            def _():
                _Exchange(h_ins, h_outs, *sems).wait()

    out = pl.pallas_call(
        body, name="in_proj_dx", grid=(nm, N_SEG),
        in_specs=_du_specs(tm, lambda m, s: m, lambda m, s: s) + [
            pl.BlockSpec((D_MODEL, SEG), lambda m, s: (0, s)),
            pl.BlockSpec((tm, D_MODEL), lambda m, s: (m, 0))] + h_in,
        out_specs=[pl.BlockSpec((tm, D_MODEL), lambda m, s: (m, 0))] + h_out,
        out_shape=[jax.ShapeDtypeStruct((T, D_MODEL), F32)] + h_shape,
        scratch_shapes=h_sems,
        compiler_params=_params("arbitrary", "arbitrary"),
    )(dqkv, du_rest, w_in, dres, *hosted)
    return out[0], out[1:]


def _in_proj_dw(xb, dqkv, du_rest):
    T = xb.shape[0]
    tk = min(T, 1024)
    nt = T // tk

    def body(x_ref, a1_ref, a2_ref, dw_ref, db_ref, acc_ref):
        s, t = pl.program_id(0), pl.program_id(1)

        @pl.when(t == 0)
        def _():
            acc_ref[...] = jnp.zeros_like(acc_ref)
            db_ref[...] = jnp.zeros_like(db_ref)

        def acc(a):
            acc_ref[...] += _dot_tn(x_ref[...], a)
            db_ref[...] += jnp.sum(a.astype(F32), axis=0, keepdims=True)

        @pl.when(s < N_QKV_SEG)
        def _():
            acc(a1_ref[...])

        @pl.when(s >= N_QKV_SEG)
        def _():
            acc(a2_ref[...])

        @pl.when(t == nt - 1)
        def _():
            dw_ref[...] = acc_ref[...].astype(BF16)

    return pl.pallas_call(
        body, name="in_proj_dw", grid=(N_SEG, nt),
        in_specs=[pl.BlockSpec((tk, D_MODEL), lambda s, t: (t, 0))] + _du_specs(tk, lambda s, t: t, lambda s, t: s),
        out_specs=[pl.BlockSpec((D_MODEL, SEG), lambda s, t: (0, s)),
                   pl.BlockSpec((None, 1, SEG), lambda s, t: (s, 0, 0))],
        out_shape=[jax.ShapeDtypeStruct((D_MODEL, D_IN), BF16), jax.ShapeDtypeStruct((N_SEG, 1, SEG), F32)],
        scratch_shapes=[pltpu.VMEM((D_MODEL, SEG), F32)],
        compiler_params=_params("parallel", "arbitrary"),
    )(xb, dqkv, du_rest)


ATT_CHUNK = 32
ATT_Q_PER_STEP = 2


def _chunks():
    return [slice(r, r + ATT_CHUNK) for r in range(0, ATT_BLOCK, ATT_CHUNK)]


def _twice(rhs):
    return jnp.concatenate([rhs, rhs], axis=0)


def _fail_and_log_beta(z):
    t = jnp.maximum(z, 0.0) + jnp.log(1.0 + jnp.exp(-jnp.abs(z)))
    return t, z - t


def _store_split(dst, rows, val):
    hi = val.astype(BF16)
    dst[rows, 0:ATT_BLOCK] = hi
    dst[rows, ATT_BLOCK:2 * ATT_BLOCK] = (val - hi.astype(F32)).astype(BF16)


def _fill_causal(keep):
    row = lax.broadcasted_iota(jnp.int32, (ATT_BLOCK, ATT_BLOCK), 0)
    col = lax.broadcasted_iota(jnp.int32, (ATT_BLOCK, ATT_BLOCK), 1)
    keep[...] = (col < row).astype(F32)


def _triangles():
    row = lax.broadcasted_iota(jnp.int32, (ATT_BLOCK, ATT_BLOCK), 0)
    col = lax.broadcasted_iota(jnp.int32, (ATT_BLOCK, ATT_BLOCK), 1)
    return _twice((row > col).astype(BF16)), _twice((row >= col).astype(BF16))


def _staggered(stages):
    stages[0](0)
    stages[0](1)
    for k in range(1, len(stages), 2):
        for h in range(2):
            stages[k](h)
            stages[k + 1](h)


EXP_UNDERFLOW = 104.0


def _sweep_keys(i, tile, rsum):
    tile(i, True)

    def live(carry):
        j, low = carry
        return jnp.logical_and(j >= 0, low < EXP_UNDERFLOW)

    def step(carry):
        j, _ = carry
        tile(j, False)
        return j - 1, jnp.min(rsum[...])

    lax.while_loop(live, step, (i - 1, jnp.min(rsum[...])))


def _attn_fwd(u, gather=()):
    T = u.shape[0]
    B = ATT_BLOCK
    per_step = ATT_Q_PER_STEP
    ns = T // (B * per_step)
    n_pairs = D_ATT // 128
    n_g = len(gather)

    def body(*refs):
        (q_ref, k_ref, v_ref), g_ins, (o_ref,), g_outs, scratch = _split_refs(refs, 3, n_g, 1, n_g)
        kb, vb, acc, rsum, t0, zs, lt, cat, keep = scratch[:9]
        p, s = pl.program_id(0), pl.program_id(1)

        if n_g:
            @pl.when(jnp.logical_and(p == 0, s == 0))
            def _():
                _Gather(g_ins, g_outs, *scratch[9:]).start()

            @pl.when(jnp.logical_and(p == n_pairs - 1, s == 0))
            def _():
                _Gather(g_ins, g_outs, *scratch[9:]).forward()

        @pl.when(s == 0)
        def _():
            _fill_causal(keep)
            for h in range(2):
                kb[h] = k_ref[:, HEAD_DIM * h:HEAD_DIM * (h + 1)].astype(BF16)
                vb[h] = v_ref[:, HEAD_DIM * h:HEAD_DIM * (h + 1)].astype(BF16)

        tri2, _ = _triangles()
        hs = [slice(HEAD_DIM * h, HEAD_DIM * (h + 1)) for h in range(2)]

        def query_block(sb, carry):
            i = s * per_step + sb
            qrows = pl.ds(pl.multiple_of(sb * B, B), B)
            q = [(q_ref[qrows, hs[h]] * ATT_SCALE).astype(BF16) for h in range(2)]
            acc[...] = jnp.zeros_like(acc)
            rsum[...] = jnp.zeros_like(rsum)
            _sweep_keys(i, functools.partial(tile, q), rsum)
            for h in range(2):
                o_ref[qrows, hs[h]] = acc[h]
            return carry

        def tile(q, j, masked):
            keys = pl.ds(pl.multiple_of(j * B, B), B)

            def scores(h):
                zs[h] = _dot_nt(q[h], kb[h, keys, :])

            def fails(h):
                for r in _chunks():
                    t, a = _fail_and_log_beta(zs[h, r, :])
                    if masked:
                        t = t * keep[r, :]
                    _store_split(cat.at[h], r, t)
                    zs[h, r, :] = a
                    t0[h, r, :] = t[:, 0:1]

            def later_sums(h):
                lt[h] = _dot(cat[h], tri2)

            def weights(h):
                for r in _chunks():
                    later = lt[h, r, :]
                    w = jnp.exp(zs[h, r, :] - later - rsum[h, r, :])
                    if masked:
                        w = w * keep[r, :]
                    _store_split(cat.at[h], r, w)
                    rsum[h, r, :] += later[:, 0:1] + t0[h, r, :]

            def values(h):
                acc[h] += _dot(cat[h], _twice(vb[h, keys, :]))

            _staggered([scores, fails, later_sums, weights, values])

        lax.fori_loop(0, per_step, query_block, 0)

        if n_g:
            @pl.when(jnp.logical_and(p == n_pairs - 1, s == ns - 1))
            def _():
                _Gather(g_ins, g_outs, *scratch[9:]).wait()

    out = pl.pallas_call(
        body, name="attn_fwd", grid=(n_pairs, ns),
        in_specs=[pl.BlockSpec((per_step * B, 128), lambda p, s: (s, p)),
                  pl.BlockSpec((T, 128), lambda p, s: (0, 4 + p)),
                  pl.BlockSpec((T, 128), lambda p, s: (0, 8 + p))] + [ANY] * n_g,
        out_specs=[pl.BlockSpec((per_step * B, 128), lambda p, s: (s, p))] + [ANY] * n_g,
        out_shape=[jax.ShapeDtypeStruct((T, D_ATT), F32)] + _gathered_shapes(gather),
        scratch_shapes=[pltpu.VMEM((2, T, HEAD_DIM), BF16), pltpu.VMEM((2, T, HEAD_DIM), BF16),
                        pltpu.VMEM((2, B, HEAD_DIM), F32), pltpu.VMEM((2, B, 1), F32), pltpu.VMEM((2, B, 1), F32),
                        pltpu.VMEM((2, B, B), F32), pltpu.VMEM((2, B, B), F32), pltpu.VMEM((2, B, 2 * B), BF16),
                        pltpu.VMEM((B, B), F32)]
        + (_Gather.semaphores(n_g) if n_g else []),
        compiler_params=_params("arbitrary", "arbitrary"),
    )(u, u, u, *gather)
    return out[0], out[1:]


def _attn_bwd(u, att, datt, hosted=()):
    T = u.shape[0]
    B = ATT_BLOCK
    per_step = ATT_Q_PER_STEP
    ns = T // (B * per_step)
    n_pairs = D_ATT // 128
    n_h = len(hosted)
    h_in, h_out, h_shape, h_sems = _host(hosted)

    def body(*refs):
        (q_ref, k_ref, v_ref, o_ref, do_ref), h_ins, (dqkv_ref,), h_outs, scratch = _split_refs(refs, 5, n_h, 1, n_h)
        n_scratch = 17
        (kb, vb, dkacc, dvacc, dqacc, rsum, gsum, total, t0, lt0, zs, gs, lt, cat, wb, dzb,
         keep) = scratch[:n_scratch]
        p, s = pl.program_id(0), pl.program_id(1)

        if n_h:
            @pl.when(jnp.logical_and(p == 0, s == 0))
            def _():
                _Exchange(h_ins, h_outs, *scratch[n_scratch:]).start()

        @pl.when(s == 0)
        def _():
            _fill_causal(keep)
            for h in range(2):
                kb[h] = k_ref[:, HEAD_DIM * h:HEAD_DIM * (h + 1)].astype(BF16)
                vb[h] = v_ref[:, HEAD_DIM * h:HEAD_DIM * (h + 1)].astype(BF16)
            dkacc[...] = jnp.zeros_like(dkacc)
            dvacc[...] = jnp.zeros_like(dvacc)

        tri2, tri_incl2 = _triangles()
        hs = [slice(HEAD_DIM * h, HEAD_DIM * (h + 1)) for h in range(2)]
        both = lambda acc2: jnp.concatenate([acc2[0], acc2[1]], axis=1)

        def query_block(sb, carry):
            i = s * per_step + sb
            local = pl.ds(pl.multiple_of(sb * B, B), B)
            q = [(q_ref[local, hs[h]] * ATT_SCALE).astype(BF16) for h in range(2)]
            dob = [do_ref[local, hs[h]].astype(BF16) for h in range(2)]
            for h in range(2):
                total[h] = jnp.sum(dob[h].astype(F32) * o_ref[local, hs[h]], axis=1, keepdims=True)
            dqacc[...] = jnp.zeros_like(dqacc)
            rsum[...] = jnp.zeros_like(rsum)
            gsum[...] = jnp.zeros_like(gsum)
            _sweep_keys(i, functools.partial(tile, q, dob), rsum)
            dqkv_ref[0, pl.ds(pl.multiple_of(i * B, B), B), :] = (both(dqacc) * ATT_SCALE).astype(BF16)
            return carry

        def tile(q, dob, j, masked):
            keys = pl.ds(pl.multiple_of(j * B, B), B)

            def scores(h):
                zs[h] = _dot_nt(q[h], kb[h, keys, :])
                gs[h] = _dot_nt(dob[h], vb[h, keys, :])

            def fails(h):
                for r in _chunks():
                    t, a = _fail_and_log_beta(zs[h, r, :])
                    if masked:
                        t = t * keep[r, :]
                    _store_split(cat.at[h], r, t)
                    zs[h, r, :] = a
                    t0[h, r, :] = t[:, 0:1]

            def later_sums(h):
                lt[h] = _dot(cat[h], tri2)

            def weights(h):
                for r in _chunks():
                    later = lt[h, r, :]
                    w = jnp.exp(zs[h, r, :] - later - rsum[h, r, :])
                    if masked:
                        w = w * keep[r, :]
                    g = gs[h, r, :] * w
                    gs[h, r, :] = g
                    _store_split(cat.at[h], r, g)
                    wb[h, r, :] = w.astype(BF16)
                    lt0[h, r, :] = later[:, 0:1]

            def suffix_sums(h):
                lt[h] = _dot(cat[h], tri_incl2)

            def score_grads(h):
                for r in _chunks():
                    suffix = lt[h, r, :]
                    g = gs[h, r, :]
                    before = total[h, r, :] - (suffix + gsum[h, r, :])
                    dz = g - jnp.exp(zs[h, r, :]) * (g + before)
                    if masked:
                        dz = dz * keep[r, :]
                    dzb[h, r, :] = dz.astype(BF16)
                    rsum[h, r, :] += lt0[h, r, :] + t0[h, r, :]
                    gsum[h, r, :] += suffix[:, 0:1]

            def input_grads(h):
                dqacc[h] += _dot(dzb[h], kb[h, keys, :])
                dkacc[h, keys, :] += _dot_tn(dzb[h], q[h])
                dvacc[h, keys, :] += _dot_tn(wb[h], dob[h])

            _staggered([scores, fails, later_sums, weights, suffix_sums, score_grads, input_grads])

        lax.fori_loop(0, per_step, query_block, 0)

        @pl.when(s == ns - 1)
        def _():
            dqkv_ref[1] = both(dkacc).astype(BF16)
            dqkv_ref[2] = both(dvacc).astype(BF16)

        if n_h:
            @pl.when(jnp.logical_and(p == n_pairs - 1, s == ns - 1))
            def _():
                _Exchange(h_ins, h_outs, *scratch[n_scratch:]).wait()

    rows_spec = pl.BlockSpec((per_step * B, 128), lambda p, s: (s, p))
    out = pl.pallas_call(
        body, name="attn_bwd", grid=(n_pairs, ns),
        in_specs=[rows_spec,
                  pl.BlockSpec((T, 128), lambda p, s: (0, 4 + p)),
                  pl.BlockSpec((T, 128), lambda p, s: (0, 8 + p)),
                  rows_spec, rows_spec] + h_in,
        out_specs=[pl.BlockSpec((3, T, 128), lambda p, s: (0, 0, p))] + h_out,
        out_shape=[jax.ShapeDtypeStruct((3, T, D_ATT), BF16)] + h_shape,
        scratch_shapes=[pltpu.VMEM((2, T, HEAD_DIM), BF16), pltpu.VMEM((2, T, HEAD_DIM), BF16),
                        pltpu.VMEM((2, T, HEAD_DIM), F32), pltpu.VMEM((2, T, HEAD_DIM), F32),
                        pltpu.VMEM((2, B, HEAD_DIM), F32)]
        + [pltpu.VMEM((2, B, 1), F32)] * 5
        + [pltpu.VMEM((2, B, B), F32)] * 3
        + [pltpu.VMEM((2, B, 2 * B), BF16), pltpu.VMEM((2, B, B), BF16), pltpu.VMEM((2, B, B), BF16)]
        + [pltpu.VMEM((B, B), F32)]
        + h_sems,
        compiler_params=_params("arbitrary", "arbitrary"),
    )(u, u, u, att, datt, *hosted)
    return out[0], out[1:]


MIX_TILE = 256
MIX_CHUNK = 16


def _u_specs(tm, tile_of):
    return [pl.BlockSpec((tm, SEG), functools.partial(lambda k, *g: (tile_of(*g), k), k)) for k in range(3, N_SEG)]


SHIFT_ROWS = 24


def _shifted_scratch(tm):
    return pltpu.VMEM((7, tm + SHIFT_ROWS, D_CONV), F32)


def _make_shifted(base, shifted, tm):
    for b in range(1, 8):
        shifted[b - 1] = base[b:b + tm + SHIFT_ROWS, :]


def _tap_window(base, shifted, start, rows):
    whole, b = divmod(start, 8)
    if b == 0:
        return base[start + rows.start:start + rows.stop, :]
    return shifted[b - 1, 8 * whole + rows.start:8 * whole + rows.stop, :]


def _conv(ext, ext_shifted, cw_ref, rows):
    acc = cw_ref[0:1, :] * _tap_window(ext, ext_shifted, 2, rows)
    for k in range(1, CONV_WIDTH):
        acc = acc + cw_ref[k:k + 1, :] * _tap_window(ext, ext_shifted, 2 + k, rows)
    return acc


def _norm_stats(v):
    mu = jnp.mean(v, axis=-1, keepdims=True)
    vc = v - mu
    rstd = lax.rsqrt(jnp.mean(vc * vc, axis=-1, keepdims=True) + LN_EPS)
    return vc * rstd, rstd


def _norm_bwd(dy_scaled, xhat, rstd):
    return rstd * (dy_scaled - jnp.mean(dy_scaled, axis=-1, keepdims=True)
                   - xhat * jnp.mean(dy_scaled * xhat, axis=-1, keepdims=True))


def _mix_fwd(x, u, att, wap, wcp, wout, cw, cvec, dvec, layer):
    T = x.shape[0]
    tm = MIX_TILE

    def body(x_ref, za_ref, ga_ref, gb_ref, zc_ref, ga0_ref, ga1_ref, gc0_ref, gc1_ref, att_ref,
             wap_ref, wcp_ref, wout_ref, cw_ref, cvec_ref, dvec_ref, xn_ref, xnb_ref, h_ref, ext, ext_shifted):
        i = pl.program_id(0)

        @pl.when(i == 0)
        def _():
            ext[0:CONV_PAD, :] = jnp.zeros((CONV_PAD, D_CONV), F32)

        za = za_ref[...]
        attg = att_ref[...] * (za * _sig(za))
        ab = _dot(attg.astype(BF16), wap_ref[...])

        ext[CONV_PAD:CONV_PAD + tm, :] = ga_ref[...] * _sig(gb_ref[...])
        _make_shifted(ext, ext_shifted, tm)
        c1 = _conv(ext, ext_shifted, cw_ref, slice(0, tm)) + cvec_ref[0:1, :]
        ext[0:CONV_PAD, :] = ext[tm:tm + CONV_PAD, :]
        xh, _ = _norm_stats(c1)
        c2 = xh * cvec_ref[1:2, :] + cvec_ref[2:3, :]
        zc = zc_ref[...]
        cz = (c2 * _sig(c2)) * (zc * _sig(zc))
        cb = _dot(cz.astype(BF16), wcp_ref[...]) + dvec_ref[0:1, :]

        g_att = jnp.concatenate([ga0_ref[...], ga1_ref[...]], axis=1)
        g_conv = jnp.concatenate([gc0_ref[...], gc1_ref[...]], axis=1)
        merged = _sig(g_att) * ab + _sig(g_conv) * cb
        h = DEEPNORM_ALPHA * x_ref[...] + _dot(merged.astype(BF16), wout_ref[...])
        h_ref[...] = h
        hh, _ = _norm_stats(h)
        xn = hh * dvec_ref[1:2, :] + dvec_ref[2:3, :]
        xn_ref[...] = xn
        xnb_ref[...] = xn.astype(BF16)

    row = pl.BlockSpec((tm, D_MODEL), lambda i: (i, 0))
    return pl.pallas_call(
        body, name="mix_fwd", grid=(T // tm,),
        in_specs=[row] + _u_specs(tm, lambda i: i) + [
            pl.BlockSpec((tm, D_ATT), lambda i: (i, 0)),
            _resident((D_ATT, D_MODEL), lambda i: (0, 0)),
            _resident((D_CONV, D_MODEL), lambda i: (0, 0)),
            _resident((D_MODEL, D_MODEL), lambda i: (0, 0)),
            _resident((CONV_PAD, D_CONV), lambda i: (0, 0)),
            pl.BlockSpec((None, 8, D_CONV), lambda i: (layer, 0, 0)),
            pl.BlockSpec((None, 8, D_MODEL), lambda i: (layer, 0, 0))],
        out_specs=[row, row, row],
        out_shape=[jax.ShapeDtypeStruct((T, D_MODEL), F32), jax.ShapeDtypeStruct((T, D_MODEL), BF16),
                   jax.ShapeDtypeStruct((T, D_MODEL), F32)],
        scratch_shapes=[pltpu.VMEM((tm + CONV_PAD, D_CONV), F32), _shifted_scratch(tm)],
        compiler_params=_params("arbitrary"),
    )(x, *([u] * 8), att, wap, wcp, wout, cw, cvec, dvec)


def _mix_bwd(dxo, h, u, att, wap, wcp, wout, cw, cvec, dvec, layer, hosted=()):
    T = dxo.shape[0]
    tm = MIX_TILE
    nt = T // tm
    halo_per_tile = tm // CONV_PAD
    n_h = len(hosted)
    h_in, h_out, h_shape, h_sems = _host(hosted)

    def body(*refs):
        ins, h_ins, outs, h_outs, scratch = _split_refs(refs, 19, n_h, 9, n_h)
        (dxo_ref, h_ref, za_ref, ga_ref, gb_ref, zc_ref, ga0_ref, ga1_ref, gc0_ref, gc1_ref, hga_ref, hgb_ref,
         att_ref, wap_ref, wcp_ref, wout_ref, cw_ref, cvec_ref, dvec_ref) = ins
        datt_ref, dur_ref, dres_ref, dwap_ref, dwcp_ref, dwout_ref, dcw_ref, dcvec_ref, ddvec_ref = outs
        n_scratch = 19
        (ext, dext, ext_shifted, dext_shifted, attg_s, cz_s, merged_s, dy_s, dab_s, dcb_s,
         xh_s, dattg_s, dcz_s, ab_s, cb_s, sga_s, sgc_s, dm_s, rstd_s) = scratch[:n_scratch]
        r = pl.program_id(0)
        i = nt - 1 - r

        if n_h:
            @pl.when(r == 0)
            def _():
                _Exchange(h_ins, h_outs, *scratch[n_scratch:]).start()

        @pl.when(r == 0)
        def _():
            for ref in (dwap_ref, dwcp_ref, dwout_ref, dcw_ref, dcvec_ref, ddvec_ref):
                ref[...] = jnp.zeros_like(ref)
            dext[tm:tm + CONV_PAD, :] = jnp.zeros((CONV_PAD, D_CONV), F32)

        chunks = [slice(c, c + MIX_CHUNK) for c in range(0, tm, MIX_CHUNK)]
        whole = slice(0, tm)
        ln_g_c, ln_b_c = cvec_ref[1:2, :], cvec_ref[2:3, :]

        halo = hga_ref[...] * _sig(hgb_ref[...])
        ext[0:CONV_PAD, :] = jnp.where(i > 0, halo, 0.0)
        for c in chunks:
            za = za_ref[c, :]
            attg_s[c, :] = (att_ref[c, :] * (za * _sig(za))).astype(BF16)
            ext[CONV_PAD + c.start:CONV_PAD + c.stop, :] = ga_ref[c, :] * _sig(gb_ref[c, :])
        _make_shifted(ext, ext_shifted, tm)
        for c in chunks:
            c1 = _conv(ext, ext_shifted, cw_ref, c) + cvec_ref[0:1, :]
            xh, rstd_c = _norm_stats(c1)
            xh_s[c, :] = xh
            rstd_s[c, :] = rstd_c
            c2 = xh * ln_g_c + ln_b_c
            zc = zc_ref[c, :]
            cz_s[c, :] = ((c2 * _sig(c2)) * (zc * _sig(zc))).astype(BF16)
        ab_s[...] = _dot(attg_s[...], wap_ref[...])
        cb_s[...] = _dot(cz_s[...], wcp_ref[...]) + dvec_ref[0:1, :]

        for c in chunks:
            s_ga = _sig(jnp.concatenate([ga0_ref[c, :], ga1_ref[c, :]], axis=1))
            s_gc = _sig(jnp.concatenate([gc0_ref[c, :], gc1_ref[c, :]], axis=1))
            sga_s[c, :] = s_ga
            sgc_s[c, :] = s_gc
            merged_s[c, :] = (s_ga * ab_s[c, :] + s_gc * cb_s[c, :]).astype(BF16)
            dxo_c = dxo_ref[c, :]
            hh, rstd_h = _norm_stats(h_ref[c, :])
            ddvec_ref[1:2, :] += jnp.sum(dxo_c * hh, axis=0, keepdims=True)
            ddvec_ref[2:3, :] += jnp.sum(dxo_c, axis=0, keepdims=True)
            dh = _norm_bwd(dxo_c * dvec_ref[1:2, :], hh, rstd_h)
            dres_ref[c, :] = DEEPNORM_ALPHA * dh
            dy_s[c, :] = dh.astype(BF16)
        dwout_ref[...] += _dot_tn(merged_s[...], dy_s[...])
        dm_s[...] = _dot_nt(dy_s[...], wout_ref[...])

        for c in chunks:
            dm, s_ga, s_gc = dm_s[c, :], sga_s[c, :], sgc_s[c, :]
            dab_s[c, :] = (dm * s_ga).astype(BF16)
            dcb = dm * s_gc
            dcb_s[c, :] = dcb.astype(BF16)
            ddvec_ref[0:1, :] += jnp.sum(dcb, axis=0, keepdims=True)
            dur_ref[c, 4 * SEG:6 * SEG] = (dm * ab_s[c, :] * (s_ga * (1.0 - s_ga))).astype(BF16)
            dur_ref[c, 6 * SEG:8 * SEG] = (dm * cb_s[c, :] * (s_gc * (1.0 - s_gc))).astype(BF16)
        dwap_ref[...] += _dot_tn(attg_s[...], dab_s[...])
        dattg_s[...] = _dot_nt(dab_s[...], wap_ref[...])
        dwcp_ref[...] += _dot_tn(cz_s[...], dcb_s[...])
        dcz_s[...] = _dot_nt(dcb_s[...], wcp_ref[...])

        for c in chunks:
            za = za_ref[c, :]
            s_za = _sig(za)
            dattg = dattg_s[c, :]
            datt_ref[c, :] = dattg * (za * s_za)
            dur_ref[c, 0:SEG] = (dattg * att_ref[c, :] * (s_za * (1.0 + za * (1.0 - s_za)))).astype(BF16)
            xh = xh_s[c, :]
            c2 = xh * ln_g_c + ln_b_c
            s_c2 = _sig(c2)
            zc = zc_ref[c, :]
            s_zc = _sig(zc)
            dcz = dcz_s[c, :]
            dur_ref[c, 3 * SEG:4 * SEG] = (dcz * (c2 * s_c2) * (s_zc * (1.0 + zc * (1.0 - s_zc)))).astype(BF16)
            dc2 = dcz * (zc * s_zc) * (s_c2 * (1.0 + c2 * (1.0 - s_c2)))
            dcvec_ref[1:2, :] += jnp.sum(dc2 * xh, axis=0, keepdims=True)
            dcvec_ref[2:3, :] += jnp.sum(dc2, axis=0, keepdims=True)
            dc1 = _norm_bwd(dc2 * ln_g_c, xh, rstd_s[c, :])
            dcvec_ref[0:1, :] += jnp.sum(dc1, axis=0, keepdims=True)
            dext[c, :] = dc1

        dc1_tile = dext[0:tm, :]
        for k in range(CONV_WIDTH):
            dcw_ref[k:k + 1, :] += jnp.sum(dc1_tile * _tap_window(ext, ext_shifted, 2 + k, whole),
                                           axis=0, keepdims=True)
        _make_shifted(dext, dext_shifted, tm)
        for c in chunks:
            dc0 = cw_ref[0:1, :] * _tap_window(dext, dext_shifted, CONV_WIDTH - 1, c)
            for k in range(1, CONV_WIDTH):
                dc0 = dc0 + cw_ref[k:k + 1, :] * _tap_window(dext, dext_shifted, CONV_WIDTH - 1 - k, c)
            s_gb = _sig(gb_ref[c, :])
            dur_ref[c, SEG:2 * SEG] = (dc0 * s_gb).astype(BF16)
            dur_ref[c, 2 * SEG:3 * SEG] = (dc0 * ga_ref[c, :] * (s_gb * (1.0 - s_gb))).astype(BF16)
        dext[tm:tm + CONV_PAD, :] = dext[0:CONV_PAD, :]

        if n_h:
            @pl.when(r == nt - 1)
            def _():
                _Exchange(h_ins, h_outs, *scratch[n_scratch:]).wait()

    rev = lambda r: nt - 1 - r
    row = pl.BlockSpec((tm, D_MODEL), lambda r: (rev(r), 0))
    halo_spec = [pl.BlockSpec((CONV_PAD, SEG),
                              functools.partial(lambda k, r: (jnp.maximum(rev(r) * halo_per_tile - 1, 0), k), k))
                 for k in (4, 5)]
    const = lambda shape: _resident(shape, lambda r: (0,) * len(shape))
    out = pl.pallas_call(
        body, name="mix_bwd", grid=(nt,),
        in_specs=[row, row] + _u_specs(tm, rev) + halo_spec + [
            pl.BlockSpec((tm, D_ATT), lambda r: (rev(r), 0)),
            _resident((D_ATT, D_MODEL), lambda r: (0, 0)),
            _resident((D_CONV, D_MODEL), lambda r: (0, 0)),
            _resident((D_MODEL, D_MODEL), lambda r: (0, 0)),
            _resident((CONV_PAD, D_CONV), lambda r: (0, 0)),
            pl.BlockSpec((None, 8, D_CONV), lambda r: (layer, 0, 0)),
            pl.BlockSpec((None, 8, D_MODEL), lambda r: (layer, 0, 0))] + h_in,
        out_specs=[pl.BlockSpec((tm, D_ATT), lambda r: (rev(r), 0)),
                   pl.BlockSpec((tm, 8 * SEG), lambda r: (rev(r), 0)),
                   row,
                   const((D_ATT, D_MODEL)), const((D_CONV, D_MODEL)), const((D_MODEL, D_MODEL)),
                   const((CONV_PAD, D_CONV)), const((8, D_CONV)), const((8, D_MODEL))] + h_out,
        out_shape=[jax.ShapeDtypeStruct((T, D_ATT), F32), jax.ShapeDtypeStruct((T, 8 * SEG), BF16),
                   jax.ShapeDtypeStruct((T, D_MODEL), F32),
                   jax.ShapeDtypeStruct((D_ATT, D_MODEL), F32), jax.ShapeDtypeStruct((D_CONV, D_MODEL), F32),
                   jax.ShapeDtypeStruct((D_MODEL, D_MODEL), F32), jax.ShapeDtypeStruct((CONV_PAD, D_CONV), F32),
                   jax.ShapeDtypeStruct((8, D_CONV), F32), jax.ShapeDtypeStruct((8, D_MODEL), F32)] + h_shape,
        scratch_shapes=[pltpu.VMEM((tm + CONV_PAD, D_CONV), F32), pltpu.VMEM((tm + CONV_PAD, D_CONV), F32),
                        _shifted_scratch(tm), _shifted_scratch(tm),
                        pltpu.VMEM((tm, D_ATT), BF16), pltpu.VMEM((tm, D_CONV), BF16)]
        + [pltpu.VMEM((tm, D_MODEL), BF16)] * 4
        + [pltpu.VMEM((tm, D_CONV), F32)] * 3
        + [pltpu.VMEM((tm, D_MODEL), F32)] * 5
        + [pltpu.VMEM((tm, 1), F32)]
        + h_sems,
        compiler_params=_params("arbitrary"),
    )(dxo, h, *([u] * 10), att, wap, wcp, wout, cw, cvec, dvec, *hosted)
    return out[:9], out[9:]


def _loss_head(y, target):
    T = y.shape[0]
    tm = min(T, 512)

    def body(y_ref, t_ref, dy_ref, loss_ref):
        @pl.when(pl.program_id(0) == 0)
        def _():
            loss_ref[...] = jnp.zeros_like(loss_ref)

        err = y_ref[...] - t_ref[...]
        dy_ref[...] = err * (1.0 / D_MODEL)
        loss_ref[...] += 0.5 * jnp.sum(jnp.mean(err * err, axis=-1, keepdims=True))

    row = pl.BlockSpec((tm, D_MODEL), lambda i: (i, 0))
    return pl.pallas_call(
        body, name="loss_head", grid=(T // tm,), in_specs=[row, row],
        out_specs=[row, pl.BlockSpec((8, 128), lambda i: (0, 0))],
        out_shape=[jax.ShapeDtypeStruct((T, D_MODEL), F32), jax.ShapeDtypeStruct((8, 128), F32)],
        compiler_params=_params("arbitrary"),
    )(y, target)


def _adamw(w, g, m, v):
    m = ADAM_B1 * m + (1.0 - ADAM_B1) * g
    v = ADAM_B2 * v + (1.0 - ADAM_B2) * (g * g)
    m_hat = m / (1.0 - ADAM_B1 ** ADAM_STEP)
    v_hat = v / (1.0 - ADAM_B2 ** ADAM_STEP)
    return -ADAM_LR * (m_hat / (jnp.sqrt(v_hat) + ADAM_EPS) + ADAM_WD * w), m, v


def _sum_slots(ref):
    total = ref[0].astype(F32)
    for j in range(1, N_DEV):
        total = total + ref[j].astype(F32)
    return total


def _adamw_sharded(recv, w, m, v, prev, layer, tr, name):
    n_layers, R, C = w.shape

    def body(*refs):
        recv_ref, w_ref, m_ref, v_ref = refs[:4]
        g_ref, d_ref, nm_ref, nv_ref = refs[-4:]
        g = _sum_slots(recv_ref)
        g_ref[...] = g
        d_ref[...], nm_ref[...], nv_ref[...] = _adamw(w_ref[...], g, m_ref[...], v_ref[...])

    slab = pl.BlockSpec((None, tr, C), lambda i: (layer, i, 0))
    n_prev = 0 if prev is None else 4
    return pl.pallas_call(
        body, name=name, grid=(R // tr,),
        in_specs=[pl.BlockSpec((N_DEV, tr, C), lambda i: (0, i, 0)), slab, slab, slab] + [ANY] * n_prev,
        out_specs=[slab] * 4,
        out_shape=[jax.ShapeDtypeStruct(w.shape, F32)] * 4,
        input_output_aliases={4 + k: k for k in range(n_prev)},
        compiler_params=_params("parallel"),
    )(recv, w, m, v, *(prev or ()))


def _sum8(g):
    R = g.shape[1]

    def body(g_ref, o_ref):
        o_ref[...] = _sum_slots(g_ref)

    return pl.pallas_call(body, name="sum_small_grads", out_shape=jax.ShapeDtypeStruct((R, 128), F32))(g)


def _adamw_small(w, g, m, v):
    def body(w_ref, g_ref, m_ref, v_ref, d_ref, nm_ref, nv_ref):
        d_ref[...], nm_ref[...], nv_ref[...] = _adamw(w_ref[...], g_ref[...], m_ref[...], v_ref[...])

    return pl.pallas_call(body, name="adamw_small", out_shape=[jax.ShapeDtypeStruct(w.shape, F32)] * 3)(w, g, m, v)


REPLICATED = (("b_in", D_IN), ("conv_b", D_CONV), ("conv_ln_g", D_CONV), ("conv_ln_b", D_CONV),
              ("b_conv_proj", D_MODEL), ("ln_g", D_MODEL), ("ln_b", D_MODEL))
N_REPLICATED = sum(n for _, n in REPLICATED)
CONV_W_SHARD_PAD = 2048


def _pack_small(rep, conv_w_shard):
    flat = jnp.concatenate([rep[k] for k, _ in REPLICATED], axis=1)
    cws = jnp.pad(conv_w_shard.reshape(DEPTH, -1), ((0, 0), (0, CONV_W_SHARD_PAD - CONV_WIDTH * 64)))
    return jnp.concatenate([flat, cws], axis=1).reshape(-1, 128)


def _unpack_small(packed):
    per = packed.reshape(DEPTH, N_REPLICATED + CONV_W_SHARD_PAD)
    out, off = {}, 0
    for k, n in REPLICATED:
        out[k] = per[:, off:off + n]
        off += n
    out["conv_w"] = per[:, off:off + CONV_WIDTH * 64].reshape(DEPTH, CONV_WIDTH, 64)
    return out


def kernel(x, w_in, b_in, conv_w, conv_b, conv_ln_g, conv_ln_b, w_att_proj, w_conv_proj, b_conv_proj, w_out, ln_g, ln_b, loss_target, m_w_in, m_b_in, m_conv_w, m_conv_b, m_conv_ln_g, m_conv_ln_b, m_w_att_proj, m_w_conv_proj, m_b_conv_proj, m_w_out, m_ln_g, m_ln_b, v_w_in, v_b_in, v_conv_w, v_conv_b, v_conv_ln_g, v_conv_ln_b, v_w_att_proj, v_w_conv_proj, v_b_conv_proj, v_w_out, v_ln_g, v_ln_b):
    x0 = x[0]
    target = loss_target[0]

    def shards(l):
        return [w_in[l].astype(BF16), w_att_proj[l].astype(BF16), w_conv_proj[l].astype(BF16),
                w_out[l].astype(BF16), conv_w[l]]

    def full_weights(g_in, g_ap, g_cp, g_out, g_cw):
        return (g_in.transpose(1, 0, 2).reshape(D_MODEL, D_IN), g_ap.transpose(1, 0, 2).reshape(D_ATT, D_MODEL),
                g_cp.transpose(1, 0, 2).reshape(D_CONV, D_MODEL), g_out.reshape(D_MODEL, D_MODEL),
                jnp.pad(g_cw.transpose(1, 0, 2).reshape(CONV_WIDTH, D_CONV), ((0, 1), (0, 0))))

    pad8 = lambda rows: jnp.pad(jnp.stack(rows, axis=1), ((0, 0), (0, 8 - len(rows)), (0, 0)))
    cvec = pad8([conv_b, conv_ln_g, conv_ln_b])
    dvec = pad8([b_conv_proj, ln_g, ln_b])
    b_in3 = b_in.reshape(DEPTH, 1, D_IN)

    weights = [full_weights(*_all_gather(shards(0), "gather_weights"))]
    xs, xbs, us, atts, hs = [x0], [x0.astype(BF16)], [], [], []
    for l in range(DEPTH):
        w_in_l, wap, wcp, wout, cw = weights[l]
        u = _in_proj(xbs[l], w_in_l, b_in3, l)
        att, gathered = _attn_fwd(u, gather=shards(l + 1) if l + 1 < DEPTH else ())
        if gathered:
            weights.append(full_weights(*gathered))
        xn, xnb, h = _mix_fwd(xs[l], u, att, wap, wcp, wout, cw, cvec, dvec, l)
        us.append(u), atts.append(att), hs.append(h), xs.append(xn), xbs.append(xnb)
    dxo, loss_part = _loss_head(xs[DEPTH], target)
    loss = lax.psum(loss_part[0, 0], ("x", "y", "c"))

    big = {"w_in": (w_in, m_w_in, v_w_in, 256), "w_att_proj": (w_att_proj, m_w_att_proj, v_w_att_proj, D_ATT),
           "w_conv_proj": (w_conv_proj, m_w_conv_proj, v_w_conv_proj, D_CONV), "w_out": (w_out, m_w_out, v_w_out, 128)}
    results = {k: None for k in big}
    small = [None] * DEPTH

    def update(k, recv, l):
        w, m, v, tr = big[k]
        results[k] = _adamw_sharded(recv, w, m, v, results[k], l, tr, "adamw_" + k)

    pending_w_in = ()
    for l in reversed(range(DEPTH)):
        w_in_l, wap, wcp, wout, cw = weights[l]
        (datt, du_rest, dres, dwap, dwcp, dwout, dcw, dcvec, ddvec), got = _mix_bwd(
            dxo, hs[l], us[l], atts[l], wap, wcp, wout, cw, cvec, dvec, l, hosted=pending_w_in)
        if got:
            update("w_in", got[0], l + 1)
        proj = [dwap.reshape(D_ATT, N_DEV, D_MODEL // N_DEV).transpose(1, 0, 2).astype(BF16),
                dwcp.reshape(D_CONV, N_DEV, D_MODEL // N_DEV).transpose(1, 0, 2).astype(BF16),
                dwout.reshape(N_DEV, D_MODEL // N_DEV, D_MODEL).astype(BF16)]
        dqkv, got = _attn_bwd(us[l], atts[l], datt, hosted=proj)
        for k, r in zip(("w_att_proj", "w_conv_proj", "w_out"), got):
            update(k, r, l)
        dwin, dbin = _in_proj_dw(xbs[l], dqkv, du_rest)
        small[l] = jnp.concatenate([dbin.reshape(-1), dcvec[:3].reshape(-1), ddvec[:3].reshape(-1), dcw.reshape(-1)])
        pending_w_in = (dwin.reshape(D_MODEL, N_DEV, W_IN_SHARD).transpose(1, 0, 2).astype(BF16),)
        dxo, got = _in_proj_dx(dqkv, du_rest, w_in_l, dres, hosted=pending_w_in if l == 0 else ())
        if got:
            update("w_in", got[0], l)
    grad_x = dxo[None]

    small_all = jnp.stack(small).reshape(-1, 128)
    (gathered,) = _all_gather([small_all], "gather_small_grads")
    per_layer = _sum8(gathered).reshape(DEPTH, -1)
    g_rep, off = {}, 0
    for k, n in (("b_in", D_IN), ("conv_b", D_CONV), ("conv_ln_g", D_CONV), ("conv_ln_b", D_CONV),
                 ("b_conv_proj", D_MODEL), ("ln_g", D_MODEL), ("ln_b", D_MODEL)):
        g_rep[k] = per_layer[:, off:off + n]
        off += n
    g_cw_full = per_layer[:, off:].reshape(DEPTH, CONV_PAD, D_CONV)[:, :CONV_WIDTH]
    g_cw_shard = lax.dynamic_slice_in_dim(g_cw_full, _my_index() * 64, 64, axis=2)
    rep_w = dict(b_in=b_in, conv_b=conv_b, conv_ln_g=conv_ln_g, conv_ln_b=conv_ln_b, b_conv_proj=b_conv_proj,
                 ln_g=ln_g, ln_b=ln_b)
    rep_m = dict(b_in=m_b_in, conv_b=m_conv_b, conv_ln_g=m_conv_ln_g, conv_ln_b=m_conv_ln_b,
                 b_conv_proj=m_b_conv_proj, ln_g=m_ln_g, ln_b=m_ln_b)
    rep_v = dict(b_in=v_b_in, conv_b=v_conv_b, conv_ln_g=v_conv_ln_g, conv_ln_b=v_conv_ln_b,
                 b_conv_proj=v_b_conv_proj, ln_g=v_ln_g, ln_b=v_ln_b)
    sd, sm, sv = _adamw_small(_pack_small(rep_w, conv_w), _pack_small(g_rep, g_cw_shard),
                              _pack_small(rep_m, m_conv_w), _pack_small(rep_v, v_conv_w))
    small_out = [dict(g_rep, conv_w=g_cw_shard), _unpack_small(sd), _unpack_small(sm), _unpack_small(sv)]

    order = ["w_in", "b_in", "conv_w", "conv_b", "conv_ln_g", "conv_ln_b", "w_att_proj", "w_conv_proj",
             "b_conv_proj", "w_out", "ln_g", "ln_b"]
    outs = [loss, grad_x]
    for kind in range(4):
        outs += [results[k][kind] if k in big else small_out[kind][k] for k in order]
    return tuple(outs)
```

```python
import functools

import jax
import jax.numpy as jnp
from jax import lax
from jax.experimental import pallas as pl
from jax.experimental.pallas import tpu as pltpu

F32, BF16 = jnp.float32, jnp.bfloat16

D_MODEL = 1024
D_ATT = 512
D_CONV = 512
HEAD_DIM = 64
PAIR = 2 * HEAD_DIM
CONV_WIDTH = 31
CONV_PAD = 32
DEPTH = 4
N_DEV = 8
SEG = 512
D_IN = 4 * D_ATT + 3 * D_CONV + 2 * D_MODEL
N_SEG = D_IN // SEG
N_QKV_SEG = 3
W_IN_SHARD = D_IN // N_DEV
CONV_W_SHARD = D_CONV // N_DEV
LANES = 128
LN_EPS = 1e-5
DEEPNORM_ALPHA = (2 * DEPTH) ** 0.25
ATT_SCALE = HEAD_DIM ** -0.5
ATT_BLOCK = 256
ADAM_LR, ADAM_B1, ADAM_B2, ADAM_EPS, ADAM_WD, ADAM_STEP = 0.001, 0.9, 0.999, 1e-08, 0.01, 10
VMEM_LIMIT = 56 * 1024 * 1024
MESH = pl.DeviceIdType.MESH
ANY = pl.BlockSpec(memory_space=pl.ANY)


def _resident(shape, index_map):
    return pl.BlockSpec(shape, index_map, pipeline_mode=pl.Buffered(1))


def _params(*semantics):
    return pltpu.CompilerParams(dimension_semantics=semantics, vmem_limit_bytes=VMEM_LIMIT)


def _sig(x):
    return 1.0 / (1.0 + jnp.exp(-x))


def _dot(a, b):
    return jnp.dot(a, b, preferred_element_type=F32)


def _dot_nt(a, b):
    return lax.dot_general(a, b, (((1,), (1,)), ((), ())), preferred_element_type=F32)


def _dot_tn(a, b):
    return lax.dot_general(a, b, (((0,), (0,)), ((), ())), preferred_element_type=F32)


def _my_index():
    return 4 * lax.axis_index("x") + 2 * lax.axis_index("y") + lax.axis_index("c")


class _Gather:
    def __init__(self, ins, outs, send_sems, recv_sems, local_sems):
        self.n = n = len(ins)
        x, y, c = lax.axis_index("x"), lax.axis_index("y"), lax.axis_index("c")
        me, sibling = (x, y, c), (x, y, 1 - c)
        chips = [(1 - x, y), (x, 1 - y), (1 - x, 1 - y)]

        def slot(a, dev):
            return outs[a].at[4 * dev[0] + 2 * dev[1] + dev[2]]

        def copy(a, k, block, to, src=None):
            return pltpu.make_async_remote_copy(
                src_ref=slot(a, block) if src is None else src, dst_ref=slot(a, block),
                send_sem=send_sems.at[7 * a + k], recv_sem=recv_sems.at[7 * a + k], device_id=to, device_id_type=MESH)

        self.mine = [pltpu.make_async_copy(ins[a], slot(a, me), local_sems.at[a]) for a in range(n)]
        self.first, self.landed, self.passed, self.last = [], [], [], []
        for a in range(n):
            self.first.append(copy(a, 0, me, sibling, src=ins[a]))
            self.first += [copy(a, 1 + j, me, (*chip, c), src=ins[a]) for j, chip in enumerate(chips)]
        for j, chip in enumerate(chips):
            for a in range(n):
                self.landed.append(copy(a, 1 + j, (*chip, c), me))
                self.passed.append(copy(a, 4 + j, (*chip, c), sibling))
        for a in range(n):
            self.last.append(copy(a, 0, sibling, me))
            self.last += [copy(a, 4 + j, (*chip, 1 - c), me) for j, chip in enumerate(chips)]

    @staticmethod
    def semaphores(n):
        return [pltpu.SemaphoreType.DMA((7 * n,)), pltpu.SemaphoreType.DMA((7 * n,)), pltpu.SemaphoreType.DMA((n,))]

    def start(self):
        for cp in self.mine + self.first:
            cp.start()

    def forward(self):
        for landed, passed in zip(self.landed, self.passed):
            landed.wait_recv()
            passed.start()

    def wait(self):
        for cp in self.last:
            cp.wait_recv()
        for cp in self.first + self.passed:
            cp.wait_send()
        for cp in self.mine:
            cp.wait()


def _gathered_shapes(shards):
    return [jax.ShapeDtypeStruct((N_DEV,) + s.shape, s.dtype) for s in shards]


def _all_gather(shards, name):
    n = len(shards)

    def body(*refs):
        g = _Gather(refs[:n], refs[n:2 * n], *refs[2 * n:])
        g.start()
        g.forward()
        g.wait()

    return pl.pallas_call(
        body, name=name, out_shape=_gathered_shapes(shards),
        in_specs=[ANY] * n, out_specs=[ANY] * n, scratch_shapes=_Gather.semaphores(n),
    )(*shards)


class _Exchange:
    def __init__(self, ins, outs, send_sems, recv_sems, local_sems):
        n = len(ins)
        x, y, c = lax.axis_index("x"), lax.axis_index("y"), lax.axis_index("c")
        me_idx = 4 * x + 2 * y + c
        self.mine = [pltpu.make_async_copy(ins[a].at[me_idx], outs[a].at[me_idx], local_sems.at[a])
                     for a in range(n)]
        self.sends, self.recvs = [], []
        for d in (1, 2, 4, 3, 5, 6, 7):
            px = 1 - x if d & 4 else x
            py = 1 - y if d & 2 else y
            pc = 1 - c if d & 1 else c
            idx = 4 * px + 2 * py + pc
            for a in range(n):
                sems = dict(send_sem=send_sems.at[7 * a + d - 1], recv_sem=recv_sems.at[7 * a + d - 1],
                            device_id=(px, py, pc), device_id_type=MESH)
                self.sends.append(pltpu.make_async_remote_copy(
                    src_ref=ins[a].at[idx], dst_ref=outs[a].at[me_idx], **sems))
                self.recvs.append(pltpu.make_async_remote_copy(
                    src_ref=ins[a].at[idx], dst_ref=outs[a].at[idx], **sems))

    @staticmethod
    def semaphores(n):
        return [pltpu.SemaphoreType.DMA((7 * n,)), pltpu.SemaphoreType.DMA((7 * n,)), pltpu.SemaphoreType.DMA((n,))]

    def start(self):
        for cp in self.mine + self.sends:
            cp.start()

    def wait(self):
        for cp in self.recvs:
            cp.wait_recv()
        for cp in self.sends:
            cp.wait_send()
        for cp in self.mine:
            cp.wait()


def _host(hosted):
    n = len(hosted)
    return ([ANY] * n, [ANY] * n, [jax.ShapeDtypeStruct(p.shape, p.dtype) for p in hosted],
            _Exchange.semaphores(n) if n else [])


def _split_refs(refs, *counts):
    out, at = [], 0
    for k in counts:
        out.append(refs[at:at + k])
        at += k
    return out + [refs[at:]]


def _in_proj(xb, w_in, b_in, layer):
    T = xb.shape[0]
    tm, tn = min(T, 2048), SEG

    def body(x_ref, w_ref, b_ref, u_ref):
        u_ref[...] = _dot(x_ref[...], w_ref[...]) + b_ref[...]

    return pl.pallas_call(
        body, name="in_proj", grid=(T // tm, D_IN // tn),
        in_specs=[pl.BlockSpec((tm, D_MODEL), lambda m, n: (m, 0)),
                  pl.BlockSpec((D_MODEL, tn), lambda m, n: (0, n)),
                  pl.BlockSpec((None, 1, tn), lambda m, n: (layer, 0, n))],
        out_specs=pl.BlockSpec((tm, tn), lambda m, n: (m, n)),
        out_shape=jax.ShapeDtypeStruct((T, D_IN), F32),
        compiler_params=_params("parallel", "parallel"),
    )(xb, w_in, b_in)


def _du_specs(tm, row_of, seg_of):
    return [pl.BlockSpec((None, tm, SEG), lambda *g: (jnp.minimum(seg_of(*g), N_QKV_SEG - 1), row_of(*g), 0)),
            pl.BlockSpec((tm, SEG), lambda *g: (row_of(*g), jnp.maximum(seg_of(*g) - N_QKV_SEG, 0)))]


def _in_proj_dx(dqkv, du_rest, w_in, dres, hosted=()):
    T = du_rest.shape[0]
    tm = min(T, 2048)
    nm = T // tm
    n_h = len(hosted)
    h_in, h_out, h_shape, h_sems = _host(hosted)

    def body(*refs):
        (a1_ref, a2_ref, w_ref, r_ref), h_ins, (o_ref,), h_outs, sems = _split_refs(refs, 4, n_h, 1, n_h)
        m, s = pl.program_id(0), pl.program_id(1)

        if n_h:
            @pl.when(jnp.logical_and(m == 0, s == 0))
            def _():
                _Exchange(h_ins, h_outs, *sems).start()

        @pl.when(s == 0)
        def _():
            o_ref[...] = r_ref[...]

        @pl.when(s < N_QKV_SEG)
        def _():
            o_ref[...] += _dot_nt(a1_ref[...], w_ref[...])

        @pl.when(s >= N_QKV_SEG)
        def _():
            o_ref[...] += _dot_nt(a2_ref[...], w_ref[...])

        if n_h:
            @pl.when(jnp.logical_and(m == nm - 1, s == N_SEG - 1))
            def _():
                _Exchange(h_ins, h_outs, *sems).wait()

    out = pl.pallas_call(
        body, name="in_proj_dx", grid=(nm, N_SEG),
        in_specs=_du_specs(tm, lambda m, s: m, lambda m, s: s) + [
            pl.BlockSpec((D_MODEL, SEG), lambda m, s: (0, s)),
            pl.BlockSpec((tm, D_MODEL), lambda m, s: (m, 0))] + h_in,
        out_specs=[pl.BlockSpec((tm, D_MODEL), lambda m, s: (m, 0))] + h_out,
        out_shape=[jax.ShapeDtypeStruct((T, D_MODEL), F32)] + h_shape,
        scratch_shapes=h_sems,
        compiler_params=_params("arbitrary", "arbitrary"),
    )(dqkv, du_rest, w_in, dres, *hosted)
    return out[0], out[1:]


def _in_proj_dw(xb, dqkv, du_rest):
    T = xb.shape[0]
    tk = min(T, 1024)
    nt = T // tk

    def body(x_ref, a1_ref, a2_ref, dw_ref, db_ref, acc_ref):
        s, t = pl.program_id(0), pl.program_id(1)

        @pl.when(t == 0)
        def _():
            acc_ref[...] = jnp.zeros_like(acc_ref)
            db_ref[...] = jnp.zeros_like(db_ref)

        def acc(a):
            acc_ref[...] += _dot_tn(x_ref[...], a)
            db_ref[...] += jnp.sum(a.astype(F32), axis=0, keepdims=True)

        @pl.when(s < N_QKV_SEG)
        def _():
            acc(a1_ref[...])

        @pl.when(s >= N_QKV_SEG)
        def _():
            acc(a2_ref[...])

        @pl.when(t == nt - 1)
        def _():
            dw_ref[...] = acc_ref[...].astype(BF16)

    return pl.pallas_call(
        body, name="in_proj_dw", grid=(N_SEG, nt),
        in_specs=[pl.BlockSpec((tk, D_MODEL), lambda s, t: (t, 0))] + _du_specs(tk, lambda s, t: t, lambda s, t: s),
        out_specs=[pl.BlockSpec((D_MODEL, SEG), lambda s, t: (0, s)),
                   pl.BlockSpec((None, 1, SEG), lambda s, t: (s, 0, 0))],
        out_shape=[jax.ShapeDtypeStruct((D_MODEL, D_IN), BF16), jax.ShapeDtypeStruct((N_SEG, 1, SEG), F32)],
        scratch_shapes=[pltpu.VMEM((D_MODEL, SEG), F32)],
        compiler_params=_params("parallel", "arbitrary"),
    )(xb, dqkv, du_rest)


ATT_CHUNK = 32
ATT_Q_PER_STEP = 2


def _chunks():
    return [slice(r, r + ATT_CHUNK) for r in range(0, ATT_BLOCK, ATT_CHUNK)]


def _twice(rhs):
    return jnp.concatenate([rhs, rhs], axis=0)


def _fail_and_log_beta(z):
    t = jnp.maximum(z, 0.0) + jnp.log(1.0 + jnp.exp(-jnp.abs(z)))
    return t, z - t


def _store_split(dst, rows, val):
    hi = val.astype(BF16)
    dst[rows, 0:ATT_BLOCK] = hi
    dst[rows, ATT_BLOCK:2 * ATT_BLOCK] = (val - hi.astype(F32)).astype(BF16)


def _fill_causal(keep):
    row = lax.broadcasted_iota(jnp.int32, (ATT_BLOCK, ATT_BLOCK), 0)
    col = lax.broadcasted_iota(jnp.int32, (ATT_BLOCK, ATT_BLOCK), 1)
    keep[...] = (col < row).astype(F32)


def _triangles():
    row = lax.broadcasted_iota(jnp.int32, (ATT_BLOCK, ATT_BLOCK), 0)
    col = lax.broadcasted_iota(jnp.int32, (ATT_BLOCK, ATT_BLOCK), 1)
    return _twice((row > col).astype(BF16)), _twice((row >= col).astype(BF16))


def _staggered(stages):
    stages[0](0)
    stages[0](1)
    for k in range(1, len(stages), 2):
        for h in range(2):
            stages[k](h)
            stages[k + 1](h)


EXP_UNDERFLOW = 104.0


def _sweep_keys(i, tile, rsum):
    tile(i, True)

    def live(carry):
        j, low = carry
        return jnp.logical_and(j >= 0, low < EXP_UNDERFLOW)

    def step(carry):
        j, _ = carry
        tile(j, False)
        return j - 1, jnp.min(rsum[...])

    lax.while_loop(live, step, (i - 1, jnp.min(rsum[...])))


def _attn_fwd(u, gather=()):
    T = u.shape[0]
    B = ATT_BLOCK
    per_step = ATT_Q_PER_STEP
    ns = T // (B * per_step)
    n_pairs = D_ATT // PAIR
    n_g = len(gather)

    def body(*refs):
        (q_ref, k_ref, v_ref), g_ins, (o_ref,), g_outs, scratch = _split_refs(refs, 3, n_g, 1, n_g)
        kb, vb, acc, rsum, t0, zs, lt, cat, keep = scratch[:9]
        p, s = pl.program_id(0), pl.program_id(1)

        if n_g:
            @pl.when(jnp.logical_and(p == 0, s == 0))
            def _():
                _Gather(g_ins, g_outs, *scratch[9:]).start()

            @pl.when(jnp.logical_and(p == n_pairs - 1, s == 0))
            def _():
                _Gather(g_ins, g_outs, *scratch[9:]).forward()

        @pl.when(s == 0)
        def _():
            _fill_causal(keep)
            for h in range(2):
                kb[h] = k_ref[:, HEAD_DIM * h:HEAD_DIM * (h + 1)].astype(BF16)
                vb[h] = v_ref[:, HEAD_DIM * h:HEAD_DIM * (h + 1)].astype(BF16)

        tri2, _ = _triangles()
        hs = [slice(HEAD_DIM * h, HEAD_DIM * (h + 1)) for h in range(2)]

        def query_block(sb, carry):
            i = s * per_step + sb
            qrows = pl.ds(pl.multiple_of(sb * B, B), B)
            q = [(q_ref[qrows, hs[h]] * ATT_SCALE).astype(BF16) for h in range(2)]
            acc[...] = jnp.zeros_like(acc)
            rsum[...] = jnp.zeros_like(rsum)
            _sweep_keys(i, functools.partial(tile, q), rsum)
            for h in range(2):
                o_ref[qrows, hs[h]] = acc[h]
            return carry

        def tile(q, j, masked):
            keys = pl.ds(pl.multiple_of(j * B, B), B)

            def scores(h):
                zs[h] = _dot_nt(q[h], kb[h, keys, :])

            def fails(h):
                for r in _chunks():
                    t, a = _fail_and_log_beta(zs[h, r, :])
                    if masked:
                        t = t * keep[r, :]
                    _store_split(cat.at[h], r, t)
                    zs[h, r, :] = a
                    t0[h, r, :] = t[:, 0:1]

            def later_sums(h):
                lt[h] = _dot(cat[h], tri2)

            def weights(h):
                for r in _chunks():
                    later = lt[h, r, :]
                    w = jnp.exp(zs[h, r, :] - later - rsum[h, r, :])
                    if masked:
                        w = w * keep[r, :]
                    _store_split(cat.at[h], r, w)
                    rsum[h, r, :] += later[:, 0:1] + t0[h, r, :]

            def values(h):
                acc[h] += _dot(cat[h], _twice(vb[h, keys, :]))

            _staggered([scores, fails, later_sums, weights, values])

        lax.fori_loop(0, per_step, query_block, 0)

        if n_g:
            @pl.when(jnp.logical_and(p == n_pairs - 1, s == ns - 1))
            def _():
                _Gather(g_ins, g_outs, *scratch[9:]).wait()

    out = pl.pallas_call(
        body, name="attn_fwd", grid=(n_pairs, ns),
        in_specs=[pl.BlockSpec((per_step * B, PAIR), lambda p, s: (s, p)),
                  pl.BlockSpec((T, PAIR), lambda p, s: (0, 4 + p)),
                  pl.BlockSpec((T, PAIR), lambda p, s: (0, 8 + p))] + [ANY] * n_g,
        out_specs=[pl.BlockSpec((per_step * B, PAIR), lambda p, s: (s, p))] + [ANY] * n_g,
        out_shape=[jax.ShapeDtypeStruct((T, D_ATT), F32)] + _gathered_shapes(gather),
        scratch_shapes=[pltpu.VMEM((2, T, HEAD_DIM), BF16), pltpu.VMEM((2, T, HEAD_DIM), BF16),
                        pltpu.VMEM((2, B, HEAD_DIM), F32), pltpu.VMEM((2, B, 1), F32), pltpu.VMEM((2, B, 1), F32),
                        pltpu.VMEM((2, B, B), F32), pltpu.VMEM((2, B, B), F32), pltpu.VMEM((2, B, 2 * B), BF16),
                        pltpu.VMEM((B, B), F32)]
        + (_Gather.semaphores(n_g) if n_g else []),
        compiler_params=_params("arbitrary", "arbitrary"),
    )(u, u, u, *gather)
    return out[0], out[1:]


def _attn_bwd(u, att, datt, hosted=()):
    T = u.shape[0]
    B = ATT_BLOCK
    per_step = ATT_Q_PER_STEP
    ns = T // (B * per_step)
    n_pairs = D_ATT // PAIR
    n_h = len(hosted)
    h_in, h_out, h_shape, h_sems = _host(hosted)

    def body(*refs):
        (q_ref, k_ref, v_ref, o_ref, do_ref), h_ins, (dqkv_ref,), h_outs, scratch = _split_refs(refs, 5, n_h, 1, n_h)
        n_scratch = 17
        (kb, vb, dkacc, dvacc, dqacc, rsum, gsum, total, t0, lt0, zs, gs, lt, cat, wb, dzb,
         keep) = scratch[:n_scratch]
        p, s = pl.program_id(0), pl.program_id(1)

        if n_h:
            @pl.when(jnp.logical_and(p == 0, s == 0))
            def _():
                _Exchange(h_ins, h_outs, *scratch[n_scratch:]).start()

        @pl.when(s == 0)
        def _():
            _fill_causal(keep)
            for h in range(2):
                kb[h] = k_ref[:, HEAD_DIM * h:HEAD_DIM * (h + 1)].astype(BF16)
                vb[h] = v_ref[:, HEAD_DIM * h:HEAD_DIM * (h + 1)].astype(BF16)
            dkacc[...] = jnp.zeros_like(dkacc)
            dvacc[...] = jnp.zeros_like(dvacc)

        tri2, tri_incl2 = _triangles()
        hs = [slice(HEAD_DIM * h, HEAD_DIM * (h + 1)) for h in range(2)]
        both = lambda acc2: jnp.concatenate([acc2[0], acc2[1]], axis=1)

        def query_block(sb, carry):
            i = s * per_step + sb
            local = pl.ds(pl.multiple_of(sb * B, B), B)
            q = [(q_ref[local, hs[h]] * ATT_SCALE).astype(BF16) for h in range(2)]
            dob = [do_ref[local, hs[h]].astype(BF16) for h in range(2)]
            for h in range(2):
                total[h] = jnp.sum(dob[h].astype(F32) * o_ref[local, hs[h]], axis=1, keepdims=True)
            dqacc[...] = jnp.zeros_like(dqacc)
            rsum[...] = jnp.zeros_like(rsum)
            gsum[...] = jnp.zeros_like(gsum)
            _sweep_keys(i, functools.partial(tile, q, dob), rsum)
            dqkv_ref[0, pl.ds(pl.multiple_of(i * B, B), B), :] = (both(dqacc) * ATT_SCALE).astype(BF16)
            return carry

        def tile(q, dob, j, masked):
            keys = pl.ds(pl.multiple_of(j * B, B), B)

            def scores(h):
                zs[h] = _dot_nt(q[h], kb[h, keys, :])
                gs[h] = _dot_nt(dob[h], vb[h, keys, :])

            def fails(h):
                for r in _chunks():
                    t, a = _fail_and_log_beta(zs[h, r, :])
                    if masked:
                        t = t * keep[r, :]
                    _store_split(cat.at[h], r, t)
                    zs[h, r, :] = a
                    t0[h, r, :] = t[:, 0:1]

            def later_sums(h):
                lt[h] = _dot(cat[h], tri2)

            def weights(h):
                for r in _chunks():
                    later = lt[h, r, :]
                    w = jnp.exp(zs[h, r, :] - later - rsum[h, r, :])
                    if masked:
                        w = w * keep[r, :]
                    g = gs[h, r, :] * w
                    gs[h, r, :] = g
                    _store_split(cat.at[h], r, g)
                    wb[h, r, :] = w.astype(BF16)
                    lt0[h, r, :] = later[:, 0:1]

            def suffix_sums(h):
                lt[h] = _dot(cat[h], tri_incl2)

            def score_grads(h):
                for r in _chunks():
                    suffix = lt[h, r, :]
                    g = gs[h, r, :]
                    before = total[h, r, :] - (suffix + gsum[h, r, :])
                    dz = g - jnp.exp(zs[h, r, :]) * (g + before)
                    if masked:
                        dz = dz * keep[r, :]
                    dzb[h, r, :] = dz.astype(BF16)
                    rsum[h, r, :] += lt0[h, r, :] + t0[h, r, :]
                    gsum[h, r, :] += suffix[:, 0:1]

            def input_grads(h):
                dqacc[h] += _dot(dzb[h], kb[h, keys, :])
                dkacc[h, keys, :] += _dot_tn(dzb[h], q[h])
                dvacc[h, keys, :] += _dot_tn(wb[h], dob[h])

            _staggered([scores, fails, later_sums, weights, suffix_sums, score_grads, input_grads])

        lax.fori_loop(0, per_step, query_block, 0)

        @pl.when(s == ns - 1)
        def _():
            dqkv_ref[1] = both(dkacc).astype(BF16)
            dqkv_ref[2] = both(dvacc).astype(BF16)

        if n_h:
            @pl.when(jnp.logical_and(p == n_pairs - 1, s == ns - 1))
            def _():
                _Exchange(h_ins, h_outs, *scratch[n_scratch:]).wait()

    rows_spec = pl.BlockSpec((per_step * B, PAIR), lambda p, s: (s, p))
    out = pl.pallas_call(
        body, name="attn_bwd", grid=(n_pairs, ns),
        in_specs=[rows_spec,
                  pl.BlockSpec((T, PAIR), lambda p, s: (0, 4 + p)),
                  pl.BlockSpec((T, PAIR), lambda p, s: (0, 8 + p)),
                  rows_spec, rows_spec] + h_in,
        out_specs=[pl.BlockSpec((3, T, PAIR), lambda p, s: (0, 0, p))] + h_out,
        out_shape=[jax.ShapeDtypeStruct((3, T, D_ATT), BF16)] + h_shape,
        scratch_shapes=[pltpu.VMEM((2, T, HEAD_DIM), BF16), pltpu.VMEM((2, T, HEAD_DIM), BF16),
                        pltpu.VMEM((2, T, HEAD_DIM), F32), pltpu.VMEM((2, T, HEAD_DIM), F32),
                        pltpu.VMEM((2, B, HEAD_DIM), F32)]
        + [pltpu.VMEM((2, B, 1), F32)] * 5
        + [pltpu.VMEM((2, B, B), F32)] * 3
        + [pltpu.VMEM((2, B, 2 * B), BF16), pltpu.VMEM((2, B, B), BF16), pltpu.VMEM((2, B, B), BF16)]
        + [pltpu.VMEM((B, B), F32)]
        + h_sems,
        compiler_params=_params("arbitrary", "arbitrary"),
    )(u, u, u, att, datt, *hosted)
    return out[0], out[1:]


MIX_TILE = 256
MIX_CHUNK = 16


def _u_specs(tm, tile_of):
    return [pl.BlockSpec((tm, SEG), functools.partial(lambda k, *g: (tile_of(*g), k), k)) for k in range(3, N_SEG)]


SHIFT_ROWS = 24


def _shifted_scratch(tm):
    return pltpu.VMEM((7, tm + SHIFT_ROWS, D_CONV), F32)


def _make_shifted(base, shifted, tm):
    for b in range(1, 8):
        shifted[b - 1] = base[b:b + tm + SHIFT_ROWS, :]


def _tap_window(base, shifted, start, rows):
    whole, b = divmod(start, 8)
    if b == 0:
        return base[start + rows.start:start + rows.stop, :]
    return shifted[b - 1, 8 * whole + rows.start:8 * whole + rows.stop, :]


def _conv(ext, ext_shifted, cw_ref, rows):
    acc = cw_ref[0:1, :] * _tap_window(ext, ext_shifted, 2, rows)
    for k in range(1, CONV_WIDTH):
        acc = acc + cw_ref[k:k + 1, :] * _tap_window(ext, ext_shifted, 2 + k, rows)
    return acc


def _norm_stats(v):
    mu = jnp.mean(v, axis=-1, keepdims=True)
    vc = v - mu
    rstd = lax.rsqrt(jnp.mean(vc * vc, axis=-1, keepdims=True) + LN_EPS)
    return vc * rstd, rstd


def _norm_bwd(dy_scaled, xhat, rstd):
    return rstd * (dy_scaled - jnp.mean(dy_scaled, axis=-1, keepdims=True)
                   - xhat * jnp.mean(dy_scaled * xhat, axis=-1, keepdims=True))


def _mix_fwd(x, u, att, wap, wcp, wout, cw, cvec, dvec, layer):
    T = x.shape[0]
    tm = MIX_TILE

    def body(x_ref, za_ref, ga_ref, gb_ref, zc_ref, ga0_ref, ga1_ref, gc0_ref, gc1_ref, att_ref,
             wap_ref, wcp_ref, wout_ref, cw_ref, cvec_ref, dvec_ref, xn_ref, xnb_ref, h_ref, c1_ref,
             ext, ext_shifted):
        i = pl.program_id(0)

        @pl.when(i == 0)
        def _():
            ext[0:CONV_PAD, :] = jnp.zeros((CONV_PAD, D_CONV), F32)

        za = za_ref[...]
        attg = att_ref[...] * (za * _sig(za))
        ab = _dot(attg.astype(BF16), wap_ref[...])

        ext[CONV_PAD:CONV_PAD + tm, :] = ga_ref[...] * _sig(gb_ref[...])
        _make_shifted(ext, ext_shifted, tm)
        c1 = _conv(ext, ext_shifted, cw_ref, slice(0, tm)) + cvec_ref[0:1, :]
        c1_ref[...] = c1
        ext[0:CONV_PAD, :] = ext[tm:tm + CONV_PAD, :]
        xh, _ = _norm_stats(c1)
        c2 = xh * cvec_ref[1:2, :] + cvec_ref[2:3, :]
        zc = zc_ref[...]
        cz = (c2 * _sig(c2)) * (zc * _sig(zc))
        cb = _dot(cz.astype(BF16), wcp_ref[...]) + dvec_ref[0:1, :]

        g_att = jnp.concatenate([ga0_ref[...], ga1_ref[...]], axis=1)
        g_conv = jnp.concatenate([gc0_ref[...], gc1_ref[...]], axis=1)
        merged = _sig(g_att) * ab + _sig(g_conv) * cb
        h = DEEPNORM_ALPHA * x_ref[...] + _dot(merged.astype(BF16), wout_ref[...])
        h_ref[...] = h
        hh, _ = _norm_stats(h)
        xn = hh * dvec_ref[1:2, :] + dvec_ref[2:3, :]
        xn_ref[...] = xn
        xnb_ref[...] = xn.astype(BF16)

    row = pl.BlockSpec((tm, D_MODEL), lambda i: (i, 0))
    return pl.pallas_call(
        body, name="mix_fwd", grid=(T // tm,),
        in_specs=[row] + _u_specs(tm, lambda i: i) + [
            pl.BlockSpec((tm, D_ATT), lambda i: (i, 0)),
            _resident((D_ATT, D_MODEL), lambda i: (0, 0)),
            _resident((D_CONV, D_MODEL), lambda i: (0, 0)),
            _resident((D_MODEL, D_MODEL), lambda i: (0, 0)),
            _resident((CONV_PAD, D_CONV), lambda i: (0, 0)),
            pl.BlockSpec((None, 8, D_CONV), lambda i: (layer, 0, 0)),
            pl.BlockSpec((None, 8, D_MODEL), lambda i: (layer, 0, 0))],
        out_specs=[row, row, row, pl.BlockSpec((tm, D_CONV), lambda i: (i, 0))],
        out_shape=[jax.ShapeDtypeStruct((T, D_MODEL), F32), jax.ShapeDtypeStruct((T, D_MODEL), BF16),
                   jax.ShapeDtypeStruct((T, D_MODEL), F32), jax.ShapeDtypeStruct((T, D_CONV), F32)],
        scratch_shapes=[pltpu.VMEM((tm + CONV_PAD, D_CONV), F32), _shifted_scratch(tm)],
        compiler_params=_params("arbitrary"),
    )(x, *([u] * 8), att, wap, wcp, wout, cw, cvec, dvec)


def _mix_bwd(dxo, h, c1, u, att, wap, wcp, wout, cw, cvec, dvec, layer, hosted=()):
    T = dxo.shape[0]
    tm = MIX_TILE
    nt = T // tm
    halo_per_tile = tm // CONV_PAD
    n_h = len(hosted)
    h_in, h_out, h_shape, h_sems = _host(hosted)

    def body(*refs):
        ins, h_ins, outs, h_outs, scratch = _split_refs(refs, 20, n_h, 9, n_h)
        (dxo_ref, h_ref, c1_ref, za_ref, ga_ref, gb_ref, zc_ref, ga0_ref, ga1_ref, gc0_ref, gc1_ref, hga_ref, hgb_ref,
         att_ref, wap_ref, wcp_ref, wout_ref, cw_ref, cvec_ref, dvec_ref) = ins
        datt_ref, dur_ref, dres_ref, dwap_ref, dwcp_ref, dwout_ref, dcw_ref, dcvec_ref, ddvec_ref = outs
        n_scratch = 19
        (ext, dext, ext_shifted, dext_shifted, attg_s, cz_s, merged_s, dy_s, dab_s, dcb_s,
         xh_s, dattg_s, dcz_s, ab_s, cb_s, sga_s, sgc_s, dm_s, rstd_s) = scratch[:n_scratch]
        r = pl.program_id(0)
        i = nt - 1 - r

        if n_h:
            @pl.when(r == 0)
            def _():
                _Exchange(h_ins, h_outs, *scratch[n_scratch:]).start()

        @pl.when(r == 0)
        def _():
            for ref in (dwap_ref, dwcp_ref, dwout_ref, dcw_ref, dcvec_ref, ddvec_ref):
                ref[...] = jnp.zeros_like(ref)
            dext[tm:tm + CONV_PAD, :] = jnp.zeros((CONV_PAD, D_CONV), F32)

        chunks = [slice(c, c + MIX_CHUNK) for c in range(0, tm, MIX_CHUNK)]
        whole = slice(0, tm)
        ln_g_c, ln_b_c = cvec_ref[1:2, :], cvec_ref[2:3, :]

        halo = hga_ref[...] * _sig(hgb_ref[...])
        ext[0:CONV_PAD, :] = jnp.where(i > 0, halo, 0.0)
        for c in chunks:
            za = za_ref[c, :]
            attg_s[c, :] = (att_ref[c, :] * (za * _sig(za))).astype(BF16)
            ext[CONV_PAD + c.start:CONV_PAD + c.stop, :] = ga_ref[c, :] * _sig(gb_ref[c, :])
        _make_shifted(ext, ext_shifted, tm)
        for c in chunks:
            xh, rstd_c = _norm_stats(c1_ref[c, :])
            xh_s[c, :] = xh
            rstd_s[c, :] = rstd_c
            c2 = xh * ln_g_c + ln_b_c
            zc = zc_ref[c, :]
            cz_s[c, :] = ((c2 * _sig(c2)) * (zc * _sig(zc))).astype(BF16)
        ab_s[...] = _dot(attg_s[...], wap_ref[...])
        cb_s[...] = _dot(cz_s[...], wcp_ref[...]) + dvec_ref[0:1, :]

        for c in chunks:
            s_ga = _sig(jnp.concatenate([ga0_ref[c, :], ga1_ref[c, :]], axis=1))
            s_gc = _sig(jnp.concatenate([gc0_ref[c, :], gc1_ref[c, :]], axis=1))
            sga_s[c, :] = s_ga
            sgc_s[c, :] = s_gc
            merged_s[c, :] = (s_ga * ab_s[c, :] + s_gc * cb_s[c, :]).astype(BF16)
            dxo_c = dxo_ref[c, :]
            hh, rstd_h = _norm_stats(h_ref[c, :])
            ddvec_ref[1:2, :] += jnp.sum(dxo_c * hh, axis=0, keepdims=True)
            ddvec_ref[2:3, :] += jnp.sum(dxo_c, axis=0, keepdims=True)
            dh = _norm_bwd(dxo_c * dvec_ref[1:2, :], hh, rstd_h)
            dres_ref[c, :] = DEEPNORM_ALPHA * dh
            dy_s[c, :] = dh.astype(BF16)
        dwout_ref[...] += _dot_tn(merged_s[...], dy_s[...])
        dm_s[...] = _dot_nt(dy_s[...], wout_ref[...])

        for c in chunks:
            dm, s_ga, s_gc = dm_s[c, :], sga_s[c, :], sgc_s[c, :]
            dab_s[c, :] = (dm * s_ga).astype(BF16)
            dcb = dm * s_gc
            dcb_s[c, :] = dcb.astype(BF16)
            ddvec_ref[0:1, :] += jnp.sum(dcb, axis=0, keepdims=True)
            dur_ref[c, 4 * SEG:6 * SEG] = (dm * ab_s[c, :] * (s_ga * (1.0 - s_ga))).astype(BF16)
            dur_ref[c, 6 * SEG:8 * SEG] = (dm * cb_s[c, :] * (s_gc * (1.0 - s_gc))).astype(BF16)
        dwap_ref[...] += _dot_tn(attg_s[...], dab_s[...])
        dattg_s[...] = _dot_nt(dab_s[...], wap_ref[...])
        dwcp_ref[...] += _dot_tn(cz_s[...], dcb_s[...])
        dcz_s[...] = _dot_nt(dcb_s[...], wcp_ref[...])

        for c in chunks:
            za = za_ref[c, :]
            s_za = _sig(za)
            dattg = dattg_s[c, :]
            datt_ref[c, :] = dattg * (za * s_za)
            dur_ref[c, 0:SEG] = (dattg * att_ref[c, :] * (s_za * (1.0 + za * (1.0 - s_za)))).astype(BF16)
            xh = xh_s[c, :]
            c2 = xh * ln_g_c + ln_b_c
            s_c2 = _sig(c2)
            zc = zc_ref[c, :]
            s_zc = _sig(zc)
            dcz = dcz_s[c, :]
            dur_ref[c, 3 * SEG:4 * SEG] = (dcz * (c2 * s_c2) * (s_zc * (1.0 + zc * (1.0 - s_zc)))).astype(BF16)
            dc2 = dcz * (zc * s_zc) * (s_c2 * (1.0 + c2 * (1.0 - s_c2)))
            dcvec_ref[1:2, :] += jnp.sum(dc2 * xh, axis=0, keepdims=True)
            dcvec_ref[2:3, :] += jnp.sum(dc2, axis=0, keepdims=True)
            dc1 = _norm_bwd(dc2 * ln_g_c, xh, rstd_s[c, :])
            dcvec_ref[0:1, :] += jnp.sum(dc1, axis=0, keepdims=True)
            dext[c, :] = dc1

        dc1_tile = dext[0:tm, :]
        for k in range(CONV_WIDTH):
            dcw_ref[k:k + 1, :] += jnp.sum(dc1_tile * _tap_window(ext, ext_shifted, 2 + k, whole),
                                           axis=0, keepdims=True)
        _make_shifted(dext, dext_shifted, tm)
        for c in chunks:
            dc0 = cw_ref[0:1, :] * _tap_window(dext, dext_shifted, CONV_WIDTH - 1, c)
            for k in range(1, CONV_WIDTH):
                dc0 = dc0 + cw_ref[k:k + 1, :] * _tap_window(dext, dext_shifted, CONV_WIDTH - 1 - k, c)
            s_gb = _sig(gb_ref[c, :])
            dur_ref[c, SEG:2 * SEG] = (dc0 * s_gb).astype(BF16)
            dur_ref[c, 2 * SEG:3 * SEG] = (dc0 * ga_ref[c, :] * (s_gb * (1.0 - s_gb))).astype(BF16)
        dext[tm:tm + CONV_PAD, :] = dext[0:CONV_PAD, :]

        if n_h:
            @pl.when(r == nt - 1)
            def _():
                _Exchange(h_ins, h_outs, *scratch[n_scratch:]).wait()

    rev = lambda r: nt - 1 - r
    row = pl.BlockSpec((tm, D_MODEL), lambda r: (rev(r), 0))
    halo_spec = [pl.BlockSpec((CONV_PAD, SEG),
                              functools.partial(lambda k, r: (jnp.maximum(rev(r) * halo_per_tile - 1, 0), k), k))
                 for k in (4, 5)]
    const = lambda shape: _resident(shape, lambda r: (0,) * len(shape))
    out = pl.pallas_call(
        body, name="mix_bwd", grid=(nt,),
        in_specs=[row, row, pl.BlockSpec((tm, D_CONV), lambda r: (rev(r), 0))] + _u_specs(tm, rev) + halo_spec + [
            pl.BlockSpec((tm, D_ATT), lambda r: (rev(r), 0)),
            _resident((D_ATT, D_MODEL), lambda r: (0, 0)),
            _resident((D_CONV, D_MODEL), lambda r: (0, 0)),
            _resident((D_MODEL, D_MODEL), lambda r: (0, 0)),
            _resident((CONV_PAD, D_CONV), lambda r: (0, 0)),
            pl.BlockSpec((None, 8, D_CONV), lambda r: (layer, 0, 0)),
            pl.BlockSpec((None, 8, D_MODEL), lambda r: (layer, 0, 0))] + h_in,
        out_specs=[pl.BlockSpec((tm, D_ATT), lambda r: (rev(r), 0)),
                   pl.BlockSpec((tm, 8 * SEG), lambda r: (rev(r), 0)),
                   row,
                   const((D_ATT, D_MODEL)), const((D_CONV, D_MODEL)), const((D_MODEL, D_MODEL)),
                   const((CONV_PAD, D_CONV)), const((8, D_CONV)), const((8, D_MODEL))] + h_out,
        out_shape=[jax.ShapeDtypeStruct((T, D_ATT), F32), jax.ShapeDtypeStruct((T, 8 * SEG), BF16),
                   jax.ShapeDtypeStruct((T, D_MODEL), F32),
                   jax.ShapeDtypeStruct((D_ATT, D_MODEL), F32), jax.ShapeDtypeStruct((D_CONV, D_MODEL), F32),
                   jax.ShapeDtypeStruct((D_MODEL, D_MODEL), F32), jax.ShapeDtypeStruct((CONV_PAD, D_CONV), F32),
                   jax.ShapeDtypeStruct((8, D_CONV), F32), jax.ShapeDtypeStruct((8, D_MODEL), F32)] + h_shape,
        scratch_shapes=[pltpu.VMEM((tm + CONV_PAD, D_CONV), F32), pltpu.VMEM((tm + CONV_PAD, D_CONV), F32),
                        _shifted_scratch(tm), _shifted_scratch(tm),
                        pltpu.VMEM((tm, D_ATT), BF16), pltpu.VMEM((tm, D_CONV), BF16)]
        + [pltpu.VMEM((tm, D_MODEL), BF16)] * 4
        + [pltpu.VMEM((tm, D_CONV), F32)] * 3
        + [pltpu.VMEM((tm, D_MODEL), F32)] * 5
        + [pltpu.VMEM((tm, 1), F32)]
        + h_sems,
        compiler_params=_params("arbitrary"),
    )(dxo, h, c1, *([u] * 10), att, wap, wcp, wout, cw, cvec, dvec, *hosted)
    return out[:9], out[9:]


def _loss_head(y, target):
    T = y.shape[0]
    tm = min(T, 512)

    def body(y_ref, t_ref, dy_ref, loss_ref):
        @pl.when(pl.program_id(0) == 0)
        def _():
            loss_ref[...] = jnp.zeros_like(loss_ref)

        err = y_ref[...] - t_ref[...]
        dy_ref[...] = err * (1.0 / D_MODEL)
        loss_ref[...] += 0.5 * jnp.sum(jnp.mean(err * err, axis=-1, keepdims=True))

    row = pl.BlockSpec((tm, D_MODEL), lambda i: (i, 0))
    return pl.pallas_call(
        body, name="loss_head", grid=(T // tm,), in_specs=[row, row],
        out_specs=[row, pl.BlockSpec((8, 128), lambda i: (0, 0))],
        out_shape=[jax.ShapeDtypeStruct((T, D_MODEL), F32), jax.ShapeDtypeStruct((8, 128), F32)],
        compiler_params=_params("arbitrary"),
    )(y, target)


def _adamw(w, g, m, v):
    m = ADAM_B1 * m + (1.0 - ADAM_B1) * g
    v = ADAM_B2 * v + (1.0 - ADAM_B2) * (g * g)
    m_hat = m / (1.0 - ADAM_B1 ** ADAM_STEP)
    v_hat = v / (1.0 - ADAM_B2 ** ADAM_STEP)
    return -ADAM_LR * (m_hat / (jnp.sqrt(v_hat) + ADAM_EPS) + ADAM_WD * w), m, v


def _sum_slots(ref):
    total = ref[0].astype(F32)
    for j in range(1, N_DEV):
        total = total + ref[j].astype(F32)
    return total


def _adamw_sharded(recv, w, m, v, prev, layer, tr, name):
    n_layers, R, C = w.shape

    def body(*refs):
        recv_ref, w_ref, m_ref, v_ref = refs[:4]
        g_ref, d_ref, nm_ref, nv_ref = refs[-4:]
        g = _sum_slots(recv_ref)
        g_ref[...] = g
        d_ref[...], nm_ref[...], nv_ref[...] = _adamw(w_ref[...], g, m_ref[...], v_ref[...])

    slab = pl.BlockSpec((None, tr, C), lambda i: (layer, i, 0))
    n_prev = 0 if prev is None else 4
    return pl.pallas_call(
        body, name=name, grid=(R // tr,),
        in_specs=[pl.BlockSpec((N_DEV, tr, C), lambda i: (0, i, 0)), slab, slab, slab] + [ANY] * n_prev,
        out_specs=[slab] * 4,
        out_shape=[jax.ShapeDtypeStruct(w.shape, F32)] * 4,
        input_output_aliases={4 + k: k for k in range(n_prev)},
        compiler_params=_params("parallel"),
    )(recv, w, m, v, *(prev or ()))


def _sum8(g):
    R = g.shape[1]

    def body(g_ref, o_ref):
        o_ref[...] = _sum_slots(g_ref)

    return pl.pallas_call(body, name="sum_small_grads", out_shape=jax.ShapeDtypeStruct((R, 128), F32))(g)


def _adamw_small(w, g, m, v):
    def body(w_ref, g_ref, m_ref, v_ref, d_ref, nm_ref, nv_ref):
        d_ref[...], nm_ref[...], nv_ref[...] = _adamw(w_ref[...], g_ref[...], m_ref[...], v_ref[...])

    return pl.pallas_call(body, name="adamw_small", out_shape=[jax.ShapeDtypeStruct(w.shape, F32)] * 3)(w, g, m, v)


REPLICATED = (("b_in", D_IN), ("conv_b", D_CONV), ("conv_ln_g", D_CONV), ("conv_ln_b", D_CONV),
              ("b_conv_proj", D_MODEL), ("ln_g", D_MODEL), ("ln_b", D_MODEL))
N_REPLICATED = sum(n for _, n in REPLICATED)
CONV_W_SHARD_PAD = 2048


def _pack_small(rep, conv_w_shard):
    flat = jnp.concatenate([rep[k] for k, _ in REPLICATED], axis=1)
    cws = jnp.pad(conv_w_shard.reshape(DEPTH, -1), ((0, 0), (0, CONV_W_SHARD_PAD - CONV_WIDTH * CONV_W_SHARD)))
    return jnp.concatenate([flat, cws], axis=1).reshape(-1, LANES)


def _unpack_small(packed):
    per = packed.reshape(DEPTH, N_REPLICATED + CONV_W_SHARD_PAD)
    out, off = {}, 0
    for k, n in REPLICATED:
        out[k] = per[:, off:off + n]
        off += n
    out["conv_w"] = per[:, off:off + CONV_WIDTH * CONV_W_SHARD].reshape(DEPTH, CONV_WIDTH, CONV_W_SHARD)
    return out


def kernel(x, w_in, b_in, conv_w, conv_b, conv_ln_g, conv_ln_b, w_att_proj, w_conv_proj, b_conv_proj, w_out, ln_g, ln_b, loss_target, m_w_in, m_b_in, m_conv_w, m_conv_b, m_conv_ln_g, m_conv_ln_b, m_w_att_proj, m_w_conv_proj, m_b_conv_proj, m_w_out, m_ln_g, m_ln_b, v_w_in, v_b_in, v_conv_w, v_conv_b, v_conv_ln_g, v_conv_ln_b, v_w_att_proj, v_w_conv_proj, v_b_conv_proj, v_w_out, v_ln_g, v_ln_b):
    x0 = x[0]
    target = loss_target[0]

    def shards(l):
        return [w_in[l].astype(BF16), w_att_proj[l].astype(BF16), w_conv_proj[l].astype(BF16),
                w_out[l].astype(BF16), conv_w[l]]

    def full_weights(g_in, g_ap, g_cp, g_out, g_cw):
        return (g_in.transpose(1, 0, 2).reshape(D_MODEL, D_IN), g_ap.transpose(1, 0, 2).reshape(D_ATT, D_MODEL),
                g_cp.transpose(1, 0, 2).reshape(D_CONV, D_MODEL), g_out.reshape(D_MODEL, D_MODEL),
                jnp.pad(g_cw.transpose(1, 0, 2).reshape(CONV_WIDTH, D_CONV), ((0, 1), (0, 0))))

    pad8 = lambda rows: jnp.pad(jnp.stack(rows, axis=1), ((0, 0), (0, 8 - len(rows)), (0, 0)))
    cvec = pad8([conv_b, conv_ln_g, conv_ln_b])
    dvec = pad8([b_conv_proj, ln_g, ln_b])
    b_in3 = b_in.reshape(DEPTH, 1, D_IN)

    weights = [full_weights(*_all_gather(shards(0), "gather_weights"))]
    xs, xbs, us, atts, hs, c1s = [x0], [x0.astype(BF16)], [], [], [], []
    for l in range(DEPTH):
        w_in_l, wap, wcp, wout, cw = weights[l]
        u = _in_proj(xbs[l], w_in_l, b_in3, l)
        att, gathered = _attn_fwd(u, gather=shards(l + 1) if l + 1 < DEPTH else ())
        if gathered:
            weights.append(full_weights(*gathered))
        xn, xnb, h, c1 = _mix_fwd(xs[l], u, att, wap, wcp, wout, cw, cvec, dvec, l)
        us.append(u), atts.append(att), hs.append(h), c1s.append(c1), xs.append(xn), xbs.append(xnb)
    dxo, loss_part = _loss_head(xs[DEPTH], target)
    loss = lax.psum(loss_part[0, 0], ("x", "y", "c"))

    big = {"w_in": (w_in, m_w_in, v_w_in, 256), "w_att_proj": (w_att_proj, m_w_att_proj, v_w_att_proj, D_ATT),
           "w_conv_proj": (w_conv_proj, m_w_conv_proj, v_w_conv_proj, D_CONV), "w_out": (w_out, m_w_out, v_w_out, 128)}
    results = {k: None for k in big}
    small = [None] * DEPTH

    def update(k, recv, l):
        w, m, v, tr = big[k]
        results[k] = _adamw_sharded(recv, w, m, v, results[k], l, tr, "adamw_" + k)

    pending_w_in = ()
    for l in reversed(range(DEPTH)):
        w_in_l, wap, wcp, wout, cw = weights[l]
        (datt, du_rest, dres, dwap, dwcp, dwout, dcw, dcvec, ddvec), got = _mix_bwd(
            dxo, hs[l], c1s[l], us[l], atts[l], wap, wcp, wout, cw, cvec, dvec, l, hosted=pending_w_in)
        if got:
            update("w_in", got[0], l + 1)
        proj = [dwap.reshape(D_ATT, N_DEV, D_MODEL // N_DEV).transpose(1, 0, 2).astype(BF16),
                dwcp.reshape(D_CONV, N_DEV, D_MODEL // N_DEV).transpose(1, 0, 2).astype(BF16),
                dwout.reshape(N_DEV, D_MODEL // N_DEV, D_MODEL).astype(BF16)]
        dqkv, got = _attn_bwd(us[l], atts[l], datt, hosted=proj)
        for k, r in zip(("w_att_proj", "w_conv_proj", "w_out"), got):
            update(k, r, l)
        dwin, dbin = _in_proj_dw(xbs[l], dqkv, du_rest)
        small[l] = jnp.concatenate([dbin.reshape(-1), dcvec[:3].reshape(-1), ddvec[:3].reshape(-1), dcw.reshape(-1)])
        pending_w_in = (dwin.reshape(D_MODEL, N_DEV, W_IN_SHARD).transpose(1, 0, 2).astype(BF16),)
        dxo, got = _in_proj_dx(dqkv, du_rest, w_in_l, dres, hosted=pending_w_in if l == 0 else ())
        if got:
            update("w_in", got[0], l)
    grad_x = dxo[None]

    small_all = jnp.stack(small).reshape(-1, LANES)
    (gathered,) = _all_gather([small_all], "gather_small_grads")
    per_layer = _sum8(gathered).reshape(DEPTH, -1)
    g_rep, off = {}, 0
    for k, n in (("b_in", D_IN), ("conv_b", D_CONV), ("conv_ln_g", D_CONV), ("conv_ln_b", D_CONV),
                 ("b_conv_proj", D_MODEL), ("ln_g", D_MODEL), ("ln_b", D_MODEL)):
        g_rep[k] = per_layer[:, off:off + n]
        off += n
    g_cw_full = per_layer[:, off:].reshape(DEPTH, CONV_PAD, D_CONV)[:, :CONV_WIDTH]
    g_cw_shard = lax.dynamic_slice_in_dim(g_cw_full, _my_index() * CONV_W_SHARD, CONV_W_SHARD, axis=2)
    rep_w = dict(b_in=b_in, conv_b=conv_b, conv_ln_g=conv_ln_g, conv_ln_b=conv_ln_b, b_conv_proj=b_conv_proj,
                 ln_g=ln_g, ln_b=ln_b)
    rep_m = dict(b_in=m_b_in, conv_b=m_conv_b, conv_ln_g=m_conv_ln_g, conv_ln_b=m_conv_ln_b,
                 b_conv_proj=m_b_conv_proj, ln_g=m_ln_g, ln_b=m_ln_b)
    rep_v = dict(b_in=v_b_in, conv_b=v_conv_b, conv_ln_g=v_conv_ln_g, conv_ln_b=v_conv_ln_b,
                 b_conv_proj=v_b_conv_proj, ln_g=v_ln_g, ln_b=v_ln_b)
    sd, sm, sv = _adamw_small(_pack_small(rep_w, conv_w), _pack_small(g_rep, g_cw_shard),
                              _pack_small(rep_m, m_conv_w), _pack_small(rep_v, v_conv_w))
    small_out = [dict(g_rep, conv_w=g_cw_shard), _unpack_small(sd), _unpack_small(sm), _unpack_small(sv)]

    order = ["w_in", "b_in", "conv_w", "conv_b", "conv_ln_g", "conv_ln_b", "w_att_proj", "w_conv_proj",
             "b_conv_proj", "w_out", "ln_g", "ln_b"]
    outs = [loss, grad_x]
    for kind in range(4):
        outs += [results[k][kind] if k in big else small_out[kind][k] for k in order]
    return tuple(outs)
```

```python
import functools

import jax
import jax.numpy as jnp
from jax import lax
from jax.experimental import pallas as pl
from jax.experimental.pallas import tpu as pltpu

F32, BF16 = jnp.float32, jnp.bfloat16

D_MODEL = 1024
D_ATT = 512
D_CONV = 512
HEAD_DIM = 64
PAIR = 2 * HEAD_DIM
CONV_WIDTH = 31
CONV_PAD = 32
DEPTH = 4
N_DEV = 8
SEG = 512
D_IN = 4 * D_ATT + 3 * D_CONV + 2 * D_MODEL
N_SEG = D_IN // SEG
N_QKV_SEG = 3
W_IN_SHARD = D_IN // N_DEV
CONV_W_SHARD = D_CONV // N_DEV
LANES = 128
LN_EPS = 1e-5
DEEPNORM_ALPHA = (2 * DEPTH) ** 0.25
ATT_SCALE = HEAD_DIM ** -0.5
ATT_BLOCK = 256
ADAM_LR, ADAM_B1, ADAM_B2, ADAM_EPS, ADAM_WD, ADAM_STEP = 0.001, 0.9, 0.999, 1e-08, 0.01, 10
VMEM_LIMIT = 56 * 1024 * 1024
MESH = pl.DeviceIdType.MESH
ANY = pl.BlockSpec(memory_space=pl.ANY)


def _resident(shape, index_map):
    return pl.BlockSpec(shape, index_map, pipeline_mode=pl.Buffered(1))


def _params(*semantics):
    return pltpu.CompilerParams(dimension_semantics=semantics, vmem_limit_bytes=VMEM_LIMIT)


def _sig(x):
    return 1.0 / (1.0 + jnp.exp(-x))


def _dot(a, b):
    return jnp.dot(a, b, preferred_element_type=F32)


def _dot_nt(a, b):
    return lax.dot_general(a, b, (((1,), (1,)), ((), ())), preferred_element_type=F32)


def _dot_tn(a, b):
    return lax.dot_general(a, b, (((0,), (0,)), ((), ())), preferred_element_type=F32)


def _my_index():
    return 4 * lax.axis_index("x") + 2 * lax.axis_index("y") + lax.axis_index("c")


class _Gather:
    def __init__(self, ins, outs, send_sems, recv_sems, local_sems):
        self.n = n = len(ins)
        x, y, c = lax.axis_index("x"), lax.axis_index("y"), lax.axis_index("c")
        me, sibling = (x, y, c), (x, y, 1 - c)
        chips = [(1 - x, y), (x, 1 - y), (1 - x, 1 - y)]

        def slot(a, dev):
            return outs[a].at[4 * dev[0] + 2 * dev[1] + dev[2]]

        def copy(a, k, block, to, src=None):
            return pltpu.make_async_remote_copy(
                src_ref=slot(a, block) if src is None else src, dst_ref=slot(a, block),
                send_sem=send_sems.at[7 * a + k], recv_sem=recv_sems.at[7 * a + k], device_id=to, device_id_type=MESH)

        self.mine = [pltpu.make_async_copy(ins[a], slot(a, me), local_sems.at[a]) for a in range(n)]
        self.first, self.landed, self.passed, self.last = [], [], [], []
        for a in range(n):
            self.first.append(copy(a, 0, me, sibling, src=ins[a]))
            self.first += [copy(a, 1 + j, me, (*chip, c), src=ins[a]) for j, chip in enumerate(chips)]
        for j, chip in enumerate(chips):
            for a in range(n):
                self.landed.append(copy(a, 1 + j, (*chip, c), me))
                self.passed.append(copy(a, 4 + j, (*chip, c), sibling))
        for a in range(n):
            self.last.append(copy(a, 0, sibling, me))
            self.last += [copy(a, 4 + j, (*chip, 1 - c), me) for j, chip in enumerate(chips)]

    @staticmethod
    def semaphores(n):
        return [pltpu.SemaphoreType.DMA((7 * n,)), pltpu.SemaphoreType.DMA((7 * n,)), pltpu.SemaphoreType.DMA((n,))]

    def start(self):
        for cp in self.mine + self.first:
            cp.start()

    def forward(self):
        for landed, passed in zip(self.landed, self.passed):
            landed.wait_recv()
            passed.start()

    def wait(self):
        for cp in self.last:
            cp.wait_recv()
        for cp in self.first + self.passed:
            cp.wait_send()
        for cp in self.mine:
            cp.wait()


def _gathered_shapes(shards):
    return [jax.ShapeDtypeStruct((N_DEV,) + s.shape, s.dtype) for s in shards]


def _all_gather(shards, name):
    n = len(shards)

    def body(*refs):
        g = _Gather(refs[:n], refs[n:2 * n], *refs[2 * n:])
        g.start()
        g.forward()
        g.wait()

    return pl.pallas_call(
        body, name=name, out_shape=_gathered_shapes(shards),
        in_specs=[ANY] * n, out_specs=[ANY] * n, scratch_shapes=_Gather.semaphores(n),
    )(*shards)


class _Exchange:
    def __init__(self, ins, outs, send_sems, recv_sems, local_sems):
        n = len(ins)
        x, y, c = lax.axis_index("x"), lax.axis_index("y"), lax.axis_index("c")
        me_idx = 4 * x + 2 * y + c
        self.mine = [pltpu.make_async_copy(ins[a].at[me_idx], outs[a].at[me_idx], local_sems.at[a])
                     for a in range(n)]
        self.sends, self.recvs = [], []
        for d in (1, 2, 4, 3, 5, 6, 7):
            px = 1 - x if d & 4 else x
            py = 1 - y if d & 2 else y
            pc = 1 - c if d & 1 else c
            idx = 4 * px + 2 * py + pc
            for a in range(n):
                sems = dict(send_sem=send_sems.at[7 * a + d - 1], recv_sem=recv_sems.at[7 * a + d - 1],
                            device_id=(px, py, pc), device_id_type=MESH)
                self.sends.append(pltpu.make_async_remote_copy(
                    src_ref=ins[a].at[idx], dst_ref=outs[a].at[me_idx], **sems))
                self.recvs.append(pltpu.make_async_remote_copy(
                    src_ref=ins[a].at[idx], dst_ref=outs[a].at[idx], **sems))

    @staticmethod
    def semaphores(n):
        return [pltpu.SemaphoreType.DMA((7 * n,)), pltpu.SemaphoreType.DMA((7 * n,)), pltpu.SemaphoreType.DMA((n,))]

    def start(self):
        for cp in self.mine + self.sends:
            cp.start()

    def wait(self):
        for cp in self.recvs:
            cp.wait_recv()
        for cp in self.sends:
            cp.wait_send()
        for cp in self.mine:
            cp.wait()


def _host(hosted):
    n = len(hosted)
    return ([ANY] * n, [ANY] * n, [jax.ShapeDtypeStruct(p.shape, p.dtype) for p in hosted],
            _Exchange.semaphores(n) if n else [])


def _split_refs(refs, *counts):
    out, at = [], 0
    for k in counts:
        out.append(refs[at:at + k])
        at += k
    return out + [refs[at:]]


def _in_proj(xb, w_in, b_in, layer):
    T = xb.shape[0]
    tm, tn = min(T, 2048), SEG

    def body(x_ref, w_ref, b_ref, u_ref):
        u_ref[...] = _dot(x_ref[...], w_ref[...]) + b_ref[...]

    return pl.pallas_call(
        body, name="in_proj", grid=(T // tm, D_IN // tn),
        in_specs=[pl.BlockSpec((tm, D_MODEL), lambda m, n: (m, 0)),
                  pl.BlockSpec((D_MODEL, tn), lambda m, n: (0, n)),
                  pl.BlockSpec((None, 1, tn), lambda m, n: (layer, 0, n))],
        out_specs=pl.BlockSpec((tm, tn), lambda m, n: (m, n)),
        out_shape=jax.ShapeDtypeStruct((T, D_IN), F32),
        compiler_params=_params("parallel", "parallel"),
    )(xb, w_in, b_in)


def _du_specs(tm, row_of, seg_of):
    return [pl.BlockSpec((None, tm, SEG), lambda *g: (jnp.minimum(seg_of(*g), N_QKV_SEG - 1), row_of(*g), 0)),
            pl.BlockSpec((tm, SEG), lambda *g: (row_of(*g), jnp.maximum(seg_of(*g) - N_QKV_SEG, 0)))]


def _in_proj_dx(dqkv, du_rest, w_in, dres, hosted=()):
    T = du_rest.shape[0]
    tm = min(T, 512)
    nm = T // tm
    n_h = len(hosted)
    h_in, h_out, h_shape, h_sems = _host(hosted)
    n_rest = du_rest.shape[1]

    def body(*refs):
        (q_ref, k_ref, v_ref, a_ref, w_ref, r_ref), h_ins, (o_ref,), h_outs, sems = _split_refs(refs, 6, n_h, 1, n_h)
        m = pl.program_id(0)

        if n_h:
            @pl.when(m == 0)
            def _():
                _Exchange(h_ins, h_outs, *sems).start()

        acc = r_ref[...] + _dot_nt(a_ref[...], w_ref[:, N_QKV_SEG * SEG:N_QKV_SEG * SEG + n_rest])
        for k, seg_ref in enumerate((q_ref, k_ref, v_ref)):
            acc = acc + _dot_nt(seg_ref[...], w_ref[:, k * SEG:(k + 1) * SEG])
        o_ref[...] = acc

        if n_h:
            @pl.when(m == nm - 1)
            def _():
                _Exchange(h_ins, h_outs, *sems).wait()

    out = pl.pallas_call(
        body, name="in_proj_dx", grid=(nm,),
        in_specs=[pl.BlockSpec((None, tm, SEG), functools.partial(lambda k, m: (k, m, 0), k))
                  for k in range(N_QKV_SEG)] + [
            pl.BlockSpec((tm, n_rest), lambda m: (m, 0)),
            _resident((D_MODEL, D_IN), lambda m: (0, 0)),
            pl.BlockSpec((tm, D_MODEL), lambda m: (m, 0))] + h_in,
        out_specs=[pl.BlockSpec((tm, D_MODEL), lambda m: (m, 0))] + h_out,
        out_shape=[jax.ShapeDtypeStruct((T, D_MODEL), F32)] + h_shape,
        scratch_shapes=h_sems,
        compiler_params=_params("arbitrary"),
    )(dqkv, dqkv, dqkv, du_rest, w_in, dres, *hosted)
    return out[0], out[1:]


def _in_proj_dw(xb, dqkv, du_rest):
    T = xb.shape[0]
    tk = min(T, 1024)
    nt = T // tk

    def body(x_ref, a1_ref, a2_ref, dw_ref, db_ref, acc_ref):
        s, t = pl.program_id(0), pl.program_id(1)

        @pl.when(t == 0)
        def _():
            acc_ref[...] = jnp.zeros_like(acc_ref)
            db_ref[...] = jnp.zeros_like(db_ref)

        def acc(a):
            acc_ref[...] += _dot_tn(x_ref[...], a)
            db_ref[...] += jnp.sum(a.astype(F32), axis=0, keepdims=True)

        @pl.when(s < N_QKV_SEG)
        def _():
            acc(a1_ref[...])

        @pl.when(s >= N_QKV_SEG)
        def _():
            acc(a2_ref[...])

        @pl.when(t == nt - 1)
        def _():
            dw_ref[...] = acc_ref[...].astype(BF16)

    return pl.pallas_call(
        body, name="in_proj_dw", grid=(N_SEG, nt),
        in_specs=[pl.BlockSpec((tk, D_MODEL), lambda s, t: (t, 0))] + _du_specs(tk, lambda s, t: t, lambda s, t: s),
        out_specs=[pl.BlockSpec((D_MODEL, SEG), lambda s, t: (0, s)),
                   pl.BlockSpec((None, 1, SEG), lambda s, t: (s, 0, 0))],
        out_shape=[jax.ShapeDtypeStruct((D_MODEL, D_IN), BF16), jax.ShapeDtypeStruct((N_SEG, 1, SEG), F32)],
        scratch_shapes=[pltpu.VMEM((D_MODEL, SEG), F32)],
        compiler_params=_params("parallel", "arbitrary"),
    )(xb, dqkv, du_rest)


ATT_CHUNK = 32
ATT_Q_PER_STEP = 2


def _chunks():
    return [slice(r, r + ATT_CHUNK) for r in range(0, ATT_BLOCK, ATT_CHUNK)]


def _twice(rhs):
    return jnp.concatenate([rhs, rhs], axis=0)


def _fail_and_log_beta(z):
    t = jnp.maximum(z, 0.0) + jnp.log(1.0 + jnp.exp(-jnp.abs(z)))
    return t, z - t


def _store_split(dst, rows, val):
    hi = val.astype(BF16)
    dst[rows, 0:ATT_BLOCK] = hi
    dst[rows, ATT_BLOCK:2 * ATT_BLOCK] = (val - hi.astype(F32)).astype(BF16)


def _fill_causal(keep):
    row = lax.broadcasted_iota(jnp.int32, (ATT_BLOCK, ATT_BLOCK), 0)
    col = lax.broadcasted_iota(jnp.int32, (ATT_BLOCK, ATT_BLOCK), 1)
    keep[...] = (col < row).astype(F32)


def _triangles():
    row = lax.broadcasted_iota(jnp.int32, (ATT_BLOCK, ATT_BLOCK), 0)
    col = lax.broadcasted_iota(jnp.int32, (ATT_BLOCK, ATT_BLOCK), 1)
    return _twice((row > col).astype(BF16)), _twice((row >= col).astype(BF16))


def _staggered(stages):
    stages[0](0)
    stages[0](1)
    for k in range(1, len(stages), 2):
        for h in range(2):
            stages[k](h)
            stages[k + 1](h)


EXP_UNDERFLOW = 104.0


def _sweep_keys(i, tile, rsum):
    tile(i, True)

    def live(carry):
        j, low = carry
        return jnp.logical_and(j >= 0, low < EXP_UNDERFLOW)

    def step(carry):
        j, _ = carry
        tile(j, False)
        return j - 1, jnp.min(rsum[...])

    lax.while_loop(live, step, (i - 1, jnp.min(rsum[...])))


def _attn_fwd(u, gather=()):
    T = u.shape[0]
    B = ATT_BLOCK
    per_step = ATT_Q_PER_STEP
    ns = T // (B * per_step)
    n_pairs = D_ATT // PAIR
    n_g = len(gather)

    def body(*refs):
        (q_ref, k_ref, v_ref), g_ins, (o_ref,), g_outs, scratch = _split_refs(refs, 3, n_g, 1, n_g)
        kb, vb, acc, rsum, t0, zs, lt, cat, keep = scratch[:9]
        p, s = pl.program_id(0), pl.program_id(1)

        if n_g:
            @pl.when(jnp.logical_and(p == 0, s == 0))
            def _():
                _Gather(g_ins, g_outs, *scratch[9:]).start()

            @pl.when(jnp.logical_and(p == n_pairs - 1, s == 0))
            def _():
                _Gather(g_ins, g_outs, *scratch[9:]).forward()

        @pl.when(s == 0)
        def _():
            _fill_causal(keep)
            for h in range(2):
                kb[h] = k_ref[:, HEAD_DIM * h:HEAD_DIM * (h + 1)].astype(BF16)
                vb[h] = v_ref[:, HEAD_DIM * h:HEAD_DIM * (h + 1)].astype(BF16)

        tri2, _ = _triangles()
        hs = [slice(HEAD_DIM * h, HEAD_DIM * (h + 1)) for h in range(2)]

        def query_block(sb, carry):
            i = s * per_step + sb
            qrows = pl.ds(pl.multiple_of(sb * B, B), B)
            q = [(q_ref[qrows, hs[h]] * ATT_SCALE).astype(BF16) for h in range(2)]
            acc[...] = jnp.zeros_like(acc)
            rsum[...] = jnp.zeros_like(rsum)
            _sweep_keys(i, functools.partial(tile, q), rsum)
            for h in range(2):
                o_ref[qrows, hs[h]] = acc[h]
            return carry

        def tile(q, j, masked):
            keys = pl.ds(pl.multiple_of(j * B, B), B)

            def scores(h):
                zs[h] = _dot_nt(q[h], kb[h, keys, :])

            def fails(h):
                for r in _chunks():
                    t, a = _fail_and_log_beta(zs[h, r, :])
                    if masked:
                        t = t * keep[r, :]
                    _store_split(cat.at[h], r, t)
                    zs[h, r, :] = a
                    t0[h, r, :] = t[:, 0:1]

            def later_sums(h):
                lt[h] = _dot(cat[h], tri2)

            def weights(h):
                for r in _chunks():
                    later = lt[h, r, :]
                    w = jnp.exp(zs[h, r, :] - later - rsum[h, r, :])
                    if masked:
                        w = w * keep[r, :]
                    _store_split(cat.at[h], r, w)
                    rsum[h, r, :] += later[:, 0:1] + t0[h, r, :]

            def values(h):
                acc[h] += _dot(cat[h], _twice(vb[h, keys, :]))

            _staggered([scores, fails, later_sums, weights, values])

        lax.fori_loop(0, per_step, query_block, 0)

        if n_g:
            @pl.when(jnp.logical_and(p == n_pairs - 1, s == ns - 1))
            def _():
                _Gather(g_ins, g_outs, *scratch[9:]).wait()

    out = pl.pallas_call(
        body, name="attn_fwd", grid=(n_pairs, ns),
        in_specs=[pl.BlockSpec((per_step * B, PAIR), lambda p, s: (s, p)),
                  pl.BlockSpec((T, PAIR), lambda p, s: (0, 4 + p)),
                  pl.BlockSpec((T, PAIR), lambda p, s: (0, 8 + p))] + [ANY] * n_g,
        out_specs=[pl.BlockSpec((per_step * B, PAIR), lambda p, s: (s, p))] + [ANY] * n_g,
        out_shape=[jax.ShapeDtypeStruct((T, D_ATT), F32)] + _gathered_shapes(gather),
        scratch_shapes=[pltpu.VMEM((2, T, HEAD_DIM), BF16), pltpu.VMEM((2, T, HEAD_DIM), BF16),
                        pltpu.VMEM((2, B, HEAD_DIM), F32), pltpu.VMEM((2, B, 1), F32), pltpu.VMEM((2, B, 1), F32),
                        pltpu.VMEM((2, B, B), F32), pltpu.VMEM((2, B, B), F32), pltpu.VMEM((2, B, 2 * B), BF16),
                        pltpu.VMEM((B, B), F32)]
        + (_Gather.semaphores(n_g) if n_g else []),
        compiler_params=_params("arbitrary", "arbitrary"),
    )(u, u, u, *gather)
    return out[0], out[1:]


def _attn_bwd(u, att, datt, hosted=()):
    T = u.shape[0]
    B = ATT_BLOCK
    per_step = ATT_Q_PER_STEP
    ns = T // (B * per_step)
    n_pairs = D_ATT // PAIR
    n_h = len(hosted)
    h_in, h_out, h_shape, h_sems = _host(hosted)

    def body(*refs):
        (q_ref, k_ref, v_ref, o_ref, do_ref), h_ins, (dqkv_ref,), h_outs, scratch = _split_refs(refs, 5, n_h, 1, n_h)
        n_scratch = 17
        (kb, vb, dkacc, dvacc, dqacc, rsum, gsum, total, t0, lt0, zs, gs, lt, cat, wb, dzb,
         keep) = scratch[:n_scratch]
        p, s = pl.program_id(0), pl.program_id(1)

        if n_h:
            @pl.when(jnp.logical_and(p == 0, s == 0))
            def _():
                _Exchange(h_ins, h_outs, *scratch[n_scratch:]).start()

        @pl.when(s == 0)
        def _():
            _fill_causal(keep)
            for h in range(2):
                kb[h] = k_ref[:, HEAD_DIM * h:HEAD_DIM * (h + 1)].astype(BF16)
                vb[h] = v_ref[:, HEAD_DIM * h:HEAD_DIM * (h + 1)].astype(BF16)
            dkacc[...] = jnp.zeros_like(dkacc)
            dvacc[...] = jnp.zeros_like(dvacc)

        tri2, tri_incl2 = _triangles()
        hs = [slice(HEAD_DIM * h, HEAD_DIM * (h + 1)) for h in range(2)]
        both = lambda acc2: jnp.concatenate([acc2[0], acc2[1]], axis=1)

        def query_block(sb, carry):
            i = s * per_step + sb
            local = pl.ds(pl.multiple_of(sb * B, B), B)
            q = [(q_ref[local, hs[h]] * ATT_SCALE).astype(BF16) for h in range(2)]
            dob = [do_ref[local, hs[h]].astype(BF16) for h in range(2)]
            for h in range(2):
                total[h] = jnp.sum(dob[h].astype(F32) * o_ref[local, hs[h]], axis=1, keepdims=True)
            dqacc[...] = jnp.zeros_like(dqacc)
            rsum[...] = jnp.zeros_like(rsum)
            gsum[...] = jnp.zeros_like(gsum)
            _sweep_keys(i, functools.partial(tile, q, dob), rsum)
            dqkv_ref[0, pl.ds(pl.multiple_of(i * B, B), B), :] = (both(dqacc) * ATT_SCALE).astype(BF16)
            return carry

        def tile(q, dob, j, masked):
            keys = pl.ds(pl.multiple_of(j * B, B), B)

            def scores(h):
                zs[h] = _dot_nt(q[h], kb[h, keys, :])
                gs[h] = _dot_nt(dob[h], vb[h, keys, :])

            def fails(h):
                for r in _chunks():
                    t, a = _fail_and_log_beta(zs[h, r, :])
                    if masked:
                        t = t * keep[r, :]
                    _store_split(cat.at[h], r, t)
                    zs[h, r, :] = a
                    t0[h, r, :] = t[:, 0:1]

            def later_sums(h):
                lt[h] = _dot(cat[h], tri2)

            def weights(h):
                for r in _chunks():
                    later = lt[h, r, :]
                    w = jnp.exp(zs[h, r, :] - later - rsum[h, r, :])
                    if masked:
                        w = w * keep[r, :]
                    g = gs[h, r, :] * w
                    gs[h, r, :] = g
                    _store_split(cat.at[h], r, g)
                    wb[h, r, :] = w.astype(BF16)
                    lt0[h, r, :] = later[:, 0:1]

            def suffix_sums(h):
                lt[h] = _dot(cat[h], tri_incl2)

            def score_grads(h):
                for r in _chunks():
                    suffix = lt[h, r, :]
                    g = gs[h, r, :]
                    before = total[h, r, :] - (suffix + gsum[h, r, :])
                    dz = g - jnp.exp(zs[h, r, :]) * (g + before)
                    if masked:
                        dz = dz * keep[r, :]
                    dzb[h, r, :] = dz.astype(BF16)
                    rsum[h, r, :] += lt0[h, r, :] + t0[h, r, :]
                    gsum[h, r, :] += suffix[:, 0:1]

            def input_grads(h):
                dqacc[h] += _dot(dzb[h], kb[h, keys, :])
                dkacc[h, keys, :] += _dot_tn(dzb[h], q[h])
                dvacc[h, keys, :] += _dot_tn(wb[h], dob[h])

            _staggered([scores, fails, later_sums, weights, suffix_sums, score_grads, input_grads])

        lax.fori_loop(0, per_step, query_block, 0)

        @pl.when(s == ns - 1)
        def _():
            dqkv_ref[1] = both(dkacc).astype(BF16)
            dqkv_ref[2] = both(dvacc).astype(BF16)

        if n_h:
            @pl.when(jnp.logical_and(p == n_pairs - 1, s == ns - 1))
            def _():
                _Exchange(h_ins, h_outs, *scratch[n_scratch:]).wait()

    rows_spec = pl.BlockSpec((per_step * B, PAIR), lambda p, s: (s, p))
    out = pl.pallas_call(
        body, name="attn_bwd", grid=(n_pairs, ns),
        in_specs=[rows_spec,
                  pl.BlockSpec((T, PAIR), lambda p, s: (0, 4 + p)),
                  pl.BlockSpec((T, PAIR), lambda p, s: (0, 8 + p)),
                  rows_spec, rows_spec] + h_in,
        out_specs=[pl.BlockSpec((3, T, PAIR), lambda p, s: (0, 0, p))] + h_out,
        out_shape=[jax.ShapeDtypeStruct((3, T, D_ATT), BF16)] + h_shape,
        scratch_shapes=[pltpu.VMEM((2, T, HEAD_DIM), BF16), pltpu.VMEM((2, T, HEAD_DIM), BF16),
                        pltpu.VMEM((2, T, HEAD_DIM), F32), pltpu.VMEM((2, T, HEAD_DIM), F32),
                        pltpu.VMEM((2, B, HEAD_DIM), F32)]
        + [pltpu.VMEM((2, B, 1), F32)] * 5
        + [pltpu.VMEM((2, B, B), F32)] * 3
        + [pltpu.VMEM((2, B, 2 * B), BF16), pltpu.VMEM((2, B, B), BF16), pltpu.VMEM((2, B, B), BF16)]
        + [pltpu.VMEM((B, B), F32)]
        + h_sems,
        compiler_params=_params("arbitrary", "arbitrary"),
    )(u, u, u, att, datt, *hosted)
    return out[0], out[1:]


MIX_TILE = 256
MIX_CHUNK = 16


def _u_specs(tm, tile_of):
    return [pl.BlockSpec((tm, SEG), functools.partial(lambda k, *g: (tile_of(*g), k), k)) for k in range(3, N_SEG)]


SHIFT_ROWS = 24


def _shifted_scratch(tm):
    return pltpu.VMEM((7, tm + SHIFT_ROWS, D_CONV), F32)


def _make_shifted(base, shifted, tm):
    for b in range(1, 8):
        shifted[b - 1] = base[b:b + tm + SHIFT_ROWS, :]


def _tap_window(base, shifted, start, rows):
    whole, b = divmod(start, 8)
    if b == 0:
        return base[start + rows.start:start + rows.stop, :]
    return shifted[b - 1, 8 * whole + rows.start:8 * whole + rows.stop, :]


def _conv(ext, ext_shifted, cw_ref, rows):
    acc = cw_ref[0:1, :] * _tap_window(ext, ext_shifted, 2, rows)
    for k in range(1, CONV_WIDTH):
        acc = acc + cw_ref[k:k + 1, :] * _tap_window(ext, ext_shifted, 2 + k, rows)
    return acc


def _norm_stats(v):
    mu = jnp.mean(v, axis=-1, keepdims=True)
    vc = v - mu
    rstd = lax.rsqrt(jnp.mean(vc * vc, axis=-1, keepdims=True) + LN_EPS)
    return vc * rstd, rstd


def _norm_bwd(dy_scaled, xhat, rstd):
    return rstd * (dy_scaled - jnp.mean(dy_scaled, axis=-1, keepdims=True)
                   - xhat * jnp.mean(dy_scaled * xhat, axis=-1, keepdims=True))


def _mix_fwd(x, u, att, wap, wcp, wout, cw, cvec, dvec, layer):
    T = x.shape[0]
    tm = MIX_TILE

    def body(x_ref, za_ref, ga_ref, gb_ref, zc_ref, ga0_ref, ga1_ref, gc0_ref, gc1_ref, att_ref,
             wap_ref, wcp_ref, wout_ref, cw_ref, cvec_ref, dvec_ref, xn_ref, xnb_ref, h_ref, c1_ref,
             ext, ext_shifted):
        i = pl.program_id(0)

        @pl.when(i == 0)
        def _():
            ext[0:CONV_PAD, :] = jnp.zeros((CONV_PAD, D_CONV), F32)

        za = za_ref[...]
        attg = att_ref[...] * (za * _sig(za))
        ab = _dot(attg.astype(BF16), wap_ref[...])

        ext[CONV_PAD:CONV_PAD + tm, :] = ga_ref[...] * _sig(gb_ref[...])
        _make_shifted(ext, ext_shifted, tm)
        c1 = _conv(ext, ext_shifted, cw_ref, slice(0, tm)) + cvec_ref[0:1, :]
        c1_ref[...] = c1
        ext[0:CONV_PAD, :] = ext[tm:tm + CONV_PAD, :]
        xh, _ = _norm_stats(c1)
        c2 = xh * cvec_ref[1:2, :] + cvec_ref[2:3, :]
        zc = zc_ref[...]
        cz = (c2 * _sig(c2)) * (zc * _sig(zc))
        cb = _dot(cz.astype(BF16), wcp_ref[...]) + dvec_ref[0:1, :]

        g_att = jnp.concatenate([ga0_ref[...], ga1_ref[...]], axis=1)
        g_conv = jnp.concatenate([gc0_ref[...], gc1_ref[...]], axis=1)
        merged = _sig(g_att) * ab + _sig(g_conv) * cb
        h = DEEPNORM_ALPHA * x_ref[...] + _dot(merged.astype(BF16), wout_ref[...])
        h_ref[...] = h
        hh, _ = _norm_stats(h)
        xn = hh * dvec_ref[1:2, :] + dvec_ref[2:3, :]
        xn_ref[...] = xn
        xnb_ref[...] = xn.astype(BF16)

    row = pl.BlockSpec((tm, D_MODEL), lambda i: (i, 0))
    return pl.pallas_call(
        body, name="mix_fwd", grid=(T // tm,),
        in_specs=[row] + _u_specs(tm, lambda i: i) + [
            pl.BlockSpec((tm, D_ATT), lambda i: (i, 0)),
            _resident((D_ATT, D_MODEL), lambda i: (0, 0)),
            _resident((D_CONV, D_MODEL), lambda i: (0, 0)),
            _resident((D_MODEL, D_MODEL), lambda i: (0, 0)),
            _resident((CONV_PAD, D_CONV), lambda i: (0, 0)),
            pl.BlockSpec((None, 8, D_CONV), lambda i: (layer, 0, 0)),
            pl.BlockSpec((None, 8, D_MODEL), lambda i: (layer, 0, 0))],
        out_specs=[row, row, row, pl.BlockSpec((tm, D_CONV), lambda i: (i, 0))],
        out_shape=[jax.ShapeDtypeStruct((T, D_MODEL), F32), jax.ShapeDtypeStruct((T, D_MODEL), BF16),
                   jax.ShapeDtypeStruct((T, D_MODEL), F32), jax.ShapeDtypeStruct((T, D_CONV), F32)],
        scratch_shapes=[pltpu.VMEM((tm + CONV_PAD, D_CONV), F32), _shifted_scratch(tm)],
        compiler_params=_params("arbitrary"),
    )(x, *([u] * 8), att, wap, wcp, wout, cw, cvec, dvec)


def _mix_bwd(dxo, h, c1, u, att, wap, wcp, wout, cw, cvec, dvec, layer, hosted=()):
    T = dxo.shape[0]
    tm = MIX_TILE
    nt = T // tm
    halo_per_tile = tm // CONV_PAD
    n_h = len(hosted)
    h_in, h_out, h_shape, h_sems = _host(hosted)

    def body(*refs):
        ins, h_ins, outs, h_outs, scratch = _split_refs(refs, 20, n_h, 9, n_h)
        (dxo_ref, h_ref, c1_ref, za_ref, ga_ref, gb_ref, zc_ref, ga0_ref, ga1_ref, gc0_ref, gc1_ref, hga_ref, hgb_ref,
         att_ref, wap_ref, wcp_ref, wout_ref, cw_ref, cvec_ref, dvec_ref) = ins
        datt_ref, dur_ref, dres_ref, dwap_ref, dwcp_ref, dwout_ref, dcw_ref, dcvec_ref, ddvec_ref = outs
        n_scratch = 19
        (ext, dext, ext_shifted, dext_shifted, attg_s, cz_s, merged_s, dy_s, dab_s, dcb_s,
         xh_s, dattg_s, dcz_s, ab_s, cb_s, sga_s, sgc_s, dm_s, rstd_s) = scratch[:n_scratch]
        r = pl.program_id(0)
        i = nt - 1 - r

        if n_h:
            @pl.when(r == 0)
            def _():
                _Exchange(h_ins, h_outs, *scratch[n_scratch:]).start()

        @pl.when(r == 0)
        def _():
            for ref in (dwap_ref, dwcp_ref, dwout_ref, dcw_ref, dcvec_ref, ddvec_ref):
                ref[...] = jnp.zeros_like(ref)
            dext[tm:tm + CONV_PAD, :] = jnp.zeros((CONV_PAD, D_CONV), F32)

        chunks = [slice(c, c + MIX_CHUNK) for c in range(0, tm, MIX_CHUNK)]
        whole = slice(0, tm)
        ln_g_c, ln_b_c = cvec_ref[1:2, :], cvec_ref[2:3, :]

        halo = hga_ref[...] * _sig(hgb_ref[...])
        ext[0:CONV_PAD, :] = jnp.where(i > 0, halo, 0.0)
        for c in chunks:
            za = za_ref[c, :]
            attg_s[c, :] = (att_ref[c, :] * (za * _sig(za))).astype(BF16)
            ext[CONV_PAD + c.start:CONV_PAD + c.stop, :] = ga_ref[c, :] * _sig(gb_ref[c, :])
        _make_shifted(ext, ext_shifted, tm)
        for c in chunks:
            xh, rstd_c = _norm_stats(c1_ref[c, :])
            xh_s[c, :] = xh
            rstd_s[c, :] = rstd_c
            c2 = xh * ln_g_c + ln_b_c
            zc = zc_ref[c, :]
            cz_s[c, :] = ((c2 * _sig(c2)) * (zc * _sig(zc))).astype(BF16)
        ab_s[...] = _dot(attg_s[...], wap_ref[...])
        cb_s[...] = _dot(cz_s[...], wcp_ref[...]) + dvec_ref[0:1, :]

        for c in chunks:
            s_ga = _sig(jnp.concatenate([ga0_ref[c, :], ga1_ref[c, :]], axis=1))
            s_gc = _sig(jnp.concatenate([gc0_ref[c, :], gc1_ref[c, :]], axis=1))
            sga_s[c, :] = s_ga
            sgc_s[c, :] = s_gc
            merged_s[c, :] = (s_ga * ab_s[c, :] + s_gc * cb_s[c, :]).astype(BF16)
            dxo_c = dxo_ref[c, :]
            hh, rstd_h = _norm_stats(h_ref[c, :])
            ddvec_ref[1:2, :] += jnp.sum(dxo_c * hh, axis=0, keepdims=True)
            ddvec_ref[2:3, :] += jnp.sum(dxo_c, axis=0, keepdims=True)
            dh = _norm_bwd(dxo_c * dvec_ref[1:2, :], hh, rstd_h)
            dres_ref[c, :] = DEEPNORM_ALPHA * dh
            dy_s[c, :] = dh.astype(BF16)
        dwout_ref[...] += _dot_tn(merged_s[...], dy_s[...])
        dm_s[...] = _dot_nt(dy_s[...], wout_ref[...])

        for c in chunks:
            dm, s_ga, s_gc = dm_s[c, :], sga_s[c, :], sgc_s[c, :]
            dab_s[c, :] = (dm * s_ga).astype(BF16)
            dcb = dm * s_gc
            dcb_s[c, :] = dcb.astype(BF16)
            ddvec_ref[0:1, :] += jnp.sum(dcb, axis=0, keepdims=True)
            dur_ref[c, 4 * SEG:6 * SEG] = (dm * ab_s[c, :] * (s_ga * (1.0 - s_ga))).astype(BF16)
            dur_ref[c, 6 * SEG:8 * SEG] = (dm * cb_s[c, :] * (s_gc * (1.0 - s_gc))).astype(BF16)
        dwap_ref[...] += _dot_tn(attg_s[...], dab_s[...])
        dattg_s[...] = _dot_nt(dab_s[...], wap_ref[...])
        dwcp_ref[...] += _dot_tn(cz_s[...], dcb_s[...])
        dcz_s[...] = _dot_nt(dcb_s[...], wcp_ref[...])

        for c in chunks:
            za = za_ref[c, :]
            s_za = _sig(za)
            dattg = dattg_s[c, :]
            datt_ref[c, :] = dattg * (za * s_za)
            dur_ref[c, 0:SEG] = (dattg * att_ref[c, :] * (s_za * (1.0 + za * (1.0 - s_za)))).astype(BF16)
            xh = xh_s[c, :]
            c2 = xh * ln_g_c + ln_b_c
            s_c2 = _sig(c2)
            zc = zc_ref[c, :]
            s_zc = _sig(zc)
            dcz = dcz_s[c, :]
            dur_ref[c, 3 * SEG:4 * SEG] = (dcz * (c2 * s_c2) * (s_zc * (1.0 + zc * (1.0 - s_zc)))).astype(BF16)
            dc2 = dcz * (zc * s_zc) * (s_c2 * (1.0 + c2 * (1.0 - s_c2)))
            dcvec_ref[1:2, :] += jnp.sum(dc2 * xh, axis=0, keepdims=True)
            dcvec_ref[2:3, :] += jnp.sum(dc2, axis=0, keepdims=True)
            dc1 = _norm_bwd(dc2 * ln_g_c, xh, rstd_s[c, :])
            dcvec_ref[0:1, :] += jnp.sum(dc1, axis=0, keepdims=True)
            dext[c, :] = dc1

        dc1_tile = dext[0:tm, :]
        for k in range(CONV_WIDTH):
            dcw_ref[k:k + 1, :] += jnp.sum(dc1_tile * _tap_window(ext, ext_shifted, 2 + k, whole),
                                           axis=0, keepdims=True)
        _make_shifted(dext, dext_shifted, tm)
        for c in chunks:
            dc0 = cw_ref[0:1, :] * _tap_window(dext, dext_shifted, CONV_WIDTH - 1, c)
            for k in range(1, CONV_WIDTH):
                dc0 = dc0 + cw_ref[k:k + 1, :] * _tap_window(dext, dext_shifted, CONV_WIDTH - 1 - k, c)
            s_gb = _sig(gb_ref[c, :])
            dur_ref[c, SEG:2 * SEG] = (dc0 * s_gb).astype(BF16)
            dur_ref[c, 2 * SEG:3 * SEG] = (dc0 * ga_ref[c, :] * (s_gb * (1.0 - s_gb))).astype(BF16)
        dext[tm:tm + CONV_PAD, :] = dext[0:CONV_PAD, :]

        if n_h:
            @pl.when(r == nt - 1)
            def _():
                _Exchange(h_ins, h_outs, *scratch[n_scratch:]).wait()

    rev = lambda r: nt - 1 - r
    row = pl.BlockSpec((tm, D_MODEL), lambda r: (rev(r), 0))
    halo_spec = [pl.BlockSpec((CONV_PAD, SEG),
                              functools.partial(lambda k, r: (jnp.maximum(rev(r) * halo_per_tile - 1, 0), k), k))
                 for k in (4, 5)]
    const = lambda shape: _resident(shape, lambda r: (0,) * len(shape))
    out = pl.pallas_call(
        body, name="mix_bwd", grid=(nt,),
        in_specs=[row, row, pl.BlockSpec((tm, D_CONV), lambda r: (rev(r), 0))] + _u_specs(tm, rev) + halo_spec + [
            pl.BlockSpec((tm, D_ATT), lambda r: (rev(r), 0)),
            _resident((D_ATT, D_MODEL), lambda r: (0, 0)),
            _resident((D_CONV, D_MODEL), lambda r: (0, 0)),
            _resident((D_MODEL, D_MODEL), lambda r: (0, 0)),
            _resident((CONV_PAD, D_CONV), lambda r: (0, 0)),
            pl.BlockSpec((None, 8, D_CONV), lambda r: (layer, 0, 0)),
            pl.BlockSpec((None, 8, D_MODEL), lambda r: (layer, 0, 0))] + h_in,
        out_specs=[pl.BlockSpec((tm, D_ATT), lambda r: (rev(r), 0)),
                   pl.BlockSpec((tm, 8 * SEG), lambda r: (rev(r), 0)),
                   row,
                   const((D_ATT, D_MODEL)), const((D_CONV, D_MODEL)), const((D_MODEL, D_MODEL)),
                   const((CONV_PAD, D_CONV)), const((8, D_CONV)), const((8, D_MODEL))] + h_out,
        out_shape=[jax.ShapeDtypeStruct((T, D_ATT), F32), jax.ShapeDtypeStruct((T, 8 * SEG), BF16),
                   jax.ShapeDtypeStruct((T, D_MODEL), F32),
                   jax.ShapeDtypeStruct((D_ATT, D_MODEL), F32), jax.ShapeDtypeStruct((D_CONV, D_MODEL), F32),
                   jax.ShapeDtypeStruct((D_MODEL, D_MODEL), F32), jax.ShapeDtypeStruct((CONV_PAD, D_CONV), F32),
                   jax.ShapeDtypeStruct((8, D_CONV), F32), jax.ShapeDtypeStruct((8, D_MODEL), F32)] + h_shape,
        scratch_shapes=[pltpu.VMEM((tm + CONV_PAD, D_CONV), F32), pltpu.VMEM((tm + CONV_PAD, D_CONV), F32),
                        _shifted_scratch(tm), _shifted_scratch(tm),
                        pltpu.VMEM((tm, D_ATT), BF16), pltpu.VMEM((tm, D_CONV), BF16)]
        + [pltpu.VMEM((tm, D_MODEL), BF16)] * 4
        + [pltpu.VMEM((tm, D_CONV), F32)] * 3
        + [pltpu.VMEM((tm, D_MODEL), F32)] * 5
        + [pltpu.VMEM((tm, 1), F32)]
        + h_sems,
        compiler_params=_params("arbitrary"),
    )(dxo, h, c1, *([u] * 10), att, wap, wcp, wout, cw, cvec, dvec, *hosted)
    return out[:9], out[9:]


def _loss_head(y, target):
    T = y.shape[0]
    tm = min(T, 512)

    def body(y_ref, t_ref, dy_ref, loss_ref):
        @pl.when(pl.program_id(0) == 0)
        def _():
            loss_ref[...] = jnp.zeros_like(loss_ref)

        err = y_ref[...] - t_ref[...]
        dy_ref[...] = err * (1.0 / D_MODEL)
        loss_ref[...] += 0.5 * jnp.sum(jnp.mean(err * err, axis=-1, keepdims=True))

    row = pl.BlockSpec((tm, D_MODEL), lambda i: (i, 0))
    return pl.pallas_call(
        body, name="loss_head", grid=(T // tm,), in_specs=[row, row],
        out_specs=[row, pl.BlockSpec((8, 128), lambda i: (0, 0))],
        out_shape=[jax.ShapeDtypeStruct((T, D_MODEL), F32), jax.ShapeDtypeStruct((8, 128), F32)],
        compiler_params=_params("arbitrary"),
    )(y, target)


def _adamw(w, g, m, v):
    m = ADAM_B1 * m + (1.0 - ADAM_B1) * g
    v = ADAM_B2 * v + (1.0 - ADAM_B2) * (g * g)
    m_hat = m / (1.0 - ADAM_B1 ** ADAM_STEP)
    v_hat = v / (1.0 - ADAM_B2 ** ADAM_STEP)
    return -ADAM_LR * (m_hat / (jnp.sqrt(v_hat) + ADAM_EPS) + ADAM_WD * w), m, v


def _sum_slots(ref):
    total = ref[0].astype(F32)
    for j in range(1, N_DEV):
        total = total + ref[j].astype(F32)
    return total


def _adamw_sharded(recv, w, m, v, prev, layer, tr, name):
    n_layers, R, C = w.shape

    def body(*refs):
        recv_ref, w_ref, m_ref, v_ref = refs[:4]
        g_ref, d_ref, nm_ref, nv_ref = refs[-4:]
        g = _sum_slots(recv_ref)
        g_ref[...] = g
        d_ref[...], nm_ref[...], nv_ref[...] = _adamw(w_ref[...], g, m_ref[...], v_ref[...])

    slab = pl.BlockSpec((None, tr, C), lambda i: (layer, i, 0))
    n_prev = 0 if prev is None else 4
    return pl.pallas_call(
        body, name=name, grid=(R // tr,),
        in_specs=[pl.BlockSpec((N_DEV, tr, C), lambda i: (0, i, 0)), slab, slab, slab] + [ANY] * n_prev,
        out_specs=[slab] * 4,
        out_shape=[jax.ShapeDtypeStruct(w.shape, F32)] * 4,
        input_output_aliases={4 + k: k for k in range(n_prev)},
        compiler_params=_params("parallel"),
    )(recv, w, m, v, *(prev or ()))


def _sum8(g):
    R = g.shape[1]

    def body(g_ref, o_ref):
        o_ref[...] = _sum_slots(g_ref)

    return pl.pallas_call(body, name="sum_small_grads", out_shape=jax.ShapeDtypeStruct((R, 128), F32))(g)


def _adamw_small(w, g, m, v):
    def body(w_ref, g_ref, m_ref, v_ref, d_ref, nm_ref, nv_ref):
        d_ref[...], nm_ref[...], nv_ref[...] = _adamw(w_ref[...], g_ref[...], m_ref[...], v_ref[...])

    return pl.pallas_call(body, name="adamw_small", out_shape=[jax.ShapeDtypeStruct(w.shape, F32)] * 3)(w, g, m, v)


REPLICATED = (("b_in", D_IN), ("conv_b", D_CONV), ("conv_ln_g", D_CONV), ("conv_ln_b", D_CONV),
              ("b_conv_proj", D_MODEL), ("ln_g", D_MODEL), ("ln_b", D_MODEL))
N_REPLICATED = sum(n for _, n in REPLICATED)
CONV_W_SHARD_PAD = 2048


def _pack_small(rep, conv_w_shard):
    flat = jnp.concatenate([rep[k] for k, _ in REPLICATED], axis=1)
    cws = jnp.pad(conv_w_shard.reshape(DEPTH, -1), ((0, 0), (0, CONV_W_SHARD_PAD - CONV_WIDTH * CONV_W_SHARD)))
    return jnp.concatenate([flat, cws], axis=1).reshape(-1, LANES)


def _unpack_small(packed):
    per = packed.reshape(DEPTH, N_REPLICATED + CONV_W_SHARD_PAD)
    out, off = {}, 0
    for k, n in REPLICATED:
        out[k] = per[:, off:off + n]
        off += n
    out["conv_w"] = per[:, off:off + CONV_WIDTH * CONV_W_SHARD].reshape(DEPTH, CONV_WIDTH, CONV_W_SHARD)
    return out


def kernel(x, w_in, b_in, conv_w, conv_b, conv_ln_g, conv_ln_b, w_att_proj, w_conv_proj, b_conv_proj, w_out, ln_g, ln_b, loss_target, m_w_in, m_b_in, m_conv_w, m_conv_b, m_conv_ln_g, m_conv_ln_b, m_w_att_proj, m_w_conv_proj, m_b_conv_proj, m_w_out, m_ln_g, m_ln_b, v_w_in, v_b_in, v_conv_w, v_conv_b, v_conv_ln_g, v_conv_ln_b, v_w_att_proj, v_w_conv_proj, v_b_conv_proj, v_w_out, v_ln_g, v_ln_b):
    x0 = x[0]
    target = loss_target[0]

    def shards(l):
        return [w_in[l].astype(BF16), w_att_proj[l].astype(BF16), w_conv_proj[l].astype(BF16),
                w_out[l].astype(BF16), conv_w[l]]

    def full_weights(g_in, g_ap, g_cp, g_out, g_cw):
        return (g_in.transpose(1, 0, 2).reshape(D_MODEL, D_IN), g_ap.transpose(1, 0, 2).reshape(D_ATT, D_MODEL),
                g_cp.transpose(1, 0, 2).reshape(D_CONV, D_MODEL), g_out.reshape(D_MODEL, D_MODEL),
                jnp.pad(g_cw.transpose(1, 0, 2).reshape(CONV_WIDTH, D_CONV), ((0, 1), (0, 0))))

    pad8 = lambda rows: jnp.pad(jnp.stack(rows, axis=1), ((0, 0), (0, 8 - len(rows)), (0, 0)))
    cvec = pad8([conv_b, conv_ln_g, conv_ln_b])
    dvec = pad8([b_conv_proj, ln_g, ln_b])
    b_in3 = b_in.reshape(DEPTH, 1, D_IN)

    weights = [full_weights(*_all_gather(shards(0), "gather_weights"))]
    xs, xbs, us, atts, hs, c1s = [x0], [x0.astype(BF16)], [], [], [], []
    for l in range(DEPTH):
        w_in_l, wap, wcp, wout, cw = weights[l]
        u = _in_proj(xbs[l], w_in_l, b_in3, l)
        att, gathered = _attn_fwd(u, gather=shards(l + 1) if l + 1 < DEPTH else ())
        if gathered:
            weights.append(full_weights(*gathered))
        xn, xnb, h, c1 = _mix_fwd(xs[l], u, att, wap, wcp, wout, cw, cvec, dvec, l)
        us.append(u), atts.append(att), hs.append(h), c1s.append(c1), xs.append(xn), xbs.append(xnb)
    dxo, loss_part = _loss_head(xs[DEPTH], target)
    loss = lax.psum(loss_part[0, 0], ("x", "y", "c"))

    big = {"w_in": (w_in, m_w_in, v_w_in, 256), "w_att_proj": (w_att_proj, m_w_att_proj, v_w_att_proj, D_ATT),
           "w_conv_proj": (w_conv_proj, m_w_conv_proj, v_w_conv_proj, D_CONV), "w_out": (w_out, m_w_out, v_w_out, 128)}
    results = {k: None for k in big}
    small = [None] * DEPTH

    def update(k, recv, l):
        w, m, v, tr = big[k]
        results[k] = _adamw_sharded(recv, w, m, v, results[k], l, tr, "adamw_" + k)

    pending_w_in = ()
    for l in reversed(range(DEPTH)):
        w_in_l, wap, wcp, wout, cw = weights[l]
        (datt, du_rest, dres, dwap, dwcp, dwout, dcw, dcvec, ddvec), got = _mix_bwd(
            dxo, hs[l], c1s[l], us[l], atts[l], wap, wcp, wout, cw, cvec, dvec, l, hosted=pending_w_in)
        if got:
            update("w_in", got[0], l + 1)
        proj = [dwap.reshape(D_ATT, N_DEV, D_MODEL // N_DEV).transpose(1, 0, 2).astype(BF16),
                dwcp.reshape(D_CONV, N_DEV, D_MODEL // N_DEV).transpose(1, 0, 2).astype(BF16),
                dwout.reshape(N_DEV, D_MODEL // N_DEV, D_MODEL).astype(BF16)]
        dqkv, got = _attn_bwd(us[l], atts[l], datt, hosted=proj)
        for k, r in zip(("w_att_proj", "w_conv_proj", "w_out"), got):
            update(k, r, l)
        dwin, dbin = _in_proj_dw(xbs[l], dqkv, du_rest)
        small[l] = jnp.concatenate([dbin.reshape(-1), dcvec[:3].reshape(-1), ddvec[:3].reshape(-1), dcw.reshape(-1)])
        pending_w_in = (dwin.reshape(D_MODEL, N_DEV, W_IN_SHARD).transpose(1, 0, 2).astype(BF16),)
        dxo, got = _in_proj_dx(dqkv, du_rest, w_in_l, dres, hosted=pending_w_in if l == 0 else ())
        if got:
            update("w_in", got[0], l)
    grad_x = dxo[None]

    small_all = jnp.stack(small).reshape(-1, LANES)
    (gathered,) = _all_gather([small_all], "gather_small_grads")
    per_layer = _sum8(gathered).reshape(DEPTH, -1)
    g_rep, off = {}, 0
    for k, n in (("b_in", D_IN), ("conv_b", D_CONV), ("conv_ln_g", D_CONV), ("conv_ln_b", D_CONV),
                 ("b_conv_proj", D_MODEL), ("ln_g", D_MODEL), ("ln_b", D_MODEL)):
        g_rep[k] = per_layer[:, off:off + n]
        off += n
    g_cw_full = per_layer[:, off:].reshape(DEPTH, CONV_PAD, D_CONV)[:, :CONV_WIDTH]
    g_cw_shard = lax.dynamic_slice_in_dim(g_cw_full, _my_index() * CONV_W_SHARD, CONV_W_SHARD, axis=2)
    rep_w = dict(b_in=b_in, conv_b=conv_b, conv_ln_g=conv_ln_g, conv_ln_b=conv_ln_b, b_conv_proj=b_conv_proj,
                 ln_g=ln_g, ln_b=ln_b)
    rep_m = dict(b_in=m_b_in, conv_b=m_conv_b, conv_ln_g=m_conv_ln_g, conv_ln_b=m_conv_ln_b,
                 b_conv_proj=m_b_conv_proj, ln_g=m_ln_g, ln_b=m_ln_b)
    rep_v = dict(b_in=v_b_in, conv_b=v_conv_b, conv_ln_g=v_conv_ln_g, conv_ln_b=v_conv_ln_b,
                 b_conv_proj=v_b_conv_proj, ln_g=v_ln_g, ln_b=v_ln_b)
    sd, sm, sv = _adamw_small(_pack_small(rep_w, conv_w), _pack_small(g_rep, g_cw_shard),
                              _pack_small(rep_m, m_conv_w), _pack_small(rep_v, v_conv_w))
    small_out = [dict(g_rep, conv_w=g_cw_shard), _unpack_small(sd), _unpack_small(sm), _unpack_small(sv)]

    order = ["w_in", "b_in", "conv_w", "conv_b", "conv_ln_g", "conv_ln_b", "w_att_proj", "w_conv_proj",
             "b_conv_proj", "w_out", "ln_g", "ln_b"]
    outs = [loss, grad_x]
    for kind in range(4):
        outs += [results[k][kind] if k in big else small_out[kind][k] for k in order]
    return tuple(outs)
```

```python
import functools

import jax
import jax.numpy as jnp
from jax import lax
from jax.experimental import pallas as pl
from jax.experimental.pallas import tpu as pltpu

F32, BF16 = jnp.float32, jnp.bfloat16

D_MODEL = 1024
D_ATT = 512
D_CONV = 512
HEAD_DIM = 64
PAIR = 2 * HEAD_DIM
CONV_WIDTH = 31
CONV_PAD = 32
DEPTH = 4
N_DEV = 8
SEG = 512
D_IN = 4 * D_ATT + 3 * D_CONV + 2 * D_MODEL
N_SEG = D_IN // SEG
N_QKV_SEG = 3
W_IN_SHARD = D_IN // N_DEV
CONV_W_SHARD = D_CONV // N_DEV
LANES = 128
LN_EPS = 1e-5
DEEPNORM_ALPHA = (2 * DEPTH) ** 0.25
ATT_SCALE = HEAD_DIM ** -0.5
ATT_BLOCK = 256
ADAM_LR, ADAM_B1, ADAM_B2, ADAM_EPS, ADAM_WD, ADAM_STEP = 0.001, 0.9, 0.999, 1e-08, 0.01, 10
VMEM_LIMIT = 56 * 1024 * 1024
MESH = pl.DeviceIdType.MESH
ANY = pl.BlockSpec(memory_space=pl.ANY)


def _resident(shape, index_map):
    return pl.BlockSpec(shape, index_map, pipeline_mode=pl.Buffered(1))


def _params(*semantics):
    return pltpu.CompilerParams(dimension_semantics=semantics, vmem_limit_bytes=VMEM_LIMIT)


def _sig(x):
    return 1.0 / (1.0 + jnp.exp(-x))


def _dot(a, b):
    return jnp.dot(a, b, preferred_element_type=F32)


def _dot_nt(a, b):
    return lax.dot_general(a, b, (((1,), (1,)), ((), ())), preferred_element_type=F32)


def _dot_tn(a, b):
    return lax.dot_general(a, b, (((0,), (0,)), ((), ())), preferred_element_type=F32)


def _my_index():
    return 4 * lax.axis_index("x") + 2 * lax.axis_index("y") + lax.axis_index("c")


class _Gather:
    def __init__(self, ins, outs, send_sems, recv_sems, local_sems):
        self.n = n = len(ins)
        x, y, c = lax.axis_index("x"), lax.axis_index("y"), lax.axis_index("c")
        me, sibling = (x, y, c), (x, y, 1 - c)
        chips = [(1 - x, y), (x, 1 - y), (1 - x, 1 - y)]

        def slot(a, dev):
            return outs[a].at[4 * dev[0] + 2 * dev[1] + dev[2]]

        def copy(a, k, block, to, src=None):
            return pltpu.make_async_remote_copy(
                src_ref=slot(a, block) if src is None else src, dst_ref=slot(a, block),
                send_sem=send_sems.at[7 * a + k], recv_sem=recv_sems.at[7 * a + k], device_id=to, device_id_type=MESH)

        self.mine = [pltpu.make_async_copy(ins[a], slot(a, me), local_sems.at[a]) for a in range(n)]
        self.first, self.landed, self.passed, self.last = [], [], [], []
        for a in range(n):
            self.first.append(copy(a, 0, me, sibling, src=ins[a]))
            self.first += [copy(a, 1 + j, me, (*chip, c), src=ins[a]) for j, chip in enumerate(chips)]
        for j, chip in enumerate(chips):
            for a in range(n):
                self.landed.append(copy(a, 1 + j, (*chip, c), me))
                self.passed.append(copy(a, 4 + j, (*chip, c), sibling))
        for a in range(n):
            self.last.append(copy(a, 0, sibling, me))
            self.last += [copy(a, 4 + j, (*chip, 1 - c), me) for j, chip in enumerate(chips)]

    @staticmethod
    def semaphores(n):
        return [pltpu.SemaphoreType.DMA((7 * n,)), pltpu.SemaphoreType.DMA((7 * n,)), pltpu.SemaphoreType.DMA((n,))]

    def start(self):
        for cp in self.mine + self.first:
            cp.start()

    def forward(self):
        for landed, passed in zip(self.landed, self.passed):
            landed.wait_recv()
            passed.start()

    def wait(self):
        for cp in self.last:
            cp.wait_recv()
        for cp in self.first + self.passed:
            cp.wait_send()
        for cp in self.mine:
            cp.wait()


def _gathered_shapes(shards):
    return [jax.ShapeDtypeStruct((N_DEV,) + s.shape, s.dtype) for s in shards]


def _all_gather(shards, name):
    n = len(shards)

    def body(*refs):
        g = _Gather(refs[:n], refs[n:2 * n], *refs[2 * n:])
        g.start()
        g.forward()
        g.wait()

    return pl.pallas_call(
        body, name=name, out_shape=_gathered_shapes(shards),
        in_specs=[ANY] * n, out_specs=[ANY] * n, scratch_shapes=_Gather.semaphores(n),
    )(*shards)


class _Exchange:
    def __init__(self, ins, outs, send_sems, recv_sems, local_sems):
        n = len(ins)
        x, y, c = lax.axis_index("x"), lax.axis_index("y"), lax.axis_index("c")
        me_idx = 4 * x + 2 * y + c
        self.mine = [pltpu.make_async_copy(ins[a].at[me_idx], outs[a].at[me_idx], local_sems.at[a])
                     for a in range(n)]
        self.sends, self.recvs = [], []
        for d in (1, 2, 4, 3, 5, 6, 7):
            px = 1 - x if d & 4 else x
            py = 1 - y if d & 2 else y
            pc = 1 - c if d & 1 else c
            idx = 4 * px + 2 * py + pc
            for a in range(n):
                sems = dict(send_sem=send_sems.at[7 * a + d - 1], recv_sem=recv_sems.at[7 * a + d - 1],
                            device_id=(px, py, pc), device_id_type=MESH)
                self.sends.append(pltpu.make_async_remote_copy(
                    src_ref=ins[a].at[idx], dst_ref=outs[a].at[me_idx], **sems))
                self.recvs.append(pltpu.make_async_remote_copy(
                    src_ref=ins[a].at[idx], dst_ref=outs[a].at[idx], **sems))

    @staticmethod
    def semaphores(n):
        return [pltpu.SemaphoreType.DMA((7 * n,)), pltpu.SemaphoreType.DMA((7 * n,)), pltpu.SemaphoreType.DMA((n,))]

    def start(self):
        for cp in self.mine + self.sends:
            cp.start()

    def wait(self):
        for cp in self.recvs:
            cp.wait_recv()
        for cp in self.sends:
            cp.wait_send()
        for cp in self.mine:
            cp.wait()


def _host(hosted):
    n = len(hosted)
    return ([ANY] * n, [ANY] * n, [jax.ShapeDtypeStruct(p.shape, p.dtype) for p in hosted],
            _Exchange.semaphores(n) if n else [])


def _split_refs(refs, *counts):
    out, at = [], 0
    for k in counts:
        out.append(refs[at:at + k])
        at += k
    return out + [refs[at:]]


def _in_proj(xb, w_in, b_in, layer):
    T = xb.shape[0]
    tm, tn = min(T, 2048), SEG

    def body(x_ref, w_ref, b_ref, u_ref):
        u_ref[...] = _dot(x_ref[...], w_ref[...]) + b_ref[...]

    return pl.pallas_call(
        body, name="in_proj", grid=(T // tm, D_IN // tn),
        in_specs=[pl.BlockSpec((tm, D_MODEL), lambda m, n: (m, 0)),
                  pl.BlockSpec((D_MODEL, tn), lambda m, n: (0, n)),
                  pl.BlockSpec((None, 1, tn), lambda m, n: (layer, 0, n))],
        out_specs=pl.BlockSpec((tm, tn), lambda m, n: (m, n)),
        out_shape=jax.ShapeDtypeStruct((T, D_IN), F32),
        compiler_params=_params("parallel", "parallel"),
    )(xb, w_in, b_in)


def _du_specs(tm, row_of, seg_of):
    return [pl.BlockSpec((None, tm, SEG), lambda *g: (jnp.minimum(seg_of(*g), N_QKV_SEG - 1), row_of(*g), 0)),
            pl.BlockSpec((tm, SEG), lambda *g: (row_of(*g), jnp.maximum(seg_of(*g) - N_QKV_SEG, 0)))]


def _in_proj_dx(dqkv, du_rest, w_in, dres, hosted=()):
    T = du_rest.shape[0]
    tm = min(T, 512)
    nm = T // tm
    n_h = len(hosted)
    h_in, h_out, h_shape, h_sems = _host(hosted)
    n_rest = du_rest.shape[1]

    def body(*refs):
        (q_ref, k_ref, v_ref, a_ref, w_ref, r_ref), h_ins, (o_ref,), h_outs, sems = _split_refs(refs, 6, n_h, 1, n_h)
        m = pl.program_id(0)

        if n_h:
            @pl.when(m == 0)
            def _():
                _Exchange(h_ins, h_outs, *sems).start()

        acc = r_ref[...] + _dot_nt(a_ref[...], w_ref[:, N_QKV_SEG * SEG:N_QKV_SEG * SEG + n_rest])
        for k, seg_ref in enumerate((q_ref, k_ref, v_ref)):
            acc = acc + _dot_nt(seg_ref[...], w_ref[:, k * SEG:(k + 1) * SEG])
        o_ref[...] = acc

        if n_h:
            @pl.when(m == nm - 1)
            def _():
                _Exchange(h_ins, h_outs, *sems).wait()

    out = pl.pallas_call(
        body, name="in_proj_dx", grid=(nm,),
        in_specs=[pl.BlockSpec((None, tm, SEG), functools.partial(lambda k, m: (k, m, 0), k))
                  for k in range(N_QKV_SEG)] + [
            pl.BlockSpec((tm, n_rest), lambda m: (m, 0)),
            _resident((D_MODEL, D_IN), lambda m: (0, 0)),
            pl.BlockSpec((tm, D_MODEL), lambda m: (m, 0))] + h_in,
        out_specs=[pl.BlockSpec((tm, D_MODEL), lambda m: (m, 0))] + h_out,
        out_shape=[jax.ShapeDtypeStruct((T, D_MODEL), F32)] + h_shape,
        scratch_shapes=h_sems,
        compiler_params=_params("arbitrary"),
    )(dqkv, dqkv, dqkv, du_rest, w_in, dres, *hosted)
    return out[0], out[1:]


def _in_proj_dw(xb, dqkv, du_rest):
    T = xb.shape[0]
    tk = min(T, 4096)
    nt = T // tk

    def body(x_ref, a1_ref, a2_ref, dw_ref, db_ref, acc_ref):
        s, t = pl.program_id(0), pl.program_id(1)

        @pl.when(t == 0)
        def _():
            acc_ref[...] = jnp.zeros_like(acc_ref)
            db_ref[...] = jnp.zeros_like(db_ref)

        def acc(a):
            acc_ref[...] += _dot_tn(x_ref[...], a)
            db_ref[...] += jnp.sum(a.astype(F32), axis=0, keepdims=True)

        @pl.when(s < N_QKV_SEG)
        def _():
            acc(a1_ref[...])

        @pl.when(s >= N_QKV_SEG)
        def _():
            acc(a2_ref[...])

        @pl.when(t == nt - 1)
        def _():
            dw_ref[...] = acc_ref[...].astype(BF16)

    return pl.pallas_call(
        body, name="in_proj_dw", grid=(N_SEG, nt),
        in_specs=[pl.BlockSpec((tk, D_MODEL), lambda s, t: (t, 0))] + _du_specs(tk, lambda s, t: t, lambda s, t: s),
        out_specs=[pl.BlockSpec((D_MODEL, SEG), lambda s, t: (0, s)),
                   pl.BlockSpec((None, 1, SEG), lambda s, t: (s, 0, 0))],
        out_shape=[jax.ShapeDtypeStruct((D_MODEL, D_IN), BF16), jax.ShapeDtypeStruct((N_SEG, 1, SEG), F32)],
        scratch_shapes=[pltpu.VMEM((D_MODEL, SEG), F32)],
        compiler_params=_params("parallel", "arbitrary"),
    )(xb, dqkv, du_rest)


ATT_CHUNK = 32
ATT_Q_PER_STEP = 2


def _chunks():
    return [slice(r, r + ATT_CHUNK) for r in range(0, ATT_BLOCK, ATT_CHUNK)]


def _twice(rhs):
    return jnp.concatenate([rhs, rhs], axis=0)


def _fail_and_log_beta(z):
    t = jnp.maximum(z, 0.0) + jnp.log(1.0 + jnp.exp(-jnp.abs(z)))
    return t, z - t


def _store_split(dst, rows, val):
    hi = val.astype(BF16)
    dst[rows, 0:ATT_BLOCK] = hi
    dst[rows, ATT_BLOCK:2 * ATT_BLOCK] = (val - hi.astype(F32)).astype(BF16)


def _fill_causal(keep):
    row = lax.broadcasted_iota(jnp.int32, (ATT_BLOCK, ATT_BLOCK), 0)
    col = lax.broadcasted_iota(jnp.int32, (ATT_BLOCK, ATT_BLOCK), 1)
    keep[...] = (col < row).astype(F32)


def _triangles():
    row = lax.broadcasted_iota(jnp.int32, (ATT_BLOCK, ATT_BLOCK), 0)
    col = lax.broadcasted_iota(jnp.int32, (ATT_BLOCK, ATT_BLOCK), 1)
    return _twice((row > col).astype(BF16)), _twice((row >= col).astype(BF16))


def _staggered(stages):
    stages[0](0)
    stages[0](1)
    for k in range(1, len(stages), 2):
        for h in range(2):
            stages[k](h)
            stages[k + 1](h)


EXP_UNDERFLOW = 104.0


def _sweep_keys(i, tile, rsum):
    tile(i, True)

    def live(carry):
        j, low = carry
        return jnp.logical_and(j >= 0, low < EXP_UNDERFLOW)

    def step(carry):
        j, _ = carry
        tile(j, False)
        return j - 1, jnp.min(rsum[...])

    lax.while_loop(live, step, (i - 1, jnp.min(rsum[...])))


def _attn_fwd(u, gather=()):
    T = u.shape[0]
    B = ATT_BLOCK
    per_step = ATT_Q_PER_STEP
    ns = T // (B * per_step)
    n_pairs = D_ATT // PAIR
    n_g = len(gather)

    def body(*refs):
        (q_ref, k_ref, v_ref), g_ins, (o_ref,), g_outs, scratch = _split_refs(refs, 3, n_g, 1, n_g)
        kb, vb, acc, rsum, t0, zs, lt, cat, keep = scratch[:9]
        p, s = pl.program_id(0), pl.program_id(1)

        if n_g:
            @pl.when(jnp.logical_and(p == 0, s == 0))
            def _():
                _Gather(g_ins, g_outs, *scratch[9:]).start()

            @pl.when(jnp.logical_and(p == n_pairs - 1, s == 0))
            def _():
                _Gather(g_ins, g_outs, *scratch[9:]).forward()

        @pl.when(s == 0)
        def _():
            _fill_causal(keep)
            for h in range(2):
                kb[h] = k_ref[:, HEAD_DIM * h:HEAD_DIM * (h + 1)].astype(BF16)
                vb[h] = v_ref[:, HEAD_DIM * h:HEAD_DIM * (h + 1)].astype(BF16)

        tri2, _ = _triangles()
        hs = [slice(HEAD_DIM * h, HEAD_DIM * (h + 1)) for h in range(2)]

        def query_block(sb, carry):
            i = s * per_step + sb
            qrows = pl.ds(pl.multiple_of(sb * B, B), B)
            q = [(q_ref[qrows, hs[h]] * ATT_SCALE).astype(BF16) for h in range(2)]
            acc[...] = jnp.zeros_like(acc)
            rsum[...] = jnp.zeros_like(rsum)
            _sweep_keys(i, functools.partial(tile, q), rsum)
            for h in range(2):
                o_ref[qrows, hs[h]] = acc[h]
            return carry

        def tile(q, j, masked):
            keys = pl.ds(pl.multiple_of(j * B, B), B)

            def scores(h):
                zs[h] = _dot_nt(q[h], kb[h, keys, :])

            def fails(h):
                for r in _chunks():
                    t, a = _fail_and_log_beta(zs[h, r, :])
                    if masked:
                        t = t * keep[r, :]
                    _store_split(cat.at[h], r, t)
                    zs[h, r, :] = a
                    t0[h, r, :] = t[:, 0:1]

            def later_sums(h):
                lt[h] = _dot(cat[h], tri2)

            def weights(h):
                for r in _chunks():
                    later = lt[h, r, :]
                    w = jnp.exp(zs[h, r, :] - later - rsum[h, r, :])
                    if masked:
                        w = w * keep[r, :]
                    _store_split(cat.at[h], r, w)
                    rsum[h, r, :] += later[:, 0:1] + t0[h, r, :]

            def values(h):
                acc[h] += _dot(cat[h], _twice(vb[h, keys, :]))

            _staggered([scores, fails, later_sums, weights, values])

        lax.fori_loop(0, per_step, query_block, 0)

        if n_g:
            @pl.when(jnp.logical_and(p == n_pairs - 1, s == ns - 1))
            def _():
                _Gather(g_ins, g_outs, *scratch[9:]).wait()

    out = pl.pallas_call(
        body, name="attn_fwd", grid=(n_pairs, ns),
        in_specs=[pl.BlockSpec((per_step * B, PAIR), lambda p, s: (s, p)),
                  pl.BlockSpec((T, PAIR), lambda p, s: (0, 4 + p)),
                  pl.BlockSpec((T, PAIR), lambda p, s: (0, 8 + p))] + [ANY] * n_g,
        out_specs=[pl.BlockSpec((per_step * B, PAIR), lambda p, s: (s, p))] + [ANY] * n_g,
        out_shape=[jax.ShapeDtypeStruct((T, D_ATT), F32)] + _gathered_shapes(gather),
        scratch_shapes=[pltpu.VMEM((2, T, HEAD_DIM), BF16), pltpu.VMEM((2, T, HEAD_DIM), BF16),
                        pltpu.VMEM((2, B, HEAD_DIM), F32), pltpu.VMEM((2, B, 1), F32), pltpu.VMEM((2, B, 1), F32),
                        pltpu.VMEM((2, B, B), F32), pltpu.VMEM((2, B, B), F32), pltpu.VMEM((2, B, 2 * B), BF16),
                        pltpu.VMEM((B, B), F32)]
        + (_Gather.semaphores(n_g) if n_g else []),
        compiler_params=_params("arbitrary", "arbitrary"),
    )(u, u, u, *gather)
    return out[0], out[1:]


def _attn_bwd(u, att, datt, hosted=()):
    T = u.shape[0]
    B = ATT_BLOCK
    per_step = ATT_Q_PER_STEP
    ns = T // (B * per_step)
    n_pairs = D_ATT // PAIR
    n_h = len(hosted)
    h_in, h_out, h_shape, h_sems = _host(hosted)

    def body(*refs):
        (q_ref, k_ref, v_ref, o_ref, do_ref), h_ins, (dqkv_ref,), h_outs, scratch = _split_refs(refs, 5, n_h, 1, n_h)
        n_scratch = 17
        (kb, vb, dkacc, dvacc, dqacc, rsum, gsum, total, t0, lt0, zs, gs, lt, cat, wb, dzb,
         keep) = scratch[:n_scratch]
        p, s = pl.program_id(0), pl.program_id(1)

        if n_h:
            @pl.when(jnp.logical_and(p == 0, s == 0))
            def _():
                _Exchange(h_ins, h_outs, *scratch[n_scratch:]).start()

        @pl.when(s == 0)
        def _():
            _fill_causal(keep)
            for h in range(2):
                kb[h] = k_ref[:, HEAD_DIM * h:HEAD_DIM * (h + 1)].astype(BF16)
                vb[h] = v_ref[:, HEAD_DIM * h:HEAD_DIM * (h + 1)].astype(BF16)
            dkacc[...] = jnp.zeros_like(dkacc)
            dvacc[...] = jnp.zeros_like(dvacc)

        tri2, tri_incl2 = _triangles()
        hs = [slice(HEAD_DIM * h, HEAD_DIM * (h + 1)) for h in range(2)]
        both = lambda acc2: jnp.concatenate([acc2[0], acc2[1]], axis=1)

        def query_block(sb, carry):
            i = s * per_step + sb
            local = pl.ds(pl.multiple_of(sb * B, B), B)
            q = [(q_ref[local, hs[h]] * ATT_SCALE).astype(BF16) for h in range(2)]
            dob = [do_ref[local, hs[h]].astype(BF16) for h in range(2)]
            for h in range(2):
                total[h] = jnp.sum(dob[h].astype(F32) * o_ref[local, hs[h]], axis=1, keepdims=True)
            dqacc[...] = jnp.zeros_like(dqacc)
            rsum[...] = jnp.zeros_like(rsum)
            gsum[...] = jnp.zeros_like(gsum)
            _sweep_keys(i, functools.partial(tile, q, dob), rsum)
            dqkv_ref[0, pl.ds(pl.multiple_of(i * B, B), B), :] = (both(dqacc) * ATT_SCALE).astype(BF16)
            return carry

        def tile(q, dob, j, masked):
            keys = pl.ds(pl.multiple_of(j * B, B), B)

            def scores(h):
                zs[h] = _dot_nt(q[h], kb[h, keys, :])
                gs[h] = _dot_nt(dob[h], vb[h, keys, :])

            def fails(h):
                for r in _chunks():
                    t, a = _fail_and_log_beta(zs[h, r, :])
                    if masked:
                        t = t * keep[r, :]
                    _store_split(cat.at[h], r, t)
                    zs[h, r, :] = a
                    t0[h, r, :] = t[:, 0:1]

            def later_sums(h):
                lt[h] = _dot(cat[h], tri2)

            def weights(h):
                for r in _chunks():
                    later = lt[h, r, :]
                    w = jnp.exp(zs[h, r, :] - later - rsum[h, r, :])
                    if masked:
                        w = w * keep[r, :]
                    g = gs[h, r, :] * w
                    gs[h, r, :] = g
                    _store_split(cat.at[h], r, g)
                    wb[h, r, :] = w.astype(BF16)
                    lt0[h, r, :] = later[:, 0:1]

            def suffix_sums(h):
                lt[h] = _dot(cat[h], tri_incl2)

            def score_grads(h):
                for r in _chunks():
                    suffix = lt[h, r, :]
                    g = gs[h, r, :]
                    before = total[h, r, :] - (suffix + gsum[h, r, :])
                    dz = g - jnp.exp(zs[h, r, :]) * (g + before)
                    if masked:
                        dz = dz * keep[r, :]
                    dzb[h, r, :] = dz.astype(BF16)
                    rsum[h, r, :] += lt0[h, r, :] + t0[h, r, :]
                    gsum[h, r, :] += suffix[:, 0:1]

            def input_grads(h):
                dqacc[h] += _dot(dzb[h], kb[h, keys, :])
                dkacc[h, keys, :] += _dot_tn(dzb[h], q[h])
                dvacc[h, keys, :] += _dot_tn(wb[h], dob[h])

            _staggered([scores, fails, later_sums, weights, suffix_sums, score_grads, input_grads])

        lax.fori_loop(0, per_step, query_block, 0)

        @pl.when(s == ns - 1)
        def _():
            dqkv_ref[1] = both(dkacc).astype(BF16)
            dqkv_ref[2] = both(dvacc).astype(BF16)

        if n_h:
            @pl.when(jnp.logical_and(p == n_pairs - 1, s == ns - 1))
            def _():
                _Exchange(h_ins, h_outs, *scratch[n_scratch:]).wait()

    rows_spec = pl.BlockSpec((per_step * B, PAIR), lambda p, s: (s, p))
    out = pl.pallas_call(
        body, name="attn_bwd", grid=(n_pairs, ns),
        in_specs=[rows_spec,
                  pl.BlockSpec((T, PAIR), lambda p, s: (0, 4 + p)),
                  pl.BlockSpec((T, PAIR), lambda p, s: (0, 8 + p)),
                  rows_spec, rows_spec] + h_in,
        out_specs=[pl.BlockSpec((3, T, PAIR), lambda p, s: (0, 0, p))] + h_out,
        out_shape=[jax.ShapeDtypeStruct((3, T, D_ATT), BF16)] + h_shape,
        scratch_shapes=[pltpu.VMEM((2, T, HEAD_DIM), BF16), pltpu.VMEM((2, T, HEAD_DIM), BF16),
                        pltpu.VMEM((2, T, HEAD_DIM), F32), pltpu.VMEM((2, T, HEAD_DIM), F32),
                        pltpu.VMEM((2, B, HEAD_DIM), F32)]
        + [pltpu.VMEM((2, B, 1), F32)] * 5
        + [pltpu.VMEM((2, B, B), F32)] * 3
        + [pltpu.VMEM((2, B, 2 * B), BF16), pltpu.VMEM((2, B, B), BF16), pltpu.VMEM((2, B, B), BF16)]
        + [pltpu.VMEM((B, B), F32)]
        + h_sems,
        compiler_params=_params("arbitrary", "arbitrary"),
    )(u, u, u, att, datt, *hosted)
    return out[0], out[1:]


MIX_TILE = 256
MIX_CHUNK = 16


def _u_specs(tm, tile_of):
    return [pl.BlockSpec((tm, SEG), functools.partial(lambda k, *g: (tile_of(*g), k), k)) for k in range(3, N_SEG)]


SHIFT_ROWS = 24


def _shifted_scratch(tm):
    return pltpu.VMEM((7, tm + SHIFT_ROWS, D_CONV), F32)


def _make_shifted(base, shifted, tm):
    for b in range(1, 8):
        shifted[b - 1] = base[b:b + tm + SHIFT_ROWS, :]


def _tap_window(base, shifted, start, rows):
    whole, b = divmod(start, 8)
    if b == 0:
        return base[start + rows.start:start + rows.stop, :]
    return shifted[b - 1, 8 * whole + rows.start:8 * whole + rows.stop, :]


def _conv(ext, ext_shifted, cw_ref, rows):
    acc = cw_ref[0:1, :] * _tap_window(ext, ext_shifted, 2, rows)
    for k in range(1, CONV_WIDTH):
        acc = acc + cw_ref[k:k + 1, :] * _tap_window(ext, ext_shifted, 2 + k, rows)
    return acc


def _norm_stats(v):
    mu = jnp.mean(v, axis=-1, keepdims=True)
    vc = v - mu
    rstd = lax.rsqrt(jnp.mean(vc * vc, axis=-1, keepdims=True) + LN_EPS)
    return vc * rstd, rstd


def _norm_bwd(dy_scaled, xhat, rstd):
    return rstd * (dy_scaled - jnp.mean(dy_scaled, axis=-1, keepdims=True)
                   - xhat * jnp.mean(dy_scaled * xhat, axis=-1, keepdims=True))


def _mix_fwd(x, u, att, wap, wcp, wout, cw, cvec, dvec, layer):
    T = x.shape[0]
    tm = MIX_TILE

    def body(x_ref, za_ref, ga_ref, gb_ref, zc_ref, ga0_ref, ga1_ref, gc0_ref, gc1_ref, att_ref,
             wap_ref, wcp_ref, wout_ref, cw_ref, cvec_ref, dvec_ref, xn_ref, xnb_ref, h_ref, c1_ref,
             ext, ext_shifted):
        i = pl.program_id(0)

        @pl.when(i == 0)
        def _():
            ext[0:CONV_PAD, :] = jnp.zeros((CONV_PAD, D_CONV), F32)

        za = za_ref[...]
        attg = att_ref[...] * (za * _sig(za))
        ab = _dot(attg.astype(BF16), wap_ref[...])

        ext[CONV_PAD:CONV_PAD + tm, :] = ga_ref[...] * _sig(gb_ref[...])
        _make_shifted(ext, ext_shifted, tm)
        c1 = _conv(ext, ext_shifted, cw_ref, slice(0, tm)) + cvec_ref[0:1, :]
        c1_ref[...] = c1
        ext[0:CONV_PAD, :] = ext[tm:tm + CONV_PAD, :]
        xh, _ = _norm_stats(c1)
        c2 = xh * cvec_ref[1:2, :] + cvec_ref[2:3, :]
        zc = zc_ref[...]
        cz = (c2 * _sig(c2)) * (zc * _sig(zc))
        cb = _dot(cz.astype(BF16), wcp_ref[...]) + dvec_ref[0:1, :]

        g_att = jnp.concatenate([ga0_ref[...], ga1_ref[...]], axis=1)
        g_conv = jnp.concatenate([gc0_ref[...], gc1_ref[...]], axis=1)
        merged = _sig(g_att) * ab + _sig(g_conv) * cb
        h = DEEPNORM_ALPHA * x_ref[...] + _dot(merged.astype(BF16), wout_ref[...])
        h_ref[...] = h
        hh, _ = _norm_stats(h)
        xn = hh * dvec_ref[1:2, :] + dvec_ref[2:3, :]
        xn_ref[...] = xn
        xnb_ref[...] = xn.astype(BF16)

    row = pl.BlockSpec((tm, D_MODEL), lambda i: (i, 0))
    return pl.pallas_call(
        body, name="mix_fwd", grid=(T // tm,),
        in_specs=[row] + _u_specs(tm, lambda i: i) + [
            pl.BlockSpec((tm, D_ATT), lambda i: (i, 0)),
            _resident((D_ATT, D_MODEL), lambda i: (0, 0)),
            _resident((D_CONV, D_MODEL), lambda i: (0, 0)),
            _resident((D_MODEL, D_MODEL), lambda i: (0, 0)),
            _resident((CONV_PAD, D_CONV), lambda i: (0, 0)),
            pl.BlockSpec((None, 8, D_CONV), lambda i: (layer, 0, 0)),
            pl.BlockSpec((None, 8, D_MODEL), lambda i: (layer, 0, 0))],
        out_specs=[row, row, row, pl.BlockSpec((tm, D_CONV), lambda i: (i, 0))],
        out_shape=[jax.ShapeDtypeStruct((T, D_MODEL), F32), jax.ShapeDtypeStruct((T, D_MODEL), BF16),
                   jax.ShapeDtypeStruct((T, D_MODEL), F32), jax.ShapeDtypeStruct((T, D_CONV), F32)],
        scratch_shapes=[pltpu.VMEM((tm + CONV_PAD, D_CONV), F32), _shifted_scratch(tm)],
        compiler_params=_params("arbitrary"),
    )(x, *([u] * 8), att, wap, wcp, wout, cw, cvec, dvec)


def _mix_bwd(dxo, h, c1, u, att, wap, wcp, wout, cw, cvec, dvec, layer, hosted=()):
    T = dxo.shape[0]
    tm = MIX_TILE
    nt = T // tm
    halo_per_tile = tm // CONV_PAD
    n_h = len(hosted)
    h_in, h_out, h_shape, h_sems = _host(hosted)

    def body(*refs):
        ins, h_ins, outs, h_outs, scratch = _split_refs(refs, 20, n_h, 9, n_h)
        (dxo_ref, h_ref, c1_ref, za_ref, ga_ref, gb_ref, zc_ref, ga0_ref, ga1_ref, gc0_ref, gc1_ref, hga_ref, hgb_ref,
         att_ref, wap_ref, wcp_ref, wout_ref, cw_ref, cvec_ref, dvec_ref) = ins
        datt_ref, dur_ref, dres_ref, dwap_ref, dwcp_ref, dwout_ref, dcw_ref, dcvec_ref, ddvec_ref = outs
        n_scratch = 19
        (ext, dext, ext_shifted, dext_shifted, attg_s, cz_s, merged_s, dy_s, dab_s, dcb_s,
         xh_s, dattg_s, dcz_s, ab_s, cb_s, sga_s, sgc_s, dm_s, rstd_s) = scratch[:n_scratch]
        r = pl.program_id(0)
        i = nt - 1 - r

        if n_h:
            @pl.when(r == 0)
            def _():
                _Exchange(h_ins, h_outs, *scratch[n_scratch:]).start()

        @pl.when(r == 0)
        def _():
            for ref in (dwap_ref, dwcp_ref, dwout_ref, dcw_ref, dcvec_ref, ddvec_ref):
                ref[...] = jnp.zeros_like(ref)
            dext[tm:tm + CONV_PAD, :] = jnp.zeros((CONV_PAD, D_CONV), F32)

        chunks = [slice(c, c + MIX_CHUNK) for c in range(0, tm, MIX_CHUNK)]
        whole = slice(0, tm)
        ln_g_c, ln_b_c = cvec_ref[1:2, :], cvec_ref[2:3, :]

        halo = hga_ref[...] * _sig(hgb_ref[...])
        ext[0:CONV_PAD, :] = jnp.where(i > 0, halo, 0.0)
        for c in chunks:
            za = za_ref[c, :]
            attg_s[c, :] = (att_ref[c, :] * (za * _sig(za))).astype(BF16)
            ext[CONV_PAD + c.start:CONV_PAD + c.stop, :] = ga_ref[c, :] * _sig(gb_ref[c, :])
        _make_shifted(ext, ext_shifted, tm)
        for c in chunks:
            xh, rstd_c = _norm_stats(c1_ref[c, :])
            xh_s[c, :] = xh
            rstd_s[c, :] = rstd_c
            c2 = xh * ln_g_c + ln_b_c
            zc = zc_ref[c, :]
            cz_s[c, :] = ((c2 * _sig(c2)) * (zc * _sig(zc))).astype(BF16)
        ab_s[...] = _dot(attg_s[...], wap_ref[...])
        cb_s[...] = _dot(cz_s[...], wcp_ref[...]) + dvec_ref[0:1, :]

        for c in chunks:
            s_ga = _sig(jnp.concatenate([ga0_ref[c, :], ga1_ref[c, :]], axis=1))
            s_gc = _sig(jnp.concatenate([gc0_ref[c, :], gc1_ref[c, :]], axis=1))
            sga_s[c, :] = s_ga
            sgc_s[c, :] = s_gc
            merged_s[c, :] = (s_ga * ab_s[c, :] + s_gc * cb_s[c, :]).astype(BF16)
            dxo_c = dxo_ref[c, :]
            hh, rstd_h = _norm_stats(h_ref[c, :])
            ddvec_ref[1:2, :] += jnp.sum(dxo_c * hh, axis=0, keepdims=True)
            ddvec_ref[2:3, :] += jnp.sum(dxo_c, axis=0, keepdims=True)
            dh = _norm_bwd(dxo_c * dvec_ref[1:2, :], hh, rstd_h)
            dres_ref[c, :] = DEEPNORM_ALPHA * dh
            dy_s[c, :] = dh.astype(BF16)
        dwout_ref[...] += _dot_tn(merged_s[...], dy_s[...])
        dm_s[...] = _dot_nt(dy_s[...], wout_ref[...])

        for c in chunks:
            dm, s_ga, s_gc = dm_s[c, :], sga_s[c, :], sgc_s[c, :]
            dab_s[c, :] = (dm * s_ga).astype(BF16)
            dcb = dm * s_gc
            dcb_s[c, :] = dcb.astype(BF16)
            ddvec_ref[0:1, :] += jnp.sum(dcb, axis=0, keepdims=True)
            dur_ref[c, 4 * SEG:6 * SEG] = (dm * ab_s[c, :] * (s_ga * (1.0 - s_ga))).astype(BF16)
            dur_ref[c, 6 * SEG:8 * SEG] = (dm * cb_s[c, :] * (s_gc * (1.0 - s_gc))).astype(BF16)
        dwap_ref[...] += _dot_tn(attg_s[...], dab_s[...])
        dattg_s[...] = _dot_nt(dab_s[...], wap_ref[...])
        dwcp_ref[...] += _dot_tn(cz_s[...], dcb_s[...])
        dcz_s[...] = _dot_nt(dcb_s[...], wcp_ref[...])

        for c in chunks:
            za = za_ref[c, :]
            s_za = _sig(za)
            dattg = dattg_s[c, :]
            datt_ref[c, :] = dattg * (za * s_za)
            dur_ref[c, 0:SEG] = (dattg * att_ref[c, :] * (s_za * (1.0 + za * (1.0 - s_za)))).astype(BF16)
            xh = xh_s[c, :]
            c2 = xh * ln_g_c + ln_b_c
            s_c2 = _sig(c2)
            zc = zc_ref[c, :]
            s_zc = _sig(zc)
            dcz = dcz_s[c, :]
            dur_ref[c, 3 * SEG:4 * SEG] = (dcz * (c2 * s_c2) * (s_zc * (1.0 + zc * (1.0 - s_zc)))).astype(BF16)
            dc2 = dcz * (zc * s_zc) * (s_c2 * (1.0 + c2 * (1.0 - s_c2)))
            dcvec_ref[1:2, :] += jnp.sum(dc2 * xh, axis=0, keepdims=True)
            dcvec_ref[2:3, :] += jnp.sum(dc2, axis=0, keepdims=True)
            dc1 = _norm_bwd(dc2 * ln_g_c, xh, rstd_s[c, :])
            dcvec_ref[0:1, :] += jnp.sum(dc1, axis=0, keepdims=True)
            dext[c, :] = dc1

        dc1_tile = dext[0:tm, :]
        for k in range(CONV_WIDTH):
            dcw_ref[k:k + 1, :] += jnp.sum(dc1_tile * _tap_window(ext, ext_shifted, 2 + k, whole),
                                           axis=0, keepdims=True)
        _make_shifted(dext, dext_shifted, tm)
        for c in chunks:
            dc0 = cw_ref[0:1, :] * _tap_window(dext, dext_shifted, CONV_WIDTH - 1, c)
            for k in range(1, CONV_WIDTH):
                dc0 = dc0 + cw_ref[k:k + 1, :] * _tap_window(dext, dext_shifted, CONV_WIDTH - 1 - k, c)
            s_gb = _sig(gb_ref[c, :])
            dur_ref[c, SEG:2 * SEG] = (dc0 * s_gb).astype(BF16)
            dur_ref[c, 2 * SEG:3 * SEG] = (dc0 * ga_ref[c, :] * (s_gb * (1.0 - s_gb))).astype(BF16)
        dext[tm:tm + CONV_PAD, :] = dext[0:CONV_PAD, :]

        if n_h:
            @pl.when(r == nt - 1)
            def _():
                _Exchange(h_ins, h_outs, *scratch[n_scratch:]).wait()

    rev = lambda r: nt - 1 - r
    row = pl.BlockSpec((tm, D_MODEL), lambda r: (rev(r), 0))
    halo_spec = [pl.BlockSpec((CONV_PAD, SEG),
                              functools.partial(lambda k, r: (jnp.maximum(rev(r) * halo_per_tile - 1, 0), k), k))
                 for k in (4, 5)]
    const = lambda shape: _resident(shape, lambda r: (0,) * len(shape))
    out = pl.pallas_call(
        body, name="mix_bwd", grid=(nt,),
        in_specs=[row, row, pl.BlockSpec((tm, D_CONV), lambda r: (rev(r), 0))] + _u_specs(tm, rev) + halo_spec + [
            pl.BlockSpec((tm, D_ATT), lambda r: (rev(r), 0)),
            _resident((D_ATT, D_MODEL), lambda r: (0, 0)),
            _resident((D_CONV, D_MODEL), lambda r: (0, 0)),
            _resident((D_MODEL, D_MODEL), lambda r: (0, 0)),
            _resident((CONV_PAD, D_CONV), lambda r: (0, 0)),
            pl.BlockSpec((None, 8, D_CONV), lambda r: (layer, 0, 0)),
            pl.BlockSpec((None, 8, D_MODEL), lambda r: (layer, 0, 0))] + h_in,
        out_specs=[pl.BlockSpec((tm, D_ATT), lambda r: (rev(r), 0)),
                   pl.BlockSpec((tm, 8 * SEG), lambda r: (rev(r), 0)),
                   row,
                   const((D_ATT, D_MODEL)), const((D_CONV, D_MODEL)), const((D_MODEL, D_MODEL)),
                   const((CONV_PAD, D_CONV)), const((8, D_CONV)), const((8, D_MODEL))] + h_out,
        out_shape=[jax.ShapeDtypeStruct((T, D_ATT), F32), jax.ShapeDtypeStruct((T, 8 * SEG), BF16),
                   jax.ShapeDtypeStruct((T, D_MODEL), F32),
                   jax.ShapeDtypeStruct((D_ATT, D_MODEL), F32), jax.ShapeDtypeStruct((D_CONV, D_MODEL), F32),
                   jax.ShapeDtypeStruct((D_MODEL, D_MODEL), F32), jax.ShapeDtypeStruct((CONV_PAD, D_CONV), F32),
                   jax.ShapeDtypeStruct((8, D_CONV), F32), jax.ShapeDtypeStruct((8, D_MODEL), F32)] + h_shape,
        scratch_shapes=[pltpu.VMEM((tm + CONV_PAD, D_CONV), F32), pltpu.VMEM((tm + CONV_PAD, D_CONV), F32),
                        _shifted_scratch(tm), _shifted_scratch(tm),
                        pltpu.VMEM((tm, D_ATT), BF16), pltpu.VMEM((tm, D_CONV), BF16)]
        + [pltpu.VMEM((tm, D_MODEL), BF16)] * 4
        + [pltpu.VMEM((tm, D_CONV), F32)] * 3
        + [pltpu.VMEM((tm, D_MODEL), F32)] * 5
        + [pltpu.VMEM((tm, 1), F32)]
        + h_sems,
        compiler_params=_params("arbitrary"),
    )(dxo, h, c1, *([u] * 10), att, wap, wcp, wout, cw, cvec, dvec, *hosted)
    return out[:9], out[9:]


def _loss_head(y, target):
    T = y.shape[0]
    tm = min(T, 512)

    def body(y_ref, t_ref, dy_ref, loss_ref):
        @pl.when(pl.program_id(0) == 0)
        def _():
            loss_ref[...] = jnp.zeros_like(loss_ref)

        err = y_ref[...] - t_ref[...]
        dy_ref[...] = err * (1.0 / D_MODEL)
        loss_ref[...] += 0.5 * jnp.sum(jnp.mean(err * err, axis=-1, keepdims=True))

    row = pl.BlockSpec((tm, D_MODEL), lambda i: (i, 0))
    return pl.pallas_call(
        body, name="loss_head", grid=(T // tm,), in_specs=[row, row],
        out_specs=[row, pl.BlockSpec((8, 128), lambda i: (0, 0))],
        out_shape=[jax.ShapeDtypeStruct((T, D_MODEL), F32), jax.ShapeDtypeStruct((8, 128), F32)],
        compiler_params=_params("arbitrary"),
    )(y, target)


def _adamw(w, g, m, v):
    m = ADAM_B1 * m + (1.0 - ADAM_B1) * g
    v = ADAM_B2 * v + (1.0 - ADAM_B2) * (g * g)
    m_hat = m / (1.0 - ADAM_B1 ** ADAM_STEP)
    v_hat = v / (1.0 - ADAM_B2 ** ADAM_STEP)
    return -ADAM_LR * (m_hat / (jnp.sqrt(v_hat) + ADAM_EPS) + ADAM_WD * w), m, v


def _sum_slots(ref):
    total = ref[0].astype(F32)
    for j in range(1, N_DEV):
        total = total + ref[j].astype(F32)
    return total


def _adamw_sharded(recv, w, m, v, prev, layer, tr, name):
    n_layers, R, C = w.shape

    def body(*refs):
        recv_ref, w_ref, m_ref, v_ref = refs[:4]
        g_ref, d_ref, nm_ref, nv_ref = refs[-4:]
        g = _sum_slots(recv_ref)
        g_ref[...] = g
        d_ref[...], nm_ref[...], nv_ref[...] = _adamw(w_ref[...], g, m_ref[...], v_ref[...])

    slab = pl.BlockSpec((None, tr, C), lambda i: (layer, i, 0))
    n_prev = 0 if prev is None else 4
    return pl.pallas_call(
        body, name=name, grid=(R // tr,),
        in_specs=[pl.BlockSpec((N_DEV, tr, C), lambda i: (0, i, 0)), slab, slab, slab] + [ANY] * n_prev,
        out_specs=[slab] * 4,
        out_shape=[jax.ShapeDtypeStruct(w.shape, F32)] * 4,
        input_output_aliases={4 + k: k for k in range(n_prev)},
        compiler_params=_params("parallel"),
    )(recv, w, m, v, *(prev or ()))


def _sum8(g):
    R = g.shape[1]

    def body(g_ref, o_ref):
        o_ref[...] = _sum_slots(g_ref)

    return pl.pallas_call(body, name="sum_small_grads", out_shape=jax.ShapeDtypeStruct((R, 128), F32))(g)


def _adamw_small(w, g, m, v):
    def body(w_ref, g_ref, m_ref, v_ref, d_ref, nm_ref, nv_ref):
        d_ref[...], nm_ref[...], nv_ref[...] = _adamw(w_ref[...], g_ref[...], m_ref[...], v_ref[...])

    return pl.pallas_call(body, name="adamw_small", out_shape=[jax.ShapeDtypeStruct(w.shape, F32)] * 3)(w, g, m, v)


REPLICATED = (("b_in", D_IN), ("conv_b", D_CONV), ("conv_ln_g", D_CONV), ("conv_ln_b", D_CONV),
              ("b_conv_proj", D_MODEL), ("ln_g", D_MODEL), ("ln_b", D_MODEL))
N_REPLICATED = sum(n for _, n in REPLICATED)
CONV_W_SHARD_PAD = 2048


def _pack_small(rep, conv_w_shard):
    flat = jnp.concatenate([rep[k] for k, _ in REPLICATED], axis=1)
    cws = jnp.pad(conv_w_shard.reshape(DEPTH, -1), ((0, 0), (0, CONV_W_SHARD_PAD - CONV_WIDTH * CONV_W_SHARD)))
    return jnp.concatenate([flat, cws], axis=1).reshape(-1, LANES)


def _unpack_small(packed):
    per = packed.reshape(DEPTH, N_REPLICATED + CONV_W_SHARD_PAD)
    out, off = {}, 0
    for k, n in REPLICATED:
        out[k] = per[:, off:off + n]
        off += n
    out["conv_w"] = per[:, off:off + CONV_WIDTH * CONV_W_SHARD].reshape(DEPTH, CONV_WIDTH, CONV_W_SHARD)
    return out


def kernel(x, w_in, b_in, conv_w, conv_b, conv_ln_g, conv_ln_b, w_att_proj, w_conv_proj, b_conv_proj, w_out, ln_g, ln_b, loss_target, m_w_in, m_b_in, m_conv_w, m_conv_b, m_conv_ln_g, m_conv_ln_b, m_w_att_proj, m_w_conv_proj, m_b_conv_proj, m_w_out, m_ln_g, m_ln_b, v_w_in, v_b_in, v_conv_w, v_conv_b, v_conv_ln_g, v_conv_ln_b, v_w_att_proj, v_w_conv_proj, v_b_conv_proj, v_w_out, v_ln_g, v_ln_b):
    x0 = x[0]
    target = loss_target[0]

    def shards(l):
        return [w_in[l].astype(BF16), w_att_proj[l].astype(BF16), w_conv_proj[l].astype(BF16),
                w_out[l].astype(BF16), conv_w[l]]

    def full_weights(g_in, g_ap, g_cp, g_out, g_cw):
        return (g_in.transpose(1, 0, 2).reshape(D_MODEL, D_IN), g_ap.transpose(1, 0, 2).reshape(D_ATT, D_MODEL),
                g_cp.transpose(1, 0, 2).reshape(D_CONV, D_MODEL), g_out.reshape(D_MODEL, D_MODEL),
                jnp.pad(g_cw.transpose(1, 0, 2).reshape(CONV_WIDTH, D_CONV), ((0, 1), (0, 0))))

    pad8 = lambda rows: jnp.pad(jnp.stack(rows, axis=1), ((0, 0), (0, 8 - len(rows)), (0, 0)))
    cvec = pad8([conv_b, conv_ln_g, conv_ln_b])
    dvec = pad8([b_conv_proj, ln_g, ln_b])
    b_in3 = b_in.reshape(DEPTH, 1, D_IN)

    weights = [full_weights(*_all_gather(shards(0), "gather_weights"))]
    xs, xbs, us, atts, hs, c1s = [x0], [x0.astype(BF16)], [], [], [], []
    for l in range(DEPTH):
        w_in_l, wap, wcp, wout, cw = weights[l]
        u = _in_proj(xbs[l], w_in_l, b_in3, l)
        att, gathered = _attn_fwd(u, gather=shards(l + 1) if l + 1 < DEPTH else ())
        if gathered:
            weights.append(full_weights(*gathered))
        xn, xnb, h, c1 = _mix_fwd(xs[l], u, att, wap, wcp, wout, cw, cvec, dvec, l)
        us.append(u), atts.append(att), hs.append(h), c1s.append(c1), xs.append(xn), xbs.append(xnb)
    dxo, loss_part = _loss_head(xs[DEPTH], target)
    loss = lax.psum(loss_part[0, 0], ("x", "y", "c"))

    big = {"w_in": (w_in, m_w_in, v_w_in, 256), "w_att_proj": (w_att_proj, m_w_att_proj, v_w_att_proj, D_ATT),
           "w_conv_proj": (w_conv_proj, m_w_conv_proj, v_w_conv_proj, D_CONV), "w_out": (w_out, m_w_out, v_w_out, 128)}
    results = {k: None for k in big}
    small = [None] * DEPTH

    def update(k, recv, l):
        w, m, v, tr = big[k]
        results[k] = _adamw_sharded(recv, w, m, v, results[k], l, tr, "adamw_" + k)

    pending_w_in = ()
    for l in reversed(range(DEPTH)):
        w_in_l, wap, wcp, wout, cw = weights[l]
        (datt, du_rest, dres, dwap, dwcp, dwout, dcw, dcvec, ddvec), got = _mix_bwd(
            dxo, hs[l], c1s[l], us[l], atts[l], wap, wcp, wout, cw, cvec, dvec, l, hosted=pending_w_in)
        if got:
            update("w_in", got[0], l + 1)
        proj = [dwap.reshape(D_ATT, N_DEV, D_MODEL // N_DEV).transpose(1, 0, 2).astype(BF16),
                dwcp.reshape(D_CONV, N_DEV, D_MODEL // N_DEV).transpose(1, 0, 2).astype(BF16),
                dwout.reshape(N_DEV, D_MODEL // N_DEV, D_MODEL).astype(BF16)]
        dqkv, got = _attn_bwd(us[l], atts[l], datt, hosted=proj)
        for k, r in zip(("w_att_proj", "w_conv_proj", "w_out"), got):
            update(k, r, l)
        dwin, dbin = _in_proj_dw(xbs[l], dqkv, du_rest)
        small[l] = jnp.concatenate([dbin.reshape(-1), dcvec[:3].reshape(-1), ddvec[:3].reshape(-1), dcw.reshape(-1)])
        pending_w_in = (dwin.reshape(D_MODEL, N_DEV, W_IN_SHARD).transpose(1, 0, 2).astype(BF16),)
        dxo, got = _in_proj_dx(dqkv, du_rest, w_in_l, dres, hosted=pending_w_in if l == 0 else ())
        if got:
            update("w_in", got[0], l)
    grad_x = dxo[None]

    small_all = jnp.stack(small).reshape(-1, LANES)
    (gathered,) = _all_gather([small_all], "gather_small_grads")
    per_layer = _sum8(gathered).reshape(DEPTH, -1)
    g_rep, off = {}, 0
    for k, n in (("b_in", D_IN), ("conv_b", D_CONV), ("conv_ln_g", D_CONV), ("conv_ln_b", D_CONV),
                 ("b_conv_proj", D_MODEL), ("ln_g", D_MODEL), ("ln_b", D_MODEL)):
        g_rep[k] = per_layer[:, off:off + n]
        off += n
    g_cw_full = per_layer[:, off:].reshape(DEPTH, CONV_PAD, D_CONV)[:, :CONV_WIDTH]
    g_cw_shard = lax.dynamic_slice_in_dim(g_cw_full, _my_index() * CONV_W_SHARD, CONV_W_SHARD, axis=2)
    rep_w = dict(b_in=b_in, conv_b=conv_b, conv_ln_g=conv_ln_g, conv_ln_b=conv_ln_b, b_conv_proj=b_conv_proj,
                 ln_g=ln_g, ln_b=ln_b)
    rep_m = dict(b_in=m_b_in, conv_b=m_conv_b, conv_ln_g=m_conv_ln_g, conv_ln_b=m_conv_ln_b,
                 b_conv_proj=m_b_conv_proj, ln_g=m_ln_g, ln_b=m_ln_b)
    rep_v = dict(b_in=v_b_in, conv_b=v_conv_b, conv_ln_g=v_conv_ln_g, conv_ln_b=v_conv_ln_b,
                 b_conv_proj=v_b_conv_proj, ln_g=v_ln_g, ln_b=v_ln_b)
    sd, sm, sv = _adamw_small(_pack_small(rep_w, conv_w), _pack_small(g_rep, g_cw_shard),
                              _pack_small(rep_m, m_conv_w), _pack_small(rep_v, v_conv_w))
    small_out = [dict(g_rep, conv_w=g_cw_shard), _unpack_small(sd), _unpack_small(sm), _unpack_small(sv)]

    order = ["w_in", "b_in", "conv_w", "conv_b", "conv_ln_g", "conv_ln_b", "w_att_proj", "w_conv_proj",
             "b_conv_proj", "w_out", "ln_g", "ln_b"]
    outs = [loss, grad_x]
    for kind in range(4):
        outs += [results[k][kind] if k in big else small_out[kind][k] for k in order]
    return tuple(outs)
```

```python
import functools

import jax
import jax.numpy as jnp
from jax import lax
from jax.experimental import pallas as pl
from jax.experimental.pallas import tpu as pltpu

F32, BF16 = jnp.float32, jnp.bfloat16

D_MODEL = 1024
D_ATT = 512
D_CONV = 512
HEAD_DIM = 64
PAIR = 2 * HEAD_DIM
CONV_WIDTH = 31
CONV_PAD = 32
DEPTH = 4
N_DEV = 8
SEG = 512
D_IN = 4 * D_ATT + 3 * D_CONV + 2 * D_MODEL
N_SEG = D_IN // SEG
N_QKV_SEG = 3
W_IN_SHARD = D_IN // N_DEV
CONV_W_SHARD = D_CONV // N_DEV
LANES = 128
LN_EPS = 1e-5
DEEPNORM_ALPHA = (2 * DEPTH) ** 0.25
ATT_SCALE = HEAD_DIM ** -0.5
ATT_BLOCK = 256
ADAM_LR, ADAM_B1, ADAM_B2, ADAM_EPS, ADAM_WD, ADAM_STEP = 0.001, 0.9, 0.999, 1e-08, 0.01, 10
VMEM_LIMIT = 56 * 1024 * 1024
MESH = pl.DeviceIdType.MESH
ANY = pl.BlockSpec(memory_space=pl.ANY)


def _resident(shape, index_map):
    return pl.BlockSpec(shape, index_map, pipeline_mode=pl.Buffered(1))


def _params(*semantics):
    return pltpu.CompilerParams(dimension_semantics=semantics, vmem_limit_bytes=VMEM_LIMIT)


def _sig(x):
    return 1.0 / (1.0 + jnp.exp(-x))


def _dot(a, b):
    return jnp.dot(a, b, preferred_element_type=F32)


def _dot_nt(a, b):
    return lax.dot_general(a, b, (((1,), (1,)), ((), ())), preferred_element_type=F32)


def _dot_tn(a, b):
    return lax.dot_general(a, b, (((0,), (0,)), ((), ())), preferred_element_type=F32)


def _my_index():
    return 4 * lax.axis_index("x") + 2 * lax.axis_index("y") + lax.axis_index("c")


class _Gather:
    def __init__(self, ins, outs, send_sems, recv_sems, local_sems):
        self.n = n = len(ins)
        x, y, c = lax.axis_index("x"), lax.axis_index("y"), lax.axis_index("c")
        me, sibling = (x, y, c), (x, y, 1 - c)
        chips = [(1 - x, y), (x, 1 - y), (1 - x, 1 - y)]

        def slot(a, dev):
            return outs[a].at[4 * dev[0] + 2 * dev[1] + dev[2]]

        def copy(a, k, block, to, src=None):
            return pltpu.make_async_remote_copy(
                src_ref=slot(a, block) if src is None else src, dst_ref=slot(a, block),
                send_sem=send_sems.at[7 * a + k], recv_sem=recv_sems.at[7 * a + k], device_id=to, device_id_type=MESH)

        self.mine = [pltpu.make_async_copy(ins[a], slot(a, me), local_sems.at[a]) for a in range(n)]
        self.first, self.landed, self.passed, self.last = [], [], [], []
        for a in range(n):
            self.first.append(copy(a, 0, me, sibling, src=ins[a]))
            self.first += [copy(a, 1 + j, me, (*chip, c), src=ins[a]) for j, chip in enumerate(chips)]
        for j, chip in enumerate(chips):
            for a in range(n):
                self.landed.append(copy(a, 1 + j, (*chip, c), me))
                self.passed.append(copy(a, 4 + j, (*chip, c), sibling))
        for a in range(n):
            self.last.append(copy(a, 0, sibling, me))
            self.last += [copy(a, 4 + j, (*chip, 1 - c), me) for j, chip in enumerate(chips)]

    @staticmethod
    def semaphores(n):
        return [pltpu.SemaphoreType.DMA((7 * n,)), pltpu.SemaphoreType.DMA((7 * n,)), pltpu.SemaphoreType.DMA((n,))]

    def start(self):
        for cp in self.mine + self.first:
            cp.start()

    def forward(self):
        for landed, passed in zip(self.landed, self.passed):
            landed.wait_recv()
            passed.start()

    def wait(self):
        for cp in self.last:
            cp.wait_recv()
        for cp in self.first + self.passed:
            cp.wait_send()
        for cp in self.mine:
            cp.wait()


def _gathered_shapes(shards):
    return [jax.ShapeDtypeStruct((N_DEV,) + s.shape, s.dtype) for s in shards]


def _all_gather(shards, name):
    n = len(shards)

    def body(*refs):
        g = _Gather(refs[:n], refs[n:2 * n], *refs[2 * n:])
        g.start()
        g.forward()
        g.wait()

    return pl.pallas_call(
        body, name=name, out_shape=_gathered_shapes(shards),
        in_specs=[ANY] * n, out_specs=[ANY] * n, scratch_shapes=_Gather.semaphores(n),
    )(*shards)


class _Exchange:
    def __init__(self, ins, outs, send_sems, recv_sems, local_sems):
        n = len(ins)
        x, y, c = lax.axis_index("x"), lax.axis_index("y"), lax.axis_index("c")
        me_idx = 4 * x + 2 * y + c
        self.mine = [pltpu.make_async_copy(ins[a].at[me_idx], outs[a].at[me_idx], local_sems.at[a])
                     for a in range(n)]
        self.sends, self.recvs = [], []
        for d in (1, 2, 4, 3, 5, 6, 7):
            px = 1 - x if d & 4 else x
            py = 1 - y if d & 2 else y
            pc = 1 - c if d & 1 else c
            idx = 4 * px + 2 * py + pc
            for a in range(n):
                sems = dict(send_sem=send_sems.at[7 * a + d - 1], recv_sem=recv_sems.at[7 * a + d - 1],
                            device_id=(px, py, pc), device_id_type=MESH)
                self.sends.append(pltpu.make_async_remote_copy(
                    src_ref=ins[a].at[idx], dst_ref=outs[a].at[me_idx], **sems))
                self.recvs.append(pltpu.make_async_remote_copy(
                    src_ref=ins[a].at[idx], dst_ref=outs[a].at[idx], **sems))

    @staticmethod
    def semaphores(n):
        return [pltpu.SemaphoreType.DMA((7 * n,)), pltpu.SemaphoreType.DMA((7 * n,)), pltpu.SemaphoreType.DMA((n,))]

    def start(self):
        for cp in self.mine + self.sends:
            cp.start()

    def wait(self):
        for cp in self.recvs:
            cp.wait_recv()
        for cp in self.sends:
            cp.wait_send()
        for cp in self.mine:
            cp.wait()


def _host(hosted):
    n = len(hosted)
    return ([ANY] * n, [ANY] * n, [jax.ShapeDtypeStruct(p.shape, p.dtype) for p in hosted],
            _Exchange.semaphores(n) if n else [])


def _split_refs(refs, *counts):
    out, at = [], 0
    for k in counts:
        out.append(refs[at:at + k])
        at += k
    return out + [refs[at:]]


def _in_proj(xb, w_in, b_in, layer):
    T = xb.shape[0]
    tm, tn = min(T, 2048), SEG

    def body(x_ref, w_ref, b_ref, u_ref):
        u_ref[...] = _dot(x_ref[...], w_ref[...]) + b_ref[...]

    return pl.pallas_call(
        body, name="in_proj", grid=(T // tm, D_IN // tn),
        in_specs=[pl.BlockSpec((tm, D_MODEL), lambda m, n: (m, 0)),
                  pl.BlockSpec((D_MODEL, tn), lambda m, n: (0, n)),
                  pl.BlockSpec((None, 1, tn), lambda m, n: (layer, 0, n))],
        out_specs=pl.BlockSpec((tm, tn), lambda m, n: (m, n)),
        out_shape=jax.ShapeDtypeStruct((T, D_IN), F32),
        compiler_params=_params("parallel", "parallel"),
    )(xb, w_in, b_in)


def _du_specs(tm, row_of, seg_of):
    return [pl.BlockSpec((None, tm, SEG), lambda *g: (jnp.minimum(seg_of(*g), N_QKV_SEG - 1), row_of(*g), 0)),
            pl.BlockSpec((tm, SEG), lambda *g: (row_of(*g), jnp.maximum(seg_of(*g) - N_QKV_SEG, 0)))]


def _in_proj_dx(dqkv, du_rest, w_in, dres, hosted=()):
    T = du_rest.shape[0]
    tm = min(T, 512)
    nm = T // tm
    n_h = len(hosted)
    h_in, h_out, h_shape, h_sems = _host(hosted)
    n_rest = du_rest.shape[1]

    def body(*refs):
        (q_ref, k_ref, v_ref, a_ref, w_ref, r_ref), h_ins, (o_ref,), h_outs, sems = _split_refs(refs, 6, n_h, 1, n_h)
        m = pl.program_id(0)

        if n_h:
            @pl.when(m == 0)
            def _():
                _Exchange(h_ins, h_outs, *sems).start()

        acc = r_ref[...] + _dot_nt(a_ref[...], w_ref[:, N_QKV_SEG * SEG:N_QKV_SEG * SEG + n_rest])
        for k, seg_ref in enumerate((q_ref, k_ref, v_ref)):
            acc = acc + _dot_nt(seg_ref[...], w_ref[:, k * SEG:(k + 1) * SEG])
        o_ref[...] = acc

        if n_h:
            @pl.when(m == nm - 1)
            def _():
                _Exchange(h_ins, h_outs, *sems).wait()

    out = pl.pallas_call(
        body, name="in_proj_dx", grid=(nm,),
        in_specs=[pl.BlockSpec((None, tm, SEG), functools.partial(lambda k, m: (k, m, 0), k))
                  for k in range(N_QKV_SEG)] + [
            pl.BlockSpec((tm, n_rest), lambda m: (m, 0)),
            _resident((D_MODEL, D_IN), lambda m: (0, 0)),
            pl.BlockSpec((tm, D_MODEL), lambda m: (m, 0))] + h_in,
        out_specs=[pl.BlockSpec((tm, D_MODEL), lambda m: (m, 0))] + h_out,
        out_shape=[jax.ShapeDtypeStruct((T, D_MODEL), F32)] + h_shape,
        scratch_shapes=h_sems,
        compiler_params=_params("arbitrary"),
    )(dqkv, dqkv, dqkv, du_rest, w_in, dres, *hosted)
    return out[0], out[1:]


def _in_proj_dw(xb, dqkv, du_rest):
    T = xb.shape[0]
    tk = min(T, 4096)
    nt = T // tk

    def body(x_ref, a1_ref, a2_ref, dw_ref, db_ref, acc_ref):
        s, t = pl.program_id(0), pl.program_id(1)

        @pl.when(t == 0)
        def _():
            acc_ref[...] = jnp.zeros_like(acc_ref)
            db_ref[...] = jnp.zeros_like(db_ref)

        def acc(a):
            acc_ref[...] += _dot_tn(x_ref[...], a)
            db_ref[...] += jnp.sum(a.astype(F32), axis=0, keepdims=True)

        @pl.when(s < N_QKV_SEG)
        def _():
            acc(a1_ref[...])

        @pl.when(s >= N_QKV_SEG)
        def _():
            acc(a2_ref[...])

        @pl.when(t == nt - 1)
        def _():
            dw_ref[...] = acc_ref[...].astype(BF16)

    return pl.pallas_call(
        body, name="in_proj_dw", grid=(N_SEG, nt),
        in_specs=[pl.BlockSpec((tk, D_MODEL), lambda s, t: (t, 0))] + _du_specs(tk, lambda s, t: t, lambda s, t: s),
        out_specs=[pl.BlockSpec((D_MODEL, SEG), lambda s, t: (0, s)),
                   pl.BlockSpec((None, 1, SEG), lambda s, t: (s, 0, 0))],
        out_shape=[jax.ShapeDtypeStruct((D_MODEL, D_IN), BF16), jax.ShapeDtypeStruct((N_SEG, 1, SEG), F32)],
        scratch_shapes=[pltpu.VMEM((D_MODEL, SEG), F32)],
        compiler_params=_params("parallel", "arbitrary"),
    )(xb, dqkv, du_rest)


ATT_CHUNK = 32
ATT_Q_PER_STEP = 2


def _chunks():
    return [slice(r, r + ATT_CHUNK) for r in range(0, ATT_BLOCK, ATT_CHUNK)]


def _twice(rhs):
    return jnp.concatenate([rhs, rhs], axis=0)


def _fail_and_log_beta(z):
    t = jnp.maximum(z, 0.0) + jnp.log(1.0 + jnp.exp(-jnp.abs(z)))
    return t, z - t


def _store_split(dst, rows, val):
    hi = val.astype(BF16)
    dst[rows, 0:ATT_BLOCK] = hi
    dst[rows, ATT_BLOCK:2 * ATT_BLOCK] = (val - hi.astype(F32)).astype(BF16)


def _fill_causal(keep):
    row = lax.broadcasted_iota(jnp.int32, (ATT_BLOCK, ATT_BLOCK), 0)
    col = lax.broadcasted_iota(jnp.int32, (ATT_BLOCK, ATT_BLOCK), 1)
    keep[...] = (col < row).astype(F32)


def _triangles():
    row = lax.broadcasted_iota(jnp.int32, (ATT_BLOCK, ATT_BLOCK), 0)
    col = lax.broadcasted_iota(jnp.int32, (ATT_BLOCK, ATT_BLOCK), 1)
    return _twice((row > col).astype(BF16)), _twice((row >= col).astype(BF16))


def _cast_pair(p, k_ref, v_ref, kb, vb):
    for pair in range(D_ATT // PAIR):
        @pl.when(p == pair)
        def _():
            for h in range(2):
                cols = slice(PAIR * pair + HEAD_DIM * h, PAIR * pair + HEAD_DIM * (h + 1))
                kb[h] = k_ref[:, cols].astype(BF16)
                vb[h] = v_ref[:, cols].astype(BF16)


def _staggered(stages):
    stages[0](0)
    stages[0](1)
    for k in range(1, len(stages), 2):
        for h in range(2):
            stages[k](h)
            stages[k + 1](h)


EXP_UNDERFLOW = 104.0


def _sweep_keys(i, tile, rsum):
    tile(i, True)

    def live(carry):
        j, low = carry
        return jnp.logical_and(j >= 0, low < EXP_UNDERFLOW)

    def step(carry):
        j, _ = carry
        tile(j, False)
        return j - 1, jnp.min(rsum[...])

    lax.while_loop(live, step, (i - 1, jnp.min(rsum[...])))


def _attn_fwd(u, gather=()):
    T = u.shape[0]
    B = ATT_BLOCK
    per_step = ATT_Q_PER_STEP
    ns = T // (B * per_step)
    n_pairs = D_ATT // PAIR
    n_g = len(gather)

    def body(*refs):
        (q_ref, k_ref, v_ref), g_ins, (o_ref,), g_outs, scratch = _split_refs(refs, 3, n_g, 1, n_g)
        kb, vb, acc, rsum, t0, zs, lt, cat, keep = scratch[:9]
        p, s = pl.program_id(0), pl.program_id(1)

        if n_g:
            @pl.when(jnp.logical_and(p == 0, s == 0))
            def _():
                _Gather(g_ins, g_outs, *scratch[9:]).start()

            @pl.when(jnp.logical_and(p == n_pairs - 1, s == 0))
            def _():
                _Gather(g_ins, g_outs, *scratch[9:]).forward()

        @pl.when(s == 0)
        def _():
            _fill_causal(keep)
            _cast_pair(p, k_ref, v_ref, kb, vb)

        tri2, _ = _triangles()
        hs = [slice(HEAD_DIM * h, HEAD_DIM * (h + 1)) for h in range(2)]

        def query_block(sb, carry):
            i = s * per_step + sb
            qrows = pl.ds(pl.multiple_of(sb * B, B), B)
            q = [(q_ref[qrows, hs[h]] * ATT_SCALE).astype(BF16) for h in range(2)]
            acc[...] = jnp.zeros_like(acc)
            rsum[...] = jnp.zeros_like(rsum)
            _sweep_keys(i, functools.partial(tile, q), rsum)
            for h in range(2):
                o_ref[qrows, hs[h]] = acc[h]
            return carry

        def tile(q, j, masked):
            keys = pl.ds(pl.multiple_of(j * B, B), B)

            def scores(h):
                zs[h] = _dot_nt(q[h], kb[h, keys, :])

            def fails(h):
                for r in _chunks():
                    t, a = _fail_and_log_beta(zs[h, r, :])
                    if masked:
                        t = t * keep[r, :]
                    _store_split(cat.at[h], r, t)
                    zs[h, r, :] = a
                    t0[h, r, :] = t[:, 0:1]

            def later_sums(h):
                lt[h] = _dot(cat[h], tri2)

            def weights(h):
                for r in _chunks():
                    later = lt[h, r, :]
                    w = jnp.exp(zs[h, r, :] - later - rsum[h, r, :])
                    if masked:
                        w = w * keep[r, :]
                    _store_split(cat.at[h], r, w)
                    rsum[h, r, :] += later[:, 0:1] + t0[h, r, :]

            def values(h):
                acc[h] += _dot(cat[h], _twice(vb[h, keys, :]))

            _staggered([scores, fails, later_sums, weights, values])

        lax.fori_loop(0, per_step, query_block, 0)

        if n_g:
            @pl.when(jnp.logical_and(p == n_pairs - 1, s == ns - 1))
            def _():
                _Gather(g_ins, g_outs, *scratch[9:]).wait()

    out = pl.pallas_call(
        body, name="attn_fwd", grid=(n_pairs, ns),
        in_specs=[pl.BlockSpec((per_step * B, PAIR), lambda p, s: (s, p)),
                  _resident((T, D_ATT), lambda p, s: (0, 1)),
                  _resident((T, D_ATT), lambda p, s: (0, 2))] + [ANY] * n_g,
        out_specs=[pl.BlockSpec((per_step * B, PAIR), lambda p, s: (s, p))] + [ANY] * n_g,
        out_shape=[jax.ShapeDtypeStruct((T, D_ATT), F32)] + _gathered_shapes(gather),
        scratch_shapes=[pltpu.VMEM((2, T, HEAD_DIM), BF16), pltpu.VMEM((2, T, HEAD_DIM), BF16),
                        pltpu.VMEM((2, B, HEAD_DIM), F32), pltpu.VMEM((2, B, 1), F32), pltpu.VMEM((2, B, 1), F32),
                        pltpu.VMEM((2, B, B), F32), pltpu.VMEM((2, B, B), F32), pltpu.VMEM((2, B, 2 * B), BF16),
                        pltpu.VMEM((B, B), F32)]
        + (_Gather.semaphores(n_g) if n_g else []),
        compiler_params=_params("arbitrary", "arbitrary"),
    )(u, u, u, *gather)
    return out[0], out[1:]


def _attn_bwd(u, att, datt, hosted=()):
    T = u.shape[0]
    B = ATT_BLOCK
    per_step = ATT_Q_PER_STEP
    ns = T // (B * per_step)
    n_pairs = D_ATT // PAIR
    n_h = len(hosted)
    h_in, h_out, h_shape, h_sems = _host(hosted)

    def body(*refs):
        (q_ref, k_ref, v_ref, o_ref, do_ref), h_ins, (dqkv_ref,), h_outs, scratch = _split_refs(refs, 5, n_h, 1, n_h)
        n_scratch = 17
        (kb, vb, dkacc, dvacc, dqacc, rsum, gsum, total, t0, lt0, zs, gs, lt, cat, wb, dzb,
         keep) = scratch[:n_scratch]
        p, s = pl.program_id(0), pl.program_id(1)

        if n_h:
            @pl.when(jnp.logical_and(p == 0, s == 0))
            def _():
                _Exchange(h_ins, h_outs, *scratch[n_scratch:]).start()

        @pl.when(s == 0)
        def _():
            _fill_causal(keep)
            _cast_pair(p, k_ref, v_ref, kb, vb)
            dkacc[...] = jnp.zeros_like(dkacc)
            dvacc[...] = jnp.zeros_like(dvacc)

        tri2, tri_incl2 = _triangles()
        hs = [slice(HEAD_DIM * h, HEAD_DIM * (h + 1)) for h in range(2)]
        both = lambda acc2: jnp.concatenate([acc2[0], acc2[1]], axis=1)

        def query_block(sb, carry):
            i = s * per_step + sb
            local = pl.ds(pl.multiple_of(sb * B, B), B)
            q = [(q_ref[local, hs[h]] * ATT_SCALE).astype(BF16) for h in range(2)]
            dob = [do_ref[local, hs[h]].astype(BF16) for h in range(2)]
            for h in range(2):
                total[h] = jnp.sum(dob[h].astype(F32) * o_ref[local, hs[h]], axis=1, keepdims=True)
            dqacc[...] = jnp.zeros_like(dqacc)
            rsum[...] = jnp.zeros_like(rsum)
            gsum[...] = jnp.zeros_like(gsum)
            _sweep_keys(i, functools.partial(tile, q, dob), rsum)
            dqkv_ref[0, pl.ds(pl.multiple_of(i * B, B), B), :] = (both(dqacc) * ATT_SCALE).astype(BF16)
            return carry

        def tile(q, dob, j, masked):
            keys = pl.ds(pl.multiple_of(j * B, B), B)

            def scores(h):
                zs[h] = _dot_nt(q[h], kb[h, keys, :])
                gs[h] = _dot_nt(dob[h], vb[h, keys, :])

            def fails(h):
                for r in _chunks():
                    t, a = _fail_and_log_beta(zs[h, r, :])
                    if masked:
                        t = t * keep[r, :]
                    _store_split(cat.at[h], r, t)
                    zs[h, r, :] = a
                    t0[h, r, :] = t[:, 0:1]

            def later_sums(h):
                lt[h] = _dot(cat[h], tri2)

            def weights(h):
                for r in _chunks():
                    later = lt[h, r, :]
                    w = jnp.exp(zs[h, r, :] - later - rsum[h, r, :])
                    if masked:
                        w = w * keep[r, :]
                    g = gs[h, r, :] * w
                    gs[h, r, :] = g
                    _store_split(cat.at[h], r, g)
                    wb[h, r, :] = w.astype(BF16)
                    lt0[h, r, :] = later[:, 0:1]

            def suffix_sums(h):
                lt[h] = _dot(cat[h], tri_incl2)

            def score_grads(h):
                for r in _chunks():
                    suffix = lt[h, r, :]
                    g = gs[h, r, :]
                    before = total[h, r, :] - (suffix + gsum[h, r, :])
                    dz = g - jnp.exp(zs[h, r, :]) * (g + before)
                    if masked:
                        dz = dz * keep[r, :]
                    dzb[h, r, :] = dz.astype(BF16)
                    rsum[h, r, :] += lt0[h, r, :] + t0[h, r, :]
                    gsum[h, r, :] += suffix[:, 0:1]

            def input_grads(h):
                dqacc[h] += _dot(dzb[h], kb[h, keys, :])
                dkacc[h, keys, :] += _dot_tn(dzb[h], q[h])
                dvacc[h, keys, :] += _dot_tn(wb[h], dob[h])

            _staggered([scores, fails, later_sums, weights, suffix_sums, score_grads, input_grads])

        lax.fori_loop(0, per_step, query_block, 0)

        @pl.when(s == ns - 1)
        def _():
            dqkv_ref[1] = both(dkacc).astype(BF16)
            dqkv_ref[2] = both(dvacc).astype(BF16)

        if n_h:
            @pl.when(jnp.logical_and(p == n_pairs - 1, s == ns - 1))
            def _():
                _Exchange(h_ins, h_outs, *scratch[n_scratch:]).wait()

    rows_spec = pl.BlockSpec((per_step * B, PAIR), lambda p, s: (s, p))
    out = pl.pallas_call(
        body, name="attn_bwd", grid=(n_pairs, ns),
        in_specs=[rows_spec,
                  _resident((T, D_ATT), lambda p, s: (0, 1)),
                  _resident((T, D_ATT), lambda p, s: (0, 2)),
                  rows_spec, rows_spec] + h_in,
        out_specs=[pl.BlockSpec((3, T, PAIR), lambda p, s: (0, 0, p))] + h_out,
        out_shape=[jax.ShapeDtypeStruct((3, T, D_ATT), BF16)] + h_shape,
        scratch_shapes=[pltpu.VMEM((2, T, HEAD_DIM), BF16), pltpu.VMEM((2, T, HEAD_DIM), BF16),
                        pltpu.VMEM((2, T, HEAD_DIM), F32), pltpu.VMEM((2, T, HEAD_DIM), F32),
                        pltpu.VMEM((2, B, HEAD_DIM), F32)]
        + [pltpu.VMEM((2, B, 1), F32)] * 5
        + [pltpu.VMEM((2, B, B), F32)] * 3
        + [pltpu.VMEM((2, B, 2 * B), BF16), pltpu.VMEM((2, B, B), BF16), pltpu.VMEM((2, B, B), BF16)]
        + [pltpu.VMEM((B, B), F32)]
        + h_sems,
        compiler_params=_params("arbitrary", "arbitrary"),
    )(u, u, u, att, datt, *hosted)
    return out[0], out[1:]


MIX_TILE = 256
MIX_CHUNK = 16


def _u_specs(tm, tile_of):
    return [pl.BlockSpec((tm, SEG), functools.partial(lambda k, *g: (tile_of(*g), k), k)) for k in range(3, N_SEG)]


SHIFT_ROWS = 24


def _shifted_scratch(tm):
    return pltpu.VMEM((7, tm + SHIFT_ROWS, D_CONV), F32)


def _make_shifted(base, shifted, tm):
    for b in range(1, 8):
        shifted[b - 1] = base[b:b + tm + SHIFT_ROWS, :]


def _tap_window(base, shifted, start, rows):
    whole, b = divmod(start, 8)
    if b == 0:
        return base[start + rows.start:start + rows.stop, :]
    return shifted[b - 1, 8 * whole + rows.start:8 * whole + rows.stop, :]


def _conv(ext, ext_shifted, cw_ref, rows):
    acc = cw_ref[0:1, :] * _tap_window(ext, ext_shifted, 2, rows)
    for k in range(1, CONV_WIDTH):
        acc = acc + cw_ref[k:k + 1, :] * _tap_window(ext, ext_shifted, 2 + k, rows)
    return acc


def _norm_stats(v):
    mu = jnp.mean(v, axis=-1, keepdims=True)
    vc = v - mu
    rstd = lax.rsqrt(jnp.mean(vc * vc, axis=-1, keepdims=True) + LN_EPS)
    return vc * rstd, rstd


def _norm_bwd(dy_scaled, xhat, rstd):
    return rstd * (dy_scaled - jnp.mean(dy_scaled, axis=-1, keepdims=True)
                   - xhat * jnp.mean(dy_scaled * xhat, axis=-1, keepdims=True))


def _mix_fwd(x, u, att, wap, wcp, wout, cw, cvec, dvec, layer):
    T = x.shape[0]
    tm = MIX_TILE

    def body(x_ref, za_ref, ga_ref, gb_ref, zc_ref, ga0_ref, ga1_ref, gc0_ref, gc1_ref, att_ref,
             wap_ref, wcp_ref, wout_ref, cw_ref, cvec_ref, dvec_ref, xn_ref, xnb_ref, h_ref, c1_ref,
             ext, ext_shifted):
        i = pl.program_id(0)

        @pl.when(i == 0)
        def _():
            ext[0:CONV_PAD, :] = jnp.zeros((CONV_PAD, D_CONV), F32)

        za = za_ref[...]
        attg = att_ref[...] * (za * _sig(za))
        ab = _dot(attg.astype(BF16), wap_ref[...])

        ext[CONV_PAD:CONV_PAD + tm, :] = ga_ref[...] * _sig(gb_ref[...])
        _make_shifted(ext, ext_shifted, tm)
        c1 = _conv(ext, ext_shifted, cw_ref, slice(0, tm)) + cvec_ref[0:1, :]
        c1_ref[...] = c1
        ext[0:CONV_PAD, :] = ext[tm:tm + CONV_PAD, :]
        xh, _ = _norm_stats(c1)
        c2 = xh * cvec_ref[1:2, :] + cvec_ref[2:3, :]
        zc = zc_ref[...]
        cz = (c2 * _sig(c2)) * (zc * _sig(zc))
        cb = _dot(cz.astype(BF16), wcp_ref[...]) + dvec_ref[0:1, :]

        g_att = jnp.concatenate([ga0_ref[...], ga1_ref[...]], axis=1)
        g_conv = jnp.concatenate([gc0_ref[...], gc1_ref[...]], axis=1)
        merged = _sig(g_att) * ab + _sig(g_conv) * cb
        h = DEEPNORM_ALPHA * x_ref[...] + _dot(merged.astype(BF16), wout_ref[...])
        h_ref[...] = h
        hh, _ = _norm_stats(h)
        xn = hh * dvec_ref[1:2, :] + dvec_ref[2:3, :]
        xn_ref[...] = xn
        xnb_ref[...] = xn.astype(BF16)

    row = pl.BlockSpec((tm, D_MODEL), lambda i: (i, 0))
    return pl.pallas_call(
        body, name="mix_fwd", grid=(T // tm,),
        in_specs=[row] + _u_specs(tm, lambda i: i) + [
            pl.BlockSpec((tm, D_ATT), lambda i: (i, 0)),
            _resident((D_ATT, D_MODEL), lambda i: (0, 0)),
            _resident((D_CONV, D_MODEL), lambda i: (0, 0)),
            _resident((D_MODEL, D_MODEL), lambda i: (0, 0)),
            _resident((CONV_PAD, D_CONV), lambda i: (0, 0)),
            pl.BlockSpec((None, 8, D_CONV), lambda i: (layer, 0, 0)),
            pl.BlockSpec((None, 8, D_MODEL), lambda i: (layer, 0, 0))],
        out_specs=[row, row, row, pl.BlockSpec((tm, D_CONV), lambda i: (i, 0))],
        out_shape=[jax.ShapeDtypeStruct((T, D_MODEL), F32), jax.ShapeDtypeStruct((T, D_MODEL), BF16),
                   jax.ShapeDtypeStruct((T, D_MODEL), F32), jax.ShapeDtypeStruct((T, D_CONV), F32)],
        scratch_shapes=[pltpu.VMEM((tm + CONV_PAD, D_CONV), F32), _shifted_scratch(tm)],
        compiler_params=_params("arbitrary"),
    )(x, *([u] * 8), att, wap, wcp, wout, cw, cvec, dvec)


def _mix_bwd(dxo, h, c1, u, att, wap, wcp, wout, cw, cvec, dvec, layer, hosted=()):
    T = dxo.shape[0]
    tm = MIX_TILE
    nt = T // tm
    halo_per_tile = tm // CONV_PAD
    n_h = len(hosted)
    h_in, h_out, h_shape, h_sems = _host(hosted)

    def body(*refs):
        ins, h_ins, outs, h_outs, scratch = _split_refs(refs, 20, n_h, 9, n_h)
        (dxo_ref, h_ref, c1_ref, za_ref, ga_ref, gb_ref, zc_ref, ga0_ref, ga1_ref, gc0_ref, gc1_ref, hga_ref, hgb_ref,
         att_ref, wap_ref, wcp_ref, wout_ref, cw_ref, cvec_ref, dvec_ref) = ins
        datt_ref, dur_ref, dres_ref, dwap_ref, dwcp_ref, dwout_ref, dcw_ref, dcvec_ref, ddvec_ref = outs
        n_scratch = 19
        (ext, dext, ext_shifted, dext_shifted, attg_s, cz_s, merged_s, dy_s, dab_s, dcb_s,
         xh_s, dattg_s, dcz_s, ab_s, cb_s, sga_s, sgc_s, dm_s, rstd_s) = scratch[:n_scratch]
        r = pl.program_id(0)
        i = nt - 1 - r

        if n_h:
            @pl.when(r == 0)
            def _():
                _Exchange(h_ins, h_outs, *scratch[n_scratch:]).start()

        @pl.when(r == 0)
        def _():
            for ref in (dwap_ref, dwcp_ref, dwout_ref, dcw_ref, dcvec_ref, ddvec_ref):
                ref[...] = jnp.zeros_like(ref)
            dext[tm:tm + CONV_PAD, :] = jnp.zeros((CONV_PAD, D_CONV), F32)

        chunks = [slice(c, c + MIX_CHUNK) for c in range(0, tm, MIX_CHUNK)]
        whole = slice(0, tm)
        ln_g_c, ln_b_c = cvec_ref[1:2, :], cvec_ref[2:3, :]

        halo = hga_ref[...] * _sig(hgb_ref[...])
        ext[0:CONV_PAD, :] = jnp.where(i > 0, halo, 0.0)
        for c in chunks:
            za = za_ref[c, :]
            attg_s[c, :] = (att_ref[c, :] * (za * _sig(za))).astype(BF16)
            ext[CONV_PAD + c.start:CONV_PAD + c.stop, :] = ga_ref[c, :] * _sig(gb_ref[c, :])
        _make_shifted(ext, ext_shifted, tm)
        for c in chunks:
            xh, rstd_c = _norm_stats(c1_ref[c, :])
            xh_s[c, :] = xh
            rstd_s[c, :] = rstd_c
            c2 = xh * ln_g_c + ln_b_c
            zc = zc_ref[c, :]
            cz_s[c, :] = ((c2 * _sig(c2)) * (zc * _sig(zc))).astype(BF16)
        ab_s[...] = _dot(attg_s[...], wap_ref[...])
        cb_s[...] = _dot(cz_s[...], wcp_ref[...]) + dvec_ref[0:1, :]

        for c in chunks:
            s_ga = _sig(jnp.concatenate([ga0_ref[c, :], ga1_ref[c, :]], axis=1))
            s_gc = _sig(jnp.concatenate([gc0_ref[c, :], gc1_ref[c, :]], axis=1))
            sga_s[c, :] = s_ga
            sgc_s[c, :] = s_gc
            merged_s[c, :] = (s_ga * ab_s[c, :] + s_gc * cb_s[c, :]).astype(BF16)
            dxo_c = dxo_ref[c, :]
            hh, rstd_h = _norm_stats(h_ref[c, :])
            ddvec_ref[1:2, :] += jnp.sum(dxo_c * hh, axis=0, keepdims=True)
            ddvec_ref[2:3, :] += jnp.sum(dxo_c, axis=0, keepdims=True)
            dh = _norm_bwd(dxo_c * dvec_ref[1:2, :], hh, rstd_h)
            dres_ref[c, :] = DEEPNORM_ALPHA * dh
            dy_s[c, :] = dh.astype(BF16)
        dwout_ref[...] += _dot_tn(merged_s[...], dy_s[...])
        dm_s[...] = _dot_nt(dy_s[...], wout_ref[...])

        for c in chunks:
            dm, s_ga, s_gc = dm_s[c, :], sga_s[c, :], sgc_s[c, :]
            dab_s[c, :] = (dm * s_ga).astype(BF16)
            dcb = dm * s_gc
            dcb_s[c, :] = dcb.astype(BF16)
            ddvec_ref[0:1, :] += jnp.sum(dcb, axis=0, keepdims=True)
            dur_ref[c, 4 * SEG:6 * SEG] = (dm * ab_s[c, :] * (s_ga * (1.0 - s_ga))).astype(BF16)
            dur_ref[c, 6 * SEG:8 * SEG] = (dm * cb_s[c, :] * (s_gc * (1.0 - s_gc))).astype(BF16)
        dwap_ref[...] += _dot_tn(attg_s[...], dab_s[...])
        dattg_s[...] = _dot_nt(dab_s[...], wap_ref[...])
        dwcp_ref[...] += _dot_tn(cz_s[...], dcb_s[...])
        dcz_s[...] = _dot_nt(dcb_s[...], wcp_ref[...])

        for c in chunks:
            za = za_ref[c, :]
            s_za = _sig(za)
            dattg = dattg_s[c, :]
            datt_ref[c, :] = dattg * (za * s_za)
            dur_ref[c, 0:SEG] = (dattg * att_ref[c, :] * (s_za * (1.0 + za * (1.0 - s_za)))).astype(BF16)
            xh = xh_s[c, :]
            c2 = xh * ln_g_c + ln_b_c
            s_c2 = _sig(c2)
            zc = zc_ref[c, :]
            s_zc = _sig(zc)
            dcz = dcz_s[c, :]
            dur_ref[c, 3 * SEG:4 * SEG] = (dcz * (c2 * s_c2) * (s_zc * (1.0 + zc * (1.0 - s_zc)))).astype(BF16)
            dc2 = dcz * (zc * s_zc) * (s_c2 * (1.0 + c2 * (1.0 - s_c2)))
            dcvec_ref[1:2, :] += jnp.sum(dc2 * xh, axis=0, keepdims=True)
            dcvec_ref[2:3, :] += jnp.sum(dc2, axis=0, keepdims=True)
            dc1 = _norm_bwd(dc2 * ln_g_c, xh, rstd_s[c, :])
            dcvec_ref[0:1, :] += jnp.sum(dc1, axis=0, keepdims=True)
            dext[c, :] = dc1

        dc1_tile = dext[0:tm, :]
        for k in range(CONV_WIDTH):
            dcw_ref[k:k + 1, :] += jnp.sum(dc1_tile * _tap_window(ext, ext_shifted, 2 + k, whole),
                                           axis=0, keepdims=True)
        _make_shifted(dext, dext_shifted, tm)
        for c in chunks:
            dc0 = cw_ref[0:1, :] * _tap_window(dext, dext_shifted, CONV_WIDTH - 1, c)
            for k in range(1, CONV_WIDTH):
                dc0 = dc0 + cw_ref[k:k + 1, :] * _tap_window(dext, dext_shifted, CONV_WIDTH - 1 - k, c)
            s_gb = _sig(gb_ref[c, :])
            dur_ref[c, SEG:2 * SEG] = (dc0 * s_gb).astype(BF16)
            dur_ref[c, 2 * SEG:3 * SEG] = (dc0 * ga_ref[c, :] * (s_gb * (1.0 - s_gb))).astype(BF16)
        dext[tm:tm + CONV_PAD, :] = dext[0:CONV_PAD, :]

        if n_h:
            @pl.when(r == nt - 1)
            def _():
                _Exchange(h_ins, h_outs, *scratch[n_scratch:]).wait()

    rev = lambda r: nt - 1 - r
    row = pl.BlockSpec((tm, D_MODEL), lambda r: (rev(r), 0))
    halo_spec = [pl.BlockSpec((CONV_PAD, SEG),
                              functools.partial(lambda k, r: (jnp.maximum(rev(r) * halo_per_tile - 1, 0), k), k))
                 for k in (4, 5)]
    const = lambda shape: _resident(shape, lambda r: (0,) * len(shape))
    out = pl.pallas_call(
        body, name="mix_bwd", grid=(nt,),
        in_specs=[row, row, pl.BlockSpec((tm, D_CONV), lambda r: (rev(r), 0))] + _u_specs(tm, rev) + halo_spec + [
            pl.BlockSpec((tm, D_ATT), lambda r: (rev(r), 0)),
            _resident((D_ATT, D_MODEL), lambda r: (0, 0)),
            _resident((D_CONV, D_MODEL), lambda r: (0, 0)),
            _resident((D_MODEL, D_MODEL), lambda r: (0, 0)),
            _resident((CONV_PAD, D_CONV), lambda r: (0, 0)),
            pl.BlockSpec((None, 8, D_CONV), lambda r: (layer, 0, 0)),
            pl.BlockSpec((None, 8, D_MODEL), lambda r: (layer, 0, 0))] + h_in,
        out_specs=[pl.BlockSpec((tm, D_ATT), lambda r: (rev(r), 0)),
                   pl.BlockSpec((tm, 8 * SEG), lambda r: (rev(r), 0)),
                   row,
                   const((D_ATT, D_MODEL)), const((D_CONV, D_MODEL)), const((D_MODEL, D_MODEL)),
                   const((CONV_PAD, D_CONV)), const((8, D_CONV)), const((8, D_MODEL))] + h_out,
        out_shape=[jax.ShapeDtypeStruct((T, D_ATT), F32), jax.ShapeDtypeStruct((T, 8 * SEG), BF16),
                   jax.ShapeDtypeStruct((T, D_MODEL), F32),
                   jax.ShapeDtypeStruct((D_ATT, D_MODEL), F32), jax.ShapeDtypeStruct((D_CONV, D_MODEL), F32),
                   jax.ShapeDtypeStruct((D_MODEL, D_MODEL), F32), jax.ShapeDtypeStruct((CONV_PAD, D_CONV), F32),
                   jax.ShapeDtypeStruct((8, D_CONV), F32), jax.ShapeDtypeStruct((8, D_MODEL), F32)] + h_shape,
        scratch_shapes=[pltpu.VMEM((tm + CONV_PAD, D_CONV), F32), pltpu.VMEM((tm + CONV_PAD, D_CONV), F32),
                        _shifted_scratch(tm), _shifted_scratch(tm),
                        pltpu.VMEM((tm, D_ATT), BF16), pltpu.VMEM((tm, D_CONV), BF16)]
        + [pltpu.VMEM((tm, D_MODEL), BF16)] * 4
        + [pltpu.VMEM((tm, D_CONV), F32)] * 3
        + [pltpu.VMEM((tm, D_MODEL), F32)] * 5
        + [pltpu.VMEM((tm, 1), F32)]
        + h_sems,
        compiler_params=_params("arbitrary"),
    )(dxo, h, c1, *([u] * 10), att, wap, wcp, wout, cw, cvec, dvec, *hosted)
    return out[:9], out[9:]


def _loss_head(y, target):
    T = y.shape[0]
    tm = min(T, 512)

    def body(y_ref, t_ref, dy_ref, loss_ref):
        @pl.when(pl.program_id(0) == 0)
        def _():
            loss_ref[...] = jnp.zeros_like(loss_ref)

        err = y_ref[...] - t_ref[...]
        dy_ref[...] = err * (1.0 / D_MODEL)
        loss_ref[...] += 0.5 * jnp.sum(jnp.mean(err * err, axis=-1, keepdims=True))

    row = pl.BlockSpec((tm, D_MODEL), lambda i: (i, 0))
    return pl.pallas_call(
        body, name="loss_head", grid=(T // tm,), in_specs=[row, row],
        out_specs=[row, pl.BlockSpec((8, 128), lambda i: (0, 0))],
        out_shape=[jax.ShapeDtypeStruct((T, D_MODEL), F32), jax.ShapeDtypeStruct((8, 128), F32)],
        compiler_params=_params("arbitrary"),
    )(y, target)


def _adamw(w, g, m, v):
    m = ADAM_B1 * m + (1.0 - ADAM_B1) * g
    v = ADAM_B2 * v + (1.0 - ADAM_B2) * (g * g)
    m_hat = m / (1.0 - ADAM_B1 ** ADAM_STEP)
    v_hat = v / (1.0 - ADAM_B2 ** ADAM_STEP)
    return -ADAM_LR * (m_hat / (jnp.sqrt(v_hat) + ADAM_EPS) + ADAM_WD * w), m, v


def _sum_slots(ref):
    total = ref[0].astype(F32)
    for j in range(1, N_DEV):
        total = total + ref[j].astype(F32)
    return total


def _adamw_sharded(recv, w, m, v, prev, layer, tr, name):
    n_layers, R, C = w.shape

    def body(*refs):
        recv_ref, w_ref, m_ref, v_ref = refs[:4]
        g_ref, d_ref, nm_ref, nv_ref = refs[-4:]
        g = _sum_slots(recv_ref)
        g_ref[...] = g
        d_ref[...], nm_ref[...], nv_ref[...] = _adamw(w_ref[...], g, m_ref[...], v_ref[...])

    slab = pl.BlockSpec((None, tr, C), lambda i: (layer, i, 0))
    n_prev = 0 if prev is None else 4
    return pl.pallas_call(
        body, name=name, grid=(R // tr,),
        in_specs=[pl.BlockSpec((N_DEV, tr, C), lambda i: (0, i, 0)), slab, slab, slab] + [ANY] * n_prev,
        out_specs=[slab] * 4,
        out_shape=[jax.ShapeDtypeStruct(w.shape, F32)] * 4,
        input_output_aliases={4 + k: k for k in range(n_prev)},
        compiler_params=_params("parallel"),
    )(recv, w, m, v, *(prev or ()))


def _sum8(g):
    R = g.shape[1]

    def body(g_ref, o_ref):
        o_ref[...] = _sum_slots(g_ref)

    return pl.pallas_call(body, name="sum_small_grads", out_shape=jax.ShapeDtypeStruct((R, 128), F32))(g)


def _adamw_small(w, g, m, v):
    def body(w_ref, g_ref, m_ref, v_ref, d_ref, nm_ref, nv_ref):
        d_ref[...], nm_ref[...], nv_ref[...] = _adamw(w_ref[...], g_ref[...], m_ref[...], v_ref[...])

    return pl.pallas_call(body, name="adamw_small", out_shape=[jax.ShapeDtypeStruct(w.shape, F32)] * 3)(w, g, m, v)


REPLICATED = (("b_in", D_IN), ("conv_b", D_CONV), ("conv_ln_g", D_CONV), ("conv_ln_b", D_CONV),
              ("b_conv_proj", D_MODEL), ("ln_g", D_MODEL), ("ln_b", D_MODEL))
N_REPLICATED = sum(n for _, n in REPLICATED)
CONV_W_SHARD_PAD = 2048


def _pack_small(rep, conv_w_shard):
    flat = jnp.concatenate([rep[k] for k, _ in REPLICATED], axis=1)
    cws = jnp.pad(conv_w_shard.reshape(DEPTH, -1), ((0, 0), (0, CONV_W_SHARD_PAD - CONV_WIDTH * CONV_W_SHARD)))
    return jnp.concatenate([flat, cws], axis=1).reshape(-1, LANES)


def _unpack_small(packed):
    per = packed.reshape(DEPTH, N_REPLICATED + CONV_W_SHARD_PAD)
    out, off = {}, 0
    for k, n in REPLICATED:
        out[k] = per[:, off:off + n]
        off += n
    out["conv_w"] = per[:, off:off + CONV_WIDTH * CONV_W_SHARD].reshape(DEPTH, CONV_WIDTH, CONV_W_SHARD)
    return out


def kernel(x, w_in, b_in, conv_w, conv_b, conv_ln_g, conv_ln_b, w_att_proj, w_conv_proj, b_conv_proj, w_out, ln_g, ln_b, loss_target, m_w_in, m_b_in, m_conv_w, m_conv_b, m_conv_ln_g, m_conv_ln_b, m_w_att_proj, m_w_conv_proj, m_b_conv_proj, m_w_out, m_ln_g, m_ln_b, v_w_in, v_b_in, v_conv_w, v_conv_b, v_conv_ln_g, v_conv_ln_b, v_w_att_proj, v_w_conv_proj, v_b_conv_proj, v_w_out, v_ln_g, v_ln_b):
    x0 = x[0]
    target = loss_target[0]

    def shards(l):
        return [w_in[l].astype(BF16), w_att_proj[l].astype(BF16), w_conv_proj[l].astype(BF16),
                w_out[l].astype(BF16), conv_w[l]]

    def full_weights(g_in, g_ap, g_cp, g_out, g_cw):
        return (g_in.transpose(1, 0, 2).reshape(D_MODEL, D_IN), g_ap.transpose(1, 0, 2).reshape(D_ATT, D_MODEL),
                g_cp.transpose(1, 0, 2).reshape(D_CONV, D_MODEL), g_out.reshape(D_MODEL, D_MODEL),
                jnp.pad(g_cw.transpose(1, 0, 2).reshape(CONV_WIDTH, D_CONV), ((0, 1), (0, 0))))

    pad8 = lambda rows: jnp.pad(jnp.stack(rows, axis=1), ((0, 0), (0, 8 - len(rows)), (0, 0)))
    cvec = pad8([conv_b, conv_ln_g, conv_ln_b])
    dvec = pad8([b_conv_proj, ln_g, ln_b])
    b_in3 = b_in.reshape(DEPTH, 1, D_IN)

    weights = [full_weights(*_all_gather(shards(0), "gather_weights"))]
    xs, xbs, us, atts, hs, c1s = [x0], [x0.astype(BF16)], [], [], [], []
    for l in range(DEPTH):
        w_in_l, wap, wcp, wout, cw = weights[l]
        u = _in_proj(xbs[l], w_in_l, b_in3, l)
        att, gathered = _attn_fwd(u, gather=shards(l + 1) if l + 1 < DEPTH else ())
        if gathered:
            weights.append(full_weights(*gathered))
        xn, xnb, h, c1 = _mix_fwd(xs[l], u, att, wap, wcp, wout, cw, cvec, dvec, l)
        us.append(u), atts.append(att), hs.append(h), c1s.append(c1), xs.append(xn), xbs.append(xnb)
    dxo, loss_part = _loss_head(xs[DEPTH], target)
    loss = lax.psum(loss_part[0, 0], ("x", "y", "c"))

    big = {"w_in": (w_in, m_w_in, v_w_in, 256), "w_att_proj": (w_att_proj, m_w_att_proj, v_w_att_proj, D_ATT),
           "w_conv_proj": (w_conv_proj, m_w_conv_proj, v_w_conv_proj, D_CONV), "w_out": (w_out, m_w_out, v_w_out, 128)}
    results = {k: None for k in big}
    small = [None] * DEPTH

    def update(k, recv, l):
        w, m, v, tr = big[k]
        results[k] = _adamw_sharded(recv, w, m, v, results[k], l, tr, "adamw_" + k)

    pending_w_in = ()
    for l in reversed(range(DEPTH)):
        w_in_l, wap, wcp, wout, cw = weights[l]
        (datt, du_rest, dres, dwap, dwcp, dwout, dcw, dcvec, ddvec), got = _mix_bwd(
            dxo, hs[l], c1s[l], us[l], atts[l], wap, wcp, wout, cw, cvec, dvec, l, hosted=pending_w_in)
        if got:
            update("w_in", got[0], l + 1)
        proj = [dwap.reshape(D_ATT, N_DEV, D_MODEL // N_DEV).transpose(1, 0, 2).astype(BF16),
                dwcp.reshape(D_CONV, N_DEV, D_MODEL // N_DEV).transpose(1, 0, 2).astype(BF16),
                dwout.reshape(N_DEV, D_MODEL // N_DEV, D_MODEL).astype(BF16)]
        dqkv, got = _attn_bwd(us[l], atts[l], datt, hosted=proj)
        for k, r in zip(("w_att_proj", "w_conv_proj", "w_out"), got):
            update(k, r, l)
        dwin, dbin = _in_proj_dw(xbs[l], dqkv, du_rest)
        small[l] = jnp.concatenate([dbin.reshape(-1), dcvec[:3].reshape(-1), ddvec[:3].reshape(-1), dcw.reshape(-1)])
        pending_w_in = (dwin.reshape(D_MODEL, N_DEV, W_IN_SHARD).transpose(1, 0, 2).astype(BF16),)
        dxo, got = _in_proj_dx(dqkv, du_rest, w_in_l, dres, hosted=pending_w_in if l == 0 else ())
        if got:
            update("w_in", got[0], l)
    grad_x = dxo[None]

    small_all = jnp.stack(small).reshape(-1, LANES)
    (gathered,) = _all_gather([small_all], "gather_small_grads")
    per_layer = _sum8(gathered).reshape(DEPTH, -1)
    g_rep, off = {}, 0
    for k, n in (("b_in", D_IN), ("conv_b", D_CONV), ("conv_ln_g", D_CONV), ("conv_ln_b", D_CONV),
                 ("b_conv_proj", D_MODEL), ("ln_g", D_MODEL), ("ln_b", D_MODEL)):
        g_rep[k] = per_layer[:, off:off + n]
        off += n
    g_cw_full = per_layer[:, off:].reshape(DEPTH, CONV_PAD, D_CONV)[:, :CONV_WIDTH]
    g_cw_shard = lax.dynamic_slice_in_dim(g_cw_full, _my_index() * CONV_W_SHARD, CONV_W_SHARD, axis=2)
    rep_w = dict(b_in=b_in, conv_b=conv_b, conv_ln_g=conv_ln_g, conv_ln_b=conv_ln_b, b_conv_proj=b_conv_proj,
                 ln_g=ln_g, ln_b=ln_b)
    rep_m = dict(b_in=m_b_in, conv_b=m_conv_b, conv_ln_g=m_conv_ln_g, conv_ln_b=m_conv_ln_b,
                 b_conv_proj=m_b_conv_proj, ln_g=m_ln_g, ln_b=m_ln_b)
    rep_v = dict(b_in=v_b_in, conv_b=v_conv_b, conv_ln_g=v_conv_ln_g, conv_ln_b=v_conv_ln_b,
                 b_conv_proj=v_b_conv_proj, ln_g=v_ln_g, ln_b=v_ln_b)
    sd, sm, sv = _adamw_small(_pack_small(rep_w, conv_w), _pack_small(g_rep, g_cw_shard),
                              _pack_small(rep_m, m_conv_w), _pack_small(rep_v, v_conv_w))
    small_out = [dict(g_rep, conv_w=g_cw_shard), _unpack_small(sd), _unpack_small(sm), _unpack_small(sv)]

    order = ["w_in", "b_in", "conv_w", "conv_b", "conv_ln_g", "conv_ln_b", "w_att_proj", "w_conv_proj",
             "b_conv_proj", "w_out", "ln_g", "ln_b"]
    outs = [loss, grad_x]
    for kind in range(4):
        outs += [results[k][kind] if k in big else small_out[kind][k] for k in order]
    return tuple(outs)
```
